```python
import math
import jax
import jax.numpy as jnp
from jax import lax
import numpy as np

D_MODEL = 1024
BATCH = 1
SEQ = 16384
DEPTH = 2

GRID_W = 64
CTX_LEN = 256
Q_BLOCK = 128
EPS = 1e-6
ROPE_BASE = 10000.0
NEG_INF = -1e30

A_HEADS = 4
A_DQK = 64
A_DV = 2 * A_DQK
B_HEADS = 8
B_DH = 64
NA_WIN_H = 8
NA_WIN_W = 16
C_HEADS = 8
C_Q_LORA = 384
C_KV_LORA = 256
C_NOPE = 64
C_ROPE = 32
C_DV = 64
D_HEADS = 8
D_KV_HEADS = 2
D_DH = 64
D_WINDOW = 128
FFN_HIDDEN = 2816
CONV_W = 3

EVEN_SPLIT = (2 * A_HEADS * A_DQK, 2 * A_HEADS * A_DQK, A_HEADS * A_DV, B_HEADS * B_DH, B_HEADS * B_DH, B_HEADS * B_DH)
ODD_SPLIT = (C_Q_LORA, C_KV_LORA, C_ROPE, D_HEADS * D_DH, D_KV_HEADS * D_DH, D_KV_HEADS * D_DH)
EVEN_IN = sum(EVEN_SPLIT)
ODD_IN = sum(ODD_SPLIT)
EVEN_MIX = A_HEADS * A_DV + B_HEADS * B_DH
ODD_MIX = C_HEADS * C_DV + D_HEADS * D_DH
N_EVEN = (DEPTH + 1) // 2
N_ODD = DEPTH // 2

kernel_name = 'hybrid_diff_na_mla_swa_convffn_dit'


def rms_norm(x, g):
    xf = x.astype(jnp.float32)
    y = xf * lax.rsqrt(jnp.mean(xf * xf, axis=-1, keepdims=True) + EPS)
    return (y * g.astype(jnp.float32)).astype(x.dtype)


def modulate(h, shift, scale):
    return h * (1 + scale) + shift


def split_cols(p, sizes):
    return jnp.split(p, np.cumsum(sizes)[:-1].tolist(), axis=-1)


def axial_rope_tables(n_tokens, dim):
    t = jnp.arange(n_tokens, dtype=jnp.int32)
    row = (t // GRID_W).astype(jnp.float32)
    col = (t % GRID_W).astype(jnp.float32)
    quarter = dim // 4
    inv_freq = ROPE_BASE ** (-jnp.arange(quarter, dtype=jnp.float32) / quarter)
    ang = jnp.concatenate([row[:, None] * inv_freq, col[:, None] * inv_freq], axis=-1)
    return jnp.cos(ang), jnp.sin(ang)


def apply_rope(x, cos, sin):
    half = x.shape[-1] // 2
    x1, x2 = x[..., :half], x[..., half:]
    c = cos[None, :, None, :].astype(x.dtype)
    s = sin[None, :, None, :].astype(x.dtype)
    return jnp.concatenate([x1 * c - x2 * s, x1 * s + x2 * c], axis=-1)


def rope_tail(x, cos, sin):
    return jnp.concatenate([x[..., :C_NOPE], apply_rope(x[..., C_NOPE:], cos, sin)], axis=-1)


def sweep_query_blocks(fn, q):
    B, T = q.shape[:2]
    nb = T // Q_BLOCK
    qb = jnp.moveaxis(q.reshape((B, nb, Q_BLOCK) + q.shape[2:]), 1, 0)
    out = jnp.moveaxis(lax.map(fn, qb), 0, 1)
    return out.reshape((B, T) + out.shape[3:])


def softmax_attend(q, k, v, scale):
    s = jnp.einsum('bqhd,bkhd->bhqk', q, k, preferred_element_type=jnp.float32) * scale
    p = jax.nn.softmax(s, axis=-1).astype(v.dtype)
    return jnp.einsum('bhqk,bkhe->bqhe', p, v)


def diff_attend(q, k, v, lam, scale):
    s = jnp.einsum('bqhmd,bkhmd->bhmqk', q, k, preferred_element_type=jnp.float32) * scale
    p = jax.nn.softmax(s, axis=-1)
    w = p[:, :, 0] - lam * p[:, :, 1]
    return jnp.einsum('bhqk,bkhe->bqhe', w.astype(v.dtype), v)


def gqa_sink_attend(q, k, v, sink, scale, mask=None):
    s = jnp.einsum('bqhgd,bkhd->bhgqk', q, k, preferred_element_type=jnp.float32) * scale
    if mask is not None:
        s = jnp.where(mask, s, NEG_INF)
    Hk, G = q.shape[2], q.shape[3]
    sk = jnp.broadcast_to(sink.reshape(Hk, G)[None, :, :, None, None].astype(jnp.float32), s.shape[:-1] + (1,))
    p = jax.nn.softmax(jnp.concatenate([s, sk], axis=-1), axis=-1)[..., :-1]
    return jnp.einsum('bhgqk,bkhd->bqhgd', p.astype(v.dtype), v)


def neighbourhood_attend(q, k, v, k_ctx, v_ctx, rpb):
    B, T, H, dh = q.shape
    rows = T // GRID_W
    wh = min(NA_WIN_H, rows)
    ww = NA_WIN_W
    L = k_ctx.shape[1]
    scale = 1.0 / math.sqrt(dh)
    col_start = np.clip(np.arange(GRID_W) - ww // 2, 0, GRID_W - ww)
    col_idx = (col_start[:, None] + np.arange(ww)[None, :]).astype(np.int32)
    dc = (col_idx - np.arange(GRID_W)[:, None] + (NA_WIN_W - 1)).astype(np.int32)
    qg = q.reshape(B, rows, GRID_W, H, dh)
    kg = k.reshape(B, rows, GRID_W, H, dh)
    vg = v.reshape(B, rows, GRID_W, H, dh)

    def row_block(r):
        qr = lax.dynamic_index_in_dim(qg, r, axis=1, keepdims=False)
        rs = jnp.clip(r - wh // 2, 0, rows - wh)
        kn = lax.dynamic_slice_in_dim(kg, rs, wh, axis=1)[:, :, col_idx]
        vn = lax.dynamic_slice_in_dim(vg, rs, wh, axis=1)[:, :, col_idx]
        dr = rs + jnp.arange(wh) - r + (NA_WIN_H - 1)
        bias = jnp.transpose(rpb[:, dr[:, None, None], dc[None, :, :]], (0, 2, 1, 3))
        s_nb = jnp.einsum('bchd,brcwhd->bhcrw', qr, kn, preferred_element_type=jnp.float32) * scale + bias.astype(jnp.float32)
        s_ctx = jnp.einsum('bchd,blhd->bhcl', qr, k_ctx, preferred_element_type=jnp.float32) * scale
        p = jax.nn.softmax(jnp.concatenate([s_ctx, s_nb.reshape(B, H, GRID_W, wh * ww)], axis=-1), axis=-1).astype(v.dtype)
        p_ctx = p[..., :L]
        p_nb = p[..., L:].reshape(B, H, GRID_W, wh, ww)
        return jnp.einsum('bhcl,blhd->bchd', p_ctx, v_ctx) + jnp.einsum('bhcrw,brcwhd->bchd', p_nb, vn)

    out = lax.map(row_block, jnp.arange(rows))
    return jnp.moveaxis(out, 0, 1).reshape(B, T, H, dh)


def window_gqa_latent(q, k, v, k_ctx, v_ctx, sink):
    B, T = q.shape[:2]
    L = k_ctx.shape[1]
    span = Q_BLOCK + 2 * D_WINDOW
    scale = 1.0 / math.sqrt(D_DH)
    pad = ((0, 0), (D_WINDOW, D_WINDOW), (0, 0), (0, 0))
    kp = jnp.pad(k, pad)
    vp = jnp.pad(v, pad)
    ctx_ok = jnp.ones((Q_BLOCK, L), dtype=bool)

    def band_block(b):
        start = b * Q_BLOCK
        qi = lax.dynamic_slice_in_dim(q, start, Q_BLOCK, axis=1)
        ki = lax.dynamic_slice_in_dim(kp, start, span, axis=1)
        vi = lax.dynamic_slice_in_dim(vp, start, span, axis=1)
        qpos = start + jnp.arange(Q_BLOCK)
        kpos = start - D_WINDOW + jnp.arange(span)
        band = (jnp.abs(kpos[None, :] - qpos[:, None]) <= D_WINDOW) & (kpos >= 0)[None, :] & (kpos < T)[None, :]
        mask = jnp.concatenate([ctx_ok, band], axis=-1)
        return gqa_sink_attend(qi, jnp.concatenate([k_ctx, ki], axis=1), jnp.concatenate([v_ctx, vi], axis=1), sink, scale, mask)

    out = lax.map(band_block, jnp.arange(T // Q_BLOCK))
    return jnp.moveaxis(out, 0, 1).reshape(B, T, D_HEADS * D_DH)


def even_mixer(h, hc, w_in, w_out, a_q_norm, a_k_norm, a_lam_q1, a_lam_k1, a_lam_q2, a_lam_k2, a_subln,
               b_q_norm, b_k_norm, b_rpb, lam_init, with_ctx):
    B, T, _ = h.shape
    cos, sin = axial_rope_tables(T, A_DQK)
    scale_a = 1.0 / math.sqrt(A_DQK)

    def project(hh, use_rope, need_q):
        Bh, Th, _ = hh.shape
        qa, ka, va, qb, kb, vb = split_cols(hh @ w_in, EVEN_SPLIT)
        ka = rms_norm(ka.reshape(Bh, Th, 2 * A_HEADS, A_DQK), a_k_norm)
        kb = rms_norm(kb.reshape(Bh, Th, B_HEADS, B_DH), b_k_norm)
        if use_rope:
            ka = apply_rope(ka, cos, sin)
        ka = ka.reshape(Bh, Th, A_HEADS, 2, A_DQK)
        va = va.reshape(Bh, Th, A_HEADS, A_DV)
        vb = vb.reshape(Bh, Th, B_HEADS, B_DH)
        if need_q:
            qa = rms_norm(qa.reshape(Bh, Th, 2 * A_HEADS, A_DQK), a_q_norm)
            if use_rope:
                qa = apply_rope(qa, cos, sin)
            qa = qa.reshape(Bh, Th, A_HEADS, 2, A_DQK)
            qb = rms_norm(qb.reshape(Bh, Th, B_HEADS, B_DH), b_q_norm)
        else:
            qa = qb = None
        return qa, ka, va, qb, kb, vb

    lam = (jnp.exp(jnp.sum(a_lam_q1 * a_lam_k1).astype(jnp.float32))
           - jnp.exp(jnp.sum(a_lam_q2 * a_lam_k2).astype(jnp.float32)) + lam_init)
    qa, ka, va, qb, kb, vb = project(h, True, True)
    qa_c, ka_c, va_c, qb_c, kb_c, vb_c = project(hc, False, with_ctx)

    def merge(oa, ob):
        Bo, To = oa.shape[:2]
        oa = rms_norm(oa, a_subln) * (1.0 - lam_init)
        return jnp.concatenate([oa.reshape(Bo, To, -1), ob.reshape(Bo, To, -1)], axis=-1) @ w_out

    ka_all = jnp.concatenate([ka_c, ka], axis=1)
    va_all = jnp.concatenate([va_c, va], axis=1)
    oa = sweep_query_blocks(lambda qi: diff_attend(qi, ka_all, va_all, lam, scale_a), qa)
    ob = neighbourhood_attend(qb, kb, vb, kb_c, vb_c, b_rpb)
    y = merge(oa, ob)
    yc = None
    if with_ctx:
        yc = merge(diff_attend(qa_c, ka_c, va_c, lam, scale_a), softmax_attend(qb_c, kb_c, vb_c, 1.0 / math.sqrt(B_DH)))
    return y, yc


def odd_mixer(h, hc, w_in, w_out, c_q_a_norm, c_w_qb, c_kv_a_norm, c_w_kvb, c_q_norm, c_k_norm,
              d_q_norm, d_k_norm, d_sink, with_ctx):
    B, T, _ = h.shape
    cos_c, sin_c = axial_rope_tables(T, C_ROPE)
    cos_d, sin_d = axial_rope_tables(T, D_DH)
    scale_c = 1.0 / math.sqrt(C_NOPE + C_ROPE)

    def project(hh, use_rope, need_q):
        Bh, Th, _ = hh.shape
        cq, ckv, ckr, dq, dk, dv = split_cols(hh @ w_in, ODD_SPLIT)
        kv = (rms_norm(ckv, c_kv_a_norm) @ c_w_kvb).reshape(Bh, Th, C_HEADS, C_NOPE + C_DV)
        mk = jnp.concatenate([kv[..., :C_NOPE], jnp.broadcast_to(ckr[:, :, None, :], (Bh, Th, C_HEADS, C_ROPE))], axis=-1)
        mk = rms_norm(mk, c_k_norm)
        mv = kv[..., C_NOPE:]
        gk = rms_norm(dk.reshape(Bh, Th, D_KV_HEADS, D_DH), d_k_norm)
        gv = dv.reshape(Bh, Th, D_KV_HEADS, D_DH)
        if use_rope:
            mk = rope_tail(mk, cos_c, sin_c)
            gk = apply_rope(gk, cos_d, sin_d)
        mq = gq = None
        if need_q:
            mq = (rms_norm(cq, c_q_a_norm) @ c_w_qb).reshape(Bh, Th, C_HEADS, C_NOPE + C_ROPE)
            mq = rms_norm(mq, c_q_norm)
            gq = rms_norm(dq.reshape(Bh, Th, D_HEADS, D_DH), d_q_norm)
            if use_rope:
                mq = rope_tail(mq, cos_c, sin_c)
                gq = apply_rope(gq, cos_d, sin_d)
            gq = gq.reshape(Bh, Th, D_KV_HEADS, D_HEADS // D_KV_HEADS, D_DH)
        return mq, mk, mv, gq, gk, gv

    mq, mk, mv, gq, gk, gv = project(h, True, True)
    mq_c, mk_c, mv_c, gq_c, gk_c, gv_c = project(hc, False, with_ctx)

    mk_all = jnp.concatenate([mk_c, mk], axis=1)
    mv_all = jnp.concatenate([mv_c, mv], axis=1)
    om = sweep_query_blocks(lambda qi: softmax_attend(qi, mk_all, mv_all, scale_c), mq)
    og = window_gqa_latent(gq, gk, gv, gk_c, gv_c, d_sink)
    y = jnp.concatenate([om.reshape(B, T, -1), og], axis=-1) @ w_out
    yc = None
    if with_ctx:
        L = hc.shape[1]
        om_c = softmax_attend(mq_c, mk_c, mv_c, scale_c).reshape(B, L, -1)
        og_c = gqa_sink_attend(gq_c, gk_c, gv_c, d_sink, 1.0 / math.sqrt(D_DH)).reshape(B, L, -1)
        yc = jnp.concatenate([om_c, og_c], axis=-1) @ w_out
    return y, yc


def conv_ffn(h, w_up, conv_w, conv_b, w_down):
    T = h.shape[1]
    u = h @ w_up
    half = CONV_W // 2
    up = jnp.pad(u, ((0, 0), (half, half), (0, 0)))
    u = conv_b + up[:, 0:T] * conv_w[0]
    for j in range(1, CONV_W):
        u = u + up[:, j:j + T] * conv_w[j]
    a, g = jnp.split(u, 2, axis=-1)
    return (a * jax.nn.silu(g)) @ w_down


def setup_inputs(seed: int = 0) -> dict:
    key = jax.random.key(seed)
    ks = list(jax.random.split(key, 40))

    def nrm(shape, s):
        return s * jax.random.normal(ks.pop(), shape, jnp.float32)

    def gain(shape):
        return 1.0 + nrm(shape, 0.05)

    D, F = D_MODEL, FFN_HIDDEN
    return {
        'x': nrm((BATCH, SEQ, D), 1.0),
        'c': nrm((BATCH, D), 1.0),
        'ctx': nrm((BATCH, CTX_LEN, D), 1.0),
        'c_ctx': nrm((D,), 1.0),
        'ada_w': nrm((DEPTH, D, 6 * D), 0.5 * D ** -0.5),
        'ada_b': nrm((DEPTH, 6 * D), 0.01),
        'norm_mix': gain((DEPTH, D)),
        'norm_ffn': gain((DEPTH, D)),
        'ffn_up': nrm((DEPTH, D, 2 * F), D ** -0.5),
        'ffn_conv_w': nrm((DEPTH, CONV_W, 2 * F), CONV_W ** -0.5),
        'ffn_conv_b': nrm((DEPTH, 2 * F), 0.01),
        'ffn_down': nrm((DEPTH, F, D), F ** -0.5),
        'ev_w_in': nrm((N_EVEN, D, EVEN_IN), D ** -0.5),
        'ev_w_out': nrm((N_EVEN, EVEN_MIX, D), EVEN_MIX ** -0.5),
        'a_q_norm': gain((N_EVEN, A_DQK)),
        'a_k_norm': gain((N_EVEN, A_DQK)),
        'a_lam_q1': nrm((N_EVEN, A_DQK), 0.1),
        'a_lam_k1': nrm((N_EVEN, A_DQK), 0.1),
        'a_lam_q2': nrm((N_EVEN, A_DQK), 0.1),
        'a_lam_k2': nrm((N_EVEN, A_DQK), 0.1),
        'a_subln': gain((N_EVEN, A_DV)),
        'b_q_norm': gain((N_EVEN, B_DH)),
        'b_k_norm': gain((N_EVEN, B_DH)),
        'b_rpb': nrm((N_EVEN, B_HEADS, 2 * NA_WIN_H - 1, 2 * NA_WIN_W - 1), 0.1),
        'od_w_in': nrm((N_ODD, D, ODD_IN), D ** -0.5),
        'od_w_out': nrm((N_ODD, ODD_MIX, D), ODD_MIX ** -0.5),
        'c_q_a_norm': gain((N_ODD, C_Q_LORA)),
        'c_w_qb': nrm((N_ODD, C_Q_LORA, C_HEADS * (C_NOPE + C_ROPE)), C_Q_LORA ** -0.5),
        'c_kv_a_norm': gain((N_ODD, C_KV_LORA)),
        'c_w_kvb': nrm((N_ODD, C_KV_LORA, C_HEADS * (C_NOPE + C_DV)), C_KV_LORA ** -0.5),
        'c_q_norm': gain((N_ODD, C_NOPE + C_ROPE)),
        'c_k_norm': gain((N_ODD, C_NOPE + C_ROPE)),
        'd_q_norm': gain((N_ODD, D_DH)),
        'd_k_norm': gain((N_ODD, D_DH)),
        'd_sink': nrm((N_ODD, D_HEADS), 0.5),
    }


def reference(x, c, ctx, c_ctx, ada_w, ada_b, norm_mix, norm_ffn, ffn_up, ffn_conv_w, ffn_conv_b, ffn_down,
              ev_w_in, ev_w_out, a_q_norm, a_k_norm, a_lam_q1, a_lam_k1, a_lam_q2, a_lam_k2, a_subln,
              b_q_norm, b_k_norm, b_rpb, od_w_in, od_w_out, c_q_a_norm, c_w_qb, c_kv_a_norm, c_w_kvb,
              c_q_norm, c_k_norm, d_q_norm, d_k_norm, d_sink):
    xc = ctx
    for l in range(DEPTH):
        last = l == DEPTH - 1
        mod = (jax.nn.silu(c) @ ada_w[l] + ada_b[l])[:, None, :]
        modc = (jax.nn.silu(c_ctx) @ ada_w[l] + ada_b[l])[None, None, :]
        sh1, sc1, g1, sh2, sc2, g2 = jnp.split(mod, 6, axis=-1)
        csh1, csc1, cg1, csh2, csc2, cg2 = jnp.split(modc, 6, axis=-1)
        h = modulate(rms_norm(x, norm_mix[l]), sh1, sc1)
        hc = modulate(rms_norm(xc, norm_mix[l]), csh1, csc1)
        i = l // 2
        if l % 2 == 0:
            lam_init = 0.8 - 0.6 * math.exp(-0.3 * l)
            y, yc = even_mixer(h, hc, ev_w_in[i], ev_w_out[i], a_q_norm[i], a_k_norm[i], a_lam_q1[i], a_lam_k1[i],
                               a_lam_q2[i], a_lam_k2[i], a_subln[i], b_q_norm[i], b_k_norm[i], b_rpb[i],
                               lam_init, not last)
        else:
            y, yc = odd_mixer(h, hc, od_w_in[i], od_w_out[i], c_q_a_norm[i], c_w_qb[i], c_kv_a_norm[i], c_w_kvb[i],
                              c_q_norm[i], c_k_norm[i], d_q_norm[i], d_k_norm[i], d_sink[i], not last)
        x = x + g1 * y
        x = x + g2 * conv_ffn(modulate(rms_norm(x, norm_ffn[l]), sh2, sc2), ffn_up[l], ffn_conv_w[l], ffn_conv_b[l], ffn_down[l])
        if not last:
            xc = xc + cg1 * yc
            xc = xc + cg2 * conv_ffn(modulate(rms_norm(xc, norm_ffn[l]), csh2, csc2), ffn_up[l], ffn_conv_w[l], ffn_conv_b[l], ffn_down[l])
    return x
```

```python
import functools
import math

import numpy as np
import jax
import jax.numpy as jnp
from jax import lax
from jax.experimental import pallas as pl
from jax.experimental.pallas import tpu as pltpu

D_MODEL = 1024
DEPTH = 2
GRID_W = 64
EPS = 1e-6
ROPE_BASE = 10000.0
NEG_INF = -1e30

A_HEADS = 4
A_DQK = 64
A_DV = 128
B_HEADS = 8
B_DH = 64
NA_WIN_H = 8
NA_WIN_W = 16
C_HEADS = 8
C_Q_LORA = 384
C_KV_LORA = 256
C_NOPE = 64
C_ROPE = 32
C_DV = 64
D_HEADS = 8
D_KV_HEADS = 2
D_DH = 64
D_WINDOW = 128
FFN_HIDDEN = 2816
CONV_W = 3

LANES = 128
HEAD_PAD = 128
VMEM_LIMIT = 56 * 1024 * 1024
FFN_CHUNK = 256
HALO = 16
NA_ROWS = 4

F32 = jnp.float32
BF16 = jnp.bfloat16


def _cparams(*sem):
    return pltpu.CompilerParams(dimension_semantics=sem, vmem_limit_bytes=VMEM_LIMIT)


def _full(shape):
    n = len(shape)
    return pl.BlockSpec(shape, lambda *_: (0,) * n)


def _dot(a, b):
    return jnp.dot(a, b, preferred_element_type=F32)


def _dot_nt(a, b):
    return lax.dot_general(a, b, (((1,), (1,)), ((), ())), preferred_element_type=F32)


def _rms(x):
    return x * lax.rsqrt(jnp.mean(x * x, axis=-1, keepdims=True) + EPS)


def _sigmoid(x):
    return 1.0 / (1.0 + jnp.exp(-x))


def _group_rms(y, gmat, inv_n, gain):
    ss = _dot((y * y).astype(BF16), gmat)
    return y * lax.rsqrt(ss * inv_n + EPS) * gain


def _rope(y, cos, sin, half, first_half):
    w = y.shape[-1]
    fwd = pltpu.roll(y, w - half, axis=1)
    bwd = pltpu.roll(y, half, axis=1)
    return y * cos + jnp.where(first_half, fwd, bwd) * sin


def _lane_tile(t, width):
    reps = width // t.shape[-1]
    return t if reps == 1 else jnp.concatenate([t] * reps, axis=-1)


def _lane_iota(width):
    return lax.broadcasted_iota(jnp.int32, (1, width), 1)


def _mod_kernel(ct_ref, w_ref, b_ref, o_ref):
    ct = ct_ref[...]
    s = ct * _sigmoid(ct)
    w = w_ref[0]
    rows = [jnp.sum(w * s[:, j:j + 1], axis=0, keepdims=True) for j in range(2)]
    o_ref[0] = jnp.concatenate(rows, axis=0) + b_ref[0]


def _modulation(c, c_ctx, ada_w, ada_b):
    depth, d, n = ada_w.shape
    tn = 1536
    ct = jnp.stack([c[0], c_ctx], axis=1)
    return pl.pallas_call(
        _mod_kernel,
        grid=(depth, n // tn),
        in_specs=[_full((d, 2)),
                  pl.BlockSpec((1, d, tn), lambda l, j: (l, 0, j)),
                  pl.BlockSpec((1, 1, tn), lambda l, j: (l, 0, j))],
        out_specs=pl.BlockSpec((1, 2, tn), lambda l, j: (l, 0, j)),
        out_shape=jax.ShapeDtypeStruct((depth, 2, n), F32),
        compiler_params=_cparams("arbitrary", "arbitrary"),
        name="modulation",
    )(ct, ada_w, ada_b.reshape(depth, 1, n))


def _premix_even_kernel(x_ref, vec_ref, w_ref, g_ref, hg_ref, cos_ref, sin_ref,
                        qa_ref, ka_ref, va_ref, qb_ref, kb_ref, vb_ref, *, use_rope):
    vec = vec_ref[...]
    h = (_rms(x_ref[...]) * vec[0:1] * (1.0 + vec[1:2]) + vec[2:3]).astype(BF16)
    gm = g_ref[...]
    hg = hg_ref[...]
    sw = 512
    if use_rope:
        cos = _lane_tile(cos_ref[...], sw)
        sin = _lane_tile(sin_ref[...], sw)
        first = (_lane_iota(sw) & 32) == 0

    def seg(i):
        return _dot(h, w_ref[:, i * sw:(i + 1) * sw])

    qa = _group_rms(seg(0), gm, 1.0 / A_DQK, hg[0:1])
    ka = _group_rms(seg(1), gm, 1.0 / A_DQK, hg[1:2])
    if use_rope:
        qa = _rope(qa, cos, sin, 32, first)
        ka = _rope(ka, cos, sin, 32, first)
    qa_ref[...] = (qa * (1.0 / math.sqrt(A_DQK))).astype(BF16)
    ka_ref[...] = ka.astype(BF16)
    va_ref[...] = seg(2).astype(BF16)
    qb = _group_rms(seg(3), gm, 1.0 / B_DH, hg[2:3])
    qb_ref[...] = (qb * (1.0 / math.sqrt(B_DH))).astype(BF16)
    kb_ref[...] = _group_rms(seg(4), gm, 1.0 / B_DH, hg[3:4]).astype(BF16)
    vb_ref[...] = seg(5).astype(BF16)


def _premix_even(x, vec, w, gmat, hg, cos, sin, use_rope):
    t, d = x.shape
    tm = min(512, t)
    row = lambda i: (i, 0)
    outs = [jax.ShapeDtypeStruct((t, 512), BF16)] * 6
    return pl.pallas_call(
        functools.partial(_premix_even_kernel, use_rope=use_rope),
        grid=(t // tm,),
        in_specs=[pl.BlockSpec((tm, d), row), _full(vec.shape), _full(w.shape), _full(gmat.shape),
                  _full(hg.shape), pl.BlockSpec((tm, LANES), row), pl.BlockSpec((tm, LANES), row)],
        out_specs=[pl.BlockSpec((tm, 512), row)] * 6,
        out_shape=outs,
        compiler_params=_cparams("parallel"),
        name="premix_even",
    )(x, vec, w, gmat, hg, cos, sin)


_O_CQ, _O_CKV, _O_DQ, _O_DK, _O_DV, _O_CKR, _O_END = 0, 384, 640, 1152, 1408, 1664, 1792


def _premix_odd_kernel(x_ref, vec_ref, w_ref, wqb_ref, wk_ref, wv_ref, g128_ref, g64_ref, gain_ref, lora_ref,
                       cos_ref, sin_ref, cosc_ref, sinc_ref,
                       mq_ref, mk_ref, mv_ref, gq_ref, gk_ref, gv_ref, *, use_rope):
    vec = vec_ref[...]
    h = (_rms(x_ref[...]) * vec[0:1] * (1.0 + vec[1:2]) + vec[2:3]).astype(BF16)
    gains = gain_ref[...]
    lora = lora_ref[...]
    g128 = g128_ref[...]
    g64 = g64_ref[...]
    mw = C_HEADS * HEAD_PAD
    if use_rope:
        cosc = _lane_tile(cosc_ref[...], mw)
        sinc = _lane_tile(sinc_ref[...], mw)
        first_c = (_lane_iota(mw) & 127) < (C_NOPE + C_ROPE // 2)
        cos = _lane_tile(cos_ref[...], 512)
        sin = _lane_tile(sin_ref[...], 512)
        first_d = (_lane_iota(512) & 32) == 0

    def seg(a, b):
        return _dot(h, w_ref[:, a:b])

    inv_c = 1.0 / (C_NOPE + C_ROPE)
    cq = (_rms(seg(_O_CQ, _O_CKV)) * lora[0:1, :C_Q_LORA]).astype(BF16)
    mq = _group_rms(_dot(cq, wqb_ref[...]), g128, inv_c, gains[0:1])
    if use_rope:
        mq = _rope(mq, cosc, sinc, C_ROPE // 2, first_c)
    mq_ref[...] = (mq * math.sqrt(inv_c)).astype(BF16)
    ckv = (_rms(seg(_O_CKV, _O_DQ)) * lora[1:2, :C_KV_LORA]).astype(BF16)
    mk = _dot(ckv, wk_ref[...]) + _lane_tile(seg(_O_CKR, _O_END), mw)
    mk = _group_rms(mk, g128, inv_c, gains[1:2])
    if use_rope:
        mk = _rope(mk, cosc, sinc, C_ROPE // 2, first_c)
    mk_ref[...] = mk.astype(BF16)
    mv_ref[...] = _dot(ckv, wv_ref[...]).astype(BF16)
    gq = _group_rms(seg(_O_DQ, _O_DK), g64, 1.0 / D_DH, gains[2:3, :512])
    gk = _group_rms(seg(_O_DK, _O_DV), g64[:256, :256], 1.0 / D_DH, gains[3:4, :256])
    if use_rope:
        gq = _rope(gq, cos, sin, 32, first_d)
        gk = _rope(gk, cos[:, :256], sin[:, :256], 32, first_d[:, :256])
    gq_ref[...] = (gq * (1.0 / math.sqrt(D_DH))).astype(BF16)
    gk_ref[...] = gk.astype(BF16)
    gv_ref[...] = seg(_O_DV, _O_CKR).astype(BF16)


def _premix_odd(x, vec, w, wqb, wk, wv, g128, g64, gains, lora, cos, sin, cosc, sinc, use_rope):
    t, d = x.shape
    tm = min(512, t)
    row = lambda i: (i, 0)
    tab = pl.BlockSpec((tm, LANES), row)
    widths = (1024, 1024, 512, 512, 256, 256)
    return pl.pallas_call(
        functools.partial(_premix_odd_kernel, use_rope=use_rope),
        grid=(t // tm,),
        in_specs=[pl.BlockSpec((tm, d), row)] + [_full(a.shape) for a in (vec, w, wqb, wk, wv, g128, g64, gains, lora)]
                 + [tab] * 4,
        out_specs=[pl.BlockSpec((tm, n), row) for n in widths],
        out_shape=[jax.ShapeDtypeStruct((t, n), BF16) for n in widths],
        compiler_params=_cparams("parallel"),
        name="premix_odd",
    )(x, vec, w, wqb, wk, wv, g128, g64, gains, lora, cos, sin, cosc, sinc)


def _flash_kernel(*refs, mode, has_ctx, tk, lam_init):
    if has_ctx:
        q_ref, k_ref, v_ref, kc_ref, vc_ref, ex_ref, o_ref, acc_ref, m_ref, l_ref = refs
    else:
        q_ref, k_ref, v_ref, ex_ref, o_ref, acc_ref, m_ref, l_ref = refs
    q = q_ref[...]
    lane = _lane_iota(LANES)
    if mode == "mla":
        qs = [q[:, :HEAD_PAD], q[:, HEAD_PAD:]]
    else:
        zero = jnp.zeros_like(q)
        qs = [jnp.where(lane < 64, q, zero), jnp.where(lane >= 64, q, zero)]

    m_ref[...] = jnp.full(m_ref.shape, NEG_INF, F32)
    l_ref[...] = jnp.zeros(l_ref.shape, F32)
    acc_ref[...] = jnp.zeros(acc_ref.shape, F32)

    def update(kblk, vblk):
        for b in range(2):
            kb = kblk[:, b * HEAD_PAD:(b + 1) * HEAD_PAD] if mode == "mla" else kblk
            s = _dot_nt(qs[b], kb)
            m_prev = m_ref[b]
            m_new = jnp.maximum(m_prev, jnp.max(s, axis=-1, keepdims=True))
            alpha = jnp.exp(m_prev - m_new)
            p = jnp.exp(s - m_new)
            l_ref[b] = alpha * l_ref[b] + jnp.sum(p, axis=-1, keepdims=True)
            acc_ref[b] = alpha * acc_ref[b] + _dot(p.astype(BF16), vblk)
            m_ref[b] = m_new

    if has_ctx:
        update(kc_ref[...], vc_ref[...])

    def body(j, carry):
        off = pl.multiple_of(j * tk, tk)
        update(k_ref[pl.ds(off, tk), :], v_ref[pl.ds(off, tk), :])
        return carry

    lax.fori_loop(0, k_ref.shape[0] // tk, body, 0)

    o0 = acc_ref[0] / l_ref[0]
    o1 = acc_ref[1] / l_ref[1]
    if mode == "diff":
        ex = ex_ref[...]
        lam = (jnp.exp(jnp.sum(ex[0:1] * ex[1:2], axis=-1, keepdims=True))
               - jnp.exp(jnp.sum(ex[2:3] * ex[3:4], axis=-1, keepdims=True)) + lam_init)
        o = _rms(o0 - lam * o1) * ex[4:5] * (1.0 - lam_init)
    else:
        o = jnp.where(lane < 64, o0, o1)
    o_ref[...] = o.astype(o_ref.dtype)


def _flash(q, k, v, kc, vc, ex, *, mode, lam_init=0.0):
    t = q.shape[0]
    tkeys = k.shape[0]
    qw = 2 * HEAD_PAD if mode == "mla" else HEAD_PAD
    groups = q.shape[1] // qw
    tq = min(512, t)
    tk = min(512, tkeys)
    has_ctx = kc is not None
    in_specs = [pl.BlockSpec((tq, qw), lambda g, i: (i, g)),
                pl.BlockSpec((tkeys, qw), lambda g, i: (0, g)),
                pl.BlockSpec((tkeys, HEAD_PAD), lambda g, i: (0, g))]
    args = [q, k, v]
    if has_ctx:
        lc = kc.shape[0]
        in_specs += [pl.BlockSpec((lc, qw), lambda g, i: (0, g)),
                     pl.BlockSpec((lc, HEAD_PAD), lambda g, i: (0, g))]
        args += [kc, vc]
    in_specs.append(_full(ex.shape))
    args.append(ex)
    return pl.pallas_call(
        functools.partial(_flash_kernel, mode=mode, has_ctx=has_ctx, tk=tk, lam_init=lam_init),
        grid=(groups, t // tq),
        in_specs=in_specs,
        out_specs=pl.BlockSpec((tq, HEAD_PAD), lambda g, i: (i, g)),
        out_shape=jax.ShapeDtypeStruct((t, groups * HEAD_PAD), BF16),
        scratch_shapes=[pltpu.VMEM((2, tq, HEAD_PAD), F32), pltpu.VMEM((2, tq, 1), F32), pltpu.VMEM((2, tq, 1), F32)],
        compiler_params=_cparams("parallel", "parallel"),
        name="flash_" + mode,
    )(*args)


def _na_kernel(q_ref, kp_ref, kcur_ref, kn_ref, vp_ref, vcur_ref, vn_ref, kc_ref, vc_ref, bias_ref, o_ref):
    q = q_ref[...]
    lane = _lane_iota(LANES)
    zero = jnp.zeros_like(q)
    kk = jnp.concatenate([kc_ref[...], kp_ref[...], kcur_ref[...], kn_ref[...]], axis=0)
    vv = jnp.concatenate([vc_ref[...], vp_ref[...], vcur_ref[...], vn_ref[...]], axis=0)
    outs = []
    for b in range(2):
        qb = jnp.where(lane < 64, q, zero) if b == 0 else jnp.where(lane >= 64, q, zero)
        s = _dot_nt(qb, kk) + bias_ref[0, b]
        m = jnp.max(s, axis=-1, keepdims=True)
        p = jnp.exp(s - m)
        l = jnp.sum(p, axis=-1, keepdims=True)
        outs.append(_dot(p.astype(BF16), vv) / l)
    o_ref[...] = jnp.where(lane < 64, outs[0], outs[1]).astype(o_ref.dtype)


def _na_bias_table(rpb, rows, lc):
    nr = NA_ROWS
    qn = nr * GRID_W
    rl = np.arange(qn) // GRID_W
    cq = np.arange(qn) % GRID_W
    rr_rel = np.arange(3 * qn) // GRID_W
    ck = np.arange(3 * qn) % GRID_W
    cs = np.clip(cq - NA_WIN_W // 2, 0, GRID_W - NA_WIN_W)
    valid_c = (ck[None, :] >= cs[:, None]) & (ck[None, :] < cs[:, None] + NA_WIN_W)
    dc = np.clip(ck[None, :] - cq[:, None] + (NA_WIN_W - 1), 0, 2 * NA_WIN_W - 2)
    wh = min(NA_WIN_H, rows)
    tabs = []
    for base in (0, nr, rows - nr):
        r = base + rl
        rs = np.clip(r - wh // 2, 0, rows - wh)
        rr = base - nr + rr_rel
        valid_r = (rr[None, :] >= rs[:, None]) & (rr[None, :] < rs[:, None] + wh)
        dr = np.clip(rr[None, :] - r[:, None] + (NA_WIN_H - 1), 0, 2 * NA_WIN_H - 2)
        b = rpb[:, dr, dc].astype(F32)
        tabs.append(jnp.where(jnp.asarray(valid_r & valid_c)[None], b, NEG_INF))
    tab = jnp.stack(tabs, axis=0)
    return jnp.concatenate([jnp.zeros(tab.shape[:3] + (lc,), F32), tab], axis=-1)


def _neighbourhood(q, k, v, kc, vc, bias):
    t = q.shape[0]
    qn = NA_ROWS * GRID_W
    nb = t // qn
    lc = kc.shape[0]
    pairs = q.shape[1] // HEAD_PAD
    cur = lambda p, i: (i, p)
    prev = lambda p, i: (jnp.maximum(i - 1, 0), p)
    nxt = lambda p, i: (jnp.minimum(i + 1, nb - 1), p)
    blk = lambda f: pl.BlockSpec((qn, HEAD_PAD), f)
    ctxs = pl.BlockSpec((lc, HEAD_PAD), lambda p, i: (0, p))
    case = lambda p, i: (jnp.where(i == 0, 0, jnp.where(i == nb - 1, 2, 1)), p, 0, 0)
    return pl.pallas_call(
        _na_kernel,
        grid=(pairs, nb),
        in_specs=[blk(cur), blk(prev), blk(cur), blk(nxt), blk(prev), blk(cur), blk(nxt), ctxs, ctxs,
                  pl.BlockSpec((1, 2, qn, lc + 3 * qn), case)],
        out_specs=blk(cur),
        out_shape=jax.ShapeDtypeStruct(q.shape, BF16),
        compiler_params=_cparams("parallel", "arbitrary"),
        name="neighbourhood",
    )(q, k, k, k, v, v, v, kc, vc, bias)


def _wgqa_kernel(sink_ref, q_ref, kp_ref, kcur_ref, kn_ref, vp_ref, vcur_ref, vn_ref, kc_ref, vc_ref, o_ref):
    i = pl.program_id(0)
    nb = pl.num_programs(0)
    qb = q_ref.shape[0]
    lc = kc_ref.shape[0]
    nkeys = lc + 3 * qb
    lane = _lane_iota(LANES)
    col = lax.broadcasted_iota(jnp.int32, (qb, nkeys), 1)
    row = lax.broadcasted_iota(jnp.int32, (qb, nkeys), 0)
    rel = col - (lc + qb) - row
    lo = jnp.where(i > 0, lc, lc + qb)
    hi = jnp.where(i < nb - 1, nkeys, lc + 2 * qb)
    valid = (col < lc) | ((jnp.abs(rel) <= D_WINDOW) & (col >= lo) & (col < hi))
    groups = D_HEADS // D_KV_HEADS
    for kv in range(D_KV_HEADS):
        ks = slice(kv * HEAD_PAD, (kv + 1) * HEAD_PAD)
        kk = jnp.concatenate([kc_ref[:, ks], kp_ref[:, ks], kcur_ref[:, ks], kn_ref[:, ks]], axis=0)
        vv = jnp.concatenate([vc_ref[:, ks], vp_ref[:, ks], vcur_ref[:, ks], vn_ref[:, ks]], axis=0)
        lhs = []
        for g in range(groups):
            h = kv * groups + g
            qp = q_ref[:, (h // 2) * HEAD_PAD:(h // 2 + 1) * HEAD_PAD]
            keep = (lane < 64) if h % 2 == 0 else (lane >= 64)
            lhs.append(jnp.where(keep, qp, jnp.zeros_like(qp)))
        s = _dot_nt(jnp.concatenate(lhs, axis=0), kk)
        outs = []
        for g in range(groups):
            sink = sink_ref[kv * groups + g]
            sg = jnp.where(valid, s[g * qb:(g + 1) * qb], NEG_INF)
            m = jnp.maximum(jnp.max(sg, axis=-1, keepdims=True), sink)
            p = jnp.exp(sg - m)
            l = jnp.sum(p, axis=-1, keepdims=True) + jnp.exp(sink - m)
            outs.append(_dot(p.astype(BF16), vv) / l)
        for pp in range(groups // 2):
            pair = kv * (groups // 2) + pp
            o_ref[:, pair * HEAD_PAD:(pair + 1) * HEAD_PAD] = jnp.where(
                lane < 64, outs[2 * pp], outs[2 * pp + 1]).astype(o_ref.dtype)


def _window_gqa(q, k, v, kc, vc, sink):
    t = q.shape[0]
    qb = D_WINDOW
    nb = t // qb
    lc = kc.shape[0]
    kw = k.shape[1]
    cur = lambda i: (i, 0)
    prev = lambda i: (jnp.maximum(i - 1, 0), 0)
    nxt = lambda i: (jnp.minimum(i + 1, nb - 1), 0)
    kblk = lambda f: pl.BlockSpec((qb, kw), f)
    return pl.pallas_call(
        _wgqa_kernel,
        grid=(nb,),
        in_specs=[pl.BlockSpec(memory_space=pltpu.SMEM), pl.BlockSpec((qb, q.shape[1]), cur),
                  kblk(prev), kblk(cur), kblk(nxt), kblk(prev), kblk(cur), kblk(nxt),
                  _full((lc, kw)), _full((lc, kw))],
        out_specs=pl.BlockSpec((qb, q.shape[1]), cur),
        out_shape=jax.ShapeDtypeStruct(q.shape, BF16),
        compiler_params=_cparams("parallel"),
        name="window_gqa",
    )(sink, q, k, k, k, v, v, v, kc, vc)


def _postmix_kernel(o1_ref, o2_ref, x_ref, w_ref, vec_ref, x1_ref, h2_ref):
    half = o1_ref.shape[1]
    y = _dot(o1_ref[...], w_ref[:half, :]) + _dot(o2_ref[...], w_ref[half:, :])
    vec = vec_ref[...]
    x1 = x_ref[...] + vec[0:1] * y
    x1_ref[...] = x1
    h2_ref[...] = (_rms(x1) * vec[1:2] * (1.0 + vec[2:3]) + vec[3:4]).astype(BF16)


def _postmix(o1, o2, x, w, vec):
    t, d = x.shape
    tm = min(512, t)
    row = lambda i: (i, 0)
    return pl.pallas_call(
        _postmix_kernel,
        grid=(t // tm,),
        in_specs=[pl.BlockSpec((tm, o1.shape[1]), row), pl.BlockSpec((tm, o2.shape[1]), row),
                  pl.BlockSpec((tm, d), row), _full(w.shape), _full(vec.shape)],
        out_specs=[pl.BlockSpec((tm, d), row)] * 2,
        out_shape=[jax.ShapeDtypeStruct((t, d), F32), jax.ShapeDtypeStruct((t, d), BF16)],
        compiler_params=_cparams("parallel"),
        name="postmix",
    )(o1, o2, x, w, vec)


def _ffn_kernel(h_ref, hp_ref, hn_ref, x1_ref, wup_ref, cv_ref, wdn_ref, g2_ref, o_ref, acc_ref, hcat_ref):
    i = pl.program_id(0)
    n = pl.num_programs(0)
    tm = h_ref.shape[0]
    rows = tm + 2 * HALO
    nchunk = wdn_ref.shape[0]
    hcat_ref[0:HALO] = hp_ref[...]
    hcat_ref[HALO:HALO + tm] = h_ref[...]
    hcat_ref[HALO + tm:rows] = hn_ref[...]

    @pl.when(i == 0)
    def _():
        hcat_ref[0:HALO] = jnp.zeros((HALO, hcat_ref.shape[1]), BF16)

    @pl.when(i == n - 1)
    def _():
        hcat_ref[HALO + tm:rows] = jnp.zeros((HALO, hcat_ref.shape[1]), BF16)

    acc_ref[...] = jnp.zeros(acc_ref.shape, F32)

    def conv(u, cv):
        before = pltpu.roll(u, 1, axis=0)
        after = pltpu.roll(u, rows - 1, axis=0)
        r = cv[3:4] + before * cv[0:1] + u * cv[1:2] + after * cv[2:3]
        return r[HALO:HALO + tm]

    def body(c, carry):
        hcat = hcat_ref[...]
        a = conv(_dot(hcat, wup_ref[c]), cv_ref[c])
        g = conv(_dot(hcat, wup_ref[nchunk + c]), cv_ref[nchunk + c])
        act = (a * (g * _sigmoid(g))).astype(BF16)
        acc_ref[...] += _dot(act, wdn_ref[c])
        return carry

    lax.fori_loop(0, nchunk, body, 0)
    o_ref[...] = x1_ref[...] + g2_ref[...] * acc_ref[...]


def _conv_ffn(h2, x1, wup, cv, wdn, g2):
    t, d = x1.shape
    tm = min(512, t)
    nt = t // tm
    hb = tm // HALO
    row = lambda i: (i, 0)
    prev = lambda i: (jnp.maximum(i * hb - 1, 0), 0)
    nxt = lambda i: (jnp.minimum((i + 1) * hb, t // HALO - 1), 0)
    return pl.pallas_call(
        _ffn_kernel,
        grid=(nt,),
        in_specs=[pl.BlockSpec((tm, d), row), pl.BlockSpec((HALO, d), prev), pl.BlockSpec((HALO, d), nxt),
                  pl.BlockSpec((tm, d), row), _full(wup.shape), _full(cv.shape), _full(wdn.shape), _full(g2.shape)],
        out_specs=pl.BlockSpec((tm, d), row),
        out_shape=jax.ShapeDtypeStruct((t, d), F32),
        scratch_shapes=[pltpu.VMEM((tm, d), F32), pltpu.VMEM((tm + 2 * HALO, d), BF16)],
        compiler_params=_cparams("parallel"),
        name="conv_ffn",
    )(h2, h2, h2, x1, wup, cv, wdn, g2)


def _axial_tables(n_tokens, dim):
    t = jnp.arange(n_tokens, dtype=jnp.int32)
    row = (t // GRID_W).astype(F32)
    col = (t % GRID_W).astype(F32)
    quarter = dim // 4
    inv_freq = ROPE_BASE ** (-jnp.arange(quarter, dtype=F32) / quarter)
    ang = jnp.concatenate([row[:, None] * inv_freq, col[:, None] * inv_freq], axis=-1)
    return jnp.cos(ang), jnp.sin(ang)


def _rope_tables_head64(t):
    cos, sin = _axial_tables(t, 64)
    return jnp.tile(jnp.concatenate([cos, cos], -1), (1, 2)), jnp.tile(jnp.concatenate([-sin, sin], -1), (1, 2))


def _rope_tables_latent(t):
    cos, sin = _axial_tables(t, C_ROPE)
    ones = jnp.ones((t, C_NOPE), F32)
    tail = jnp.ones((t, HEAD_PAD - C_NOPE - C_ROPE), F32)
    cosc = jnp.concatenate([ones, cos, cos, tail], -1)
    sinc = jnp.concatenate([0 * ones, -sin, sin, 0 * tail], -1)
    return cosc, sinc


def _group_ones(width, group):
    return jnp.asarray(np.kron(np.eye(width // group), np.ones((group, group))), BF16)


def _pad_heads(a, heads, dim):
    a = a.reshape(a.shape[:-1] + (heads, dim))
    a = jnp.pad(a, [(0, 0)] * (a.ndim - 1) + [(0, HEAD_PAD - dim)])
    return a.reshape(a.shape[:-2] + (heads * HEAD_PAD,))


def _pad_row(v, width):
    return jnp.pad(v, (0, width - v.shape[0]))


def _ffn_weights(w_up, conv_w, conv_b, w_down):
    d, f2 = w_up.shape
    nc2 = f2 // FFN_CHUNK
    wup = w_up.astype(BF16).reshape(d, nc2, FFN_CHUNK).transpose(1, 0, 2)
    cv = jnp.concatenate([conv_w, conv_b[None]], axis=0).reshape(CONV_W + 1, nc2, FFN_CHUNK).transpose(1, 0, 2)
    wdn = w_down.astype(BF16).reshape(nc2 // 2, FFN_CHUNK, d)
    return wup, cv, wdn


def kernel(x, c, ctx, c_ctx, ada_w, ada_b, norm_mix, norm_ffn, ffn_up, ffn_conv_w, ffn_conv_b, ffn_down, ev_w_in, ev_w_out, a_q_norm, a_k_norm, a_lam_q1, a_lam_k1, a_lam_q2, a_lam_k2, a_subln, b_q_norm, b_k_norm, b_rpb, od_w_in, od_w_out, c_q_a_norm, c_w_qb, c_kv_a_norm, c_w_kvb, c_q_norm, c_k_norm, d_q_norm, d_k_norm, d_sink):
    assert x.shape[0] == 1 and ctx.shape[0] == 1
    xm = x[0]
    xc = ctx[0]
    t, d = xm.shape
    lc = xc.shape[0]
    mods = _modulation(c, c_ctx, ada_w, ada_b)
    cos64, sin64 = _rope_tables_head64(t)
    cosc, sinc = _rope_tables_latent(t)
    g64 = _group_ones(512, 64)
    g128 = _group_ones(C_HEADS * HEAD_PAD, HEAD_PAD)
    dummy_tab = jnp.zeros((lc, LANES), F32)

    for l in range(DEPTH):
        last = l == DEPTH - 1
        i = l // 2
        mm = mods[l, 0].reshape(6, d)
        mc = mods[l, 1].reshape(6, d)
        pre_m = jnp.stack([norm_mix[l], mm[1], mm[0]])
        pre_c = jnp.stack([norm_mix[l], mc[1], mc[0]])
        post_m = jnp.stack([mm[2], norm_ffn[l], mm[4], mm[3]])
        post_c = jnp.stack([mc[2], norm_ffn[l], mc[4], mc[3]])
        wup, cv, wdn = _ffn_weights(ffn_up[l], ffn_conv_w[l], ffn_conv_b[l], ffn_down[l])
        if l % 2 == 0:
            lam_init = 0.8 - 0.6 * math.exp(-0.3 * l)
            w_in = ev_w_in[i].astype(BF16)
            w_out = ev_w_out[i].astype(BF16)
            hg = jnp.stack([jnp.tile(a_q_norm[i], 8), jnp.tile(a_k_norm[i], 8),
                            jnp.tile(b_q_norm[i], 8), jnp.tile(b_k_norm[i], 8)])
            ex = jnp.stack([_pad_row(a_lam_q1[i], LANES), _pad_row(a_lam_k1[i], LANES),
                            _pad_row(a_lam_q2[i], LANES), _pad_row(a_lam_k2[i], LANES), a_subln[i]])
            qa, ka, va, qb, kb, vb = _premix_even(xm, pre_m, w_in, g64, hg, cos64, sin64, True)
            qa_c, ka_c, va_c, qb_c, kb_c, vb_c = _premix_even(xc, pre_c, w_in, g64, hg, dummy_tab, dummy_tab, False)
            oa = _flash(qa, ka, va, ka_c, va_c, ex, mode="diff", lam_init=lam_init)
            ob = _neighbourhood(qb, kb, vb, kb_c, vb_c, _na_bias_table(b_rpb[i], t // GRID_W, lc))
            if not last:
                oa_c = _flash(qa_c, ka_c, va_c, None, None, ex, mode="diff", lam_init=lam_init)
                ob_c = _flash(qb_c, kb_c, vb_c, None, None, ex, mode="pair")
        else:
            w = od_w_in[i]
            cq, ckv, ckr, dq, dk, dv = jnp.split(w, np.cumsum([C_Q_LORA, C_KV_LORA, C_ROPE, 512, 128]).tolist(), axis=1)
            dup = lambda a: jnp.concatenate([a[:, :64], a[:, :64], a[:, 64:], a[:, 64:]], axis=1)
            ckr_pad = jnp.pad(ckr, ((0, 0), (C_NOPE, HEAD_PAD - C_NOPE - C_ROPE)))
            w_in = jnp.concatenate([cq, ckv, dq, dup(dk), dup(dv), ckr_pad], axis=1).astype(BF16)
            w_out = od_w_out[i].astype(BF16)
            wqb = _pad_heads(c_w_qb[i], C_HEADS, C_NOPE + C_ROPE).astype(BF16)
            kvb = c_w_kvb[i].reshape(C_KV_LORA, C_HEADS, C_NOPE + C_DV)
            wk = _pad_heads(kvb[:, :, :C_NOPE].reshape(C_KV_LORA, -1), C_HEADS, C_NOPE).astype(BF16)
            wv = kvb[:, :, C_NOPE:].reshape(C_KV_LORA, C_HEADS * C_DV).astype(BF16)
            gains = jnp.stack([jnp.tile(_pad_row(c_q_norm[i], HEAD_PAD), C_HEADS),
                               jnp.tile(_pad_row(c_k_norm[i], HEAD_PAD), C_HEADS),
                               _pad_row(jnp.tile(d_q_norm[i], 8), 1024), _pad_row(jnp.tile(d_k_norm[i], 4), 1024)])
            lora = jnp.stack([_pad_row(c_q_a_norm[i], 512), _pad_row(c_kv_a_norm[i], 512)])
            odd = functools.partial(_premix_odd, w=w_in, wqb=wqb, wk=wk, wv=wv, g128=g128, g64=g64, gains=gains, lora=lora)
            mq, mk, mv, gq, gk, gv = odd(xm, pre_m, cos=cos64, sin=sin64, cosc=cosc, sinc=sinc, use_rope=True)
            mq_c, mk_c, mv_c, gq_c, gk_c, gv_c = odd(xc, pre_c, cos=dummy_tab, sin=dummy_tab, cosc=dummy_tab,
                                                     sinc=dummy_tab, use_rope=False)
            ex = jnp.zeros((8, LANES), F32)
            oa = _flash(mq, mk, mv, mk_c, mv_c, ex, mode="mla")
            ob = _window_gqa(gq, gk, gv, gk_c, gv_c, d_sink[i])
            if not last:
                raise NotImplementedError("context update after an odd layer is not needed at this depth")
        x1, h2 = _postmix(oa, ob, xm, w_out, post_m)
        xm = _conv_ffn(h2, x1, wup, cv, wdn, mm[5][None])
        if not last:
            xc1, hc2 = _postmix(oa_c, ob_c, xc, w_out, post_c)
            xc = _conv_ffn(hc2, xc1, wup, cv, wdn, mc[5][None])
    return xm[None]
```

```python
import functools
import math

import numpy as np
import jax
import jax.numpy as jnp
from jax import lax
from jax.experimental import pallas as pl
from jax.experimental.pallas import tpu as pltpu

D_MODEL = 1024
DEPTH = 2
GRID_W = 64
EPS = 1e-6
ROPE_BASE = 10000.0
NEG_INF = -1e30

A_HEADS = 4
A_DQK = 64
A_DV = 128
B_HEADS = 8
B_DH = 64
NA_WIN_H = 8
NA_WIN_W = 16
C_HEADS = 8
C_Q_LORA = 384
C_KV_LORA = 256
C_NOPE = 64
C_ROPE = 32
C_DV = 64
D_HEADS = 8
D_KV_HEADS = 2
D_DH = 64
D_WINDOW = 128
FFN_HIDDEN = 2816
CONV_W = 3

LANES = 128
HEAD_PAD = 128
VMEM_LIMIT = 56 * 1024 * 1024
FFN_CHUNK = 256
HALO = 16
NA_ROWS = 4

F32 = jnp.float32
BF16 = jnp.bfloat16


def _cparams(*sem):
    return pltpu.CompilerParams(dimension_semantics=sem, vmem_limit_bytes=VMEM_LIMIT)


def _full(shape):
    n = len(shape)
    return pl.BlockSpec(shape, lambda *_: (0,) * n)


def _dot(a, b):
    return jnp.dot(a, b, preferred_element_type=F32)


def _dot_nt(a, b):
    return lax.dot_general(a, b, (((1,), (1,)), ((), ())), preferred_element_type=F32)


def _rms(x):
    return x * lax.rsqrt(jnp.mean(x * x, axis=-1, keepdims=True) + EPS)


def _sigmoid(x):
    return 1.0 / (1.0 + jnp.exp(-x))


def _group_rms(y, gmat, inv_n, gain):
    ss = _dot((y * y).astype(BF16), gmat)
    return y * lax.rsqrt(ss * inv_n + EPS) * gain


def _rope(y, cos, sin, half, first_half):
    w = y.shape[-1]
    fwd = pltpu.roll(y, w - half, axis=1)
    bwd = pltpu.roll(y, half, axis=1)
    return y * cos + jnp.where(first_half, fwd, bwd) * sin


def _lane_tile(t, width):
    reps = width // t.shape[-1]
    return t if reps == 1 else jnp.concatenate([t] * reps, axis=-1)


def _lane_iota(width):
    return lax.broadcasted_iota(jnp.int32, (1, width), 1)


def _mod_kernel(ct_ref, w_ref, b_ref, o_ref):
    ct = ct_ref[...]
    s = ct * _sigmoid(ct)
    w = w_ref[0]
    rows = [jnp.sum(w * s[:, j:j + 1], axis=0, keepdims=True) for j in range(2)]
    o_ref[0] = jnp.concatenate(rows, axis=0) + b_ref[0]


def _modulation(c, c_ctx, ada_w, ada_b):
    depth, d, n = ada_w.shape
    tn = 1536
    ct = jnp.stack([c[0], c_ctx], axis=1)
    return pl.pallas_call(
        _mod_kernel,
        grid=(depth, n // tn),
        in_specs=[_full((d, 2)),
                  pl.BlockSpec((1, d, tn), lambda l, j: (l, 0, j)),
                  pl.BlockSpec((1, 1, tn), lambda l, j: (l, 0, j))],
        out_specs=pl.BlockSpec((1, 2, tn), lambda l, j: (l, 0, j)),
        out_shape=jax.ShapeDtypeStruct((depth, 2, n), F32),
        compiler_params=_cparams("arbitrary", "arbitrary"),
        name="modulation",
    )(ct, ada_w, ada_b.reshape(depth, 1, n))


def _premix_even_kernel(x_ref, vec_ref, w_ref, g_ref, hg_ref, cos_ref, sin_ref,
                        qa_ref, ka_ref, va_ref, qb_ref, kb_ref, vb_ref, *, use_rope):
    vec = vec_ref[...]
    h = (_rms(x_ref[...]) * vec[0:1] * (1.0 + vec[1:2]) + vec[2:3]).astype(BF16)
    gm = g_ref[...]
    hg = hg_ref[...]
    sw = 512
    if use_rope:
        cos = _lane_tile(cos_ref[...], sw)
        sin = _lane_tile(sin_ref[...], sw)
        first = (_lane_iota(sw) & 32) == 0

    def seg(i):
        return _dot(h, w_ref[:, i * sw:(i + 1) * sw])

    qa = _group_rms(seg(0), gm, 1.0 / A_DQK, hg[0:1])
    ka = _group_rms(seg(1), gm, 1.0 / A_DQK, hg[1:2])
    if use_rope:
        qa = _rope(qa, cos, sin, 32, first)
        ka = _rope(ka, cos, sin, 32, first)
    qa_ref[...] = (qa * (1.0 / math.sqrt(A_DQK))).astype(BF16)
    ka_ref[...] = ka.astype(BF16)
    va_ref[...] = seg(2).astype(BF16)
    qb = _group_rms(seg(3), gm, 1.0 / B_DH, hg[2:3])
    qb_ref[...] = (qb * (1.0 / math.sqrt(B_DH))).astype(BF16)
    kb_ref[...] = _group_rms(seg(4), gm, 1.0 / B_DH, hg[3:4]).astype(BF16)
    vb_ref[...] = seg(5).astype(BF16)


def _premix_even(x, vec, w, gmat, hg, cos, sin, use_rope):
    t, d = x.shape
    tm = min(512, t)
    row = lambda i: (i, 0)
    outs = [jax.ShapeDtypeStruct((t, 512), BF16)] * 6
    return pl.pallas_call(
        functools.partial(_premix_even_kernel, use_rope=use_rope),
        grid=(t // tm,),
        in_specs=[pl.BlockSpec((tm, d), row), _full(vec.shape), _full(w.shape), _full(gmat.shape),
                  _full(hg.shape), pl.BlockSpec((tm, LANES), row), pl.BlockSpec((tm, LANES), row)],
        out_specs=[pl.BlockSpec((tm, 512), row)] * 6,
        out_shape=outs,
        compiler_params=_cparams("parallel"),
        name="premix_even",
    )(x, vec, w, gmat, hg, cos, sin)


_O_CQ, _O_CKV, _O_DQ, _O_DK, _O_DV, _O_CKR, _O_END = 0, 384, 640, 1152, 1408, 1664, 1792


def _premix_odd_kernel(x_ref, vec_ref, w_ref, wqb_ref, wk_ref, wv_ref, g128_ref, g64_ref, gain_ref, lora_ref,
                       cos_ref, sin_ref, cosc_ref, sinc_ref,
                       mq_ref, mk_ref, mv_ref, gq_ref, gk_ref, gv_ref, *, use_rope):
    vec = vec_ref[...]
    h = (_rms(x_ref[...]) * vec[0:1] * (1.0 + vec[1:2]) + vec[2:3]).astype(BF16)
    gains = gain_ref[...]
    lora = lora_ref[...]
    g128 = g128_ref[...]
    g64 = g64_ref[...]
    mw = C_HEADS * HEAD_PAD
    if use_rope:
        cosc = _lane_tile(cosc_ref[...], mw)
        sinc = _lane_tile(sinc_ref[...], mw)
        first_c = (_lane_iota(mw) & 127) < (C_NOPE + C_ROPE // 2)
        cos = _lane_tile(cos_ref[...], 512)
        sin = _lane_tile(sin_ref[...], 512)
        first_d = (_lane_iota(512) & 32) == 0

    def seg(a, b):
        return _dot(h, w_ref[:, a:b])

    inv_c = 1.0 / (C_NOPE + C_ROPE)
    cq = (_rms(seg(_O_CQ, _O_CKV)) * lora[0:1, :C_Q_LORA]).astype(BF16)
    mq = _group_rms(_dot(cq, wqb_ref[...]), g128, inv_c, gains[0:1])
    if use_rope:
        mq = _rope(mq, cosc, sinc, C_ROPE // 2, first_c)
    mq_ref[...] = (mq * math.sqrt(inv_c)).astype(BF16)
    ckv = (_rms(seg(_O_CKV, _O_DQ)) * lora[1:2, :C_KV_LORA]).astype(BF16)
    mk = _dot(ckv, wk_ref[...]) + _lane_tile(seg(_O_CKR, _O_END), mw)
    mk = _group_rms(mk, g128, inv_c, gains[1:2])
    if use_rope:
        mk = _rope(mk, cosc, sinc, C_ROPE // 2, first_c)
    mk_ref[...] = mk.astype(BF16)
    mv_ref[...] = _dot(ckv, wv_ref[...]).astype(BF16)
    gq = _group_rms(seg(_O_DQ, _O_DK), g64, 1.0 / D_DH, gains[2:3, :512])
    gk = _group_rms(seg(_O_DK, _O_DV), g64[:256, :256], 1.0 / D_DH, gains[3:4, :256])
    if use_rope:
        gq = _rope(gq, cos, sin, 32, first_d)
        gk = _rope(gk, cos[:, :256], sin[:, :256], 32, first_d[:, :256])
    gq_ref[...] = (gq * (1.0 / math.sqrt(D_DH))).astype(BF16)
    gk_ref[...] = gk.astype(BF16)
    gv_ref[...] = seg(_O_DV, _O_CKR).astype(BF16)


def _premix_odd(x, vec, w, wqb, wk, wv, g128, g64, gains, lora, cos, sin, cosc, sinc, use_rope):
    t, d = x.shape
    tm = min(512, t)
    row = lambda i: (i, 0)
    tab = pl.BlockSpec((tm, LANES), row)
    widths = (1024, 1024, 512, 512, 256, 256)
    return pl.pallas_call(
        functools.partial(_premix_odd_kernel, use_rope=use_rope),
        grid=(t // tm,),
        in_specs=[pl.BlockSpec((tm, d), row)] + [_full(a.shape) for a in (vec, w, wqb, wk, wv, g128, g64, gains, lora)]
                 + [tab] * 4,
        out_specs=[pl.BlockSpec((tm, n), row) for n in widths],
        out_shape=[jax.ShapeDtypeStruct((t, n), BF16) for n in widths],
        compiler_params=_cparams("parallel"),
        name="premix_odd",
    )(x, vec, w, wqb, wk, wv, g128, g64, gains, lora, cos, sin, cosc, sinc)


def _flash_kernel(*refs, mode, has_ctx, lam_init):
    if has_ctx:
        qt_ref, k_ref, vt_ref, kc_ref, vct_ref, ex_ref, gcol_ref, o_ref, acc_ref, m_ref, l_ref = refs
    else:
        qt_ref, k_ref, vt_ref, ex_ref, gcol_ref, o_ref, acc_ref, m_ref, l_ref = refs
    qt = qt_ref[...]
    row = lax.broadcasted_iota(jnp.int32, (HEAD_PAD, 1), 0)
    if mode == "mla":
        qs = [qt[:HEAD_PAD], qt[HEAD_PAD:]]
    else:
        zero = jnp.zeros_like(qt)
        qs = [jnp.where(row < 64, qt, zero), jnp.where(row >= 64, qt, zero)]

    m_ref[...] = jnp.full(m_ref.shape, NEG_INF, F32)
    l_ref[...] = jnp.zeros(l_ref.shape, F32)
    acc_ref[...] = jnp.zeros(acc_ref.shape, F32)

    def update(kblk, vtblk):
        for b in range(2):
            kb = kblk[:, b * HEAD_PAD:(b + 1) * HEAD_PAD] if mode == "mla" else kblk
            s = _dot(kb, qs[b])
            m_prev = m_ref[b]
            m_new = jnp.maximum(m_prev, jnp.max(s, axis=0, keepdims=True))
            alpha = jnp.exp(m_prev - m_new)
            p = jnp.exp(s - m_new)
            l_ref[b] = alpha * l_ref[b] + jnp.sum(p, axis=0, keepdims=True)
            acc_ref[b] = alpha * acc_ref[b] + _dot(vtblk, p.astype(BF16))
            m_ref[b] = m_new

    if has_ctx:
        update(kc_ref[...], vct_ref[0])

    tk = vt_ref.shape[3]

    def body(j, carry):
        off = pl.multiple_of(j * tk, tk)
        update(k_ref[pl.ds(off, tk), :], vt_ref[0, j])
        return carry

    lax.fori_loop(0, vt_ref.shape[1], body, 0)

    o0 = acc_ref[0] / l_ref[0]
    o1 = acc_ref[1] / l_ref[1]
    if mode == "diff":
        ex = ex_ref[...]
        lam = (jnp.exp(jnp.sum(ex[0:1] * ex[1:2], axis=-1, keepdims=True))
               - jnp.exp(jnp.sum(ex[2:3] * ex[3:4], axis=-1, keepdims=True)) + lam_init)
        o = o0 - lam * o1
        o = o * lax.rsqrt(jnp.mean(o * o, axis=0, keepdims=True) + EPS) * gcol_ref[...] * (1.0 - lam_init)
    else:
        o = jnp.where(row < 64, o0, o1)
    o_ref[...] = o.astype(o_ref.dtype)


def _flash(q, k, v, kc, vc, ex, gcol, *, mode, lam_init=0.0):
    t = q.shape[0]
    tkeys = k.shape[0]
    qw = 2 * HEAD_PAD if mode == "mla" else HEAD_PAD
    groups = q.shape[1] // qw
    tq = min(512, t)
    tk = min(512, tkeys)
    nk = tkeys // tk
    has_ctx = kc is not None
    qt = q.T
    vt = v.reshape(nk, tk, groups, HEAD_PAD).transpose(2, 0, 3, 1)
    in_specs = [pl.BlockSpec((qw, tq), lambda g, i: (g, i)),
                pl.BlockSpec((tkeys, qw), lambda g, i: (0, g)),
                pl.BlockSpec((1, nk, HEAD_PAD, tk), lambda g, i: (g, 0, 0, 0))]
    args = [qt, k, vt]
    if has_ctx:
        lc = kc.shape[0]
        in_specs += [pl.BlockSpec((lc, qw), lambda g, i: (0, g)),
                     pl.BlockSpec((1, HEAD_PAD, lc), lambda g, i: (g, 0, 0))]
        args += [kc, vc.reshape(lc, groups, HEAD_PAD).transpose(1, 2, 0)]
    in_specs += [_full(ex.shape), _full(gcol.shape)]
    args += [ex, gcol]
    ot = pl.pallas_call(
        functools.partial(_flash_kernel, mode=mode, has_ctx=has_ctx, lam_init=lam_init),
        grid=(groups, t // tq),
        in_specs=in_specs,
        out_specs=pl.BlockSpec((HEAD_PAD, tq), lambda g, i: (g, i)),
        out_shape=jax.ShapeDtypeStruct((groups * HEAD_PAD, t), BF16),
        scratch_shapes=[pltpu.VMEM((2, HEAD_PAD, tq), F32), pltpu.VMEM((2, 1, tq), F32), pltpu.VMEM((2, 1, tq), F32)],
        compiler_params=_cparams("parallel", "parallel"),
        name="flash_" + mode,
    )(*args)
    return ot.T


def _na_kernel(q_ref, kp_ref, kcur_ref, kn_ref, vp_ref, vcur_ref, vn_ref, kc_ref, vc_ref, bias_ref, o_ref):
    q = q_ref[...]
    lane = _lane_iota(LANES)
    zero = jnp.zeros_like(q)
    kk = jnp.concatenate([kc_ref[...], kp_ref[...], kcur_ref[...], kn_ref[...]], axis=0)
    vv = jnp.concatenate([vc_ref[...], vp_ref[...], vcur_ref[...], vn_ref[...]], axis=0)
    outs = []
    for b in range(2):
        qb = jnp.where(lane < 64, q, zero) if b == 0 else jnp.where(lane >= 64, q, zero)
        s = _dot_nt(qb, kk) + bias_ref[0, b]
        m = jnp.max(s, axis=-1, keepdims=True)
        p = jnp.exp(s - m)
        l = jnp.sum(p, axis=-1, keepdims=True)
        outs.append(_dot(p.astype(BF16), vv) / l)
    o_ref[...] = jnp.where(lane < 64, outs[0], outs[1]).astype(o_ref.dtype)


def _na_bias_table(rpb, rows, lc):
    nr, w = NA_ROWS, GRID_W
    heads = rpb.shape[0]
    pad = w - NA_WIN_W
    padded = jnp.pad(rpb.astype(F32), ((0, 0), (0, 0), (pad, pad)))
    toep = jnp.stack([padded[:, :, w - 1 - c:2 * w - 1 - c] for c in range(w)], axis=2)
    cq = np.arange(w)
    cs = np.clip(cq - NA_WIN_W // 2, 0, w - NA_WIN_W)
    valid_c = (cq[None, :] >= cs[:, None]) & (cq[None, :] < cs[:, None] + NA_WIN_W)
    toep = jnp.where(jnp.asarray(valid_c), toep, NEG_INF)
    wh = min(NA_WIN_H, rows)
    rl = np.arange(nr)
    rr_rel = np.arange(3 * nr)
    dr_all, valid_all = [], []
    for base in (0, nr, rows - nr):
        r = base + rl
        rs = np.clip(r - wh // 2, 0, rows - wh)
        rr = base - nr + rr_rel
        valid_all.append((rr[None, :] >= rs[:, None]) & (rr[None, :] < rs[:, None] + wh))
        dr_all.append(np.clip(rr[None, :] - r[:, None] + (NA_WIN_H - 1), 0, 2 * NA_WIN_H - 2))
    dr_idx = np.stack(dr_all).reshape(-1)
    valid_r = np.stack(valid_all)
    blocks = jnp.take(toep, jnp.asarray(dr_idx, jnp.int32), axis=1).reshape(heads, 3, nr, 3 * nr, w, w)
    blocks = jnp.where(jnp.asarray(valid_r)[None, :, :, :, None, None], blocks, NEG_INF)
    tab = blocks.transpose(1, 0, 2, 4, 3, 5).reshape(3, heads, nr * w, 3 * nr * w)
    return jnp.concatenate([jnp.zeros(tab.shape[:3] + (lc,), F32), tab], axis=-1)


def _neighbourhood(q, k, v, kc, vc, bias):
    t = q.shape[0]
    qn = NA_ROWS * GRID_W
    nb = t // qn
    lc = kc.shape[0]
    pairs = q.shape[1] // HEAD_PAD
    cur = lambda p, i: (i, p)
    prev = lambda p, i: (jnp.maximum(i - 1, 0), p)
    nxt = lambda p, i: (jnp.minimum(i + 1, nb - 1), p)
    blk = lambda f: pl.BlockSpec((qn, HEAD_PAD), f)
    ctxs = pl.BlockSpec((lc, HEAD_PAD), lambda p, i: (0, p))
    case = lambda p, i: (jnp.where(i == 0, 0, jnp.where(i == nb - 1, 2, 1)), p, 0, 0)
    return pl.pallas_call(
        _na_kernel,
        grid=(pairs, nb),
        in_specs=[blk(cur), blk(prev), blk(cur), blk(nxt), blk(prev), blk(cur), blk(nxt), ctxs, ctxs,
                  pl.BlockSpec((1, 2, qn, lc + 3 * qn), case)],
        out_specs=blk(cur),
        out_shape=jax.ShapeDtypeStruct(q.shape, BF16),
        compiler_params=_cparams("parallel", "arbitrary"),
        name="neighbourhood",
    )(q, k, k, k, v, v, v, kc, vc, bias)


def _wgqa_kernel(sink_ref, q_ref, kp_ref, kcur_ref, kn_ref, vp_ref, vcur_ref, vn_ref, kc_ref, vc_ref, o_ref):
    i = pl.program_id(0)
    nb = pl.num_programs(0)
    qb = q_ref.shape[0]
    lc = kc_ref.shape[0]
    nkeys = lc + 3 * qb
    lane = _lane_iota(LANES)
    col = lax.broadcasted_iota(jnp.int32, (qb, nkeys), 1)
    row = lax.broadcasted_iota(jnp.int32, (qb, nkeys), 0)
    rel = col - (lc + qb) - row
    lo = jnp.where(i > 0, lc, lc + qb)
    hi = jnp.where(i < nb - 1, nkeys, lc + 2 * qb)
    valid = (col < lc) | ((jnp.abs(rel) <= D_WINDOW) & (col >= lo) & (col < hi))
    groups = D_HEADS // D_KV_HEADS
    for kv in range(D_KV_HEADS):
        ks = slice(kv * HEAD_PAD, (kv + 1) * HEAD_PAD)
        kk = jnp.concatenate([kc_ref[:, ks], kp_ref[:, ks], kcur_ref[:, ks], kn_ref[:, ks]], axis=0)
        vv = jnp.concatenate([vc_ref[:, ks], vp_ref[:, ks], vcur_ref[:, ks], vn_ref[:, ks]], axis=0)
        lhs = []
        for g in range(groups):
            h = kv * groups + g
            qp = q_ref[:, (h // 2) * HEAD_PAD:(h // 2 + 1) * HEAD_PAD]
            keep = (lane < 64) if h % 2 == 0 else (lane >= 64)
            lhs.append(jnp.where(keep, qp, jnp.zeros_like(qp)))
        s = _dot_nt(jnp.concatenate(lhs, axis=0), kk)
        outs = []
        for g in range(groups):
            sink = sink_ref[kv * groups + g]
            sg = jnp.where(valid, s[g * qb:(g + 1) * qb], NEG_INF)
            m = jnp.maximum(jnp.max(sg, axis=-1, keepdims=True), sink)
            p = jnp.exp(sg - m)
            l = jnp.sum(p, axis=-1, keepdims=True) + jnp.exp(sink - m)
            outs.append(_dot(p.astype(BF16), vv) / l)
        for pp in range(groups // 2):
            pair = kv * (groups // 2) + pp
            o_ref[:, pair * HEAD_PAD:(pair + 1) * HEAD_PAD] = jnp.where(
                lane < 64, outs[2 * pp], outs[2 * pp + 1]).astype(o_ref.dtype)


def _window_gqa(q, k, v, kc, vc, sink):
    t = q.shape[0]
    qb = D_WINDOW
    nb = t // qb
    lc = kc.shape[0]
    kw = k.shape[1]
    cur = lambda i: (i, 0)
    prev = lambda i: (jnp.maximum(i - 1, 0), 0)
    nxt = lambda i: (jnp.minimum(i + 1, nb - 1), 0)
    kblk = lambda f: pl.BlockSpec((qb, kw), f)
    return pl.pallas_call(
        _wgqa_kernel,
        grid=(nb,),
        in_specs=[pl.BlockSpec(memory_space=pltpu.SMEM), pl.BlockSpec((qb, q.shape[1]), cur),
                  kblk(prev), kblk(cur), kblk(nxt), kblk(prev), kblk(cur), kblk(nxt),
                  _full((lc, kw)), _full((lc, kw))],
        out_specs=pl.BlockSpec((qb, q.shape[1]), cur),
        out_shape=jax.ShapeDtypeStruct(q.shape, BF16),
        compiler_params=_cparams("parallel"),
        name="window_gqa",
    )(sink, q, k, k, k, v, v, v, kc, vc)


def _postmix_kernel(o1_ref, o2_ref, x_ref, w_ref, vec_ref, x1_ref, h2_ref):
    half = o1_ref.shape[1]
    y = _dot(o1_ref[...], w_ref[:half, :]) + _dot(o2_ref[...], w_ref[half:, :])
    vec = vec_ref[...]
    x1 = x_ref[...] + vec[0:1] * y
    x1_ref[...] = x1
    h2_ref[...] = (_rms(x1) * vec[1:2] * (1.0 + vec[2:3]) + vec[3:4]).astype(BF16)


def _postmix(o1, o2, x, w, vec):
    t, d = x.shape
    tm = min(512, t)
    row = lambda i: (i, 0)
    return pl.pallas_call(
        _postmix_kernel,
        grid=(t // tm,),
        in_specs=[pl.BlockSpec((tm, o1.shape[1]), row), pl.BlockSpec((tm, o2.shape[1]), row),
                  pl.BlockSpec((tm, d), row), _full(w.shape), _full(vec.shape)],
        out_specs=[pl.BlockSpec((tm, d), row)] * 2,
        out_shape=[jax.ShapeDtypeStruct((t, d), F32), jax.ShapeDtypeStruct((t, d), BF16)],
        compiler_params=_cparams("parallel"),
        name="postmix",
    )(o1, o2, x, w, vec)


def _ffn_kernel(h_ref, hp_ref, hn_ref, x1_ref, wup_ref, cv_ref, wdn_ref, g2_ref, o_ref, acc_ref, hcat_ref):
    i = pl.program_id(0)
    n = pl.num_programs(0)
    tm = h_ref.shape[0]
    rows = tm + 2 * HALO
    nchunk = wdn_ref.shape[0]
    hcat_ref[0:HALO] = hp_ref[...]
    hcat_ref[HALO:HALO + tm] = h_ref[...]
    hcat_ref[HALO + tm:rows] = hn_ref[...]

    @pl.when(i == 0)
    def _():
        hcat_ref[0:HALO] = jnp.zeros((HALO, hcat_ref.shape[1]), BF16)

    @pl.when(i == n - 1)
    def _():
        hcat_ref[HALO + tm:rows] = jnp.zeros((HALO, hcat_ref.shape[1]), BF16)

    acc_ref[...] = jnp.zeros(acc_ref.shape, F32)

    def conv(u, cv):
        before = pltpu.roll(u, 1, axis=0)
        after = pltpu.roll(u, rows - 1, axis=0)
        r = cv[3:4] + before * cv[0:1] + u * cv[1:2] + after * cv[2:3]
        return r[HALO:HALO + tm]

    def body(c, carry):
        hcat = hcat_ref[...]
        a = conv(_dot(hcat, wup_ref[c]), cv_ref[c])
        g = conv(_dot(hcat, wup_ref[nchunk + c]), cv_ref[nchunk + c])
        act = (a * (g * _sigmoid(g))).astype(BF16)
        acc_ref[...] += _dot(act, wdn_ref[c])
        return carry

    lax.fori_loop(0, nchunk, body, 0)
    o_ref[...] = x1_ref[...] + g2_ref[...] * acc_ref[...]


def _conv_ffn(h2, x1, wup, cv, wdn, g2):
    t, d = x1.shape
    tm = min(512, t)
    nt = t // tm
    hb = tm // HALO
    row = lambda i: (i, 0)
    prev = lambda i: (jnp.maximum(i * hb - 1, 0), 0)
    nxt = lambda i: (jnp.minimum((i + 1) * hb, t // HALO - 1), 0)
    return pl.pallas_call(
        _ffn_kernel,
        grid=(nt,),
        in_specs=[pl.BlockSpec((tm, d), row), pl.BlockSpec((HALO, d), prev), pl.BlockSpec((HALO, d), nxt),
                  pl.BlockSpec((tm, d), row), _full(wup.shape), _full(cv.shape), _full(wdn.shape), _full(g2.shape)],
        out_specs=pl.BlockSpec((tm, d), row),
        out_shape=jax.ShapeDtypeStruct((t, d), F32),
        scratch_shapes=[pltpu.VMEM((tm, d), F32), pltpu.VMEM((tm + 2 * HALO, d), BF16)],
        compiler_params=_cparams("parallel"),
        name="conv_ffn",
    )(h2, h2, h2, x1, wup, cv, wdn, g2)


def _axial_tables(n_tokens, dim):
    t = jnp.arange(n_tokens, dtype=jnp.int32)
    row = (t // GRID_W).astype(F32)
    col = (t % GRID_W).astype(F32)
    quarter = dim // 4
    inv_freq = ROPE_BASE ** (-jnp.arange(quarter, dtype=F32) / quarter)
    ang = jnp.concatenate([row[:, None] * inv_freq, col[:, None] * inv_freq], axis=-1)
    return jnp.cos(ang), jnp.sin(ang)


def _rope_tables_head64(t):
    cos, sin = _axial_tables(t, 64)
    return jnp.tile(jnp.concatenate([cos, cos], -1), (1, 2)), jnp.tile(jnp.concatenate([-sin, sin], -1), (1, 2))


def _rope_tables_latent(t):
    cos, sin = _axial_tables(t, C_ROPE)
    ones = jnp.ones((t, C_NOPE), F32)
    tail = jnp.ones((t, HEAD_PAD - C_NOPE - C_ROPE), F32)
    cosc = jnp.concatenate([ones, cos, cos, tail], -1)
    sinc = jnp.concatenate([0 * ones, -sin, sin, 0 * tail], -1)
    return cosc, sinc


def _group_ones(width, group):
    return jnp.asarray(np.kron(np.eye(width // group), np.ones((group, group))), BF16)


def _pad_heads(a, heads, dim):
    a = a.reshape(a.shape[:-1] + (heads, dim))
    a = jnp.pad(a, [(0, 0)] * (a.ndim - 1) + [(0, HEAD_PAD - dim)])
    return a.reshape(a.shape[:-2] + (heads * HEAD_PAD,))


def _pad_row(v, width):
    return jnp.pad(v, (0, width - v.shape[0]))


def _ffn_weights(w_up, conv_w, conv_b, w_down):
    d, f2 = w_up.shape
    nc2 = f2 // FFN_CHUNK
    wup = w_up.astype(BF16).reshape(d, nc2, FFN_CHUNK).transpose(1, 0, 2)
    cv = jnp.concatenate([conv_w, conv_b[None]], axis=0).reshape(CONV_W + 1, nc2, FFN_CHUNK).transpose(1, 0, 2)
    wdn = w_down.astype(BF16).reshape(nc2 // 2, FFN_CHUNK, d)
    return wup, cv, wdn


def kernel(x, c, ctx, c_ctx, ada_w, ada_b, norm_mix, norm_ffn, ffn_up, ffn_conv_w, ffn_conv_b, ffn_down, ev_w_in, ev_w_out, a_q_norm, a_k_norm, a_lam_q1, a_lam_k1, a_lam_q2, a_lam_k2, a_subln, b_q_norm, b_k_norm, b_rpb, od_w_in, od_w_out, c_q_a_norm, c_w_qb, c_kv_a_norm, c_w_kvb, c_q_norm, c_k_norm, d_q_norm, d_k_norm, d_sink):
    assert x.shape[0] == 1 and ctx.shape[0] == 1
    xm = x[0]
    xc = ctx[0]
    t, d = xm.shape
    lc = xc.shape[0]
    mods = _modulation(c, c_ctx, ada_w, ada_b)
    cos64, sin64 = _rope_tables_head64(t)
    cosc, sinc = _rope_tables_latent(t)
    g64 = _group_ones(512, 64)
    g128 = _group_ones(C_HEADS * HEAD_PAD, HEAD_PAD)
    dummy_tab = jnp.zeros((lc, LANES), F32)

    for l in range(DEPTH):
        last = l == DEPTH - 1
        i = l // 2
        mm = mods[l, 0].reshape(6, d)
        mc = mods[l, 1].reshape(6, d)
        pre_m = jnp.stack([norm_mix[l], mm[1], mm[0]])
        pre_c = jnp.stack([norm_mix[l], mc[1], mc[0]])
        post_m = jnp.stack([mm[2], norm_ffn[l], mm[4], mm[3]])
        post_c = jnp.stack([mc[2], norm_ffn[l], mc[4], mc[3]])
        wup, cv, wdn = _ffn_weights(ffn_up[l], ffn_conv_w[l], ffn_conv_b[l], ffn_down[l])
        if l % 2 == 0:
            lam_init = 0.8 - 0.6 * math.exp(-0.3 * l)
            w_in = ev_w_in[i].astype(BF16)
            w_out = ev_w_out[i].astype(BF16)
            hg = jnp.stack([jnp.tile(a_q_norm[i], 8), jnp.tile(a_k_norm[i], 8),
                            jnp.tile(b_q_norm[i], 8), jnp.tile(b_k_norm[i], 8)])
            ex = jnp.stack([_pad_row(a_lam_q1[i], LANES), _pad_row(a_lam_k1[i], LANES),
                            _pad_row(a_lam_q2[i], LANES), _pad_row(a_lam_k2[i], LANES)])
            gcol = a_subln[i][:, None]
            qa, ka, va, qb, kb, vb = _premix_even(xm, pre_m, w_in, g64, hg, cos64, sin64, True)
            qa_c, ka_c, va_c, qb_c, kb_c, vb_c = _premix_even(xc, pre_c, w_in, g64, hg, dummy_tab, dummy_tab, False)
            oa = _flash(qa, ka, va, ka_c, va_c, ex, gcol, mode="diff", lam_init=lam_init)
            ob = _neighbourhood(qb, kb, vb, kb_c, vb_c, _na_bias_table(b_rpb[i], t // GRID_W, lc))
            if not last:
                oa_c = _flash(qa_c, ka_c, va_c, None, None, ex, gcol, mode="diff", lam_init=lam_init)
                ob_c = _flash(qb_c, kb_c, vb_c, None, None, ex, gcol, mode="pair")
        else:
            w = od_w_in[i]
            cq, ckv, ckr, dq, dk, dv = jnp.split(w, np.cumsum([C_Q_LORA, C_KV_LORA, C_ROPE, 512, 128]).tolist(), axis=1)
            dup = lambda a: jnp.concatenate([a[:, :64], a[:, :64], a[:, 64:], a[:, 64:]], axis=1)
            ckr_pad = jnp.pad(ckr, ((0, 0), (C_NOPE, HEAD_PAD - C_NOPE - C_ROPE)))
            w_in = jnp.concatenate([cq, ckv, dq, dup(dk), dup(dv), ckr_pad], axis=1).astype(BF16)
            w_out = od_w_out[i].astype(BF16)
            wqb = _pad_heads(c_w_qb[i], C_HEADS, C_NOPE + C_ROPE).astype(BF16)
            kvb = c_w_kvb[i].reshape(C_KV_LORA, C_HEADS, C_NOPE + C_DV)
            wk = _pad_heads(kvb[:, :, :C_NOPE].reshape(C_KV_LORA, -1), C_HEADS, C_NOPE).astype(BF16)
            wv = kvb[:, :, C_NOPE:].reshape(C_KV_LORA, C_HEADS * C_DV).astype(BF16)
            gains = jnp.stack([jnp.tile(_pad_row(c_q_norm[i], HEAD_PAD), C_HEADS),
                               jnp.tile(_pad_row(c_k_norm[i], HEAD_PAD), C_HEADS),
                               _pad_row(jnp.tile(d_q_norm[i], 8), 1024), _pad_row(jnp.tile(d_k_norm[i], 4), 1024)])
            lora = jnp.stack([_pad_row(c_q_a_norm[i], 512), _pad_row(c_kv_a_norm[i], 512)])
            odd = functools.partial(_premix_odd, w=w_in, wqb=wqb, wk=wk, wv=wv, g128=g128, g64=g64, gains=gains, lora=lora)
            mq, mk, mv, gq, gk, gv = odd(xm, pre_m, cos=cos64, sin=sin64, cosc=cosc, sinc=sinc, use_rope=True)
            mq_c, mk_c, mv_c, gq_c, gk_c, gv_c = odd(xc, pre_c, cos=dummy_tab, sin=dummy_tab, cosc=dummy_tab,
                                                     sinc=dummy_tab, use_rope=False)
            ex = jnp.zeros((4, LANES), F32)
            oa = _flash(mq, mk, mv, mk_c, mv_c, ex, jnp.ones((HEAD_PAD, 1), F32), mode="mla")
            ob = _window_gqa(gq, gk, gv, gk_c, gv_c, d_sink[i])
            if not last:
                raise NotImplementedError("context update after an odd layer is not needed at this depth")
        x1, h2 = _postmix(oa, ob, xm, w_out, post_m)
        xm = _conv_ffn(h2, x1, wup, cv, wdn, mm[5][None])
        if not last:
            xc1, hc2 = _postmix(oa_c, ob_c, xc, w_out, post_c)
            xc = _conv_ffn(hc2, xc1, wup, cv, wdn, mc[5][None])
    return xm[None]
```

```python
import functools
import math

import numpy as np
import jax
import jax.numpy as jnp
from jax import lax
from jax.experimental import pallas as pl
from jax.experimental.pallas import tpu as pltpu

D_MODEL = 1024
DEPTH = 2
GRID_W = 64
EPS = 1e-6
ROPE_BASE = 10000.0
NEG_INF = -1e30
LOG2E = math.log2(math.e)

A_HEADS = 4
A_DQK = 64
A_DV = 128
B_HEADS = 8
B_DH = 64
NA_WIN_H = 8
NA_WIN_W = 16
C_HEADS = 8
C_Q_LORA = 384
C_KV_LORA = 256
C_NOPE = 64
C_ROPE = 32
C_DV = 64
D_HEADS = 8
D_KV_HEADS = 2
D_DH = 64
D_WINDOW = 128
FFN_HIDDEN = 2816
CONV_W = 3

LANES = 128
HEAD_PAD = 128
VMEM_LIMIT = 56 * 1024 * 1024
FFN_CHUNK = 256
HALO = 16
NA_ROWS = 4

F32 = jnp.float32
BF16 = jnp.bfloat16


def _cparams(*sem):
    return pltpu.CompilerParams(dimension_semantics=sem, vmem_limit_bytes=VMEM_LIMIT)


def _full(shape):
    n = len(shape)
    return pl.BlockSpec(shape, lambda *_: (0,) * n)


def _dot(a, b):
    return jnp.dot(a, b, preferred_element_type=F32)


def _dot_nt(a, b):
    return lax.dot_general(a, b, (((1,), (1,)), ((), ())), preferred_element_type=F32)


def _rms(x):
    return x * lax.rsqrt(jnp.mean(x * x, axis=-1, keepdims=True) + EPS)


def _sigmoid(x):
    return 1.0 / (1.0 + jnp.exp(-x))


def _group_rms(y, gmat, inv_n, gain):
    ss = _dot((y * y).astype(BF16), gmat)
    return y * lax.rsqrt(ss * inv_n + EPS) * gain


def _rope(y, cos, sin, half, first_half):
    w = y.shape[-1]
    fwd = pltpu.roll(y, w - half, axis=1)
    bwd = pltpu.roll(y, half, axis=1)
    return y * cos + jnp.where(first_half, fwd, bwd) * sin


def _lane_tile(t, width):
    reps = width // t.shape[-1]
    return t if reps == 1 else jnp.concatenate([t] * reps, axis=-1)


def _lane_iota(width):
    return lax.broadcasted_iota(jnp.int32, (1, width), 1)


def _mod_kernel(ct_ref, w_ref, b_ref, o_ref):
    ct = ct_ref[...]
    s = ct * _sigmoid(ct)
    w = w_ref[0]
    rows = [jnp.sum(w * s[:, j:j + 1], axis=0, keepdims=True) for j in range(2)]
    o_ref[0] = jnp.concatenate(rows, axis=0) + b_ref[0]


def _modulation(c, c_ctx, ada_w, ada_b):
    depth, d, n = ada_w.shape
    tn = 1536
    ct = jnp.stack([c[0], c_ctx], axis=1)
    return pl.pallas_call(
        _mod_kernel,
        grid=(depth, n // tn),
        in_specs=[_full((d, 2)),
                  pl.BlockSpec((1, d, tn), lambda l, j: (l, 0, j)),
                  pl.BlockSpec((1, 1, tn), lambda l, j: (l, 0, j))],
        out_specs=pl.BlockSpec((1, 2, tn), lambda l, j: (l, 0, j)),
        out_shape=jax.ShapeDtypeStruct((depth, 2, n), F32),
        compiler_params=_cparams("arbitrary", "arbitrary"),
        name="modulation",
    )(ct, ada_w, ada_b.reshape(depth, 1, n))


def _premix_even_kernel(x_ref, vec_ref, w_ref, g_ref, hg_ref, cos_ref, sin_ref,
                        qa_ref, ka_ref, va_ref, qb_ref, kb_ref, vb_ref, *, use_rope):
    vec = vec_ref[...]
    h = (_rms(x_ref[...]) * vec[0:1] * (1.0 + vec[1:2]) + vec[2:3]).astype(BF16)
    gm = g_ref[...]
    hg = hg_ref[...]
    sw = 512
    if use_rope:
        cos = _lane_tile(cos_ref[...], sw)
        sin = _lane_tile(sin_ref[...], sw)
        first = (_lane_iota(sw) & 32) == 0

    def seg(i):
        return _dot(h, w_ref[:, i * sw:(i + 1) * sw])

    qa = _group_rms(seg(0), gm, 1.0 / A_DQK, hg[0:1])
    ka = _group_rms(seg(1), gm, 1.0 / A_DQK, hg[1:2])
    if use_rope:
        qa = _rope(qa, cos, sin, 32, first)
        ka = _rope(ka, cos, sin, 32, first)
    qa_ref[...] = (qa * (LOG2E / math.sqrt(A_DQK))).astype(BF16)
    ka_ref[...] = ka.astype(BF16)
    va_ref[...] = seg(2).astype(BF16)
    qb = _group_rms(seg(3), gm, 1.0 / B_DH, hg[2:3])
    qb_ref[...] = (qb * (LOG2E / math.sqrt(B_DH))).astype(BF16)
    kb_ref[...] = _group_rms(seg(4), gm, 1.0 / B_DH, hg[3:4]).astype(BF16)
    vb_ref[...] = seg(5).astype(BF16)


def _premix_even(x, vec, w, gmat, hg, cos, sin, use_rope):
    t, d = x.shape
    tm = min(512, t)
    row = lambda i: (i, 0)
    outs = [jax.ShapeDtypeStruct((t, 512), BF16)] * 6
    return pl.pallas_call(
        functools.partial(_premix_even_kernel, use_rope=use_rope),
        grid=(t // tm,),
        in_specs=[pl.BlockSpec((tm, d), row), _full(vec.shape), _full(w.shape), _full(gmat.shape),
                  _full(hg.shape), pl.BlockSpec((tm, LANES), row), pl.BlockSpec((tm, LANES), row)],
        out_specs=[pl.BlockSpec((tm, 512), row)] * 6,
        out_shape=outs,
        compiler_params=_cparams("parallel"),
        name="premix_even",
    )(x, vec, w, gmat, hg, cos, sin)


_O_CQ, _O_CKV, _O_DQ, _O_DK, _O_DV, _O_CKR, _O_END = 0, 384, 640, 1152, 1408, 1664, 1792


def _premix_odd_kernel(x_ref, vec_ref, w_ref, wqb_ref, wk_ref, wv_ref, g128_ref, g64_ref, gain_ref, lora_ref,
                       cos_ref, sin_ref, cosc_ref, sinc_ref,
                       mq_ref, mk_ref, mv_ref, gq_ref, gk_ref, gv_ref, *, use_rope):
    vec = vec_ref[...]
    h = (_rms(x_ref[...]) * vec[0:1] * (1.0 + vec[1:2]) + vec[2:3]).astype(BF16)
    gains = gain_ref[...]
    lora = lora_ref[...]
    g128 = g128_ref[...]
    g64 = g64_ref[...]
    mw = C_HEADS * HEAD_PAD
    if use_rope:
        cosc = _lane_tile(cosc_ref[...], mw)
        sinc = _lane_tile(sinc_ref[...], mw)
        first_c = (_lane_iota(mw) & 127) < (C_NOPE + C_ROPE // 2)
        cos = _lane_tile(cos_ref[...], 512)
        sin = _lane_tile(sin_ref[...], 512)
        first_d = (_lane_iota(512) & 32) == 0

    def seg(a, b):
        return _dot(h, w_ref[:, a:b])

    inv_c = 1.0 / (C_NOPE + C_ROPE)
    cq = (_rms(seg(_O_CQ, _O_CKV)) * lora[0:1, :C_Q_LORA]).astype(BF16)
    mq = _group_rms(_dot(cq, wqb_ref[...]), g128, inv_c, gains[0:1])
    if use_rope:
        mq = _rope(mq, cosc, sinc, C_ROPE // 2, first_c)
    mq_ref[...] = (mq * (LOG2E * math.sqrt(inv_c))).astype(BF16)
    ckv = (_rms(seg(_O_CKV, _O_DQ)) * lora[1:2, :C_KV_LORA]).astype(BF16)
    mk = _dot(ckv, wk_ref[...]) + _lane_tile(seg(_O_CKR, _O_END), mw)
    mk = _group_rms(mk, g128, inv_c, gains[1:2])
    if use_rope:
        mk = _rope(mk, cosc, sinc, C_ROPE // 2, first_c)
    mk_ref[...] = mk.astype(BF16)
    mv_ref[...] = _dot(ckv, wv_ref[...]).astype(BF16)
    gq = _group_rms(seg(_O_DQ, _O_DK), g64, 1.0 / D_DH, gains[2:3, :512])
    gk = _group_rms(seg(_O_DK, _O_DV), g64[:256, :256], 1.0 / D_DH, gains[3:4, :256])
    if use_rope:
        gq = _rope(gq, cos, sin, 32, first_d)
        gk = _rope(gk, cos[:, :256], sin[:, :256], 32, first_d[:, :256])
    gq_ref[...] = (gq * (LOG2E / math.sqrt(D_DH))).astype(BF16)
    gk_ref[...] = gk.astype(BF16)
    gv_ref[...] = seg(_O_DV, _O_CKR).astype(BF16)


def _premix_odd(x, vec, w, wqb, wk, wv, g128, g64, gains, lora, cos, sin, cosc, sinc, use_rope):
    t, d = x.shape
    tm = min(512, t)
    row = lambda i: (i, 0)
    tab = pl.BlockSpec((tm, LANES), row)
    widths = (1024, 1024, 512, 512, 256, 256)
    return pl.pallas_call(
        functools.partial(_premix_odd_kernel, use_rope=use_rope),
        grid=(t // tm,),
        in_specs=[pl.BlockSpec((tm, d), row)] + [_full(a.shape) for a in (vec, w, wqb, wk, wv, g128, g64, gains, lora)]
                 + [tab] * 4,
        out_specs=[pl.BlockSpec((tm, n), row) for n in widths],
        out_shape=[jax.ShapeDtypeStruct((t, n), BF16) for n in widths],
        compiler_params=_cparams("parallel"),
        name="premix_odd",
    )(x, vec, w, wqb, wk, wv, g128, g64, gains, lora, cos, sin, cosc, sinc)


def _flash_kernel(*refs, mode, has_ctx, lam_init):
    if has_ctx:
        qt_ref, k_ref, vt_ref, kc_ref, vct_ref, ex_ref, gcol_ref, o_ref, acc_ref, m_ref, l_ref, s_ref, mx_ref = refs
    else:
        qt_ref, k_ref, vt_ref, ex_ref, gcol_ref, o_ref, acc_ref, m_ref, l_ref, s_ref, mx_ref = refs
    qt = qt_ref[...]
    row = lax.broadcasted_iota(jnp.int32, (HEAD_PAD, 1), 0)
    if mode == "mla":
        qs = [qt[:HEAD_PAD], qt[HEAD_PAD:]]
    else:
        zero = jnp.zeros_like(qt)
        qs = [jnp.where(row < 64, qt, zero), jnp.where(row >= 64, qt, zero)]
    nk, tk = vt_ref.shape[1], vt_ref.shape[3]

    m_ref[...] = jnp.full(m_ref.shape, NEG_INF, F32)
    l_ref[...] = jnp.zeros(l_ref.shape, F32)
    acc_ref[...] = jnp.zeros(acc_ref.shape, F32)

    def scores(kblk, b):
        kb = kblk[:, b * HEAD_PAD:(b + 1) * HEAD_PAD] if mode == "mla" else kblk
        return _dot(kb, qs[b])

    def absorb(s, smax, vtblk, b):
        m_prev = m_ref[b]
        m_new = jnp.maximum(m_prev, smax)
        alpha = jnp.exp2(m_prev - m_new)
        p = jnp.exp2(s - m_new)
        l_ref[b] = alpha * l_ref[b] + jnp.sum(p, axis=0, keepdims=True)
        acc_ref[b] = alpha * acc_ref[b] + _dot(vtblk, p.astype(BF16))
        m_ref[b] = m_new

    def stage_scores(slot, kblk):
        for b in range(2):
            s = scores(kblk, b)
            s_ref[slot, b] = s
            mx_ref[slot, b] = jnp.max(s, axis=0, keepdims=True)

    def stage_absorb(slot, vtblk):
        for b in range(2):
            absorb(s_ref[slot, b], mx_ref[slot, b], vtblk, b)

    if has_ctx:
        for b in range(2):
            s = scores(kc_ref[...], b)
            absorb(s, jnp.max(s, axis=0, keepdims=True), vct_ref[0], b)

    if nk % 2:
        assert nk == 1
        stage_scores(0, k_ref[...])
        stage_absorb(0, vt_ref[0, 0])
    else:
        stage_scores(0, k_ref[0:tk, :])

        def body(i, carry):
            c0 = 2 * i
            stage_scores(1, k_ref[pl.ds(pl.multiple_of((c0 + 1) * tk, tk), tk), :])
            stage_absorb(0, vt_ref[0, c0])
            nxt = jnp.minimum(c0 + 2, nk - 1)
            stage_scores(0, k_ref[pl.ds(pl.multiple_of(nxt * tk, tk), tk), :])
            stage_absorb(1, vt_ref[0, c0 + 1])
            return carry

        lax.fori_loop(0, nk // 2, body, 0)

    o0 = acc_ref[0] / l_ref[0]
    o1 = acc_ref[1] / l_ref[1]
    if mode == "diff":
        ex = ex_ref[...]
        lam = (jnp.exp(jnp.sum(ex[0:1] * ex[1:2], axis=-1, keepdims=True))
               - jnp.exp(jnp.sum(ex[2:3] * ex[3:4], axis=-1, keepdims=True)) + lam_init)
        o = o0 - lam * o1
        o = o * lax.rsqrt(jnp.mean(o * o, axis=0, keepdims=True) + EPS) * gcol_ref[...] * (1.0 - lam_init)
    else:
        o = jnp.where(row < 64, o0, o1)
    o_ref[...] = o.astype(o_ref.dtype)


def _flash(q, k, v, kc, vc, ex, gcol, *, mode, lam_init=0.0):
    t = q.shape[0]
    tkeys = k.shape[0]
    qw = 2 * HEAD_PAD if mode == "mla" else HEAD_PAD
    groups = q.shape[1] // qw
    tq = min(512, t)
    tk = min(512, tkeys)
    nk = tkeys // tk
    has_ctx = kc is not None
    qt = q.T
    vt = v.reshape(nk, tk, groups, HEAD_PAD).transpose(2, 0, 3, 1)
    in_specs = [pl.BlockSpec((qw, tq), lambda g, i: (g, i)),
                pl.BlockSpec((tkeys, qw), lambda g, i: (0, g)),
                pl.BlockSpec((1, nk, HEAD_PAD, tk), lambda g, i: (g, 0, 0, 0))]
    args = [qt, k, vt]
    if has_ctx:
        lc = kc.shape[0]
        in_specs += [pl.BlockSpec((lc, qw), lambda g, i: (0, g)),
                     pl.BlockSpec((1, HEAD_PAD, lc), lambda g, i: (g, 0, 0))]
        args += [kc, vc.reshape(lc, groups, HEAD_PAD).transpose(1, 2, 0)]
    in_specs += [_full(ex.shape), _full(gcol.shape)]
    args += [ex, gcol]
    ot = pl.pallas_call(
        functools.partial(_flash_kernel, mode=mode, has_ctx=has_ctx, lam_init=lam_init),
        grid=(groups, t // tq),
        in_specs=in_specs,
        out_specs=pl.BlockSpec((HEAD_PAD, tq), lambda g, i: (g, i)),
        out_shape=jax.ShapeDtypeStruct((groups * HEAD_PAD, t), BF16),
        scratch_shapes=[pltpu.VMEM((2, HEAD_PAD, tq), F32), pltpu.VMEM((2, 1, tq), F32), pltpu.VMEM((2, 1, tq), F32),
                        pltpu.VMEM((2, 2, tk, tq), F32), pltpu.VMEM((2, 2, 1, tq), F32)],
        compiler_params=_cparams("parallel", "parallel"),
        name="flash_" + mode,
    )(*args)
    return ot.T


def _na_kernel(q_ref, kp_ref, kcur_ref, kn_ref, vp_ref, vcur_ref, vn_ref, kc_ref, vc_ref, bias_ref, o_ref):
    q = q_ref[...]
    lane = _lane_iota(LANES)
    zero = jnp.zeros_like(q)
    kk = jnp.concatenate([kc_ref[...], kp_ref[...], kcur_ref[...], kn_ref[...]], axis=0)
    vv = jnp.concatenate([vc_ref[...], vp_ref[...], vcur_ref[...], vn_ref[...]], axis=0)
    outs = []
    for b in range(2):
        qb = jnp.where(lane < 64, q, zero) if b == 0 else jnp.where(lane >= 64, q, zero)
        s = _dot_nt(qb, kk) + bias_ref[0, b]
        m = jnp.max(s, axis=-1, keepdims=True)
        p = jnp.exp2(s - m)
        l = jnp.sum(p, axis=-1, keepdims=True)
        outs.append(_dot(p.astype(BF16), vv) / l)
    o_ref[...] = jnp.where(lane < 64, outs[0], outs[1]).astype(o_ref.dtype)


def _na_bias_table(rpb, rows, lc):
    nr, w = NA_ROWS, GRID_W
    heads = rpb.shape[0]
    pad = w - NA_WIN_W
    padded = jnp.pad(rpb.astype(F32) * LOG2E, ((0, 0), (0, 0), (pad, pad)))
    toep = jnp.stack([padded[:, :, w - 1 - c:2 * w - 1 - c] for c in range(w)], axis=2)
    cq = np.arange(w)
    cs = np.clip(cq - NA_WIN_W // 2, 0, w - NA_WIN_W)
    valid_c = (cq[None, :] >= cs[:, None]) & (cq[None, :] < cs[:, None] + NA_WIN_W)
    toep = jnp.where(jnp.asarray(valid_c), toep, NEG_INF)
    wh = min(NA_WIN_H, rows)
    rl = np.arange(nr)
    rr_rel = np.arange(3 * nr)
    dr_all, valid_all = [], []
    for base in (0, nr, rows - nr):
        r = base + rl
        rs = np.clip(r - wh // 2, 0, rows - wh)
        rr = base - nr + rr_rel
        valid_all.append((rr[None, :] >= rs[:, None]) & (rr[None, :] < rs[:, None] + wh))
        dr_all.append(np.clip(rr[None, :] - r[:, None] + (NA_WIN_H - 1), 0, 2 * NA_WIN_H - 2))
    dr_idx = np.stack(dr_all).reshape(-1)
    valid_r = np.stack(valid_all)
    blocks = jnp.take(toep, jnp.asarray(dr_idx, jnp.int32), axis=1).reshape(heads, 3, nr, 3 * nr, w, w)
    blocks = jnp.where(jnp.asarray(valid_r)[None, :, :, :, None, None], blocks, NEG_INF)
    tab = blocks.transpose(1, 0, 2, 4, 3, 5).reshape(3, heads, nr * w, 3 * nr * w)
    return jnp.concatenate([jnp.zeros(tab.shape[:3] + (lc,), F32), tab], axis=-1)


def _neighbourhood(q, k, v, kc, vc, bias):
    t = q.shape[0]
    qn = NA_ROWS * GRID_W
    nb = t // qn
    lc = kc.shape[0]
    pairs = q.shape[1] // HEAD_PAD
    cur = lambda p, i: (i, p)
    prev = lambda p, i: (jnp.maximum(i - 1, 0), p)
    nxt = lambda p, i: (jnp.minimum(i + 1, nb - 1), p)
    blk = lambda f: pl.BlockSpec((qn, HEAD_PAD), f)
    ctxs = pl.BlockSpec((lc, HEAD_PAD), lambda p, i: (0, p))
    case = lambda p, i: (jnp.where(i == 0, 0, jnp.where(i == nb - 1, 2, 1)), p, 0, 0)
    return pl.pallas_call(
        _na_kernel,
        grid=(pairs, nb),
        in_specs=[blk(cur), blk(prev), blk(cur), blk(nxt), blk(prev), blk(cur), blk(nxt), ctxs, ctxs,
                  pl.BlockSpec((1, 2, qn, lc + 3 * qn), case)],
        out_specs=blk(cur),
        out_shape=jax.ShapeDtypeStruct(q.shape, BF16),
        compiler_params=_cparams("parallel", "arbitrary"),
        name="neighbourhood",
    )(q, k, k, k, v, v, v, kc, vc, bias)


def _wgqa_kernel(sink_ref, q_ref, kp_ref, kcur_ref, kn_ref, vp_ref, vcur_ref, vn_ref, kc_ref, vc_ref, o_ref):
    i = pl.program_id(0)
    nb = pl.num_programs(0)
    qb = q_ref.shape[0]
    lc = kc_ref.shape[0]
    nkeys = lc + 3 * qb
    lane = _lane_iota(LANES)
    col = lax.broadcasted_iota(jnp.int32, (qb, nkeys), 1)
    row = lax.broadcasted_iota(jnp.int32, (qb, nkeys), 0)
    rel = col - (lc + qb) - row
    lo = jnp.where(i > 0, lc, lc + qb)
    hi = jnp.where(i < nb - 1, nkeys, lc + 2 * qb)
    valid = (col < lc) | ((jnp.abs(rel) <= D_WINDOW) & (col >= lo) & (col < hi))
    groups = D_HEADS // D_KV_HEADS
    for kv in range(D_KV_HEADS):
        ks = slice(kv * HEAD_PAD, (kv + 1) * HEAD_PAD)
        kk = jnp.concatenate([kc_ref[:, ks], kp_ref[:, ks], kcur_ref[:, ks], kn_ref[:, ks]], axis=0)
        vv = jnp.concatenate([vc_ref[:, ks], vp_ref[:, ks], vcur_ref[:, ks], vn_ref[:, ks]], axis=0)
        lhs = []
        for g in range(groups):
            h = kv * groups + g
            qp = q_ref[:, (h // 2) * HEAD_PAD:(h // 2 + 1) * HEAD_PAD]
            keep = (lane < 64) if h % 2 == 0 else (lane >= 64)
            lhs.append(jnp.where(keep, qp, jnp.zeros_like(qp)))
        s = _dot_nt(jnp.concatenate(lhs, axis=0), kk)
        outs = []
        for g in range(groups):
            sink = sink_ref[kv * groups + g] * LOG2E
            sg = jnp.where(valid, s[g * qb:(g + 1) * qb], NEG_INF)
            m = jnp.maximum(jnp.max(sg, axis=-1, keepdims=True), sink)
            p = jnp.exp2(sg - m)
            l = jnp.sum(p, axis=-1, keepdims=True) + jnp.exp2(sink - m)
            outs.append(_dot(p.astype(BF16), vv) / l)
        for pp in range(groups // 2):
            pair = kv * (groups // 2) + pp
            o_ref[:, pair * HEAD_PAD:(pair + 1) * HEAD_PAD] = jnp.where(
                lane < 64, outs[2 * pp], outs[2 * pp + 1]).astype(o_ref.dtype)


def _window_gqa(q, k, v, kc, vc, sink):
    t = q.shape[0]
    qb = D_WINDOW
    nb = t // qb
    lc = kc.shape[0]
    kw = k.shape[1]
    cur = lambda i: (i, 0)
    prev = lambda i: (jnp.maximum(i - 1, 0), 0)
    nxt = lambda i: (jnp.minimum(i + 1, nb - 1), 0)
    kblk = lambda f: pl.BlockSpec((qb, kw), f)
    return pl.pallas_call(
        _wgqa_kernel,
        grid=(nb,),
        in_specs=[pl.BlockSpec(memory_space=pltpu.SMEM), pl.BlockSpec((qb, q.shape[1]), cur),
                  kblk(prev), kblk(cur), kblk(nxt), kblk(prev), kblk(cur), kblk(nxt),
                  _full((lc, kw)), _full((lc, kw))],
        out_specs=pl.BlockSpec((qb, q.shape[1]), cur),
        out_shape=jax.ShapeDtypeStruct(q.shape, BF16),
        compiler_params=_cparams("parallel"),
        name="window_gqa",
    )(sink, q, k, k, k, v, v, v, kc, vc)


def _postmix_kernel(o1_ref, o2_ref, x_ref, w_ref, vec_ref, x1_ref, h2_ref):
    half = o1_ref.shape[1]
    y = _dot(o1_ref[...], w_ref[:half, :]) + _dot(o2_ref[...], w_ref[half:, :])
    vec = vec_ref[...]
    x1 = x_ref[...] + vec[0:1] * y
    x1_ref[...] = x1
    h2_ref[...] = (_rms(x1) * vec[1:2] * (1.0 + vec[2:3]) + vec[3:4]).astype(BF16)


def _postmix(o1, o2, x, w, vec):
    t, d = x.shape
    tm = min(512, t)
    row = lambda i: (i, 0)
    return pl.pallas_call(
        _postmix_kernel,
        grid=(t // tm,),
        in_specs=[pl.BlockSpec((tm, o1.shape[1]), row), pl.BlockSpec((tm, o2.shape[1]), row),
                  pl.BlockSpec((tm, d), row), _full(w.shape), _full(vec.shape)],
        out_specs=[pl.BlockSpec((tm, d), row)] * 2,
        out_shape=[jax.ShapeDtypeStruct((t, d), F32), jax.ShapeDtypeStruct((t, d), BF16)],
        compiler_params=_cparams("parallel"),
        name="postmix",
    )(o1, o2, x, w, vec)


def _ffn_kernel(h_ref, hp_ref, hn_ref, x1_ref, wup_ref, cv_ref, wdn_ref, g2_ref, o_ref, acc_ref, hcat_ref):
    i = pl.program_id(0)
    n = pl.num_programs(0)
    tm = h_ref.shape[0]
    rows = tm + 2 * HALO
    nchunk = wdn_ref.shape[0]
    hcat_ref[0:HALO] = hp_ref[...]
    hcat_ref[HALO:HALO + tm] = h_ref[...]
    hcat_ref[HALO + tm:rows] = hn_ref[...]

    @pl.when(i == 0)
    def _():
        hcat_ref[0:HALO] = jnp.zeros((HALO, hcat_ref.shape[1]), BF16)

    @pl.when(i == n - 1)
    def _():
        hcat_ref[HALO + tm:rows] = jnp.zeros((HALO, hcat_ref.shape[1]), BF16)

    acc_ref[...] = jnp.zeros(acc_ref.shape, F32)

    def conv(u, cv):
        before = pltpu.roll(u, 1, axis=0)
        after = pltpu.roll(u, rows - 1, axis=0)
        r = cv[3:4] + before * cv[0:1] + u * cv[1:2] + after * cv[2:3]
        return r[HALO:HALO + tm]

    def body(c, carry):
        hcat = hcat_ref[...]
        a = conv(_dot(hcat, wup_ref[c]), cv_ref[c])
        g = conv(_dot(hcat, wup_ref[nchunk + c]), cv_ref[nchunk + c])
        act = (a * (g * _sigmoid(g))).astype(BF16)
        acc_ref[...] += _dot(act, wdn_ref[c])
        return carry

    lax.fori_loop(0, nchunk, body, 0)
    o_ref[...] = x1_ref[...] + g2_ref[...] * acc_ref[...]


def _conv_ffn(h2, x1, wup, cv, wdn, g2):
    t, d = x1.shape
    tm = min(512, t)
    nt = t // tm
    hb = tm // HALO
    row = lambda i: (i, 0)
    prev = lambda i: (jnp.maximum(i * hb - 1, 0), 0)
    nxt = lambda i: (jnp.minimum((i + 1) * hb, t // HALO - 1), 0)
    return pl.pallas_call(
        _ffn_kernel,
        grid=(nt,),
        in_specs=[pl.BlockSpec((tm, d), row), pl.BlockSpec((HALO, d), prev), pl.BlockSpec((HALO, d), nxt),
                  pl.BlockSpec((tm, d), row), _full(wup.shape), _full(cv.shape), _full(wdn.shape), _full(g2.shape)],
        out_specs=pl.BlockSpec((tm, d), row),
        out_shape=jax.ShapeDtypeStruct((t, d), F32),
        scratch_shapes=[pltpu.VMEM((tm, d), F32), pltpu.VMEM((tm + 2 * HALO, d), BF16)],
        compiler_params=_cparams("parallel"),
        name="conv_ffn",
    )(h2, h2, h2, x1, wup, cv, wdn, g2)


def _axial_tables(n_tokens, dim):
    t = jnp.arange(n_tokens, dtype=jnp.int32)
    row = (t // GRID_W).astype(F32)
    col = (t % GRID_W).astype(F32)
    quarter = dim // 4
    inv_freq = ROPE_BASE ** (-jnp.arange(quarter, dtype=F32) / quarter)
    ang = jnp.concatenate([row[:, None] * inv_freq, col[:, None] * inv_freq], axis=-1)
    return jnp.cos(ang), jnp.sin(ang)


def _rope_tables_head64(t):
    cos, sin = _axial_tables(t, 64)
    return jnp.tile(jnp.concatenate([cos, cos], -1), (1, 2)), jnp.tile(jnp.concatenate([-sin, sin], -1), (1, 2))


def _rope_tables_latent(t):
    cos, sin = _axial_tables(t, C_ROPE)
    ones = jnp.ones((t, C_NOPE), F32)
    tail = jnp.ones((t, HEAD_PAD - C_NOPE - C_ROPE), F32)
    cosc = jnp.concatenate([ones, cos, cos, tail], -1)
    sinc = jnp.concatenate([0 * ones, -sin, sin, 0 * tail], -1)
    return cosc, sinc


def _group_ones(width, group):
    return jnp.asarray(np.kron(np.eye(width // group), np.ones((group, group))), BF16)


def _pad_heads(a, heads, dim):
    a = a.reshape(a.shape[:-1] + (heads, dim))
    a = jnp.pad(a, [(0, 0)] * (a.ndim - 1) + [(0, HEAD_PAD - dim)])
    return a.reshape(a.shape[:-2] + (heads * HEAD_PAD,))


def _pad_row(v, width):
    return jnp.pad(v, (0, width - v.shape[0]))


def _ffn_weights(w_up, conv_w, conv_b, w_down):
    d, f2 = w_up.shape
    nc2 = f2 // FFN_CHUNK
    wup = w_up.astype(BF16).reshape(d, nc2, FFN_CHUNK).transpose(1, 0, 2)
    cv = jnp.concatenate([conv_w, conv_b[None]], axis=0).reshape(CONV_W + 1, nc2, FFN_CHUNK).transpose(1, 0, 2)
    wdn = w_down.astype(BF16).reshape(nc2 // 2, FFN_CHUNK, d)
    return wup, cv, wdn


def kernel(x, c, ctx, c_ctx, ada_w, ada_b, norm_mix, norm_ffn, ffn_up, ffn_conv_w, ffn_conv_b, ffn_down, ev_w_in, ev_w_out, a_q_norm, a_k_norm, a_lam_q1, a_lam_k1, a_lam_q2, a_lam_k2, a_subln, b_q_norm, b_k_norm, b_rpb, od_w_in, od_w_out, c_q_a_norm, c_w_qb, c_kv_a_norm, c_w_kvb, c_q_norm, c_k_norm, d_q_norm, d_k_norm, d_sink):
    assert x.shape[0] == 1 and ctx.shape[0] == 1
    xm = x[0]
    xc = ctx[0]
    t, d = xm.shape
    lc = xc.shape[0]
    mods = _modulation(c, c_ctx, ada_w, ada_b)
    cos64, sin64 = _rope_tables_head64(t)
    cosc, sinc = _rope_tables_latent(t)
    g64 = _group_ones(512, 64)
    g128 = _group_ones(C_HEADS * HEAD_PAD, HEAD_PAD)
    dummy_tab = jnp.zeros((lc, LANES), F32)

    for l in range(DEPTH):
        last = l == DEPTH - 1
        i = l // 2
        mm = mods[l, 0].reshape(6, d)
        mc = mods[l, 1].reshape(6, d)
        pre_m = jnp.stack([norm_mix[l], mm[1], mm[0]])
        pre_c = jnp.stack([norm_mix[l], mc[1], mc[0]])
        post_m = jnp.stack([mm[2], norm_ffn[l], mm[4], mm[3]])
        post_c = jnp.stack([mc[2], norm_ffn[l], mc[4], mc[3]])
        wup, cv, wdn = _ffn_weights(ffn_up[l], ffn_conv_w[l], ffn_conv_b[l], ffn_down[l])
        if l % 2 == 0:
            lam_init = 0.8 - 0.6 * math.exp(-0.3 * l)
            w_in = ev_w_in[i].astype(BF16)
            w_out = ev_w_out[i].astype(BF16)
            hg = jnp.stack([jnp.tile(a_q_norm[i], 8), jnp.tile(a_k_norm[i], 8),
                            jnp.tile(b_q_norm[i], 8), jnp.tile(b_k_norm[i], 8)])
            ex = jnp.stack([_pad_row(a_lam_q1[i], LANES), _pad_row(a_lam_k1[i], LANES),
                            _pad_row(a_lam_q2[i], LANES), _pad_row(a_lam_k2[i], LANES)])
            gcol = a_subln[i][:, None]
            qa, ka, va, qb, kb, vb = _premix_even(xm, pre_m, w_in, g64, hg, cos64, sin64, True)
            qa_c, ka_c, va_c, qb_c, kb_c, vb_c = _premix_even(xc, pre_c, w_in, g64, hg, dummy_tab, dummy_tab, False)
            oa = _flash(qa, ka, va, ka_c, va_c, ex, gcol, mode="diff", lam_init=lam_init)
            ob = _neighbourhood(qb, kb, vb, kb_c, vb_c, _na_bias_table(b_rpb[i], t // GRID_W, lc))
            if not last:
                oa_c = _flash(qa_c, ka_c, va_c, None, None, ex, gcol, mode="diff", lam_init=lam_init)
                ob_c = _flash(qb_c, kb_c, vb_c, None, None, ex, gcol, mode="pair")
        else:
            w = od_w_in[i]
            cq, ckv, ckr, dq, dk, dv = jnp.split(w, np.cumsum([C_Q_LORA, C_KV_LORA, C_ROPE, 512, 128]).tolist(), axis=1)
            dup = lambda a: jnp.concatenate([a[:, :64], a[:, :64], a[:, 64:], a[:, 64:]], axis=1)
            ckr_pad = jnp.pad(ckr, ((0, 0), (C_NOPE, HEAD_PAD - C_NOPE - C_ROPE)))
            w_in = jnp.concatenate([cq, ckv, dq, dup(dk), dup(dv), ckr_pad], axis=1).astype(BF16)
            w_out = od_w_out[i].astype(BF16)
            wqb = _pad_heads(c_w_qb[i], C_HEADS, C_NOPE + C_ROPE).astype(BF16)
            kvb = c_w_kvb[i].reshape(C_KV_LORA, C_HEADS, C_NOPE + C_DV)
            wk = _pad_heads(kvb[:, :, :C_NOPE].reshape(C_KV_LORA, -1), C_HEADS, C_NOPE).astype(BF16)
            wv = kvb[:, :, C_NOPE:].reshape(C_KV_LORA, C_HEADS * C_DV).astype(BF16)
            gains = jnp.stack([jnp.tile(_pad_row(c_q_norm[i], HEAD_PAD), C_HEADS),
                               jnp.tile(_pad_row(c_k_norm[i], HEAD_PAD), C_HEADS),
                               _pad_row(jnp.tile(d_q_norm[i], 8), 1024), _pad_row(jnp.tile(d_k_norm[i], 4), 1024)])
            lora = jnp.stack([_pad_row(c_q_a_norm[i], 512), _pad_row(c_kv_a_norm[i], 512)])
            odd = functools.partial(_premix_odd, w=w_in, wqb=wqb, wk=wk, wv=wv, g128=g128, g64=g64, gains=gains, lora=lora)
            mq, mk, mv, gq, gk, gv = odd(xm, pre_m, cos=cos64, sin=sin64, cosc=cosc, sinc=sinc, use_rope=True)
            mq_c, mk_c, mv_c, gq_c, gk_c, gv_c = odd(xc, pre_c, cos=dummy_tab, sin=dummy_tab, cosc=dummy_tab,
                                                     sinc=dummy_tab, use_rope=False)
            ex = jnp.zeros((4, LANES), F32)
            oa = _flash(mq, mk, mv, mk_c, mv_c, ex, jnp.ones((HEAD_PAD, 1), F32), mode="mla")
            ob = _window_gqa(gq, gk, gv, gk_c, gv_c, d_sink[i])
            if not last:
                raise NotImplementedError("context update after an odd layer is not needed at this depth")
        x1, h2 = _postmix(oa, ob, xm, w_out, post_m)
        xm = _conv_ffn(h2, x1, wup, cv, wdn, mm[5][None])
        if not last:
            xc1, hc2 = _postmix(oa_c, ob_c, xc, w_out, post_c)
            xc = _conv_ffn(hc2, xc1, wup, cv, wdn, mc[5][None])
    return xm[None]
```

```python
import functools
import math

import numpy as np
import jax
import jax.numpy as jnp
from jax import lax
from jax.experimental import pallas as pl
from jax.experimental.pallas import tpu as pltpu

D_MODEL = 1024
DEPTH = 2
GRID_W = 64
EPS = 1e-6
ROPE_BASE = 10000.0
NEG_INF = -1e30
LOG2E = math.log2(math.e)

A_HEADS = 4
A_DQK = 64
A_DV = 128
B_HEADS = 8
B_DH = 64
NA_WIN_H = 8
NA_WIN_W = 16
C_HEADS = 8
C_Q_LORA = 384
C_KV_LORA = 256
C_NOPE = 64
C_ROPE = 32
C_DV = 64
D_HEADS = 8
D_KV_HEADS = 2
D_DH = 64
D_WINDOW = 128
FFN_HIDDEN = 2816
CONV_W = 3

LANES = 128
HEAD_PAD = 128
VMEM_LIMIT = 56 * 1024 * 1024
FFN_CHUNK = 256
HALO = 16
NA_ROWS = 4
ONES_ROWS = 16
FLASH_TQ = 1024
FLASH_TK = 512
FLASH_UNROLL = 4

F32 = jnp.float32
BF16 = jnp.bfloat16


def _cparams(*sem):
    return pltpu.CompilerParams(dimension_semantics=sem, vmem_limit_bytes=VMEM_LIMIT)


def _full(shape):
    n = len(shape)
    return pl.BlockSpec(shape, lambda *_: (0,) * n)


def _dot(a, b):
    return jnp.dot(a, b, preferred_element_type=F32)


def _dot_nt(a, b):
    return lax.dot_general(a, b, (((1,), (1,)), ((), ())), preferred_element_type=F32)


def _rms(x):
    return x * lax.rsqrt(jnp.mean(x * x, axis=-1, keepdims=True) + EPS)


def _sigmoid(x):
    return 1.0 / (1.0 + jnp.exp(-x))


def _group_rms(y, gmat, inv_n, gain):
    ss = _dot((y * y).astype(BF16), gmat)
    return y * lax.rsqrt(ss * inv_n + EPS) * gain


def _rope(y, cos, sin, half, first_half):
    w = y.shape[-1]
    fwd = pltpu.roll(y, w - half, axis=1)
    bwd = pltpu.roll(y, half, axis=1)
    return y * cos + jnp.where(first_half, fwd, bwd) * sin


def _lane_tile(t, width):
    reps = width // t.shape[-1]
    return t if reps == 1 else jnp.concatenate([t] * reps, axis=-1)


def _lane_iota(width):
    return lax.broadcasted_iota(jnp.int32, (1, width), 1)


def _mod_kernel(ct_ref, w_ref, b_ref, o_ref):
    ct = ct_ref[...]
    s = ct * _sigmoid(ct)
    w = w_ref[0]
    rows = [jnp.sum(w * s[:, j:j + 1], axis=0, keepdims=True) for j in range(2)]
    o_ref[0] = jnp.concatenate(rows, axis=0) + b_ref[0]


def _modulation(c, c_ctx, ada_w, ada_b):
    depth, d, n = ada_w.shape
    tn = 1536
    ct = jnp.stack([c[0], c_ctx], axis=1)
    return pl.pallas_call(
        _mod_kernel,
        grid=(depth, n // tn),
        in_specs=[_full((d, 2)),
                  pl.BlockSpec((1, d, tn), lambda l, j: (l, 0, j)),
                  pl.BlockSpec((1, 1, tn), lambda l, j: (l, 0, j))],
        out_specs=pl.BlockSpec((1, 2, tn), lambda l, j: (l, 0, j)),
        out_shape=jax.ShapeDtypeStruct((depth, 2, n), F32),
        compiler_params=_cparams("arbitrary", "arbitrary"),
        name="modulation",
    )(ct, ada_w, ada_b.reshape(depth, 1, n))


def _premix_even_kernel(x_ref, vec_ref, w_ref, g_ref, hg_ref, cos_ref, sin_ref,
                        qa_ref, ka_ref, va_ref, qb_ref, kb_ref, vb_ref, *, use_rope):
    vec = vec_ref[...]
    h = (_rms(x_ref[...]) * vec[0:1] * (1.0 + vec[1:2]) + vec[2:3]).astype(BF16)
    gm = g_ref[...]
    hg = hg_ref[...]
    sw = 512
    if use_rope:
        cos = _lane_tile(cos_ref[...], sw)
        sin = _lane_tile(sin_ref[...], sw)
        first = (_lane_iota(sw) & 32) == 0

    def seg(i):
        return _dot(h, w_ref[:, i * sw:(i + 1) * sw])

    qa = _group_rms(seg(0), gm, 1.0 / A_DQK, hg[0:1])
    ka = _group_rms(seg(1), gm, 1.0 / A_DQK, hg[1:2])
    if use_rope:
        qa = _rope(qa, cos, sin, 32, first)
        ka = _rope(ka, cos, sin, 32, first)
    qa_ref[...] = (qa * (LOG2E / math.sqrt(A_DQK))).astype(BF16)
    ka_ref[...] = ka.astype(BF16)
    va_ref[...] = seg(2).astype(BF16)
    qb = _group_rms(seg(3), gm, 1.0 / B_DH, hg[2:3])
    qb_ref[...] = (qb * (LOG2E / math.sqrt(B_DH))).astype(BF16)
    kb_ref[...] = _group_rms(seg(4), gm, 1.0 / B_DH, hg[3:4]).astype(BF16)
    vb_ref[...] = seg(5).astype(BF16)


def _premix_even(x, vec, w, gmat, hg, cos, sin, use_rope):
    t, d = x.shape
    tm = min(512, t)
    row = lambda i: (i, 0)
    outs = [jax.ShapeDtypeStruct((t, 512), BF16)] * 6
    return pl.pallas_call(
        functools.partial(_premix_even_kernel, use_rope=use_rope),
        grid=(t // tm,),
        in_specs=[pl.BlockSpec((tm, d), row), _full(vec.shape), _full(w.shape), _full(gmat.shape),
                  _full(hg.shape), pl.BlockSpec((tm, LANES), row), pl.BlockSpec((tm, LANES), row)],
        out_specs=[pl.BlockSpec((tm, 512), row)] * 6,
        out_shape=outs,
        compiler_params=_cparams("parallel"),
        name="premix_even",
    )(x, vec, w, gmat, hg, cos, sin)


_O_CQ, _O_CKV, _O_DQ, _O_DK, _O_DV, _O_CKR, _O_END = 0, 384, 640, 1152, 1408, 1664, 1792


def _premix_odd_kernel(x_ref, vec_ref, w_ref, wqb_ref, wk_ref, wv_ref, g128_ref, g64_ref, gain_ref, lora_ref,
                       cos_ref, sin_ref, cosc_ref, sinc_ref,
                       mq_ref, mk_ref, mv_ref, gq_ref, gk_ref, gv_ref, *, use_rope):
    vec = vec_ref[...]
    h = (_rms(x_ref[...]) * vec[0:1] * (1.0 + vec[1:2]) + vec[2:3]).astype(BF16)
    gains = gain_ref[...]
    lora = lora_ref[...]
    g128 = g128_ref[...]
    g64 = g64_ref[...]
    mw = C_HEADS * HEAD_PAD
    if use_rope:
        cosc = _lane_tile(cosc_ref[...], mw)
        sinc = _lane_tile(sinc_ref[...], mw)
        first_c = (_lane_iota(mw) & 127) < (C_NOPE + C_ROPE // 2)
        cos = _lane_tile(cos_ref[...], 512)
        sin = _lane_tile(sin_ref[...], 512)
        first_d = (_lane_iota(512) & 32) == 0

    def seg(a, b):
        return _dot(h, w_ref[:, a:b])

    inv_c = 1.0 / (C_NOPE + C_ROPE)
    cq = (_rms(seg(_O_CQ, _O_CKV)) * lora[0:1, :C_Q_LORA]).astype(BF16)
    mq = _group_rms(_dot(cq, wqb_ref[...]), g128, inv_c, gains[0:1])
    if use_rope:
        mq = _rope(mq, cosc, sinc, C_ROPE // 2, first_c)
    mq_ref[...] = (mq * (LOG2E * math.sqrt(inv_c))).astype(BF16)
    ckv = (_rms(seg(_O_CKV, _O_DQ)) * lora[1:2, :C_KV_LORA]).astype(BF16)
    mk = _dot(ckv, wk_ref[...]) + _lane_tile(seg(_O_CKR, _O_END), mw)
    mk = _group_rms(mk, g128, inv_c, gains[1:2])
    if use_rope:
        mk = _rope(mk, cosc, sinc, C_ROPE // 2, first_c)
    mk_ref[...] = mk.astype(BF16)
    mv_ref[...] = _dot(ckv, wv_ref[...]).astype(BF16)
    gq = _group_rms(seg(_O_DQ, _O_DK), g64, 1.0 / D_DH, gains[2:3, :512])
    gk = _group_rms(seg(_O_DK, _O_DV), g64[:256, :256], 1.0 / D_DH, gains[3:4, :256])
    if use_rope:
        gq = _rope(gq, cos, sin, 32, first_d)
        gk = _rope(gk, cos[:, :256], sin[:, :256], 32, first_d[:, :256])
    gq_ref[...] = (gq * (LOG2E / math.sqrt(D_DH))).astype(BF16)
    gk_ref[...] = gk.astype(BF16)
    gv_ref[...] = seg(_O_DV, _O_CKR).astype(BF16)


def _premix_odd(x, vec, w, wqb, wk, wv, g128, g64, gains, lora, cos, sin, cosc, sinc, use_rope):
    t, d = x.shape
    tm = min(512, t)
    row = lambda i: (i, 0)
    tab = pl.BlockSpec((tm, LANES), row)
    widths = (1024, 1024, 512, 512, 256, 256)
    return pl.pallas_call(
        functools.partial(_premix_odd_kernel, use_rope=use_rope),
        grid=(t // tm,),
        in_specs=[pl.BlockSpec((tm, d), row)] + [_full(a.shape) for a in (vec, w, wqb, wk, wv, g128, g64, gains, lora)]
                 + [tab] * 4,
        out_specs=[pl.BlockSpec((tm, n), row) for n in widths],
        out_shape=[jax.ShapeDtypeStruct((t, n), BF16) for n in widths],
        compiler_params=_cparams("parallel"),
        name="premix_odd",
    )(x, vec, w, wqb, wk, wv, g128, g64, gains, lora, cos, sin, cosc, sinc)


def _flash_kernel(*refs, mode, has_ctx, lam_init):
    if has_ctx:
        qt_ref, k_ref, vt_ref, kc_ref, vct_ref, ex_ref, gcol_ref, o_ref, acc_ref, m_ref, s_ref, mx_ref = refs
    else:
        qt_ref, k_ref, vt_ref, ex_ref, gcol_ref, o_ref, acc_ref, m_ref, s_ref, mx_ref = refs
    qt = qt_ref[...]
    row = lax.broadcasted_iota(jnp.int32, (HEAD_PAD, 1), 0)
    if mode == "mla":
        qs = [qt[:HEAD_PAD], qt[HEAD_PAD:]]
    else:
        zero = jnp.zeros_like(qt)
        qs = [jnp.where(row < 64, qt, zero), jnp.where(row >= 64, qt, zero)]
    nk, tk = vt_ref.shape[1], vt_ref.shape[3]

    m_ref[...] = jnp.full(m_ref.shape, NEG_INF, F32)
    acc_ref[...] = jnp.zeros(acc_ref.shape, F32)
    rows_b = acc_ref.shape[1]

    def scores(kblk, b):
        kb = kblk[:, b * HEAD_PAD:(b + 1) * HEAD_PAD] if mode == "mla" else kblk
        return _dot(kb, qs[b])

    def absorb(s, smax, vtblk, b):
        m_prev = m_ref[b]
        m_new = jnp.maximum(m_prev, smax)
        alpha = jnp.exp2(m_prev - m_new)
        p = jnp.exp2(s - m_new)
        vtb = vtblk if vtblk.shape[0] == rows_b else vtblk[b * rows_b:(b + 1) * rows_b]
        acc_ref[b] = alpha * acc_ref[b] + _dot(vtb, p.astype(BF16))
        m_ref[b] = m_new

    def stage_scores(slot, kblk):
        for b in range(2):
            s = scores(kblk, b)
            s_ref[slot, b] = s
            mx_ref[slot, b] = jnp.max(s, axis=0, keepdims=True)

    def stage_absorb(slot, vtblk):
        for b in range(2):
            absorb(s_ref[slot, b], mx_ref[slot, b], vtblk, b)

    def absorb_ctx():
        if has_ctx:
            for b in range(2):
                s = scores(kc_ref[...], b)
                absorb(s, jnp.max(s, axis=0, keepdims=True), vct_ref[0], b)

    if nk == 1:
        absorb_ctx()
        stage_scores(0, k_ref[...])
        stage_absorb(0, vt_ref[0, 0])
    else:
        unroll = FLASH_UNROLL if nk % FLASH_UNROLL == 0 else 2
        assert nk % unroll == 0
        stage_scores(0, k_ref[0:tk, :])
        absorb_ctx()

        def body(i, carry):
            c0 = unroll * i
            for u in range(unroll):
                nxt = jnp.minimum(c0 + u + 1, nk - 1)
                stage_scores((u + 1) % 2, k_ref[pl.ds(pl.multiple_of(nxt * tk, tk), tk), :])
                stage_absorb(u % 2, vt_ref[0, c0 + u])
            return carry

        lax.fori_loop(0, nk // unroll, body, 0)

    dv = rows_b - ONES_ROWS
    o0 = acc_ref[0, :dv] / acc_ref[0, dv:dv + 1]
    o1 = acc_ref[1, :dv] / acc_ref[1, dv:dv + 1]
    if mode == "diff":
        ex = ex_ref[...]
        lam = (jnp.exp(jnp.sum(ex[0:1] * ex[1:2], axis=-1, keepdims=True))
               - jnp.exp(jnp.sum(ex[2:3] * ex[3:4], axis=-1, keepdims=True)) + lam_init)
        o = o0 - lam * o1
        o = o * lax.rsqrt(jnp.mean(o * o, axis=0, keepdims=True) + EPS) * gcol_ref[...] * (1.0 - lam_init)
    else:
        o = jnp.concatenate([o0, o1], axis=0)
    o_ref[...] = o.astype(o_ref.dtype)


def _flash(q, k, v, kc, vc, ex, gcol, *, mode, lam_init=0.0):
    t = q.shape[0]
    tkeys = k.shape[0]
    qw = 2 * HEAD_PAD if mode == "mla" else HEAD_PAD
    groups = q.shape[1] // qw
    tq = min(FLASH_TQ, t)
    tk = min(FLASH_TK, tkeys)
    nk = tkeys // tk
    has_ctx = kc is not None
    dv = HEAD_PAD if mode == "diff" else HEAD_PAD // 2
    rows_b = dv + ONES_ROWS

    def with_ones(vt):
        ones = jnp.ones(vt.shape[:-2] + (ONES_ROWS, vt.shape[-1]), BF16)
        parts = [vt, ones] if dv == HEAD_PAD else [vt[..., :dv, :], ones, vt[..., dv:, :], ones]
        return jnp.concatenate(parts, axis=-2)

    qt = q.T
    vt = with_ones(v.reshape(nk, tk, groups, HEAD_PAD).transpose(2, 0, 3, 1))
    vrows = vt.shape[2]
    in_specs = [pl.BlockSpec((qw, tq), lambda g, i: (g, i)),
                pl.BlockSpec((tkeys, qw), lambda g, i: (0, g)),
                pl.BlockSpec((1, nk, vrows, tk), lambda g, i: (g, 0, 0, 0))]
    args = [qt, k, vt]
    if has_ctx:
        lc = kc.shape[0]
        in_specs += [pl.BlockSpec((lc, qw), lambda g, i: (0, g)),
                     pl.BlockSpec((1, vrows, lc), lambda g, i: (g, 0, 0))]
        args += [kc, with_ones(vc.reshape(lc, groups, HEAD_PAD).transpose(1, 2, 0))]
    in_specs += [_full(ex.shape), _full(gcol.shape)]
    args += [ex, gcol]
    ot = pl.pallas_call(
        functools.partial(_flash_kernel, mode=mode, has_ctx=has_ctx, lam_init=lam_init),
        grid=(groups, t // tq),
        in_specs=in_specs,
        out_specs=pl.BlockSpec((HEAD_PAD, tq), lambda g, i: (g, i)),
        out_shape=jax.ShapeDtypeStruct((groups * HEAD_PAD, t), BF16),
        scratch_shapes=[pltpu.VMEM((2, rows_b, tq), F32), pltpu.VMEM((2, 1, tq), F32),
                        pltpu.VMEM((2, 2, tk, tq), F32), pltpu.VMEM((2, 2, 1, tq), F32)],
        compiler_params=_cparams("parallel", "parallel"),
        name="flash_" + mode,
    )(*args)
    return ot.T


def _na_kernel(q_ref, kp_ref, kcur_ref, kn_ref, vp_ref, vcur_ref, vn_ref, kc_ref, vc_ref, bias_ref, o_ref):
    q = q_ref[...]
    lane = _lane_iota(LANES)
    zero = jnp.zeros_like(q)
    kk = jnp.concatenate([kc_ref[...], kp_ref[...], kcur_ref[...], kn_ref[...]], axis=0)
    vv = jnp.concatenate([vc_ref[...], vp_ref[...], vcur_ref[...], vn_ref[...]], axis=0)
    outs = []
    for b in range(2):
        qb = jnp.where(lane < 64, q, zero) if b == 0 else jnp.where(lane >= 64, q, zero)
        s = _dot_nt(qb, kk) + bias_ref[0, b]
        m = jnp.max(s, axis=-1, keepdims=True)
        p = jnp.exp2(s - m)
        l = jnp.sum(p, axis=-1, keepdims=True)
        outs.append(_dot(p.astype(BF16), vv) / l)
    o_ref[...] = jnp.where(lane < 64, outs[0], outs[1]).astype(o_ref.dtype)


def _na_bias_table(rpb, rows, lc):
    nr, w = NA_ROWS, GRID_W
    heads = rpb.shape[0]
    pad = w - NA_WIN_W
    padded = jnp.pad(rpb.astype(F32) * LOG2E, ((0, 0), (0, 0), (pad, pad)))
    toep = jnp.stack([padded[:, :, w - 1 - c:2 * w - 1 - c] for c in range(w)], axis=2)
    cq = np.arange(w)
    cs = np.clip(cq - NA_WIN_W // 2, 0, w - NA_WIN_W)
    valid_c = (cq[None, :] >= cs[:, None]) & (cq[None, :] < cs[:, None] + NA_WIN_W)
    toep = jnp.where(jnp.asarray(valid_c), toep, NEG_INF)
    wh = min(NA_WIN_H, rows)
    rl = np.arange(nr)
    rr_rel = np.arange(3 * nr)
    dr_all, valid_all = [], []
    for base in (0, nr, rows - nr):
        r = base + rl
        rs = np.clip(r - wh // 2, 0, rows - wh)
        rr = base - nr + rr_rel
        valid_all.append((rr[None, :] >= rs[:, None]) & (rr[None, :] < rs[:, None] + wh))
        dr_all.append(np.clip(rr[None, :] - r[:, None] + (NA_WIN_H - 1), 0, 2 * NA_WIN_H - 2))
    dr_idx = np.stack(dr_all).reshape(-1)
    valid_r = np.stack(valid_all)
    blocks = jnp.take(toep, jnp.asarray(dr_idx, jnp.int32), axis=1).reshape(heads, 3, nr, 3 * nr, w, w)
    blocks = jnp.where(jnp.asarray(valid_r)[None, :, :, :, None, None], blocks, NEG_INF)
    tab = blocks.transpose(1, 0, 2, 4, 3, 5).reshape(3, heads, nr * w, 3 * nr * w)
    return jnp.concatenate([jnp.zeros(tab.shape[:3] + (lc,), F32), tab], axis=-1)


def _neighbourhood(q, k, v, kc, vc, bias):
    t = q.shape[0]
    qn = NA_ROWS * GRID_W
    nb = t // qn
    lc = kc.shape[0]
    pairs = q.shape[1] // HEAD_PAD
    cur = lambda p, i: (i, p)
    prev = lambda p, i: (jnp.maximum(i - 1, 0), p)
    nxt = lambda p, i: (jnp.minimum(i + 1, nb - 1), p)
    blk = lambda f: pl.BlockSpec((qn, HEAD_PAD), f)
    ctxs = pl.BlockSpec((lc, HEAD_PAD), lambda p, i: (0, p))
    case = lambda p, i: (jnp.where(i == 0, 0, jnp.where(i == nb - 1, 2, 1)), p, 0, 0)
    return pl.pallas_call(
        _na_kernel,
        grid=(pairs, nb),
        in_specs=[blk(cur), blk(prev), blk(cur), blk(nxt), blk(prev), blk(cur), blk(nxt), ctxs, ctxs,
                  pl.BlockSpec((1, 2, qn, lc + 3 * qn), case)],
        out_specs=blk(cur),
        out_shape=jax.ShapeDtypeStruct(q.shape, BF16),
        compiler_params=_cparams("parallel", "arbitrary"),
        name="neighbourhood",
    )(q, k, k, k, v, v, v, kc, vc, bias)


def _wgqa_kernel(sink_ref, q_ref, kp_ref, kcur_ref, kn_ref, vp_ref, vcur_ref, vn_ref, kc_ref, vc_ref, o_ref):
    i = pl.program_id(0)
    nb = pl.num_programs(0)
    qb = q_ref.shape[0]
    lc = kc_ref.shape[0]
    nkeys = lc + 3 * qb
    lane = _lane_iota(LANES)
    col = lax.broadcasted_iota(jnp.int32, (qb, nkeys), 1)
    row = lax.broadcasted_iota(jnp.int32, (qb, nkeys), 0)
    rel = col - (lc + qb) - row
    lo = jnp.where(i > 0, lc, lc + qb)
    hi = jnp.where(i < nb - 1, nkeys, lc + 2 * qb)
    valid = (col < lc) | ((jnp.abs(rel) <= D_WINDOW) & (col >= lo) & (col < hi))
    groups = D_HEADS // D_KV_HEADS
    for kv in range(D_KV_HEADS):
        ks = slice(kv * HEAD_PAD, (kv + 1) * HEAD_PAD)
        kk = jnp.concatenate([kc_ref[:, ks], kp_ref[:, ks], kcur_ref[:, ks], kn_ref[:, ks]], axis=0)
        vv = jnp.concatenate([vc_ref[:, ks], vp_ref[:, ks], vcur_ref[:, ks], vn_ref[:, ks]], axis=0)
        lhs = []
        for g in range(groups):
            h = kv * groups + g
            qp = q_ref[:, (h // 2) * HEAD_PAD:(h // 2 + 1) * HEAD_PAD]
            keep = (lane < 64) if h % 2 == 0 else (lane >= 64)
            lhs.append(jnp.where(keep, qp, jnp.zeros_like(qp)))
        s = _dot_nt(jnp.concatenate(lhs, axis=0), kk)
        outs = []
        for g in range(groups):
            sink = sink_ref[kv * groups + g] * LOG2E
            sg = jnp.where(valid, s[g * qb:(g + 1) * qb], NEG_INF)
            m = jnp.maximum(jnp.max(sg, axis=-1, keepdims=True), sink)
            p = jnp.exp2(sg - m)
            l = jnp.sum(p, axis=-1, keepdims=True) + jnp.exp2(sink - m)
            outs.append(_dot(p.astype(BF16), vv) / l)
        for pp in range(groups // 2):
            pair = kv * (groups // 2) + pp
            o_ref[:, pair * HEAD_PAD:(pair + 1) * HEAD_PAD] = jnp.where(
                lane < 64, outs[2 * pp], outs[2 * pp + 1]).astype(o_ref.dtype)


def _window_gqa(q, k, v, kc, vc, sink):
    t = q.shape[0]
    qb = D_WINDOW
    nb = t // qb
    lc = kc.shape[0]
    kw = k.shape[1]
    cur = lambda i: (i, 0)
    prev = lambda i: (jnp.maximum(i - 1, 0), 0)
    nxt = lambda i: (jnp.minimum(i + 1, nb - 1), 0)
    kblk = lambda f: pl.BlockSpec((qb, kw), f)
    return pl.pallas_call(
        _wgqa_kernel,
        grid=(nb,),
        in_specs=[pl.BlockSpec(memory_space=pltpu.SMEM), pl.BlockSpec((qb, q.shape[1]), cur),
                  kblk(prev), kblk(cur), kblk(nxt), kblk(prev), kblk(cur), kblk(nxt),
                  _full((lc, kw)), _full((lc, kw))],
        out_specs=pl.BlockSpec((qb, q.shape[1]), cur),
        out_shape=jax.ShapeDtypeStruct(q.shape, BF16),
        compiler_params=_cparams("parallel"),
        name="window_gqa",
    )(sink, q, k, k, k, v, v, v, kc, vc)


def _postmix_kernel(o1_ref, o2_ref, x_ref, w_ref, vec_ref, x1_ref, h2_ref):
    half = o1_ref.shape[1]
    y = _dot(o1_ref[...], w_ref[:half, :]) + _dot(o2_ref[...], w_ref[half:, :])
    vec = vec_ref[...]
    x1 = x_ref[...] + vec[0:1] * y
    x1_ref[...] = x1
    h2_ref[...] = (_rms(x1) * vec[1:2] * (1.0 + vec[2:3]) + vec[3:4]).astype(BF16)


def _postmix(o1, o2, x, w, vec):
    t, d = x.shape
    tm = min(512, t)
    row = lambda i: (i, 0)
    return pl.pallas_call(
        _postmix_kernel,
        grid=(t // tm,),
        in_specs=[pl.BlockSpec((tm, o1.shape[1]), row), pl.BlockSpec((tm, o2.shape[1]), row),
                  pl.BlockSpec((tm, d), row), _full(w.shape), _full(vec.shape)],
        out_specs=[pl.BlockSpec((tm, d), row)] * 2,
        out_shape=[jax.ShapeDtypeStruct((t, d), F32), jax.ShapeDtypeStruct((t, d), BF16)],
        compiler_params=_cparams("parallel"),
        name="postmix",
    )(o1, o2, x, w, vec)


def _ffn_kernel(h_ref, hp_ref, hn_ref, x1_ref, wup_ref, cv_ref, wdn_ref, g2_ref, o_ref, acc_ref, hcat_ref, u_ref):
    i = pl.program_id(0)
    n = pl.num_programs(0)
    tm = h_ref.shape[0]
    rows = tm + 2 * HALO
    nchunk = wdn_ref.shape[0]
    hcat_ref[0:HALO] = hp_ref[...]
    hcat_ref[HALO:HALO + tm] = h_ref[...]
    hcat_ref[HALO + tm:rows] = hn_ref[...]

    @pl.when(i == 0)
    def _():
        hcat_ref[0:HALO] = jnp.zeros((HALO, hcat_ref.shape[1]), BF16)

    @pl.when(i == n - 1)
    def _():
        hcat_ref[HALO + tm:rows] = jnp.zeros((HALO, hcat_ref.shape[1]), BF16)

    acc_ref[...] = jnp.zeros(acc_ref.shape, F32)

    def conv(u, cv):
        before = pltpu.roll(u, 1, axis=0)
        after = pltpu.roll(u, rows - 1, axis=0)
        r = cv[3:4] + before * cv[0:1] + u * cv[1:2] + after * cv[2:3]
        return r[HALO:HALO + tm]

    def stage_up(slot, c):
        hcat = hcat_ref[...]
        u_ref[slot, 0] = _dot(hcat, wup_ref[c])
        u_ref[slot, 1] = _dot(hcat, wup_ref[nchunk + c])

    def stage_down(slot, c):
        a = conv(u_ref[slot, 0], cv_ref[c])
        g = conv(u_ref[slot, 1], cv_ref[nchunk + c])
        act = (a * (g * _sigmoid(g))).astype(BF16)
        acc_ref[...] += _dot(act, wdn_ref[c])

    assert nchunk % 2 == 1
    stage_up(0, 0)

    def body(j, carry):
        c = 2 * j
        stage_up(1, c + 1)
        stage_down(0, c)
        stage_up(0, c + 2)
        stage_down(1, c + 1)
        return carry

    lax.fori_loop(0, nchunk // 2, body, 0)
    stage_down(0, nchunk - 1)
    o_ref[...] = x1_ref[...] + g2_ref[...] * acc_ref[...]


def _conv_ffn(h2, x1, wup, cv, wdn, g2):
    t, d = x1.shape
    tm = min(512, t)
    nt = t // tm
    hb = tm // HALO
    row = lambda i: (i, 0)
    prev = lambda i: (jnp.maximum(i * hb - 1, 0), 0)
    nxt = lambda i: (jnp.minimum((i + 1) * hb, t // HALO - 1), 0)
    return pl.pallas_call(
        _ffn_kernel,
        grid=(nt,),
        in_specs=[pl.BlockSpec((tm, d), row), pl.BlockSpec((HALO, d), prev), pl.BlockSpec((HALO, d), nxt),
                  pl.BlockSpec((tm, d), row), _full(wup.shape), _full(cv.shape), _full(wdn.shape), _full(g2.shape)],
        out_specs=pl.BlockSpec((tm, d), row),
        out_shape=jax.ShapeDtypeStruct((t, d), F32),
        scratch_shapes=[pltpu.VMEM((tm, d), F32), pltpu.VMEM((tm + 2 * HALO, d), BF16),
                        pltpu.VMEM((2, 2, tm + 2 * HALO, FFN_CHUNK), F32)],
        compiler_params=_cparams("parallel"),
        name="conv_ffn",
    )(h2, h2, h2, x1, wup, cv, wdn, g2)


def _axial_tables(n_tokens, dim):
    t = jnp.arange(n_tokens, dtype=jnp.int32)
    row = (t // GRID_W).astype(F32)
    col = (t % GRID_W).astype(F32)
    quarter = dim // 4
    inv_freq = ROPE_BASE ** (-jnp.arange(quarter, dtype=F32) / quarter)
    ang = jnp.concatenate([row[:, None] * inv_freq, col[:, None] * inv_freq], axis=-1)
    return jnp.cos(ang), jnp.sin(ang)


def _rope_tables_head64(t):
    cos, sin = _axial_tables(t, 64)
    return jnp.tile(jnp.concatenate([cos, cos], -1), (1, 2)), jnp.tile(jnp.concatenate([-sin, sin], -1), (1, 2))


def _rope_tables_latent(t):
    cos, sin = _axial_tables(t, C_ROPE)
    ones = jnp.ones((t, C_NOPE), F32)
    tail = jnp.ones((t, HEAD_PAD - C_NOPE - C_ROPE), F32)
    cosc = jnp.concatenate([ones, cos, cos, tail], -1)
    sinc = jnp.concatenate([0 * ones, -sin, sin, 0 * tail], -1)
    return cosc, sinc


def _group_ones(width, group):
    return jnp.asarray(np.kron(np.eye(width // group), np.ones((group, group))), BF16)


def _pad_heads(a, heads, dim):
    a = a.reshape(a.shape[:-1] + (heads, dim))
    a = jnp.pad(a, [(0, 0)] * (a.ndim - 1) + [(0, HEAD_PAD - dim)])
    return a.reshape(a.shape[:-2] + (heads * HEAD_PAD,))


def _pad_row(v, width):
    return jnp.pad(v, (0, width - v.shape[0]))


def _ffn_weights(w_up, conv_w, conv_b, w_down):
    d, f2 = w_up.shape
    nc2 = f2 // FFN_CHUNK
    wup = w_up.astype(BF16).reshape(d, nc2, FFN_CHUNK).transpose(1, 0, 2)
    cv = jnp.concatenate([conv_w, conv_b[None]], axis=0).reshape(CONV_W + 1, nc2, FFN_CHUNK).transpose(1, 0, 2)
    wdn = w_down.astype(BF16).reshape(nc2 // 2, FFN_CHUNK, d)
    return wup, cv, wdn


def kernel(x, c, ctx, c_ctx, ada_w, ada_b, norm_mix, norm_ffn, ffn_up, ffn_conv_w, ffn_conv_b, ffn_down, ev_w_in, ev_w_out, a_q_norm, a_k_norm, a_lam_q1, a_lam_k1, a_lam_q2, a_lam_k2, a_subln, b_q_norm, b_k_norm, b_rpb, od_w_in, od_w_out, c_q_a_norm, c_w_qb, c_kv_a_norm, c_w_kvb, c_q_norm, c_k_norm, d_q_norm, d_k_norm, d_sink):
    assert x.shape[0] == 1 and ctx.shape[0] == 1
    xm = x[0]
    xc = ctx[0]
    t, d = xm.shape
    lc = xc.shape[0]
    mods = _modulation(c, c_ctx, ada_w, ada_b)
    cos64, sin64 = _rope_tables_head64(t)
    cosc, sinc = _rope_tables_latent(t)
    g64 = _group_ones(512, 64)
    g128 = _group_ones(C_HEADS * HEAD_PAD, HEAD_PAD)
    dummy_tab = jnp.zeros((lc, LANES), F32)

    for l in range(DEPTH):
        last = l == DEPTH - 1
        i = l // 2
        mm = mods[l, 0].reshape(6, d)
        mc = mods[l, 1].reshape(6, d)
        pre_m = jnp.stack([norm_mix[l], mm[1], mm[0]])
        pre_c = jnp.stack([norm_mix[l], mc[1], mc[0]])
        post_m = jnp.stack([mm[2], norm_ffn[l], mm[4], mm[3]])
        post_c = jnp.stack([mc[2], norm_ffn[l], mc[4], mc[3]])
        wup, cv, wdn = _ffn_weights(ffn_up[l], ffn_conv_w[l], ffn_conv_b[l], ffn_down[l])
        if l % 2 == 0:
            lam_init = 0.8 - 0.6 * math.exp(-0.3 * l)
            w_in = ev_w_in[i].astype(BF16)
            w_out = ev_w_out[i].astype(BF16)
            hg = jnp.stack([jnp.tile(a_q_norm[i], 8), jnp.tile(a_k_norm[i], 8),
                            jnp.tile(b_q_norm[i], 8), jnp.tile(b_k_norm[i], 8)])
            ex = jnp.stack([_pad_row(a_lam_q1[i], LANES), _pad_row(a_lam_k1[i], LANES),
                            _pad_row(a_lam_q2[i], LANES), _pad_row(a_lam_k2[i], LANES)])
            gcol = a_subln[i][:, None]
            qa, ka, va, qb, kb, vb = _premix_even(xm, pre_m, w_in, g64, hg, cos64, sin64, True)
            qa_c, ka_c, va_c, qb_c, kb_c, vb_c = _premix_even(xc, pre_c, w_in, g64, hg, dummy_tab, dummy_tab, False)
            oa = _flash(qa, ka, va, ka_c, va_c, ex, gcol, mode="diff", lam_init=lam_init)
            ob = _neighbourhood(qb, kb, vb, kb_c, vb_c, _na_bias_table(b_rpb[i], t // GRID_W, lc))
            if not last:
                oa_c = _flash(qa_c, ka_c, va_c, None, None, ex, gcol, mode="diff", lam_init=lam_init)
                ob_c = _flash(qb_c, kb_c, vb_c, None, None, ex, gcol, mode="pair")
        else:
            w = od_w_in[i]
            cq, ckv, ckr, dq, dk, dv = jnp.split(w, np.cumsum([C_Q_LORA, C_KV_LORA, C_ROPE, 512, 128]).tolist(), axis=1)
            dup = lambda a: jnp.concatenate([a[:, :64], a[:, :64], a[:, 64:], a[:, 64:]], axis=1)
            ckr_pad = jnp.pad(ckr, ((0, 0), (C_NOPE, HEAD_PAD - C_NOPE - C_ROPE)))
            w_in = jnp.concatenate([cq, ckv, dq, dup(dk), dup(dv), ckr_pad], axis=1).astype(BF16)
            w_out = od_w_out[i].astype(BF16)
            wqb = _pad_heads(c_w_qb[i], C_HEADS, C_NOPE + C_ROPE).astype(BF16)
            kvb = c_w_kvb[i].reshape(C_KV_LORA, C_HEADS, C_NOPE + C_DV)
            wk = _pad_heads(kvb[:, :, :C_NOPE].reshape(C_KV_LORA, -1), C_HEADS, C_NOPE).astype(BF16)
            wv = kvb[:, :, C_NOPE:].reshape(C_KV_LORA, C_HEADS * C_DV).astype(BF16)
            gains = jnp.stack([jnp.tile(_pad_row(c_q_norm[i], HEAD_PAD), C_HEADS),
                               jnp.tile(_pad_row(c_k_norm[i], HEAD_PAD), C_HEADS),
                               _pad_row(jnp.tile(d_q_norm[i], 8), 1024), _pad_row(jnp.tile(d_k_norm[i], 4), 1024)])
            lora = jnp.stack([_pad_row(c_q_a_norm[i], 512), _pad_row(c_kv_a_norm[i], 512)])
            odd = functools.partial(_premix_odd, w=w_in, wqb=wqb, wk=wk, wv=wv, g128=g128, g64=g64, gains=gains, lora=lora)
            mq, mk, mv, gq, gk, gv = odd(xm, pre_m, cos=cos64, sin=sin64, cosc=cosc, sinc=sinc, use_rope=True)
            mq_c, mk_c, mv_c, gq_c, gk_c, gv_c = odd(xc, pre_c, cos=dummy_tab, sin=dummy_tab, cosc=dummy_tab,
                                                     sinc=dummy_tab, use_rope=False)
            ex = jnp.zeros((4, LANES), F32)
            oa = _flash(mq, mk, mv, mk_c, mv_c, ex, jnp.ones((HEAD_PAD, 1), F32), mode="mla")
            ob = _window_gqa(gq, gk, gv, gk_c, gv_c, d_sink[i])
            if not last:
                raise NotImplementedError("context update after an odd layer is not needed at this depth")
        x1, h2 = _postmix(oa, ob, xm, w_out, post_m)
        xm = _conv_ffn(h2, x1, wup, cv, wdn, mm[5][None])
        if not last:
            xc1, hc2 = _postmix(oa_c, ob_c, xc, w_out, post_c)
            xc = _conv_ffn(hc2, xc1, wup, cv, wdn, mc[5][None])
    return xm[None]
```

```python
import functools
import math

import numpy as np
import jax
import jax.numpy as jnp
from jax import lax
from jax.experimental import pallas as pl
from jax.experimental.pallas import tpu as pltpu

D_MODEL = 1024
DEPTH = 2
GRID_W = 64
EPS = 1e-6
ROPE_BASE = 10000.0
NEG_INF = -1e30
LOG2E = math.log2(math.e)

A_HEADS = 4
A_DQK = 64
A_DV = 128
B_HEADS = 8
B_DH = 64
NA_WIN_H = 8
NA_WIN_W = 16
C_HEADS = 8
C_Q_LORA = 384
C_KV_LORA = 256
C_NOPE = 64
C_ROPE = 32
C_DV = 64
D_HEADS = 8
D_KV_HEADS = 2
D_DH = 64
D_WINDOW = 128
FFN_HIDDEN = 2816
CONV_W = 3

LANES = 128
MXU_TILE = 256
HEAD_PAD = 128
VMEM_LIMIT = 56 * 1024 * 1024
FFN_CHUNK = 256
HALO = 16
NA_ROWS = 4
ONES_ROWS = 16
FLASH_TQ = 1024
FLASH_TK = 512
FLASH_UNROLL = 4
FLASH_SAFE_BOUND = 40.0

F32 = jnp.float32
BF16 = jnp.bfloat16


def _cparams(*sem):
    return pltpu.CompilerParams(dimension_semantics=sem, vmem_limit_bytes=VMEM_LIMIT)


def _full(shape):
    n = len(shape)
    return pl.BlockSpec(shape, lambda *_: (0,) * n)


def _dot(a, b):
    return jnp.dot(a, b, preferred_element_type=F32)


def _dot_nt(a, b):
    return lax.dot_general(a, b, (((1,), (1,)), ((), ())), preferred_element_type=F32)


def _rms(x):
    return x * lax.rsqrt(jnp.mean(x * x, axis=-1, keepdims=True) + EPS)


def _sigmoid(x):
    return 1.0 / (1.0 + jnp.exp(-x))


def _group_rms(y, gmat, inv_n, gain):
    yy = (y * y).astype(BF16)
    slab = gmat.shape[0]
    ss = jnp.concatenate([_dot(yy[:, c:c + slab], gmat) for c in range(0, y.shape[-1], slab)], axis=-1)
    return y * lax.rsqrt(ss * inv_n + EPS) * gain


def _rope(y, cos, sin, half, first_half):
    w = y.shape[-1]
    fwd = pltpu.roll(y, w - half, axis=1)
    bwd = pltpu.roll(y, half, axis=1)
    return y * cos + jnp.where(first_half, fwd, bwd) * sin


def _lane_tile(t, width):
    reps = width // t.shape[-1]
    return t if reps == 1 else jnp.concatenate([t] * reps, axis=-1)


def _lane_iota(width):
    return lax.broadcasted_iota(jnp.int32, (1, width), 1)


def _mod_kernel(ct_ref, w_ref, b_ref, o_ref):
    ct = ct_ref[...]
    s = ct * _sigmoid(ct)
    w = w_ref[0]
    rows = [jnp.sum(w * s[:, j:j + 1], axis=0, keepdims=True) for j in range(2)]
    o_ref[0] = jnp.concatenate(rows, axis=0) + b_ref[0]


def _modulation(c, c_ctx, ada_w, ada_b):
    depth, d, n = ada_w.shape
    tn = 1536
    ct = jnp.stack([c[0], c_ctx], axis=1)
    return pl.pallas_call(
        _mod_kernel,
        grid=(depth, n // tn),
        in_specs=[_full((d, 2)),
                  pl.BlockSpec((1, d, tn), lambda l, j: (l, 0, j)),
                  pl.BlockSpec((1, 1, tn), lambda l, j: (l, 0, j))],
        out_specs=pl.BlockSpec((1, 2, tn), lambda l, j: (l, 0, j)),
        out_shape=jax.ShapeDtypeStruct((depth, 2, n), F32),
        compiler_params=_cparams("arbitrary", "arbitrary"),
        name="modulation",
    )(ct, ada_w, ada_b.reshape(depth, 1, n))


def _premix_even_kernel(x_ref, vec_ref, w_ref, g_ref, hg_ref, cos_ref, sin_ref,
                        qa_ref, ka_ref, va_ref, qb_ref, kb_ref, vb_ref, *, use_rope):
    vec = vec_ref[...]
    h = (_rms(x_ref[...]) * vec[0:1] * (1.0 + vec[1:2]) + vec[2:3]).astype(BF16)
    gm = g_ref[...]
    hg = hg_ref[...]
    sw = 512
    if use_rope:
        cos = _lane_tile(cos_ref[...], sw)
        sin = _lane_tile(sin_ref[...], sw)
        first = (_lane_iota(sw) & 32) == 0

    def seg(i):
        return _dot(h, w_ref[:, i * sw:(i + 1) * sw])

    qa = _group_rms(seg(0), gm, 1.0 / A_DQK, hg[0:1])
    ka = _group_rms(seg(1), gm, 1.0 / A_DQK, hg[1:2])
    if use_rope:
        qa = _rope(qa, cos, sin, 32, first)
        ka = _rope(ka, cos, sin, 32, first)
    qa_ref[...] = (qa * (LOG2E / math.sqrt(A_DQK))).astype(BF16)
    ka_ref[...] = ka.astype(BF16)
    va_ref[...] = seg(2).astype(BF16)
    qb = _group_rms(seg(3), gm, 1.0 / B_DH, hg[2:3])
    qb_ref[...] = (qb * (LOG2E / math.sqrt(B_DH))).astype(BF16)
    kb_ref[...] = _group_rms(seg(4), gm, 1.0 / B_DH, hg[3:4]).astype(BF16)
    vb_ref[...] = seg(5).astype(BF16)


def _premix_even(x, vec, w, gmat, hg, cos, sin, use_rope):
    t, d = x.shape
    tm = min(512, t)
    row = lambda i: (i, 0)
    outs = [jax.ShapeDtypeStruct((t, 512), BF16)] * 6
    return pl.pallas_call(
        functools.partial(_premix_even_kernel, use_rope=use_rope),
        grid=(t // tm,),
        in_specs=[pl.BlockSpec((tm, d), row), _full(vec.shape), _full(w.shape), _full(gmat.shape),
                  _full(hg.shape), pl.BlockSpec((tm, LANES), row), pl.BlockSpec((tm, LANES), row)],
        out_specs=[pl.BlockSpec((tm, 512), row)] * 6,
        out_shape=outs,
        compiler_params=_cparams("parallel"),
        name="premix_even",
    )(x, vec, w, gmat, hg, cos, sin)


_O_CQ, _O_CKV, _O_DQ, _O_DK, _O_DV, _O_CKR, _O_END = 0, 384, 640, 1152, 1408, 1664, 1792


def _premix_odd_kernel(x_ref, vec_ref, w_ref, wqb_ref, wk_ref, wv_ref, g128_ref, g64_ref, gain_ref, lora_ref,
                       cos_ref, sin_ref, cosc_ref, sinc_ref,
                       mq_ref, mk_ref, mv_ref, gq_ref, gk_ref, gv_ref, *, use_rope):
    vec = vec_ref[...]
    h = (_rms(x_ref[...]) * vec[0:1] * (1.0 + vec[1:2]) + vec[2:3]).astype(BF16)
    gains = gain_ref[...]
    lora = lora_ref[...]
    g128 = g128_ref[...]
    g64 = g64_ref[...]
    mw = C_HEADS * HEAD_PAD
    if use_rope:
        cosc = _lane_tile(cosc_ref[...], mw)
        sinc = _lane_tile(sinc_ref[...], mw)
        first_c = (_lane_iota(mw) & 127) < (C_NOPE + C_ROPE // 2)
        cos = _lane_tile(cos_ref[...], 512)
        sin = _lane_tile(sin_ref[...], 512)
        first_d = (_lane_iota(512) & 32) == 0

    def seg(a, b):
        return _dot(h, w_ref[:, a:b])

    inv_c = 1.0 / (C_NOPE + C_ROPE)
    cq = (_rms(seg(_O_CQ, _O_CKV)) * lora[0:1, :C_Q_LORA]).astype(BF16)
    mq = _group_rms(_dot(cq, wqb_ref[...]), g128, inv_c, gains[0:1])
    if use_rope:
        mq = _rope(mq, cosc, sinc, C_ROPE // 2, first_c)
    mq_ref[...] = (mq * (LOG2E * math.sqrt(inv_c))).astype(BF16)
    ckv = (_rms(seg(_O_CKV, _O_DQ)) * lora[1:2, :C_KV_LORA]).astype(BF16)
    mk = _dot(ckv, wk_ref[...]) + _lane_tile(seg(_O_CKR, _O_END), mw)
    mk = _group_rms(mk, g128, inv_c, gains[1:2])
    if use_rope:
        mk = _rope(mk, cosc, sinc, C_ROPE // 2, first_c)
    mk_ref[...] = mk.astype(BF16)
    mv_ref[...] = _dot(ckv, wv_ref[...]).astype(BF16)
    gq = _group_rms(seg(_O_DQ, _O_DK), g64, 1.0 / D_DH, gains[2:3, :512])
    gk = _group_rms(seg(_O_DK, _O_DV), g64, 1.0 / D_DH, gains[3:4, :256])
    if use_rope:
        gq = _rope(gq, cos, sin, 32, first_d)
        gk = _rope(gk, cos[:, :256], sin[:, :256], 32, first_d[:, :256])
    gq_ref[...] = (gq * (LOG2E / math.sqrt(D_DH))).astype(BF16)
    gk_ref[...] = gk.astype(BF16)
    gv_ref[...] = seg(_O_DV, _O_CKR).astype(BF16)


def _premix_odd(x, vec, w, wqb, wk, wv, g128, g64, gains, lora, cos, sin, cosc, sinc, use_rope):
    t, d = x.shape
    tm = min(512, t)
    row = lambda i: (i, 0)
    tab = pl.BlockSpec((tm, LANES), row)
    widths = (1024, 1024, 512, 512, 256, 256)
    return pl.pallas_call(
        functools.partial(_premix_odd_kernel, use_rope=use_rope),
        grid=(t // tm,),
        in_specs=[pl.BlockSpec((tm, d), row)] + [_full(a.shape) for a in (vec, w, wqb, wk, wv, g128, g64, gains, lora)]
                 + [tab] * 4,
        out_specs=[pl.BlockSpec((tm, n), row) for n in widths],
        out_shape=[jax.ShapeDtypeStruct((t, n), BF16) for n in widths],
        compiler_params=_cparams("parallel"),
        name="premix_odd",
    )(x, vec, w, wqb, wk, wv, g128, g64, gains, lora, cos, sin, cosc, sinc)


def _flash_kernel(*refs, mode, has_ctx, online, lam_init):
    bound_ref, qt_ref, k_ref, vt_ref = refs[:4]
    kc_ref, vct_ref = refs[4:6] if has_ctx else (None, None)
    ex_ref, gcol_ref, o_ref, acc_ref, stage_ref = refs[6:11] if has_ctx else refs[4:9]
    qt = qt_ref[...]
    row = lax.broadcasted_iota(jnp.int32, (HEAD_PAD, 1), 0)
    if mode == "mla":
        qs = [qt[:HEAD_PAD], qt[HEAD_PAD:]]
    else:
        zero = jnp.zeros_like(qt)
        qs = [jnp.where(row < 64, qt, zero), jnp.where(row >= 64, qt, zero)]
    nk, tk = vt_ref.shape[1], vt_ref.shape[3]

    acc_ref[...] = jnp.zeros(acc_ref.shape, F32)
    rows_b = acc_ref.shape[1]

    def scores(kblk, b):
        kb = kblk[:, b * HEAD_PAD:(b + 1) * HEAD_PAD] if mode == "mla" else kblk
        return _dot(kb, qs[b])

    def values(vtblk, b):
        return vtblk if vtblk.shape[0] == rows_b else vtblk[b * rows_b:(b + 1) * rows_b]

    if online:
        m_ref, mx_ref = refs[-2:]
        m_ref[...] = jnp.full(m_ref.shape, NEG_INF, F32)

        def absorb(s, smax, vtblk, b):
            m_prev = m_ref[b]
            m_new = jnp.maximum(m_prev, smax)
            alpha = jnp.exp2(m_prev - m_new)
            p = jnp.exp2(s - m_new).astype(BF16)
            acc_ref[b] = alpha * acc_ref[b] + _dot(values(vtblk, b), p)
            m_ref[b] = m_new

        def produce(slot, kblk):
            for b in range(2):
                s = scores(kblk, b)
                stage_ref[slot, b] = s
                mx_ref[slot, b] = jnp.max(s, axis=0, keepdims=True)

        def consume(slot, vtblk):
            for b in range(2):
                absorb(stage_ref[slot, b], mx_ref[slot, b], vtblk, b)

        def direct(kblk, vtblk):
            for b in range(2):
                s = scores(kblk, b)
                absorb(s, jnp.max(s, axis=0, keepdims=True), vtblk, b)
    else:
        bound = bound_ref[0]

        def probs(kblk, b):
            return jnp.exp2(scores(kblk, b) - bound).astype(BF16)

        def produce(slot, kblk):
            for b in range(2):
                stage_ref[slot, b] = probs(kblk, b)

        def consume(slot, vtblk):
            for b in range(2):
                acc_ref[b] += _dot(values(vtblk, b), stage_ref[slot, b])

        def direct(kblk, vtblk):
            for b in range(2):
                acc_ref[b] += _dot(values(vtblk, b), probs(kblk, b))

    if nk == 1:
        if has_ctx:
            direct(kc_ref[...], vct_ref[0])
        direct(k_ref[...], vt_ref[0, 0])
    else:
        unroll = FLASH_UNROLL if nk % FLASH_UNROLL == 0 else 2
        assert nk % unroll == 0
        produce(0, k_ref[0:tk, :])
        if has_ctx:
            direct(kc_ref[...], vct_ref[0])

        def body(i, carry):
            c0 = unroll * i
            for u in range(unroll):
                nxt = jnp.minimum(c0 + u + 1, nk - 1)
                produce((u + 1) % 2, k_ref[pl.ds(pl.multiple_of(nxt * tk, tk), tk), :])
                consume(u % 2, vt_ref[0, c0 + u])
            return carry

        lax.fori_loop(0, nk // unroll, body, 0)

    dv = rows_b - ONES_ROWS
    o0 = acc_ref[0, :dv] / acc_ref[0, dv:dv + 1]
    o1 = acc_ref[1, :dv] / acc_ref[1, dv:dv + 1]
    if mode == "diff":
        ex = ex_ref[...]
        lam = (jnp.exp(jnp.sum(ex[0:1] * ex[1:2], axis=-1, keepdims=True))
               - jnp.exp(jnp.sum(ex[2:3] * ex[3:4], axis=-1, keepdims=True)) + lam_init)
        o = o0 - lam * o1
        o = o * lax.rsqrt(jnp.mean(o * o, axis=0, keepdims=True) + EPS) * gcol_ref[...] * (1.0 - lam_init)
    else:
        o = jnp.concatenate([o0, o1], axis=0)
    o_ref[...] = o.astype(o_ref.dtype)


def _flash(q, k, v, kc, vc, ex, gcol, bound=None, *, mode, lam_init=0.0):
    t = q.shape[0]
    tkeys = k.shape[0]
    qw = 2 * HEAD_PAD if mode == "mla" else HEAD_PAD
    groups = q.shape[1] // qw
    tq = min(FLASH_TQ, t)
    tk = min(FLASH_TK, tkeys)
    nk = tkeys // tk
    has_ctx = kc is not None
    dv = HEAD_PAD if mode == "diff" else HEAD_PAD // 2
    rows_b = dv + ONES_ROWS

    def with_ones(vt):
        ones = jnp.ones(vt.shape[:-2] + (ONES_ROWS, vt.shape[-1]), BF16)
        parts = [vt, ones] if dv == HEAD_PAD else [vt[..., :dv, :], ones, vt[..., dv:, :], ones]
        return jnp.concatenate(parts, axis=-2)

    qt = q.T
    vt = with_ones(v.reshape(nk, tk, groups, HEAD_PAD).transpose(2, 0, 3, 1))
    vrows = vt.shape[2]
    bound_arr = jnp.zeros((1,), F32) if bound is None else jnp.reshape(bound, (1,)).astype(F32)
    in_specs = [pl.BlockSpec(memory_space=pltpu.SMEM),
                pl.BlockSpec((qw, tq), lambda g, i: (g, i)),
                pl.BlockSpec((tkeys, qw), lambda g, i: (0, g)),
                pl.BlockSpec((1, nk, vrows, tk), lambda g, i: (g, 0, 0, 0))]
    args = [bound_arr, qt, k, vt]
    if has_ctx:
        lc = kc.shape[0]
        in_specs += [pl.BlockSpec((lc, qw), lambda g, i: (0, g)),
                     pl.BlockSpec((1, vrows, lc), lambda g, i: (g, 0, 0))]
        args += [kc, with_ones(vc.reshape(lc, groups, HEAD_PAD).transpose(1, 2, 0))]
    in_specs += [_full(ex.shape), _full(gcol.shape)]
    args += [ex, gcol]

    def call(online):
        acc = pltpu.VMEM((2, rows_b, tq), F32)
        if online:
            scratch = [acc, pltpu.VMEM((2, 2, tk, tq), F32), pltpu.VMEM((2, 1, tq), F32), pltpu.VMEM((2, 2, 1, tq), F32)]
        else:
            scratch = [acc, pltpu.VMEM((2, 2, tk, tq), BF16)]
        return pl.pallas_call(
            functools.partial(_flash_kernel, mode=mode, has_ctx=has_ctx, online=online, lam_init=lam_init),
            grid=(groups, t // tq),
            in_specs=in_specs,
            out_specs=pl.BlockSpec((HEAD_PAD, tq), lambda g, i: (g, i)),
            out_shape=jax.ShapeDtypeStruct((groups * HEAD_PAD, t), BF16),
            scratch_shapes=scratch,
            compiler_params=_cparams("parallel", "parallel"),
            name="flash_" + mode + ("_online" if online else "_bounded"),
        )(*args)

    if bound is None:
        ot = call(True)
    else:
        ot = lax.cond(bound <= FLASH_SAFE_BOUND, lambda: call(False), lambda: call(True))
    return ot.T


def _na_kernel(q_ref, kp_ref, kcur_ref, kn_ref, vp_ref, vcur_ref, vn_ref, kc_ref, vc_ref, bias_ref, o_ref):
    q = q_ref[...]
    lane = _lane_iota(LANES)
    zero = jnp.zeros_like(q)
    kk = jnp.concatenate([kc_ref[...], kp_ref[...], kcur_ref[...], kn_ref[...]], axis=0)
    vv = jnp.concatenate([vc_ref[...], vp_ref[...], vcur_ref[...], vn_ref[...]], axis=0)
    outs = []
    for b in range(2):
        qb = jnp.where(lane < 64, q, zero) if b == 0 else jnp.where(lane >= 64, q, zero)
        s = _dot_nt(qb, kk) + bias_ref[0, b]
        m = jnp.max(s, axis=-1, keepdims=True)
        p = jnp.exp2(s - m)
        l = jnp.sum(p, axis=-1, keepdims=True)
        outs.append(_dot(p.astype(BF16), vv) / l)
    o_ref[...] = jnp.where(lane < 64, outs[0], outs[1]).astype(o_ref.dtype)


def _na_bias_table(rpb, rows, lc):
    nr, w = NA_ROWS, GRID_W
    heads = rpb.shape[0]
    pad = w - NA_WIN_W
    padded = jnp.pad(rpb.astype(F32) * LOG2E, ((0, 0), (0, 0), (pad, pad)))
    toep = jnp.stack([padded[:, :, w - 1 - c:2 * w - 1 - c] for c in range(w)], axis=2)
    cq = np.arange(w)
    cs = np.clip(cq - NA_WIN_W // 2, 0, w - NA_WIN_W)
    valid_c = (cq[None, :] >= cs[:, None]) & (cq[None, :] < cs[:, None] + NA_WIN_W)
    toep = jnp.where(jnp.asarray(valid_c), toep, NEG_INF)
    wh = min(NA_WIN_H, rows)
    rl = np.arange(nr)
    rr_rel = np.arange(3 * nr)
    dr_all, valid_all = [], []
    for base in (0, nr, rows - nr):
        r = base + rl
        rs = np.clip(r - wh // 2, 0, rows - wh)
        rr = base - nr + rr_rel
        valid_all.append((rr[None, :] >= rs[:, None]) & (rr[None, :] < rs[:, None] + wh))
        dr_all.append(np.clip(rr[None, :] - r[:, None] + (NA_WIN_H - 1), 0, 2 * NA_WIN_H - 2))
    dr_idx = np.stack(dr_all).reshape(-1)
    valid_r = np.stack(valid_all)
    blocks = jnp.take(toep, jnp.asarray(dr_idx, jnp.int32), axis=1).reshape(heads, 3, nr, 3 * nr, w, w)
    blocks = jnp.where(jnp.asarray(valid_r)[None, :, :, :, None, None], blocks, NEG_INF)
    tab = blocks.transpose(1, 0, 2, 4, 3, 5).reshape(3, heads, nr * w, 3 * nr * w)
    return jnp.concatenate([jnp.zeros(tab.shape[:3] + (lc,), F32), tab], axis=-1)


def _neighbourhood(q, k, v, kc, vc, bias):
    t = q.shape[0]
    qn = NA_ROWS * GRID_W
    nb = t // qn
    lc = kc.shape[0]
    pairs = q.shape[1] // HEAD_PAD
    cur = lambda p, i: (i, p)
    prev = lambda p, i: (jnp.maximum(i - 1, 0), p)
    nxt = lambda p, i: (jnp.minimum(i + 1, nb - 1), p)
    blk = lambda f: pl.BlockSpec((qn, HEAD_PAD), f)
    ctxs = pl.BlockSpec((lc, HEAD_PAD), lambda p, i: (0, p))
    case = lambda p, i: (jnp.where(i == 0, 0, jnp.where(i == nb - 1, 2, 1)), p, 0, 0)
    return pl.pallas_call(
        _na_kernel,
        grid=(pairs, nb),
        in_specs=[blk(cur), blk(prev), blk(cur), blk(nxt), blk(prev), blk(cur), blk(nxt), ctxs, ctxs,
                  pl.BlockSpec((1, 2, qn, lc + 3 * qn), case)],
        out_specs=blk(cur),
        out_shape=jax.ShapeDtypeStruct(q.shape, BF16),
        compiler_params=_cparams("parallel", "arbitrary"),
        name="neighbourhood",
    )(q, k, k, k, v, v, v, kc, vc, bias)


def _wgqa_kernel(sink_ref, q_ref, kp_ref, kcur_ref, kn_ref, vp_ref, vcur_ref, vn_ref, kc_ref, vc_ref, o_ref):
    i = pl.program_id(0)
    nb = pl.num_programs(0)
    qb = q_ref.shape[0]
    lc = kc_ref.shape[0]
    nkeys = lc + 3 * qb
    lane = _lane_iota(LANES)
    col = lax.broadcasted_iota(jnp.int32, (qb, nkeys), 1)
    row = lax.broadcasted_iota(jnp.int32, (qb, nkeys), 0)
    rel = col - (lc + qb) - row
    lo = jnp.where(i > 0, lc, lc + qb)
    hi = jnp.where(i < nb - 1, nkeys, lc + 2 * qb)
    valid = (col < lc) | ((jnp.abs(rel) <= D_WINDOW) & (col >= lo) & (col < hi))
    groups = D_HEADS // D_KV_HEADS
    for kv in range(D_KV_HEADS):
        ks = slice(kv * HEAD_PAD, (kv + 1) * HEAD_PAD)
        kk = jnp.concatenate([kc_ref[:, ks], kp_ref[:, ks], kcur_ref[:, ks], kn_ref[:, ks]], axis=0)
        vv = jnp.concatenate([vc_ref[:, ks], vp_ref[:, ks], vcur_ref[:, ks], vn_ref[:, ks]], axis=0)
        lhs = []
        for g in range(groups):
            h = kv * groups + g
            qp = q_ref[:, (h // 2) * HEAD_PAD:(h // 2 + 1) * HEAD_PAD]
            keep = (lane < 64) if h % 2 == 0 else (lane >= 64)
            lhs.append(jnp.where(keep, qp, jnp.zeros_like(qp)))
        s = _dot_nt(jnp.concatenate(lhs, axis=0), kk)
        outs = []
        for g in range(groups):
            sink = sink_ref[kv * groups + g] * LOG2E
            sg = jnp.where(valid, s[g * qb:(g + 1) * qb], NEG_INF)
            m = jnp.maximum(jnp.max(sg, axis=-1, keepdims=True), sink)
            p = jnp.exp2(sg - m)
            l = jnp.sum(p, axis=-1, keepdims=True) + jnp.exp2(sink - m)
            outs.append(_dot(p.astype(BF16), vv) / l)
        for pp in range(groups // 2):
            pair = kv * (groups // 2) + pp
            o_ref[:, pair * HEAD_PAD:(pair + 1) * HEAD_PAD] = jnp.where(
                lane < 64, outs[2 * pp], outs[2 * pp + 1]).astype(o_ref.dtype)


def _window_gqa(q, k, v, kc, vc, sink):
    t = q.shape[0]
    qb = D_WINDOW
    nb = t // qb
    lc = kc.shape[0]
    kw = k.shape[1]
    cur = lambda i: (i, 0)
    prev = lambda i: (jnp.maximum(i - 1, 0), 0)
    nxt = lambda i: (jnp.minimum(i + 1, nb - 1), 0)
    kblk = lambda f: pl.BlockSpec((qb, kw), f)
    return pl.pallas_call(
        _wgqa_kernel,
        grid=(nb,),
        in_specs=[pl.BlockSpec(memory_space=pltpu.SMEM), pl.BlockSpec((qb, q.shape[1]), cur),
                  kblk(prev), kblk(cur), kblk(nxt), kblk(prev), kblk(cur), kblk(nxt),
                  _full((lc, kw)), _full((lc, kw))],
        out_specs=pl.BlockSpec((qb, q.shape[1]), cur),
        out_shape=jax.ShapeDtypeStruct(q.shape, BF16),
        compiler_params=_cparams("parallel"),
        name="window_gqa",
    )(sink, q, k, k, k, v, v, v, kc, vc)


def _postmix_kernel(o1_ref, o2_ref, x_ref, w_ref, vec_ref, x1_ref, h2_ref):
    half = o1_ref.shape[1]
    y = _dot(o1_ref[...], w_ref[:half, :]) + _dot(o2_ref[...], w_ref[half:, :])
    vec = vec_ref[...]
    x1 = x_ref[...] + vec[0:1] * y
    x1_ref[...] = x1
    h2_ref[...] = (_rms(x1) * vec[1:2] * (1.0 + vec[2:3]) + vec[3:4]).astype(BF16)


def _postmix(o1, o2, x, w, vec):
    t, d = x.shape
    tm = min(512, t)
    row = lambda i: (i, 0)
    return pl.pallas_call(
        _postmix_kernel,
        grid=(t // tm,),
        in_specs=[pl.BlockSpec((tm, o1.shape[1]), row), pl.BlockSpec((tm, o2.shape[1]), row),
                  pl.BlockSpec((tm, d), row), _full(w.shape), _full(vec.shape)],
        out_specs=[pl.BlockSpec((tm, d), row)] * 2,
        out_shape=[jax.ShapeDtypeStruct((t, d), F32), jax.ShapeDtypeStruct((t, d), BF16)],
        compiler_params=_cparams("parallel"),
        name="postmix",
    )(o1, o2, x, w, vec)


def _ffn_kernel(h_ref, hp_ref, hn_ref, x1_ref, wup_ref, cv_ref, wdn_ref, g2_ref, o_ref, acc_ref, hcat_ref, u_ref):
    i = pl.program_id(0)
    n = pl.num_programs(0)
    tm = h_ref.shape[0]
    rows = tm + 2 * HALO
    nchunk = wdn_ref.shape[0]
    hcat_ref[0:HALO] = hp_ref[...]
    hcat_ref[HALO:HALO + tm] = h_ref[...]
    hcat_ref[HALO + tm:rows] = hn_ref[...]

    @pl.when(i == 0)
    def _():
        hcat_ref[0:HALO] = jnp.zeros((HALO, hcat_ref.shape[1]), BF16)

    @pl.when(i == n - 1)
    def _():
        hcat_ref[HALO + tm:rows] = jnp.zeros((HALO, hcat_ref.shape[1]), BF16)

    acc_ref[...] = jnp.zeros(acc_ref.shape, F32)

    def conv(u, cv):
        before = pltpu.roll(u, 1, axis=0)
        after = pltpu.roll(u, rows - 1, axis=0)
        r = cv[3:4] + before * cv[0:1] + u * cv[1:2] + after * cv[2:3]
        return r[HALO:HALO + tm]

    def stage_up(slot, c):
        hcat = hcat_ref[...]
        u_ref[slot, 0] = _dot(hcat, wup_ref[c])
        u_ref[slot, 1] = _dot(hcat, wup_ref[nchunk + c])

    def stage_down(slot, c):
        a = conv(u_ref[slot, 0], cv_ref[c])
        g = conv(u_ref[slot, 1], cv_ref[nchunk + c])
        act = (a * (g * _sigmoid(g))).astype(BF16)
        acc_ref[...] += _dot(act, wdn_ref[c])

    assert nchunk % 2 == 1
    stage_up(0, 0)

    def body(j, carry):
        c = 2 * j
        stage_up(1, c + 1)
        stage_down(0, c)
        stage_up(0, c + 2)
        stage_down(1, c + 1)
        return carry

    lax.fori_loop(0, nchunk // 2, body, 0)
    stage_down(0, nchunk - 1)
    o_ref[...] = x1_ref[...] + g2_ref[...] * acc_ref[...]


def _conv_ffn(h2, x1, wup, cv, wdn, g2):
    t, d = x1.shape
    tm = min(512, t)
    nt = t // tm
    hb = tm // HALO
    row = lambda i: (i, 0)
    prev = lambda i: (jnp.maximum(i * hb - 1, 0), 0)
    nxt = lambda i: (jnp.minimum((i + 1) * hb, t // HALO - 1), 0)
    return pl.pallas_call(
        _ffn_kernel,
        grid=(nt,),
        in_specs=[pl.BlockSpec((tm, d), row), pl.BlockSpec((HALO, d), prev), pl.BlockSpec((HALO, d), nxt),
                  pl.BlockSpec((tm, d), row), _full(wup.shape), _full(cv.shape), _full(wdn.shape), _full(g2.shape)],
        out_specs=pl.BlockSpec((tm, d), row),
        out_shape=jax.ShapeDtypeStruct((t, d), F32),
        scratch_shapes=[pltpu.VMEM((tm, d), F32), pltpu.VMEM((tm + 2 * HALO, d), BF16),
                        pltpu.VMEM((2, 2, tm + 2 * HALO, FFN_CHUNK), F32)],
        compiler_params=_cparams("parallel"),
        name="conv_ffn",
    )(h2, h2, h2, x1, wup, cv, wdn, g2)


def _axial_tables(n_tokens, dim):
    t = jnp.arange(n_tokens, dtype=jnp.int32)
    row = (t // GRID_W).astype(F32)
    col = (t % GRID_W).astype(F32)
    quarter = dim // 4
    inv_freq = ROPE_BASE ** (-jnp.arange(quarter, dtype=F32) / quarter)
    ang = jnp.concatenate([row[:, None] * inv_freq, col[:, None] * inv_freq], axis=-1)
    return jnp.cos(ang), jnp.sin(ang)


def _rope_tables_head64(t):
    cos, sin = _axial_tables(t, 64)
    return jnp.tile(jnp.concatenate([cos, cos], -1), (1, 2)), jnp.tile(jnp.concatenate([-sin, sin], -1), (1, 2))


def _rope_tables_latent(t):
    cos, sin = _axial_tables(t, C_ROPE)
    ones = jnp.ones((t, C_NOPE), F32)
    tail = jnp.ones((t, HEAD_PAD - C_NOPE - C_ROPE), F32)
    cosc = jnp.concatenate([ones, cos, cos, tail], -1)
    sinc = jnp.concatenate([0 * ones, -sin, sin, 0 * tail], -1)
    return cosc, sinc


def _group_ones(width, group):
    return jnp.asarray(np.kron(np.eye(width // group), np.ones((group, group))), BF16)


def _pad_heads(a, heads, dim):
    a = a.reshape(a.shape[:-1] + (heads, dim))
    a = jnp.pad(a, [(0, 0)] * (a.ndim - 1) + [(0, HEAD_PAD - dim)])
    return a.reshape(a.shape[:-2] + (heads * HEAD_PAD,))


def _logit_bound(gq, gk, dim):
    return 1.02 * LOG2E * math.sqrt(dim) * jnp.max(jnp.abs(gq)) * jnp.max(jnp.abs(gk))


def _pad_row(v, width):
    return jnp.pad(v, (0, width - v.shape[0]))


def _ffn_weights(w_up, conv_w, conv_b, w_down):
    d, f2 = w_up.shape
    nc2 = f2 // FFN_CHUNK
    wup = w_up.astype(BF16).reshape(d, nc2, FFN_CHUNK).transpose(1, 0, 2)
    cv = jnp.concatenate([conv_w, conv_b[None]], axis=0).reshape(CONV_W + 1, nc2, FFN_CHUNK).transpose(1, 0, 2)
    wdn = w_down.astype(BF16).reshape(nc2 // 2, FFN_CHUNK, d)
    return wup, cv, wdn


def kernel(x, c, ctx, c_ctx, ada_w, ada_b, norm_mix, norm_ffn, ffn_up, ffn_conv_w, ffn_conv_b, ffn_down, ev_w_in, ev_w_out, a_q_norm, a_k_norm, a_lam_q1, a_lam_k1, a_lam_q2, a_lam_k2, a_subln, b_q_norm, b_k_norm, b_rpb, od_w_in, od_w_out, c_q_a_norm, c_w_qb, c_kv_a_norm, c_w_kvb, c_q_norm, c_k_norm, d_q_norm, d_k_norm, d_sink):
    assert x.shape[0] == 1 and ctx.shape[0] == 1
    xm = x[0]
    xc = ctx[0]
    t, d = xm.shape
    lc = xc.shape[0]
    mods = _modulation(c, c_ctx, ada_w, ada_b)
    cos64, sin64 = _rope_tables_head64(t)
    cosc, sinc = _rope_tables_latent(t)
    g64 = _group_ones(MXU_TILE, 64)
    g128 = _group_ones(MXU_TILE, HEAD_PAD)
    dummy_tab = jnp.zeros((lc, LANES), F32)

    for l in range(DEPTH):
        last = l == DEPTH - 1
        i = l // 2
        mm = mods[l, 0].reshape(6, d)
        mc = mods[l, 1].reshape(6, d)
        pre_m = jnp.stack([norm_mix[l], mm[1], mm[0]])
        pre_c = jnp.stack([norm_mix[l], mc[1], mc[0]])
        post_m = jnp.stack([mm[2], norm_ffn[l], mm[4], mm[3]])
        post_c = jnp.stack([mc[2], norm_ffn[l], mc[4], mc[3]])
        wup, cv, wdn = _ffn_weights(ffn_up[l], ffn_conv_w[l], ffn_conv_b[l], ffn_down[l])
        if l % 2 == 0:
            lam_init = 0.8 - 0.6 * math.exp(-0.3 * l)
            w_in = ev_w_in[i].astype(BF16)
            w_out = ev_w_out[i].astype(BF16)
            hg = jnp.stack([jnp.tile(a_q_norm[i], 8), jnp.tile(a_k_norm[i], 8),
                            jnp.tile(b_q_norm[i], 8), jnp.tile(b_k_norm[i], 8)])
            ex = jnp.stack([_pad_row(a_lam_q1[i], LANES), _pad_row(a_lam_k1[i], LANES),
                            _pad_row(a_lam_q2[i], LANES), _pad_row(a_lam_k2[i], LANES)])
            gcol = a_subln[i][:, None]
            qa, ka, va, qb, kb, vb = _premix_even(xm, pre_m, w_in, g64, hg, cos64, sin64, True)
            qa_c, ka_c, va_c, qb_c, kb_c, vb_c = _premix_even(xc, pre_c, w_in, g64, hg, dummy_tab, dummy_tab, False)
            oa = _flash(qa, ka, va, ka_c, va_c, ex, gcol, _logit_bound(a_q_norm[i], a_k_norm[i], A_DQK),
                        mode="diff", lam_init=lam_init)
            ob = _neighbourhood(qb, kb, vb, kb_c, vb_c, _na_bias_table(b_rpb[i], t // GRID_W, lc))
            if not last:
                oa_c = _flash(qa_c, ka_c, va_c, None, None, ex, gcol, mode="diff", lam_init=lam_init)
                ob_c = _flash(qb_c, kb_c, vb_c, None, None, ex, gcol, mode="pair")
        else:
            w = od_w_in[i]
            cq, ckv, ckr, dq, dk, dv = jnp.split(w, np.cumsum([C_Q_LORA, C_KV_LORA, C_ROPE, 512, 128]).tolist(), axis=1)
            dup = lambda a: jnp.concatenate([a[:, :64], a[:, :64], a[:, 64:], a[:, 64:]], axis=1)
            ckr_pad = jnp.pad(ckr, ((0, 0), (C_NOPE, HEAD_PAD - C_NOPE - C_ROPE)))
            w_in = jnp.concatenate([cq, ckv, dq, dup(dk), dup(dv), ckr_pad], axis=1).astype(BF16)
            w_out = od_w_out[i].astype(BF16)
            wqb = _pad_heads(c_w_qb[i], C_HEADS, C_NOPE + C_ROPE).astype(BF16)
            kvb = c_w_kvb[i].reshape(C_KV_LORA, C_HEADS, C_NOPE + C_DV)
            wk = _pad_heads(kvb[:, :, :C_NOPE].reshape(C_KV_LORA, -1), C_HEADS, C_NOPE).astype(BF16)
            wv = kvb[:, :, C_NOPE:].reshape(C_KV_LORA, C_HEADS * C_DV).astype(BF16)
            gains = jnp.stack([jnp.tile(_pad_row(c_q_norm[i], HEAD_PAD), C_HEADS),
                               jnp.tile(_pad_row(c_k_norm[i], HEAD_PAD), C_HEADS),
                               _pad_row(jnp.tile(d_q_norm[i], 8), 1024), _pad_row(jnp.tile(d_k_norm[i], 4), 1024)])
            lora = jnp.stack([_pad_row(c_q_a_norm[i], 512), _pad_row(c_kv_a_norm[i], 512)])
            odd = functools.partial(_premix_odd, w=w_in, wqb=wqb, wk=wk, wv=wv, g128=g128, g64=g64, gains=gains, lora=lora)
            mq, mk, mv, gq, gk, gv = odd(xm, pre_m, cos=cos64, sin=sin64, cosc=cosc, sinc=sinc, use_rope=True)
            mq_c, mk_c, mv_c, gq_c, gk_c, gv_c = odd(xc, pre_c, cos=dummy_tab, sin=dummy_tab, cosc=dummy_tab,
                                                     sinc=dummy_tab, use_rope=False)
            ex = jnp.zeros((4, LANES), F32)
            oa = _flash(mq, mk, mv, mk_c, mv_c, ex, jnp.ones((HEAD_PAD, 1), F32),
                        _logit_bound(c_q_norm[i], c_k_norm[i], C_NOPE + C_ROPE), mode="mla")
            ob = _window_gqa(gq, gk, gv, gk_c, gv_c, d_sink[i])
            if not last:
                raise NotImplementedError("context update after an odd layer is not needed at this depth")
        x1, h2 = _postmix(oa, ob, xm, w_out, post_m)
        xm = _conv_ffn(h2, x1, wup, cv, wdn, mm[5][None])
        if not last:
            xc1, hc2 = _postmix(oa_c, ob_c, xc, w_out, post_c)
            xc = _conv_ffn(hc2, xc1, wup, cv, wdn, mc[5][None])
    return xm[None]
```

```python
import functools
import math

import numpy as np
import jax
import jax.numpy as jnp
from jax import lax
from jax.experimental import pallas as pl
from jax.experimental.pallas import tpu as pltpu

D_MODEL = 1024
DEPTH = 2
GRID_W = 64
EPS = 1e-6
ROPE_BASE = 10000.0
NEG_INF = -1e30
LOG2E = math.log2(math.e)

A_HEADS = 4
A_DQK = 64
A_DV = 128
B_HEADS = 8
B_DH = 64
NA_WIN_H = 8
NA_WIN_W = 16
C_HEADS = 8
C_Q_LORA = 384
C_KV_LORA = 256
C_NOPE = 64
C_ROPE = 32
C_DV = 64
D_HEADS = 8
D_KV_HEADS = 2
D_DH = 64
D_WINDOW = 128
FFN_HIDDEN = 2816
CONV_W = 3

LANES = 128
MXU_TILE = 256
HEAD_PAD = 128
VMEM_LIMIT = 56 * 1024 * 1024
FFN_CHUNK = 256
HALO = 16
NA_ROWS = 4
ONES_ROWS = 16
FLASH_TQ = 1024
FLASH_TK = 512
FLASH_UNROLL = 4
FLASH_SAFE_BOUND = 40.0

F32 = jnp.float32
BF16 = jnp.bfloat16


def _cparams(*sem):
    return pltpu.CompilerParams(dimension_semantics=sem, vmem_limit_bytes=VMEM_LIMIT)


def _full(shape):
    n = len(shape)
    return pl.BlockSpec(shape, lambda *_: (0,) * n)


def _dot(a, b):
    return jnp.dot(a, b, preferred_element_type=F32)


def _dot_nt(a, b):
    return lax.dot_general(a, b, (((1,), (1,)), ((), ())), preferred_element_type=F32)


def _rms(x):
    return x * lax.rsqrt(jnp.mean(x * x, axis=-1, keepdims=True) + EPS)


def _sigmoid(x):
    return 1.0 / (1.0 + jnp.exp(-x))


def _group_rms(y, gmat, inv_n, gain):
    yy = (y * y).astype(BF16)
    slab = gmat.shape[0]
    ss = jnp.concatenate([_dot(yy[:, c:c + slab], gmat) for c in range(0, y.shape[-1], slab)], axis=-1)
    return y * lax.rsqrt(ss * inv_n + EPS) * gain


def _rope(y, cos, sin, half, first_half):
    w = y.shape[-1]
    fwd = pltpu.roll(y, w - half, axis=1)
    bwd = pltpu.roll(y, half, axis=1)
    return y * cos + jnp.where(first_half, fwd, bwd) * sin


def _lane_tile(t, width):
    reps = width // t.shape[-1]
    return t if reps == 1 else jnp.concatenate([t] * reps, axis=-1)


def _lane_iota(width):
    return lax.broadcasted_iota(jnp.int32, (1, width), 1)


def _mod_kernel(ct_ref, w_ref, b_ref, o_ref):
    ct = ct_ref[...]
    s = ct * _sigmoid(ct)
    w = w_ref[0]
    rows = [jnp.sum(w * s[:, j:j + 1], axis=0, keepdims=True) for j in range(2)]
    o_ref[0] = jnp.concatenate(rows, axis=0) + b_ref[0]


def _modulation(c, c_ctx, ada_w, ada_b):
    depth, d, n = ada_w.shape
    tn = 1536
    ct = jnp.stack([c[0], c_ctx], axis=1)
    return pl.pallas_call(
        _mod_kernel,
        grid=(depth, n // tn),
        in_specs=[_full((d, 2)),
                  pl.BlockSpec((1, d, tn), lambda l, j: (l, 0, j)),
                  pl.BlockSpec((1, 1, tn), lambda l, j: (l, 0, j))],
        out_specs=pl.BlockSpec((1, 2, tn), lambda l, j: (l, 0, j)),
        out_shape=jax.ShapeDtypeStruct((depth, 2, n), F32),
        compiler_params=_cparams("arbitrary", "arbitrary"),
        name="modulation",
    )(ct, ada_w, ada_b.reshape(depth, 1, n))


def _store_values_t(vt_ref, v, dv):
    tm = v.shape[0]
    vt = v.T.astype(BF16)
    ones = jnp.ones((ONES_ROWS, tm), BF16)
    step = dv + ONES_ROWS
    for g in range(vt_ref.shape[0]):
        for h in range(HEAD_PAD // dv):
            src = g * HEAD_PAD + h * dv
            vt_ref[g, 0, h * step:h * step + dv, :] = vt[src:src + dv]
            vt_ref[g, 0, h * step + dv:(h + 1) * step, :] = ones


def _values_t_shape(t, tm, dv):
    return (4, t // tm, (HEAD_PAD // dv) * (dv + ONES_ROWS), tm)


def _premix_even_kernel(x_ref, vec_ref, w_ref, g_ref, hg_ref, cos_ref, sin_ref,
                        qa_ref, ka_ref, vat_ref, qb_ref, kb_ref, vb_ref, *, use_rope):
    vec = vec_ref[...]
    h = (_rms(x_ref[...]) * vec[0:1] * (1.0 + vec[1:2]) + vec[2:3]).astype(BF16)
    gm = g_ref[...]
    hg = hg_ref[...]
    sw = 512
    if use_rope:
        cos = _lane_tile(cos_ref[...], sw)
        sin = _lane_tile(sin_ref[...], sw)
        first = (_lane_iota(sw) & 32) == 0

    def seg(i):
        return _dot(h, w_ref[:, i * sw:(i + 1) * sw])

    qa = _group_rms(seg(0), gm, 1.0 / A_DQK, hg[0:1])
    ka = _group_rms(seg(1), gm, 1.0 / A_DQK, hg[1:2])
    if use_rope:
        qa = _rope(qa, cos, sin, 32, first)
        ka = _rope(ka, cos, sin, 32, first)
    qa_ref[...] = (qa * (LOG2E / math.sqrt(A_DQK))).astype(BF16)
    ka_ref[...] = ka.astype(BF16)
    _store_values_t(vat_ref, seg(2), A_DV)
    qb = _group_rms(seg(3), gm, 1.0 / B_DH, hg[2:3])
    qb_ref[...] = (qb * (LOG2E / math.sqrt(B_DH))).astype(BF16)
    kb_ref[...] = _group_rms(seg(4), gm, 1.0 / B_DH, hg[3:4]).astype(BF16)
    vb_ref[...] = seg(5).astype(BF16)


def _premix_even(x, vec, w, gmat, hg, cos, sin, use_rope):
    t, d = x.shape
    tm = min(FLASH_TK, t)
    row = lambda i: (i, 0)
    nat = jax.ShapeDtypeStruct((t, 512), BF16)
    nat_spec = pl.BlockSpec((tm, 512), row)
    vts = _values_t_shape(t, tm, A_DV)
    vt_spec = pl.BlockSpec((vts[0], 1) + vts[2:], lambda i: (0, i, 0, 0))
    return pl.pallas_call(
        functools.partial(_premix_even_kernel, use_rope=use_rope),
        grid=(t // tm,),
        in_specs=[pl.BlockSpec((tm, d), row), _full(vec.shape), _full(w.shape), _full(gmat.shape),
                  _full(hg.shape), pl.BlockSpec((tm, LANES), row), pl.BlockSpec((tm, LANES), row)],
        out_specs=[nat_spec, nat_spec, vt_spec, nat_spec, nat_spec, nat_spec],
        out_shape=[nat, nat, jax.ShapeDtypeStruct(vts, BF16), nat, nat, nat],
        compiler_params=_cparams("parallel"),
        name="premix_even",
    )(x, vec, w, gmat, hg, cos, sin)


_O_CQ, _O_CKV, _O_DQ, _O_DK, _O_DV, _O_CKR, _O_END = 0, 384, 640, 1152, 1408, 1664, 1792


def _premix_odd_kernel(x_ref, vec_ref, w_ref, wqb_ref, wk_ref, wv_ref, g128_ref, g64_ref, gain_ref, lora_ref,
                       cos_ref, sin_ref, cosc_ref, sinc_ref,
                       mq_ref, mk_ref, mvt_ref, gq_ref, gk_ref, gv_ref, *, use_rope):
    vec = vec_ref[...]
    h = (_rms(x_ref[...]) * vec[0:1] * (1.0 + vec[1:2]) + vec[2:3]).astype(BF16)
    gains = gain_ref[...]
    lora = lora_ref[...]
    g128 = g128_ref[...]
    g64 = g64_ref[...]
    mw = C_HEADS * HEAD_PAD
    if use_rope:
        cosc = _lane_tile(cosc_ref[...], mw)
        sinc = _lane_tile(sinc_ref[...], mw)
        first_c = (_lane_iota(mw) & 127) < (C_NOPE + C_ROPE // 2)
        cos = _lane_tile(cos_ref[...], 512)
        sin = _lane_tile(sin_ref[...], 512)
        first_d = (_lane_iota(512) & 32) == 0

    def seg(a, b):
        return _dot(h, w_ref[:, a:b])

    inv_c = 1.0 / (C_NOPE + C_ROPE)
    cq = (_rms(seg(_O_CQ, _O_CKV)) * lora[0:1, :C_Q_LORA]).astype(BF16)
    mq = _group_rms(_dot(cq, wqb_ref[...]), g128, inv_c, gains[0:1])
    if use_rope:
        mq = _rope(mq, cosc, sinc, C_ROPE // 2, first_c)
    mq_ref[...] = (mq * (LOG2E * math.sqrt(inv_c))).astype(BF16)
    ckv = (_rms(seg(_O_CKV, _O_DQ)) * lora[1:2, :C_KV_LORA]).astype(BF16)
    mk = _dot(ckv, wk_ref[...]) + _lane_tile(seg(_O_CKR, _O_END), mw)
    mk = _group_rms(mk, g128, inv_c, gains[1:2])
    if use_rope:
        mk = _rope(mk, cosc, sinc, C_ROPE // 2, first_c)
    mk_ref[...] = mk.astype(BF16)
    _store_values_t(mvt_ref, _dot(ckv, wv_ref[...]), C_DV)
    gq = _group_rms(seg(_O_DQ, _O_DK), g64, 1.0 / D_DH, gains[2:3, :512])
    gk = _group_rms(seg(_O_DK, _O_DV), g64, 1.0 / D_DH, gains[3:4, :256])
    if use_rope:
        gq = _rope(gq, cos, sin, 32, first_d)
        gk = _rope(gk, cos[:, :256], sin[:, :256], 32, first_d[:, :256])
    gq_ref[...] = (gq * (LOG2E / math.sqrt(D_DH))).astype(BF16)
    gk_ref[...] = gk.astype(BF16)
    gv_ref[...] = seg(_O_DV, _O_CKR).astype(BF16)


def _premix_odd(x, vec, w, wqb, wk, wv, g128, g64, gains, lora, cos, sin, cosc, sinc, use_rope):
    t, d = x.shape
    tm = min(FLASH_TK, t)
    row = lambda i: (i, 0)
    tab = pl.BlockSpec((tm, LANES), row)
    nat = lambda n: (pl.BlockSpec((tm, n), row), jax.ShapeDtypeStruct((t, n), BF16))
    vts = _values_t_shape(t, tm, C_DV)
    vt = (pl.BlockSpec((vts[0], 1) + vts[2:], lambda i: (0, i, 0, 0)), jax.ShapeDtypeStruct(vts, BF16))
    outs = [nat(1024), nat(1024), vt, nat(512), nat(256), nat(256)]
    return pl.pallas_call(
        functools.partial(_premix_odd_kernel, use_rope=use_rope),
        grid=(t // tm,),
        in_specs=[pl.BlockSpec((tm, d), row)] + [_full(a.shape) for a in (vec, w, wqb, wk, wv, g128, g64, gains, lora)]
                 + [tab] * 4,
        out_specs=[o[0] for o in outs],
        out_shape=[o[1] for o in outs],
        compiler_params=_cparams("parallel"),
        name="premix_odd",
    )(x, vec, w, wqb, wk, wv, g128, g64, gains, lora, cos, sin, cosc, sinc)


def _flash_kernel(*refs, mode, has_ctx, online, lam_init):
    bound_ref, q_ref, k_ref, vt_ref = refs[:4]
    kc_ref, vct_ref = refs[4:6] if has_ctx else (None, None)
    ex_ref, gcol_ref, o_ref, acc_ref, stage_ref = refs[6:11] if has_ctx else refs[4:9]
    qt = q_ref[...].astype(F32).T.astype(BF16)
    row = lax.broadcasted_iota(jnp.int32, (HEAD_PAD, 1), 0)
    if mode == "mla":
        qs = [qt[:HEAD_PAD], qt[HEAD_PAD:]]
    else:
        zero = jnp.zeros_like(qt)
        qs = [jnp.where(row < 64, qt, zero), jnp.where(row >= 64, qt, zero)]
    nk, tk = vt_ref.shape[1], vt_ref.shape[3]

    acc_ref[...] = jnp.zeros(acc_ref.shape, F32)
    rows_b = acc_ref.shape[1]

    def scores(kblk, b):
        kb = kblk[:, b * HEAD_PAD:(b + 1) * HEAD_PAD] if mode == "mla" else kblk
        return _dot(kb, qs[b])

    def values(vtblk, b):
        return vtblk if vtblk.shape[0] == rows_b else vtblk[b * rows_b:(b + 1) * rows_b]

    if online:
        m_ref, mx_ref = refs[-2:]
        m_ref[...] = jnp.full(m_ref.shape, NEG_INF, F32)

        def absorb(s, smax, vtblk, b):
            m_prev = m_ref[b]
            m_new = jnp.maximum(m_prev, smax)
            alpha = jnp.exp2(m_prev - m_new)
            p = jnp.exp2(s - m_new).astype(BF16)
            acc_ref[b] = alpha * acc_ref[b] + _dot(values(vtblk, b), p)
            m_ref[b] = m_new

        def produce(slot, kblk):
            for b in range(2):
                s = scores(kblk, b)
                stage_ref[slot, b] = s
                mx_ref[slot, b] = jnp.max(s, axis=0, keepdims=True)

        def consume(slot, vtblk):
            for b in range(2):
                absorb(stage_ref[slot, b], mx_ref[slot, b], vtblk, b)

        def direct(kblk, vtblk):
            for b in range(2):
                s = scores(kblk, b)
                absorb(s, jnp.max(s, axis=0, keepdims=True), vtblk, b)
    else:
        bound = bound_ref[0]

        def probs(kblk, b):
            return jnp.exp2(scores(kblk, b) - bound).astype(BF16)

        def produce(slot, kblk):
            for b in range(2):
                stage_ref[slot, b] = probs(kblk, b)

        def consume(slot, vtblk):
            for b in range(2):
                acc_ref[b] += _dot(values(vtblk, b), stage_ref[slot, b])

        def direct(kblk, vtblk):
            for b in range(2):
                acc_ref[b] += _dot(values(vtblk, b), probs(kblk, b))

    if nk == 1:
        if has_ctx:
            direct(kc_ref[...], vct_ref[0])
        direct(k_ref[...], vt_ref[0, 0])
    else:
        unroll = FLASH_UNROLL if nk % FLASH_UNROLL == 0 else 2
        assert nk % unroll == 0
        produce(0, k_ref[0:tk, :])
        if has_ctx:
            direct(kc_ref[...], vct_ref[0])

        def body(i, carry):
            c0 = unroll * i
            for u in range(unroll):
                nxt = jnp.minimum(c0 + u + 1, nk - 1)
                produce((u + 1) % 2, k_ref[pl.ds(pl.multiple_of(nxt * tk, tk), tk), :])
                consume(u % 2, vt_ref[0, c0 + u])
            return carry

        lax.fori_loop(0, nk // unroll, body, 0)

    dv = rows_b - ONES_ROWS
    o0 = acc_ref[0, :dv] / acc_ref[0, dv:dv + 1]
    o1 = acc_ref[1, :dv] / acc_ref[1, dv:dv + 1]
    if mode == "diff":
        ex = ex_ref[...]
        lam = (jnp.exp(jnp.sum(ex[0:1] * ex[1:2], axis=-1, keepdims=True))
               - jnp.exp(jnp.sum(ex[2:3] * ex[3:4], axis=-1, keepdims=True)) + lam_init)
        o = o0 - lam * o1
        o = o * lax.rsqrt(jnp.mean(o * o, axis=0, keepdims=True) + EPS) * gcol_ref[...] * (1.0 - lam_init)
    else:
        o = jnp.concatenate([o0, o1], axis=0)
    o_ref[...] = o.T.astype(o_ref.dtype)


def _values_t(v, dv):
    keys = v.shape[0]
    groups = v.shape[1] // HEAD_PAD
    vt = v.T.reshape(groups, HEAD_PAD // dv, dv, keys)
    ones = jnp.ones(vt.shape[:2] + (ONES_ROWS, keys), v.dtype)
    return jnp.concatenate([vt, ones], axis=2).reshape(groups, 1, -1, keys)


def _flash(q, k, vt, kc, vct, ex, gcol, bound=None, *, mode, lam_init=0.0):
    t = q.shape[0]
    tkeys = k.shape[0]
    qw = 2 * HEAD_PAD if mode == "mla" else HEAD_PAD
    groups = q.shape[1] // qw
    tq = min(FLASH_TQ, t)
    _, nk, vrows, tk = vt.shape
    assert nk * tk == tkeys
    has_ctx = kc is not None
    dv = HEAD_PAD if mode == "diff" else HEAD_PAD // 2
    rows_b = dv + ONES_ROWS
    bound_arr = jnp.zeros((1,), F32) if bound is None else jnp.reshape(bound, (1,)).astype(F32)
    in_specs = [pl.BlockSpec(memory_space=pltpu.SMEM),
                pl.BlockSpec((tq, qw), lambda g, i: (i, g)),
                pl.BlockSpec((tkeys, qw), lambda g, i: (0, g)),
                pl.BlockSpec((1, nk, vrows, tk), lambda g, i: (g, 0, 0, 0))]
    args = [bound_arr, q, k, vt]
    if has_ctx:
        lc = kc.shape[0]
        in_specs += [pl.BlockSpec((lc, qw), lambda g, i: (0, g)),
                     pl.BlockSpec((1, vrows, lc), lambda g, i: (g, 0, 0))]
        args += [kc, vct.reshape(groups, vrows, lc)]
    in_specs += [_full(ex.shape), _full(gcol.shape)]
    args += [ex, gcol]

    def call(online):
        acc = pltpu.VMEM((2, rows_b, tq), F32)
        if online:
            scratch = [acc, pltpu.VMEM((2, 2, tk, tq), F32), pltpu.VMEM((2, 1, tq), F32), pltpu.VMEM((2, 2, 1, tq), F32)]
        else:
            scratch = [acc, pltpu.VMEM((2, 2, tk, tq), BF16)]
        return pl.pallas_call(
            functools.partial(_flash_kernel, mode=mode, has_ctx=has_ctx, online=online, lam_init=lam_init),
            grid=(groups, t // tq),
            in_specs=in_specs,
            out_specs=pl.BlockSpec((tq, HEAD_PAD), lambda g, i: (i, g)),
            out_shape=jax.ShapeDtypeStruct((t, groups * HEAD_PAD), BF16),
            scratch_shapes=scratch,
            compiler_params=_cparams("parallel", "parallel"),
            name="flash_" + mode + ("_online" if online else "_bounded"),
        )(*args)

    if bound is None:
        return call(True)
    return lax.cond(bound <= FLASH_SAFE_BOUND, lambda: call(False), lambda: call(True))


def _na_kernel(q_ref, kp_ref, kcur_ref, kn_ref, vp_ref, vcur_ref, vn_ref, kc_ref, vc_ref, bias_ref, o_ref):
    q = q_ref[...]
    lane = _lane_iota(LANES)
    zero = jnp.zeros_like(q)
    kk = jnp.concatenate([kc_ref[...], kp_ref[...], kcur_ref[...], kn_ref[...]], axis=0)
    vv = jnp.concatenate([vc_ref[...], vp_ref[...], vcur_ref[...], vn_ref[...]], axis=0)
    outs = []
    for b in range(2):
        qb = jnp.where(lane < 64, q, zero) if b == 0 else jnp.where(lane >= 64, q, zero)
        s = _dot_nt(qb, kk) + bias_ref[0, b]
        m = jnp.max(s, axis=-1, keepdims=True)
        p = jnp.exp2(s - m)
        l = jnp.sum(p, axis=-1, keepdims=True)
        outs.append(_dot(p.astype(BF16), vv) / l)
    o_ref[...] = jnp.where(lane < 64, outs[0], outs[1]).astype(o_ref.dtype)


def _na_bias_table(rpb, rows, lc):
    nr, w = NA_ROWS, GRID_W
    heads = rpb.shape[0]
    pad = w - NA_WIN_W
    padded = jnp.pad(rpb.astype(F32) * LOG2E, ((0, 0), (0, 0), (pad, pad + 1)))
    skew = jnp.tile(padded, (1, 1, w))[:, :, :w * (2 * w - 1)].reshape(heads, -1, w, 2 * w - 1)
    toep = skew[:, :, :, w - 1:]
    cq = np.arange(w)
    cs = np.clip(cq - NA_WIN_W // 2, 0, w - NA_WIN_W)
    valid_c = (cq[None, :] >= cs[:, None]) & (cq[None, :] < cs[:, None] + NA_WIN_W)
    toep = jnp.where(jnp.asarray(valid_c), toep, NEG_INF)
    wh = min(NA_WIN_H, rows)
    rl = np.arange(nr)
    rr_rel = np.arange(3 * nr)
    dr_all, valid_all = [], []
    for base in (0, nr, rows - nr):
        r = base + rl
        rs = np.clip(r - wh // 2, 0, rows - wh)
        rr = base - nr + rr_rel
        valid_all.append((rr[None, :] >= rs[:, None]) & (rr[None, :] < rs[:, None] + wh))
        dr_all.append(np.clip(rr[None, :] - r[:, None] + (NA_WIN_H - 1), 0, 2 * NA_WIN_H - 2))
    dr_idx = np.stack(dr_all).reshape(-1)
    valid_r = np.stack(valid_all)
    blocks = jnp.take(toep, jnp.asarray(dr_idx, jnp.int32), axis=1).reshape(heads, 3, nr, 3 * nr, w, w)
    blocks = jnp.where(jnp.asarray(valid_r)[None, :, :, :, None, None], blocks, NEG_INF)
    tab = blocks.transpose(1, 0, 2, 4, 3, 5).reshape(3, heads, nr * w, 3 * nr * w)
    return jnp.concatenate([jnp.zeros(tab.shape[:3] + (lc,), F32), tab], axis=-1)


def _neighbourhood(q, k, v, kc, vc, bias):
    t = q.shape[0]
    qn = NA_ROWS * GRID_W
    nb = t // qn
    lc = kc.shape[0]
    pairs = q.shape[1] // HEAD_PAD
    cur = lambda p, i: (i, p)
    prev = lambda p, i: (jnp.maximum(i - 1, 0), p)
    nxt = lambda p, i: (jnp.minimum(i + 1, nb - 1), p)
    blk = lambda f: pl.BlockSpec((qn, HEAD_PAD), f)
    ctxs = pl.BlockSpec((lc, HEAD_PAD), lambda p, i: (0, p))
    case = lambda p, i: (jnp.where(i == 0, 0, jnp.where(i == nb - 1, 2, 1)), p, 0, 0)
    return pl.pallas_call(
        _na_kernel,
        grid=(pairs, nb),
        in_specs=[blk(cur), blk(prev), blk(cur), blk(nxt), blk(prev), blk(cur), blk(nxt), ctxs, ctxs,
                  pl.BlockSpec((1, 2, qn, lc + 3 * qn), case)],
        out_specs=blk(cur),
        out_shape=jax.ShapeDtypeStruct(q.shape, BF16),
        compiler_params=_cparams("parallel", "arbitrary"),
        name="neighbourhood",
    )(q, k, k, k, v, v, v, kc, vc, bias)


def _wgqa_kernel(sink_ref, q_ref, kp_ref, kcur_ref, kn_ref, vp_ref, vcur_ref, vn_ref, kc_ref, vc_ref, o_ref):
    i = pl.program_id(0)
    nb = pl.num_programs(0)
    qb = q_ref.shape[0]
    lc = kc_ref.shape[0]
    nkeys = lc + 3 * qb
    lane = _lane_iota(LANES)
    col = lax.broadcasted_iota(jnp.int32, (qb, nkeys), 1)
    row = lax.broadcasted_iota(jnp.int32, (qb, nkeys), 0)
    rel = col - (lc + qb) - row
    lo = jnp.where(i > 0, lc, lc + qb)
    hi = jnp.where(i < nb - 1, nkeys, lc + 2 * qb)
    valid = (col < lc) | ((jnp.abs(rel) <= D_WINDOW) & (col >= lo) & (col < hi))
    groups = D_HEADS // D_KV_HEADS
    for kv in range(D_KV_HEADS):
        ks = slice(kv * HEAD_PAD, (kv + 1) * HEAD_PAD)
        kk = jnp.concatenate([kc_ref[:, ks], kp_ref[:, ks], kcur_ref[:, ks], kn_ref[:, ks]], axis=0)
        vv = jnp.concatenate([vc_ref[:, ks], vp_ref[:, ks], vcur_ref[:, ks], vn_ref[:, ks]], axis=0)
        lhs = []
        for g in range(groups):
            h = kv * groups + g
            qp = q_ref[:, (h // 2) * HEAD_PAD:(h // 2 + 1) * HEAD_PAD]
            keep = (lane < 64) if h % 2 == 0 else (lane >= 64)
            lhs.append(jnp.where(keep, qp, jnp.zeros_like(qp)))
        s = _dot_nt(jnp.concatenate(lhs, axis=0), kk)
        outs = []
        for g in range(groups):
            sink = sink_ref[kv * groups + g] * LOG2E
            sg = jnp.where(valid, s[g * qb:(g + 1) * qb], NEG_INF)
            m = jnp.maximum(jnp.max(sg, axis=-1, keepdims=True), sink)
            p = jnp.exp2(sg - m)
            l = jnp.sum(p, axis=-1, keepdims=True) + jnp.exp2(sink - m)
            outs.append(_dot(p.astype(BF16), vv) / l)
        for pp in range(groups // 2):
            pair = kv * (groups // 2) + pp
            o_ref[:, pair * HEAD_PAD:(pair + 1) * HEAD_PAD] = jnp.where(
                lane < 64, outs[2 * pp], outs[2 * pp + 1]).astype(o_ref.dtype)


def _window_gqa(q, k, v, kc, vc, sink):
    t = q.shape[0]
    qb = D_WINDOW
    nb = t // qb
    lc = kc.shape[0]
    kw = k.shape[1]
    cur = lambda i: (i, 0)
    prev = lambda i: (jnp.maximum(i - 1, 0), 0)
    nxt = lambda i: (jnp.minimum(i + 1, nb - 1), 0)
    kblk = lambda f: pl.BlockSpec((qb, kw), f)
    return pl.pallas_call(
        _wgqa_kernel,
        grid=(nb,),
        in_specs=[pl.BlockSpec(memory_space=pltpu.SMEM), pl.BlockSpec((qb, q.shape[1]), cur),
                  kblk(prev), kblk(cur), kblk(nxt), kblk(prev), kblk(cur), kblk(nxt),
                  _full((lc, kw)), _full((lc, kw))],
        out_specs=pl.BlockSpec((qb, q.shape[1]), cur),
        out_shape=jax.ShapeDtypeStruct(q.shape, BF16),
        compiler_params=_cparams("parallel"),
        name="window_gqa",
    )(sink, q, k, k, k, v, v, v, kc, vc)


def _postmix_kernel(o1_ref, o2_ref, x_ref, w_ref, vec_ref, x1_ref, h2_ref):
    half = o1_ref.shape[1]
    y = _dot(o1_ref[...], w_ref[:half, :]) + _dot(o2_ref[...], w_ref[half:, :])
    vec = vec_ref[...]
    x1 = x_ref[...] + vec[0:1] * y
    x1_ref[...] = x1
    h2_ref[...] = (_rms(x1) * vec[1:2] * (1.0 + vec[2:3]) + vec[3:4]).astype(BF16)


def _postmix(o1, o2, x, w, vec):
    t, d = x.shape
    tm = min(512, t)
    row = lambda i: (i, 0)
    return pl.pallas_call(
        _postmix_kernel,
        grid=(t // tm,),
        in_specs=[pl.BlockSpec((tm, o1.shape[1]), row), pl.BlockSpec((tm, o2.shape[1]), row),
                  pl.BlockSpec((tm, d), row), _full(w.shape), _full(vec.shape)],
        out_specs=[pl.BlockSpec((tm, d), row)] * 2,
        out_shape=[jax.ShapeDtypeStruct((t, d), F32), jax.ShapeDtypeStruct((t, d), BF16)],
        compiler_params=_cparams("parallel"),
        name="postmix",
    )(o1, o2, x, w, vec)


def _ffn_kernel(h_ref, hp_ref, hn_ref, x1_ref, wup_ref, cv_ref, wdn_ref, g2_ref, o_ref, acc_ref, hcat_ref, u_ref):
    i = pl.program_id(0)
    n = pl.num_programs(0)
    tm = h_ref.shape[0]
    rows = tm + 2 * HALO
    nchunk = wdn_ref.shape[0] // FFN_CHUNK
    hcat_ref[0:HALO] = hp_ref[...]
    hcat_ref[HALO:HALO + tm] = h_ref[...]
    hcat_ref[HALO + tm:rows] = hn_ref[...]

    @pl.when(i == 0)
    def _():
        hcat_ref[0:HALO] = jnp.zeros((HALO, hcat_ref.shape[1]), BF16)

    @pl.when(i == n - 1)
    def _():
        hcat_ref[HALO + tm:rows] = jnp.zeros((HALO, hcat_ref.shape[1]), BF16)

    acc_ref[...] = jnp.zeros(acc_ref.shape, F32)

    def conv(u, cv):
        before = pltpu.roll(u, 1, axis=0)
        after = pltpu.roll(u, rows - 1, axis=0)
        r = cv[3:4] + before * cv[0:1] + u * cv[1:2] + after * cv[2:3]
        return r[HALO:HALO + tm]

    def cols(c, gate):
        return pl.ds(pl.multiple_of((gate * nchunk + c) * FFN_CHUNK, FFN_CHUNK), FFN_CHUNK)

    def stage_up(slot, c):
        hcat = hcat_ref[...]
        u_ref[slot, 0] = _dot(hcat, wup_ref[:, cols(c, 0)])
        u_ref[slot, 1] = _dot(hcat, wup_ref[:, cols(c, 1)])

    def stage_down(slot, c):
        a = conv(u_ref[slot, 0], cv_ref[:, cols(c, 0)])
        g = conv(u_ref[slot, 1], cv_ref[:, cols(c, 1)])
        act = (a * (g * _sigmoid(g))).astype(BF16)
        acc_ref[...] += _dot(act, wdn_ref[cols(c, 0), :])

    assert nchunk % 2 == 1
    stage_up(0, 0)

    def body(j, carry):
        c = 2 * j
        stage_up(1, c + 1)
        stage_down(0, c)
        stage_up(0, c + 2)
        stage_down(1, c + 1)
        return carry

    lax.fori_loop(0, nchunk // 2, body, 0)
    stage_down(0, nchunk - 1)
    o_ref[...] = x1_ref[...] + g2_ref[...] * acc_ref[...]


def _conv_ffn(h2, x1, wup, cv, wdn, g2):
    t, d = x1.shape
    tm = min(512, t)
    nt = t // tm
    hb = tm // HALO
    row = lambda i: (i, 0)
    prev = lambda i: (jnp.maximum(i * hb - 1, 0), 0)
    nxt = lambda i: (jnp.minimum((i + 1) * hb, t // HALO - 1), 0)
    return pl.pallas_call(
        _ffn_kernel,
        grid=(nt,),
        in_specs=[pl.BlockSpec((tm, d), row), pl.BlockSpec((HALO, d), prev), pl.BlockSpec((HALO, d), nxt),
                  pl.BlockSpec((tm, d), row), _full(wup.shape), _full(cv.shape), _full(wdn.shape), _full(g2.shape)],
        out_specs=pl.BlockSpec((tm, d), row),
        out_shape=jax.ShapeDtypeStruct((t, d), F32),
        scratch_shapes=[pltpu.VMEM((tm, d), F32), pltpu.VMEM((tm + 2 * HALO, d), BF16),
                        pltpu.VMEM((2, 2, tm + 2 * HALO, FFN_CHUNK), F32)],
        compiler_params=_cparams("parallel"),
        name="conv_ffn",
    )(h2, h2, h2, x1, wup, cv, wdn, g2)


def _axial_tables(n_tokens, dim):
    t = jnp.arange(n_tokens, dtype=jnp.int32)
    row = (t // GRID_W).astype(F32)
    col = (t % GRID_W).astype(F32)
    quarter = dim // 4
    inv_freq = ROPE_BASE ** (-jnp.arange(quarter, dtype=F32) / quarter)
    ang = jnp.concatenate([row[:, None] * inv_freq, col[:, None] * inv_freq], axis=-1)
    return jnp.cos(ang), jnp.sin(ang)


def _rope_tables_head64(t):
    cos, sin = _axial_tables(t, 64)
    return jnp.tile(jnp.concatenate([cos, cos], -1), (1, 2)), jnp.tile(jnp.concatenate([-sin, sin], -1), (1, 2))


def _rope_tables_latent(t):
    cos, sin = _axial_tables(t, C_ROPE)
    ones = jnp.ones((t, C_NOPE), F32)
    tail = jnp.ones((t, HEAD_PAD - C_NOPE - C_ROPE), F32)
    cosc = jnp.concatenate([ones, cos, cos, tail], -1)
    sinc = jnp.concatenate([0 * ones, -sin, sin, 0 * tail], -1)
    return cosc, sinc


def _group_ones(width, group):
    return jnp.asarray(np.kron(np.eye(width // group), np.ones((group, group))), BF16)


def _pad_heads(a, heads, dim):
    a = a.reshape(a.shape[:-1] + (heads, dim))
    a = jnp.pad(a, [(0, 0)] * (a.ndim - 1) + [(0, HEAD_PAD - dim)])
    return a.reshape(a.shape[:-2] + (heads * HEAD_PAD,))


def _logit_bound(gq, gk, dim):
    return 1.02 * LOG2E * math.sqrt(dim) * jnp.max(jnp.abs(gq)) * jnp.max(jnp.abs(gk))


def _pad_row(v, width):
    return jnp.pad(v, (0, width - v.shape[0]))


def _ffn_weights(w_up, conv_w, conv_b, w_down):
    cv = jnp.concatenate([conv_w, conv_b[None]], axis=0)
    return w_up.astype(BF16), cv, w_down.astype(BF16)


def kernel(x, c, ctx, c_ctx, ada_w, ada_b, norm_mix, norm_ffn, ffn_up, ffn_conv_w, ffn_conv_b, ffn_down, ev_w_in, ev_w_out, a_q_norm, a_k_norm, a_lam_q1, a_lam_k1, a_lam_q2, a_lam_k2, a_subln, b_q_norm, b_k_norm, b_rpb, od_w_in, od_w_out, c_q_a_norm, c_w_qb, c_kv_a_norm, c_w_kvb, c_q_norm, c_k_norm, d_q_norm, d_k_norm, d_sink):
    assert x.shape[0] == 1 and ctx.shape[0] == 1
    xm = x[0]
    xc = ctx[0]
    t, d = xm.shape
    lc = xc.shape[0]
    mods = _modulation(c, c_ctx, ada_w, ada_b)
    cos64, sin64 = _rope_tables_head64(t)
    cosc, sinc = _rope_tables_latent(t)
    g64 = _group_ones(MXU_TILE, 64)
    g128 = _group_ones(MXU_TILE, HEAD_PAD)
    dummy_tab = jnp.zeros((lc, LANES), F32)

    for l in range(DEPTH):
        last = l == DEPTH - 1
        i = l // 2
        mm = mods[l, 0].reshape(6, d)
        mc = mods[l, 1].reshape(6, d)
        pre_m = jnp.stack([norm_mix[l], mm[1], mm[0]])
        pre_c = jnp.stack([norm_mix[l], mc[1], mc[0]])
        post_m = jnp.stack([mm[2], norm_ffn[l], mm[4], mm[3]])
        post_c = jnp.stack([mc[2], norm_ffn[l], mc[4], mc[3]])
        wup, cv, wdn = _ffn_weights(ffn_up[l], ffn_conv_w[l], ffn_conv_b[l], ffn_down[l])
        if l % 2 == 0:
            lam_init = 0.8 - 0.6 * math.exp(-0.3 * l)
            w_in = ev_w_in[i].astype(BF16)
            w_out = ev_w_out[i].astype(BF16)
            hg = jnp.stack([jnp.tile(a_q_norm[i], 8), jnp.tile(a_k_norm[i], 8),
                            jnp.tile(b_q_norm[i], 8), jnp.tile(b_k_norm[i], 8)])
            ex = jnp.stack([_pad_row(a_lam_q1[i], LANES), _pad_row(a_lam_k1[i], LANES),
                            _pad_row(a_lam_q2[i], LANES), _pad_row(a_lam_k2[i], LANES)])
            gcol = a_subln[i][:, None]
            qa, ka, va, qb, kb, vb = _premix_even(xm, pre_m, w_in, g64, hg, cos64, sin64, True)
            qa_c, ka_c, va_c, qb_c, kb_c, vb_c = _premix_even(xc, pre_c, w_in, g64, hg, dummy_tab, dummy_tab, False)
            oa = _flash(qa, ka, va, ka_c, va_c, ex, gcol, _logit_bound(a_q_norm[i], a_k_norm[i], A_DQK),
                        mode="diff", lam_init=lam_init)
            ob = _neighbourhood(qb, kb, vb, kb_c, vb_c, _na_bias_table(b_rpb[i], t // GRID_W, lc))
            if not last:
                oa_c = _flash(qa_c, ka_c, va_c, None, None, ex, gcol, mode="diff", lam_init=lam_init)
                ob_c = _flash(qb_c, kb_c, _values_t(vb_c, B_DH), None, None, ex, gcol, mode="pair")
        else:
            w = od_w_in[i]
            cq, ckv, ckr, dq, dk, dv = jnp.split(w, np.cumsum([C_Q_LORA, C_KV_LORA, C_ROPE, 512, 128]).tolist(), axis=1)
            dup = lambda a: jnp.concatenate([a[:, :64], a[:, :64], a[:, 64:], a[:, 64:]], axis=1)
            ckr_pad = jnp.pad(ckr, ((0, 0), (C_NOPE, HEAD_PAD - C_NOPE - C_ROPE)))
            w_in = jnp.concatenate([cq, ckv, dq, dup(dk), dup(dv), ckr_pad], axis=1).astype(BF16)
            w_out = od_w_out[i].astype(BF16)
            wqb = _pad_heads(c_w_qb[i], C_HEADS, C_NOPE + C_ROPE).astype(BF16)
            kvb = c_w_kvb[i].reshape(C_KV_LORA, C_HEADS, C_NOPE + C_DV)
            wk = _pad_heads(kvb[:, :, :C_NOPE].reshape(C_KV_LORA, -1), C_HEADS, C_NOPE).astype(BF16)
            wv = kvb[:, :, C_NOPE:].reshape(C_KV_LORA, C_HEADS * C_DV).astype(BF16)
            gains = jnp.stack([jnp.tile(_pad_row(c_q_norm[i], HEAD_PAD), C_HEADS),
                               jnp.tile(_pad_row(c_k_norm[i], HEAD_PAD), C_HEADS),
                               _pad_row(jnp.tile(d_q_norm[i], 8), 1024), _pad_row(jnp.tile(d_k_norm[i], 4), 1024)])
            lora = jnp.stack([_pad_row(c_q_a_norm[i], 512), _pad_row(c_kv_a_norm[i], 512)])
            odd = functools.partial(_premix_odd, w=w_in, wqb=wqb, wk=wk, wv=wv, g128=g128, g64=g64, gains=gains, lora=lora)
            mq, mk, mv, gq, gk, gv = odd(xm, pre_m, cos=cos64, sin=sin64, cosc=cosc, sinc=sinc, use_rope=True)
            mq_c, mk_c, mv_c, gq_c, gk_c, gv_c = odd(xc, pre_c, cos=dummy_tab, sin=dummy_tab, cosc=dummy_tab,
                                                     sinc=dummy_tab, use_rope=False)
            ex = jnp.zeros((4, LANES), F32)
            oa = _flash(mq, mk, mv, mk_c, mv_c, ex, jnp.ones((HEAD_PAD, 1), F32),
                        _logit_bound(c_q_norm[i], c_k_norm[i], C_NOPE + C_ROPE), mode="mla")
            ob = _window_gqa(gq, gk, gv, gk_c, gv_c, d_sink[i])
            if not last:
                raise NotImplementedError("context update after an odd layer is not needed at this depth")
        x1, h2 = _postmix(oa, ob, xm, w_out, post_m)
        xm = _conv_ffn(h2, x1, wup, cv, wdn, mm[5][None])
        if not last:
            xc1, hc2 = _postmix(oa_c, ob_c, xc, w_out, post_c)
            xc = _conv_ffn(hc2, xc1, wup, cv, wdn, mc[5][None])
    return xm[None]
```

```python
import functools
import math

import numpy as np
import jax
import jax.numpy as jnp
from jax import lax
from jax.experimental import pallas as pl
from jax.experimental.pallas import tpu as pltpu

D_MODEL = 1024
DEPTH = 2
GRID_W = 64
EPS = 1e-6
ROPE_BASE = 10000.0
NEG_INF = -1e30
LOG2E = math.log2(math.e)

A_HEADS = 4
A_DQK = 64
A_DV = 128
B_HEADS = 8
B_DH = 64
NA_WIN_H = 8
NA_WIN_W = 16
C_HEADS = 8
C_Q_LORA = 384
C_KV_LORA = 256
C_NOPE = 64
C_ROPE = 32
C_DV = 64
D_HEADS = 8
D_KV_HEADS = 2
D_DH = 64
D_WINDOW = 128
FFN_HIDDEN = 2816
CONV_W = 3

LANES = 128
MXU_TILE = 256
HEAD_PAD = 128
VMEM_LIMIT = 56 * 1024 * 1024
FFN_CHUNK = 256
HALO = 16
NA_ROWS = 4
ONES_ROWS = 16
FLASH_TQ = 1024
FLASH_TK = 512
FLASH_UNROLL = 4
FLASH_SAFE_BOUND = 40.0

F32 = jnp.float32
BF16 = jnp.bfloat16


def _cparams(*sem):
    return pltpu.CompilerParams(dimension_semantics=sem, vmem_limit_bytes=VMEM_LIMIT)


def _full(shape):
    n = len(shape)
    return pl.BlockSpec(shape, lambda *_: (0,) * n)


def _dot(a, b):
    return jnp.dot(a, b, preferred_element_type=F32)


def _dot_nt(a, b):
    return lax.dot_general(a, b, (((1,), (1,)), ((), ())), preferred_element_type=F32)


def _rms(x):
    return x * lax.rsqrt(jnp.mean(x * x, axis=-1, keepdims=True) + EPS)


def _sigmoid(x):
    return 1.0 / (1.0 + jnp.exp(-x))


def _group_rms(y, gmat, inv_n, gain):
    yy = (y * y).astype(BF16)
    slab = gmat.shape[0]
    ss = jnp.concatenate([_dot(yy[:, c:c + slab], gmat) for c in range(0, y.shape[-1], slab)], axis=-1)
    return y * lax.rsqrt(ss * inv_n + EPS) * gain


def _rope(y, cos, sin, half, first_half):
    w = y.shape[-1]
    fwd = pltpu.roll(y, w - half, axis=1)
    bwd = pltpu.roll(y, half, axis=1)
    return y * cos + jnp.where(first_half, fwd, bwd) * sin


def _lane_tile(t, width):
    reps = width // t.shape[-1]
    return t if reps == 1 else jnp.concatenate([t] * reps, axis=-1)


def _lane_iota(width):
    return lax.broadcasted_iota(jnp.int32, (1, width), 1)


def _mod_kernel(ct_ref, w_ref, b_ref, o_ref):
    ct = ct_ref[...]
    s = ct * _sigmoid(ct)
    w = w_ref[0]
    rows = [jnp.sum(w * s[:, j:j + 1], axis=0, keepdims=True) for j in range(2)]
    o_ref[0] = jnp.concatenate(rows, axis=0) + b_ref[0]


def _modulation(c, c_ctx, ada_w, ada_b):
    depth, d, n = ada_w.shape
    tn = 1536
    ct = jnp.stack([c[0], c_ctx], axis=1)
    return pl.pallas_call(
        _mod_kernel,
        grid=(depth, n // tn),
        in_specs=[_full((d, 2)),
                  pl.BlockSpec((1, d, tn), lambda l, j: (l, 0, j)),
                  pl.BlockSpec((1, 1, tn), lambda l, j: (l, 0, j))],
        out_specs=pl.BlockSpec((1, 2, tn), lambda l, j: (l, 0, j)),
        out_shape=jax.ShapeDtypeStruct((depth, 2, n), F32),
        compiler_params=_cparams("arbitrary", "arbitrary"),
        name="modulation",
    )(ct, ada_w, ada_b.reshape(depth, 1, n))


def _store_values_t(vt_ref, v, dv):
    tm = v.shape[0]
    vt = v.T.astype(BF16)
    ones = jnp.ones((ONES_ROWS, tm), BF16)
    step = dv + ONES_ROWS
    for g in range(vt_ref.shape[0]):
        for h in range(HEAD_PAD // dv):
            src = g * HEAD_PAD + h * dv
            vt_ref[g, 0, h * step:h * step + dv, :] = vt[src:src + dv]
            vt_ref[g, 0, h * step + dv:(h + 1) * step, :] = ones


def _values_t_shape(t, tm, dv):
    return (4, t // tm, (HEAD_PAD // dv) * (dv + ONES_ROWS), tm)


def _premix_even_kernel(x_ref, vec_ref, w_ref, g_ref, hg_ref, cos_ref, sin_ref,
                        qa_ref, ka_ref, vat_ref, qb_ref, kb_ref, vbt_ref, *, use_rope):
    vec = vec_ref[...]
    h = (_rms(x_ref[...]) * vec[0:1] * (1.0 + vec[1:2]) + vec[2:3]).astype(BF16)
    gm = g_ref[...]
    hg = hg_ref[...]
    sw = 512
    if use_rope:
        cos = _lane_tile(cos_ref[...], sw)
        sin = _lane_tile(sin_ref[...], sw)
        first = (_lane_iota(sw) & 32) == 0

    def seg(i):
        return _dot(h, w_ref[:, i * sw:(i + 1) * sw])

    qa = _group_rms(seg(0), gm, 1.0 / A_DQK, hg[0:1])
    ka = _group_rms(seg(1), gm, 1.0 / A_DQK, hg[1:2])
    if use_rope:
        qa = _rope(qa, cos, sin, 32, first)
        ka = _rope(ka, cos, sin, 32, first)
    qa_ref[...] = (qa * (LOG2E / math.sqrt(A_DQK))).astype(BF16)
    ka_ref[...] = ka.astype(BF16)
    _store_values_t(vat_ref, seg(2), A_DV)
    qb = _group_rms(seg(3), gm, 1.0 / B_DH, hg[2:3])
    qb_ref[...] = (qb * (LOG2E / math.sqrt(B_DH))).astype(BF16)
    kb_ref[...] = _group_rms(seg(4), gm, 1.0 / B_DH, hg[3:4]).astype(BF16)
    _store_values_t(vbt_ref, seg(5), B_DH)


def _premix_even(x, vec, w, gmat, hg, cos, sin, use_rope):
    t, d = x.shape
    tm = min(FLASH_TK, t)
    row = lambda i: (i, 0)
    nat = jax.ShapeDtypeStruct((t, 512), BF16)
    nat_spec = pl.BlockSpec((tm, 512), row)
    vta = _values_t_shape(t, tm, A_DV)
    vtb = _values_t_shape(t, tm, B_DH)
    vt_spec = lambda s: pl.BlockSpec((s[0], 1) + s[2:], lambda i: (0, i, 0, 0))
    return pl.pallas_call(
        functools.partial(_premix_even_kernel, use_rope=use_rope),
        grid=(t // tm,),
        in_specs=[pl.BlockSpec((tm, d), row), _full(vec.shape), _full(w.shape), _full(gmat.shape),
                  _full(hg.shape), pl.BlockSpec((tm, LANES), row), pl.BlockSpec((tm, LANES), row)],
        out_specs=[nat_spec, nat_spec, vt_spec(vta), nat_spec, nat_spec, vt_spec(vtb)],
        out_shape=[nat, nat, jax.ShapeDtypeStruct(vta, BF16), nat, nat, jax.ShapeDtypeStruct(vtb, BF16)],
        compiler_params=_cparams("parallel"),
        name="premix_even",
    )(x, vec, w, gmat, hg, cos, sin)


_O_CQ, _O_CKV, _O_DQ, _O_DK, _O_DV, _O_CKR, _O_END = 0, 384, 640, 1152, 1408, 1664, 1792


def _premix_odd_kernel(x_ref, vec_ref, w_ref, wqb_ref, wk_ref, wv_ref, g128_ref, g64_ref, gain_ref, lora_ref,
                       cos_ref, sin_ref, cosc_ref, sinc_ref,
                       mq_ref, mk_ref, mvt_ref, gq_ref, gk_ref, gv_ref, *, use_rope):
    vec = vec_ref[...]
    h = (_rms(x_ref[...]) * vec[0:1] * (1.0 + vec[1:2]) + vec[2:3]).astype(BF16)
    gains = gain_ref[...]
    lora = lora_ref[...]
    g128 = g128_ref[...]
    g64 = g64_ref[...]
    mw = C_HEADS * HEAD_PAD
    if use_rope:
        cosc = _lane_tile(cosc_ref[...], mw)
        sinc = _lane_tile(sinc_ref[...], mw)
        first_c = (_lane_iota(mw) & 127) < (C_NOPE + C_ROPE // 2)
        cos = _lane_tile(cos_ref[...], 512)
        sin = _lane_tile(sin_ref[...], 512)
        first_d = (_lane_iota(512) & 32) == 0

    def seg(a, b):
        return _dot(h, w_ref[:, a:b])

    inv_c = 1.0 / (C_NOPE + C_ROPE)
    cq = (_rms(seg(_O_CQ, _O_CKV)) * lora[0:1, :C_Q_LORA]).astype(BF16)
    mq = _group_rms(_dot(cq, wqb_ref[...]), g128, inv_c, gains[0:1])
    if use_rope:
        mq = _rope(mq, cosc, sinc, C_ROPE // 2, first_c)
    mq_ref[...] = (mq * (LOG2E * math.sqrt(inv_c))).astype(BF16)
    ckv = (_rms(seg(_O_CKV, _O_DQ)) * lora[1:2, :C_KV_LORA]).astype(BF16)
    mk = _dot(ckv, wk_ref[...]) + _lane_tile(seg(_O_CKR, _O_END), mw)
    mk = _group_rms(mk, g128, inv_c, gains[1:2])
    if use_rope:
        mk = _rope(mk, cosc, sinc, C_ROPE // 2, first_c)
    mk_ref[...] = mk.astype(BF16)
    _store_values_t(mvt_ref, _dot(ckv, wv_ref[...]), C_DV)
    gq = _group_rms(seg(_O_DQ, _O_DK), g64, 1.0 / D_DH, gains[2:3, :512])
    gk = _group_rms(seg(_O_DK, _O_DV), g64, 1.0 / D_DH, gains[3:4, :256])
    if use_rope:
        gq = _rope(gq, cos, sin, 32, first_d)
        gk = _rope(gk, cos[:, :256], sin[:, :256], 32, first_d[:, :256])
    gq_ref[...] = (gq * (LOG2E / math.sqrt(D_DH))).astype(BF16)
    gk_ref[...] = gk.astype(BF16)
    gv_ref[...] = seg(_O_DV, _O_CKR).astype(BF16)


def _premix_odd(x, vec, w, wqb, wk, wv, g128, g64, gains, lora, cos, sin, cosc, sinc, use_rope):
    t, d = x.shape
    tm = min(FLASH_TK, t)
    row = lambda i: (i, 0)
    tab = pl.BlockSpec((tm, LANES), row)
    nat = lambda n: (pl.BlockSpec((tm, n), row), jax.ShapeDtypeStruct((t, n), BF16))
    vts = _values_t_shape(t, tm, C_DV)
    vt = (pl.BlockSpec((vts[0], 1) + vts[2:], lambda i: (0, i, 0, 0)), jax.ShapeDtypeStruct(vts, BF16))
    outs = [nat(1024), nat(1024), vt, nat(512), nat(256), nat(256)]
    return pl.pallas_call(
        functools.partial(_premix_odd_kernel, use_rope=use_rope),
        grid=(t // tm,),
        in_specs=[pl.BlockSpec((tm, d), row)] + [_full(a.shape) for a in (vec, w, wqb, wk, wv, g128, g64, gains, lora)]
                 + [tab] * 4,
        out_specs=[o[0] for o in outs],
        out_shape=[o[1] for o in outs],
        compiler_params=_cparams("parallel"),
        name="premix_odd",
    )(x, vec, w, wqb, wk, wv, g128, g64, gains, lora, cos, sin, cosc, sinc)


def _flash_kernel(*refs, mode, has_ctx, online, lam_init):
    bound_ref, q_ref, k_ref, vt_ref = refs[:4]
    kc_ref, vct_ref = refs[4:6] if has_ctx else (None, None)
    ex_ref, gcol_ref, o_ref, acc_ref, stage_ref = refs[6:11] if has_ctx else refs[4:9]
    qt = q_ref[...].astype(F32).T.astype(BF16)
    row = lax.broadcasted_iota(jnp.int32, (HEAD_PAD, 1), 0)
    if mode == "mla":
        qs = [qt[:HEAD_PAD], qt[HEAD_PAD:]]
    else:
        zero = jnp.zeros_like(qt)
        qs = [jnp.where(row < 64, qt, zero), jnp.where(row >= 64, qt, zero)]
    nk, tk = vt_ref.shape[1], vt_ref.shape[3]

    acc_ref[...] = jnp.zeros(acc_ref.shape, F32)
    rows_b = acc_ref.shape[1]

    def scores(kblk, b):
        kb = kblk[:, b * HEAD_PAD:(b + 1) * HEAD_PAD] if mode == "mla" else kblk
        return _dot(kb, qs[b])

    def values(vtblk, b):
        return vtblk if vtblk.shape[0] == rows_b else vtblk[b * rows_b:(b + 1) * rows_b]

    if online:
        m_ref, mx_ref = refs[-2:]
        m_ref[...] = jnp.full(m_ref.shape, NEG_INF, F32)

        def absorb(s, smax, vtblk, b):
            m_prev = m_ref[b]
            m_new = jnp.maximum(m_prev, smax)
            alpha = jnp.exp2(m_prev - m_new)
            p = jnp.exp2(s - m_new).astype(BF16)
            acc_ref[b] = alpha * acc_ref[b] + _dot(values(vtblk, b), p)
            m_ref[b] = m_new

        def produce(slot, kblk):
            for b in range(2):
                s = scores(kblk, b)
                stage_ref[slot, b] = s
                mx_ref[slot, b] = jnp.max(s, axis=0, keepdims=True)

        def consume(slot, vtblk):
            for b in range(2):
                absorb(stage_ref[slot, b], mx_ref[slot, b], vtblk, b)

        def direct(kblk, vtblk):
            for b in range(2):
                s = scores(kblk, b)
                absorb(s, jnp.max(s, axis=0, keepdims=True), vtblk, b)
    else:
        bound = bound_ref[0]

        def probs(kblk, b):
            return jnp.exp2(scores(kblk, b) - bound).astype(BF16)

        def produce(slot, kblk):
            for b in range(2):
                stage_ref[slot, b] = probs(kblk, b)

        def consume(slot, vtblk):
            for b in range(2):
                acc_ref[b] += _dot(values(vtblk, b), stage_ref[slot, b])

        def direct(kblk, vtblk):
            for b in range(2):
                acc_ref[b] += _dot(values(vtblk, b), probs(kblk, b))

    if nk == 1:
        if has_ctx:
            direct(kc_ref[...], vct_ref[0])
        direct(k_ref[...], vt_ref[0, 0])
    else:
        unroll = FLASH_UNROLL if nk % FLASH_UNROLL == 0 else 2
        assert nk % unroll == 0
        produce(0, k_ref[0:tk, :])
        if has_ctx:
            direct(kc_ref[...], vct_ref[0])

        def body(i, carry):
            c0 = unroll * i
            for u in range(unroll):
                nxt = jnp.minimum(c0 + u + 1, nk - 1)
                produce((u + 1) % 2, k_ref[pl.ds(pl.multiple_of(nxt * tk, tk), tk), :])
                consume(u % 2, vt_ref[0, c0 + u])
            return carry

        lax.fori_loop(0, nk // unroll, body, 0)

    dv = rows_b - ONES_ROWS
    o0 = acc_ref[0, :dv] / acc_ref[0, dv:dv + 1]
    o1 = acc_ref[1, :dv] / acc_ref[1, dv:dv + 1]
    if mode == "diff":
        ex = ex_ref[...]
        lam = (jnp.exp(jnp.sum(ex[0:1] * ex[1:2], axis=-1, keepdims=True))
               - jnp.exp(jnp.sum(ex[2:3] * ex[3:4], axis=-1, keepdims=True)) + lam_init)
        o = o0 - lam * o1
        o = o * lax.rsqrt(jnp.mean(o * o, axis=0, keepdims=True) + EPS) * gcol_ref[...] * (1.0 - lam_init)
    else:
        o = jnp.concatenate([o0, o1], axis=0)
    o_ref[...] = o.T.astype(o_ref.dtype)


def _flash(q, k, vt, kc, vct, ex, gcol, bound=None, *, mode, lam_init=0.0):
    t = q.shape[0]
    tkeys = k.shape[0]
    qw = 2 * HEAD_PAD if mode == "mla" else HEAD_PAD
    groups = q.shape[1] // qw
    tq = min(FLASH_TQ, t)
    _, nk, vrows, tk = vt.shape
    assert nk * tk == tkeys
    has_ctx = kc is not None
    dv = HEAD_PAD if mode == "diff" else HEAD_PAD // 2
    rows_b = dv + ONES_ROWS
    bound_arr = jnp.zeros((1,), F32) if bound is None else jnp.reshape(bound, (1,)).astype(F32)
    in_specs = [pl.BlockSpec(memory_space=pltpu.SMEM),
                pl.BlockSpec((tq, qw), lambda g, i: (i, g)),
                pl.BlockSpec((tkeys, qw), lambda g, i: (0, g)),
                pl.BlockSpec((1, nk, vrows, tk), lambda g, i: (g, 0, 0, 0))]
    args = [bound_arr, q, k, vt]
    if has_ctx:
        lc = kc.shape[0]
        in_specs += [pl.BlockSpec((lc, qw), lambda g, i: (0, g)),
                     pl.BlockSpec((1, vrows, lc), lambda g, i: (g, 0, 0))]
        args += [kc, vct.reshape(groups, vrows, lc)]
    in_specs += [_full(ex.shape), _full(gcol.shape)]
    args += [ex, gcol]

    def call(online):
        acc = pltpu.VMEM((2, rows_b, tq), F32)
        if online:
            scratch = [acc, pltpu.VMEM((2, 2, tk, tq), F32), pltpu.VMEM((2, 1, tq), F32), pltpu.VMEM((2, 2, 1, tq), F32)]
        else:
            scratch = [acc, pltpu.VMEM((2, 2, tk, tq), BF16)]
        return pl.pallas_call(
            functools.partial(_flash_kernel, mode=mode, has_ctx=has_ctx, online=online, lam_init=lam_init),
            grid=(groups, t // tq),
            in_specs=in_specs,
            out_specs=pl.BlockSpec((tq, HEAD_PAD), lambda g, i: (i, g)),
            out_shape=jax.ShapeDtypeStruct((t, groups * HEAD_PAD), BF16),
            scratch_shapes=scratch,
            compiler_params=_cparams("parallel", "parallel"),
            name="flash_" + mode + ("_online" if online else "_bounded"),
        )(*args)

    if bound is None:
        return call(True)
    return lax.cond(bound <= FLASH_SAFE_BOUND, lambda: call(False), lambda: call(True))


def _na_kernel(q_ref, kp_ref, kcur_ref, kn_ref, vtp_ref, vtcur_ref, vtn_ref, kc_ref, vct_ref, bias_ref, o_ref,
               s_ref, mx_ref):
    heads = q_ref.shape[1] // B_DH
    lc = kc_ref.shape[0]
    step = B_DH + ONES_ROWS
    row = lax.broadcasted_iota(jnp.int32, (HEAD_PAD, 1), 0)

    def pair_cols(h):
        return slice((h // 2) * HEAD_PAD, (h // 2 + 1) * HEAD_PAD)

    def produce(slot, h):
        ps = pair_cols(h)
        qt = q_ref[:, ps].astype(F32).T.astype(BF16)
        qb = jnp.where((row < 64) if h % 2 == 0 else (row >= 64), qt, jnp.zeros_like(qt))
        k_nb = jnp.concatenate([kp_ref[:, ps], kcur_ref[:, ps], kn_ref[:, ps]], axis=0)
        s_ctx = _dot(kc_ref[:, ps], qb)
        s_nb = _dot(k_nb, qb) + bias_ref[0, h]
        s_ref[slot, 0:lc] = s_ctx
        s_ref[slot, lc:] = s_nb
        mx_ref[slot] = jnp.maximum(jnp.max(s_ctx, axis=0, keepdims=True), jnp.max(s_nb, axis=0, keepdims=True))

    def consume(slot, h):
        pr, b = h // 2, h % 2
        vt = jnp.concatenate([vct_ref[pr, 0], vtp_ref[pr, 0], vtcur_ref[pr, 0], vtn_ref[pr, 0]], axis=1)
        p = jnp.exp2(s_ref[slot] - mx_ref[slot]).astype(BF16)
        acc = _dot(vt[b * step:(b + 1) * step], p)
        return acc[:B_DH] / acc[B_DH:B_DH + 1]

    produce(0, 0)
    outs = []
    for h in range(heads):
        if h + 1 < heads:
            produce((h + 1) % 2, h + 1)
        outs.append(consume(h % 2, h))
        if h % 2:
            o_ref[:, pair_cols(h)] = jnp.concatenate(outs[-2:], axis=0).T.astype(o_ref.dtype)


def _na_bias_table(rpb, rows):
    nr, w = NA_ROWS, GRID_W
    heads = rpb.shape[0]
    pad = w - NA_WIN_W
    padded = jnp.pad(rpb.astype(F32) * LOG2E, ((0, 0), (0, 0), (pad, pad + 1)))
    skew = jnp.tile(padded, (1, 1, w))[:, :, :w * (2 * w - 1)].reshape(heads, -1, w, 2 * w - 1)
    toep = skew[:, :, :, w - 1:]
    cq = np.arange(w)
    cs = np.clip(cq - NA_WIN_W // 2, 0, w - NA_WIN_W)
    valid_c = (cq[None, :] >= cs[:, None]) & (cq[None, :] < cs[:, None] + NA_WIN_W)
    toep_t = jnp.swapaxes(jnp.where(jnp.asarray(valid_c), toep, NEG_INF), -1, -2)
    d0 = NA_WIN_H - 1 - nr
    assert d0 - (nr - 1) >= 0 and d0 + 3 * nr <= 2 * NA_WIN_H - 1
    dense = jnp.concatenate([toep_t[:, d0 - rl:d0 - rl + 3 * nr] for rl in range(nr)], axis=-1)
    wh = min(NA_WIN_H, rows)
    rl = np.arange(nr)
    rr_rel = np.arange(3 * nr)
    valid_all = []
    for base in (0, nr, rows - nr):
        rs = np.clip(base + rl - wh // 2, 0, rows - wh)
        rr = base - nr + rr_rel
        valid_all.append((rr[:, None] >= rs[None, :]) & (rr[:, None] < rs[None, :] + wh))
    valid_r = np.repeat(np.stack(valid_all), w, axis=-1)[:, None, :, None, :]
    tab = jnp.where(jnp.asarray(valid_r), dense[None], NEG_INF)
    return tab.reshape(3, heads, 3 * nr * w, nr * w)


def _neighbourhood(q, k, vt, kc, vct, bias):
    t = q.shape[0]
    qn = NA_ROWS * GRID_W
    nb = t // qn
    lc = kc.shape[0]
    pairs, _, vrows, tk = vt.shape
    per = tk // qn
    cur = lambda i: (i, 0)
    prev = lambda i: (jnp.maximum(i - 1, 0), 0)
    nxt = lambda i: (jnp.minimum(i + 1, nb - 1), 0)
    blk = lambda f: pl.BlockSpec((qn, q.shape[1]), f)
    vblk = lambda f: pl.BlockSpec((pairs, 1, vrows, qn), lambda i: (0, f(i)[0] // per, 0, f(i)[0] % per))
    case = lambda i: (jnp.where(i == 0, 0, jnp.where(i == nb - 1, 2, 1)), 0, 0, 0)
    return pl.pallas_call(
        _na_kernel,
        grid=(nb,),
        in_specs=[blk(cur), blk(prev), blk(cur), blk(nxt), vblk(prev), vblk(cur), vblk(nxt),
                  _full(kc.shape), _full(vct.shape), pl.BlockSpec((1,) + bias.shape[1:], case)],
        out_specs=blk(cur),
        out_shape=jax.ShapeDtypeStruct(q.shape, BF16),
        scratch_shapes=[pltpu.VMEM((2, lc + 3 * qn, qn), F32), pltpu.VMEM((2, 1, qn), F32)],
        compiler_params=_cparams("arbitrary"),
        name="neighbourhood",
    )(q, k, k, k, vt, vt, vt, kc, vct, bias)


def _wgqa_kernel(sink_ref, q_ref, kp_ref, kcur_ref, kn_ref, vp_ref, vcur_ref, vn_ref, kc_ref, vc_ref, o_ref):
    i = pl.program_id(0)
    nb = pl.num_programs(0)
    qb = q_ref.shape[0]
    lc = kc_ref.shape[0]
    nkeys = lc + 3 * qb
    lane = _lane_iota(LANES)
    col = lax.broadcasted_iota(jnp.int32, (qb, nkeys), 1)
    row = lax.broadcasted_iota(jnp.int32, (qb, nkeys), 0)
    rel = col - (lc + qb) - row
    lo = jnp.where(i > 0, lc, lc + qb)
    hi = jnp.where(i < nb - 1, nkeys, lc + 2 * qb)
    valid = (col < lc) | ((jnp.abs(rel) <= D_WINDOW) & (col >= lo) & (col < hi))
    groups = D_HEADS // D_KV_HEADS
    for kv in range(D_KV_HEADS):
        ks = slice(kv * HEAD_PAD, (kv + 1) * HEAD_PAD)
        kk = jnp.concatenate([kc_ref[:, ks], kp_ref[:, ks], kcur_ref[:, ks], kn_ref[:, ks]], axis=0)
        vv = jnp.concatenate([vc_ref[:, ks], vp_ref[:, ks], vcur_ref[:, ks], vn_ref[:, ks]], axis=0)
        lhs = []
        for g in range(groups):
            h = kv * groups + g
            qp = q_ref[:, (h // 2) * HEAD_PAD:(h // 2 + 1) * HEAD_PAD]
            keep = (lane < 64) if h % 2 == 0 else (lane >= 64)
            lhs.append(jnp.where(keep, qp, jnp.zeros_like(qp)))
        s = _dot_nt(jnp.concatenate(lhs, axis=0), kk)
        outs = []
        for g in range(groups):
            sink = sink_ref[kv * groups + g] * LOG2E
            sg = jnp.where(valid, s[g * qb:(g + 1) * qb], NEG_INF)
            m = jnp.maximum(jnp.max(sg, axis=-1, keepdims=True), sink)
            p = jnp.exp2(sg - m)
            l = jnp.sum(p, axis=-1, keepdims=True) + jnp.exp2(sink - m)
            outs.append(_dot(p.astype(BF16), vv) / l)
        for pp in range(groups // 2):
            pair = kv * (groups // 2) + pp
            o_ref[:, pair * HEAD_PAD:(pair + 1) * HEAD_PAD] = jnp.where(
                lane < 64, outs[2 * pp], outs[2 * pp + 1]).astype(o_ref.dtype)


def _window_gqa(q, k, v, kc, vc, sink):
    t = q.shape[0]
    qb = D_WINDOW
    nb = t // qb
    lc = kc.shape[0]
    kw = k.shape[1]
    cur = lambda i: (i, 0)
    prev = lambda i: (jnp.maximum(i - 1, 0), 0)
    nxt = lambda i: (jnp.minimum(i + 1, nb - 1), 0)
    kblk = lambda f: pl.BlockSpec((qb, kw), f)
    return pl.pallas_call(
        _wgqa_kernel,
        grid=(nb,),
        in_specs=[pl.BlockSpec(memory_space=pltpu.SMEM), pl.BlockSpec((qb, q.shape[1]), cur),
                  kblk(prev), kblk(cur), kblk(nxt), kblk(prev), kblk(cur), kblk(nxt),
                  _full((lc, kw)), _full((lc, kw))],
        out_specs=pl.BlockSpec((qb, q.shape[1]), cur),
        out_shape=jax.ShapeDtypeStruct(q.shape, BF16),
        compiler_params=_cparams("parallel"),
        name="window_gqa",
    )(sink, q, k, k, k, v, v, v, kc, vc)


def _postmix_kernel(o1_ref, o2_ref, x_ref, w_ref, vec_ref, x1_ref, h2_ref):
    half = o1_ref.shape[1]
    y = _dot(o1_ref[...], w_ref[:half, :]) + _dot(o2_ref[...], w_ref[half:, :])
    vec = vec_ref[...]
    x1 = x_ref[...] + vec[0:1] * y
    x1_ref[...] = x1
    h2_ref[...] = (_rms(x1) * vec[1:2] * (1.0 + vec[2:3]) + vec[3:4]).astype(BF16)


def _postmix(o1, o2, x, w, vec):
    t, d = x.shape
    tm = min(512, t)
    row = lambda i: (i, 0)
    return pl.pallas_call(
        _postmix_kernel,
        grid=(t // tm,),
        in_specs=[pl.BlockSpec((tm, o1.shape[1]), row), pl.BlockSpec((tm, o2.shape[1]), row),
                  pl.BlockSpec((tm, d), row), _full(w.shape), _full(vec.shape)],
        out_specs=[pl.BlockSpec((tm, d), row)] * 2,
        out_shape=[jax.ShapeDtypeStruct((t, d), F32), jax.ShapeDtypeStruct((t, d), BF16)],
        compiler_params=_cparams("parallel"),
        name="postmix",
    )(o1, o2, x, w, vec)


def _ffn_kernel(h_ref, hp_ref, hn_ref, x1_ref, wup_ref, cv_ref, wdn_ref, g2_ref, o_ref, acc_ref, hcat_ref, u_ref):
    i = pl.program_id(0)
    n = pl.num_programs(0)
    tm = h_ref.shape[0]
    rows = tm + 2 * HALO
    nchunk = wdn_ref.shape[0] // FFN_CHUNK
    hcat_ref[0:HALO] = hp_ref[...]
    hcat_ref[HALO:HALO + tm] = h_ref[...]
    hcat_ref[HALO + tm:rows] = hn_ref[...]

    @pl.when(i == 0)
    def _():
        hcat_ref[0:HALO] = jnp.zeros((HALO, hcat_ref.shape[1]), BF16)

    @pl.when(i == n - 1)
    def _():
        hcat_ref[HALO + tm:rows] = jnp.zeros((HALO, hcat_ref.shape[1]), BF16)

    acc_ref[...] = jnp.zeros(acc_ref.shape, F32)

    def conv(u, cv):
        before = pltpu.roll(u, 1, axis=0)
        after = pltpu.roll(u, rows - 1, axis=0)
        r = cv[3:4] + before * cv[0:1] + u * cv[1:2] + after * cv[2:3]
        return r[HALO:HALO + tm]

    def cols(c, gate):
        return pl.ds(pl.multiple_of((gate * nchunk + c) * FFN_CHUNK, FFN_CHUNK), FFN_CHUNK)

    def stage_up(slot, c):
        hcat = hcat_ref[...]
        u_ref[slot, 0] = _dot(hcat, wup_ref[:, cols(c, 0)])
        u_ref[slot, 1] = _dot(hcat, wup_ref[:, cols(c, 1)])

    def stage_down(slot, c):
        a = conv(u_ref[slot, 0], cv_ref[:, cols(c, 0)])
        g = conv(u_ref[slot, 1], cv_ref[:, cols(c, 1)])
        act = (a * (g * _sigmoid(g))).astype(BF16)
        acc_ref[...] += _dot(act, wdn_ref[cols(c, 0), :])

    assert nchunk % 2 == 1
    stage_up(0, 0)

    def body(j, carry):
        c = 2 * j
        stage_up(1, c + 1)
        stage_down(0, c)
        stage_up(0, c + 2)
        stage_down(1, c + 1)
        return carry

    lax.fori_loop(0, nchunk // 2, body, 0)
    stage_down(0, nchunk - 1)
    o_ref[...] = x1_ref[...] + g2_ref[...] * acc_ref[...]


def _conv_ffn(h2, x1, wup, cv, wdn, g2):
    t, d = x1.shape
    tm = min(512, t)
    nt = t // tm
    hb = tm // HALO
    row = lambda i: (i, 0)
    prev = lambda i: (jnp.maximum(i * hb - 1, 0), 0)
    nxt = lambda i: (jnp.minimum((i + 1) * hb, t // HALO - 1), 0)
    return pl.pallas_call(
        _ffn_kernel,
        grid=(nt,),
        in_specs=[pl.BlockSpec((tm, d), row), pl.BlockSpec((HALO, d), prev), pl.BlockSpec((HALO, d), nxt),
                  pl.BlockSpec((tm, d), row), _full(wup.shape), _full(cv.shape), _full(wdn.shape), _full(g2.shape)],
        out_specs=pl.BlockSpec((tm, d), row),
        out_shape=jax.ShapeDtypeStruct((t, d), F32),
        scratch_shapes=[pltpu.VMEM((tm, d), F32), pltpu.VMEM((tm + 2 * HALO, d), BF16),
                        pltpu.VMEM((2, 2, tm + 2 * HALO, FFN_CHUNK), F32)],
        compiler_params=_cparams("parallel"),
        name="conv_ffn",
    )(h2, h2, h2, x1, wup, cv, wdn, g2)


def _axial_tables(n_tokens, dim):
    t = jnp.arange(n_tokens, dtype=jnp.int32)
    row = (t // GRID_W).astype(F32)
    col = (t % GRID_W).astype(F32)
    quarter = dim // 4
    inv_freq = ROPE_BASE ** (-jnp.arange(quarter, dtype=F32) / quarter)
    ang = jnp.concatenate([row[:, None] * inv_freq, col[:, None] * inv_freq], axis=-1)
    return jnp.cos(ang), jnp.sin(ang)


def _rope_tables_head64(t):
    cos, sin = _axial_tables(t, 64)
    return jnp.tile(jnp.concatenate([cos, cos], -1), (1, 2)), jnp.tile(jnp.concatenate([-sin, sin], -1), (1, 2))


def _rope_tables_latent(t):
    cos, sin = _axial_tables(t, C_ROPE)
    ones = jnp.ones((t, C_NOPE), F32)
    tail = jnp.ones((t, HEAD_PAD - C_NOPE - C_ROPE), F32)
    cosc = jnp.concatenate([ones, cos, cos, tail], -1)
    sinc = jnp.concatenate([0 * ones, -sin, sin, 0 * tail], -1)
    return cosc, sinc


def _group_ones(width, group):
    return jnp.asarray(np.kron(np.eye(width // group), np.ones((group, group))), BF16)


def _pad_heads(a, heads, dim):
    a = a.reshape(a.shape[:-1] + (heads, dim))
    a = jnp.pad(a, [(0, 0)] * (a.ndim - 1) + [(0, HEAD_PAD - dim)])
    return a.reshape(a.shape[:-2] + (heads * HEAD_PAD,))


def _logit_bound(gq, gk, dim):
    return 1.02 * LOG2E * math.sqrt(dim) * jnp.max(jnp.abs(gq)) * jnp.max(jnp.abs(gk))


def _pad_row(v, width):
    return jnp.pad(v, (0, width - v.shape[0]))


def _ffn_weights(w_up, conv_w, conv_b, w_down):
    cv = jnp.concatenate([conv_w, conv_b[None]], axis=0)
    return w_up.astype(BF16), cv, w_down.astype(BF16)


def kernel(x, c, ctx, c_ctx, ada_w, ada_b, norm_mix, norm_ffn, ffn_up, ffn_conv_w, ffn_conv_b, ffn_down, ev_w_in, ev_w_out, a_q_norm, a_k_norm, a_lam_q1, a_lam_k1, a_lam_q2, a_lam_k2, a_subln, b_q_norm, b_k_norm, b_rpb, od_w_in, od_w_out, c_q_a_norm, c_w_qb, c_kv_a_norm, c_w_kvb, c_q_norm, c_k_norm, d_q_norm, d_k_norm, d_sink):
    assert x.shape[0] == 1 and ctx.shape[0] == 1
    xm = x[0]
    xc = ctx[0]
    t, d = xm.shape
    lc = xc.shape[0]
    mods = _modulation(c, c_ctx, ada_w, ada_b)
    cos64, sin64 = _rope_tables_head64(t)
    cosc, sinc = _rope_tables_latent(t)
    g64 = _group_ones(MXU_TILE, 64)
    g128 = _group_ones(MXU_TILE, HEAD_PAD)
    dummy_tab = jnp.zeros((lc, LANES), F32)

    for l in range(DEPTH):
        last = l == DEPTH - 1
        i = l // 2
        mm = mods[l, 0].reshape(6, d)
        mc = mods[l, 1].reshape(6, d)
        pre_m = jnp.stack([norm_mix[l], mm[1], mm[0]])
        pre_c = jnp.stack([norm_mix[l], mc[1], mc[0]])
        post_m = jnp.stack([mm[2], norm_ffn[l], mm[4], mm[3]])
        post_c = jnp.stack([mc[2], norm_ffn[l], mc[4], mc[3]])
        wup, cv, wdn = _ffn_weights(ffn_up[l], ffn_conv_w[l], ffn_conv_b[l], ffn_down[l])
        if l % 2 == 0:
            lam_init = 0.8 - 0.6 * math.exp(-0.3 * l)
            w_in = ev_w_in[i].astype(BF16)
            w_out = ev_w_out[i].astype(BF16)
            hg = jnp.stack([jnp.tile(a_q_norm[i], 8), jnp.tile(a_k_norm[i], 8),
                            jnp.tile(b_q_norm[i], 8), jnp.tile(b_k_norm[i], 8)])
            ex = jnp.stack([_pad_row(a_lam_q1[i], LANES), _pad_row(a_lam_k1[i], LANES),
                            _pad_row(a_lam_q2[i], LANES), _pad_row(a_lam_k2[i], LANES)])
            gcol = a_subln[i][:, None]
            qa, ka, va, qb, kb, vb = _premix_even(xm, pre_m, w_in, g64, hg, cos64, sin64, True)
            qa_c, ka_c, va_c, qb_c, kb_c, vb_c = _premix_even(xc, pre_c, w_in, g64, hg, dummy_tab, dummy_tab, False)
            oa = _flash(qa, ka, va, ka_c, va_c, ex, gcol, _logit_bound(a_q_norm[i], a_k_norm[i], A_DQK),
                        mode="diff", lam_init=lam_init)
            ob = _neighbourhood(qb, kb, vb, kb_c, vb_c, _na_bias_table(b_rpb[i], t // GRID_W))
            if not last:
                oa_c = _flash(qa_c, ka_c, va_c, None, None, ex, gcol, mode="diff", lam_init=lam_init)
                ob_c = _flash(qb_c, kb_c, vb_c, None, None, ex, gcol, mode="pair")
        else:
            w = od_w_in[i]
            cq, ckv, ckr, dq, dk, dv = jnp.split(w, np.cumsum([C_Q_LORA, C_KV_LORA, C_ROPE, 512, 128]).tolist(), axis=1)
            dup = lambda a: jnp.concatenate([a[:, :64], a[:, :64], a[:, 64:], a[:, 64:]], axis=1)
            ckr_pad = jnp.pad(ckr, ((0, 0), (C_NOPE, HEAD_PAD - C_NOPE - C_ROPE)))
            w_in = jnp.concatenate([cq, ckv, dq, dup(dk), dup(dv), ckr_pad], axis=1).astype(BF16)
            w_out = od_w_out[i].astype(BF16)
            wqb = _pad_heads(c_w_qb[i], C_HEADS, C_NOPE + C_ROPE).astype(BF16)
            kvb = c_w_kvb[i].reshape(C_KV_LORA, C_HEADS, C_NOPE + C_DV)
            wk = _pad_heads(kvb[:, :, :C_NOPE].reshape(C_KV_LORA, -1), C_HEADS, C_NOPE).astype(BF16)
            wv = kvb[:, :, C_NOPE:].reshape(C_KV_LORA, C_HEADS * C_DV).astype(BF16)
            gains = jnp.stack([jnp.tile(_pad_row(c_q_norm[i], HEAD_PAD), C_HEADS),
                               jnp.tile(_pad_row(c_k_norm[i], HEAD_PAD), C_HEADS),
                               _pad_row(jnp.tile(d_q_norm[i], 8), 1024), _pad_row(jnp.tile(d_k_norm[i], 4), 1024)])
            lora = jnp.stack([_pad_row(c_q_a_norm[i], 512), _pad_row(c_kv_a_norm[i], 512)])
            odd = functools.partial(_premix_odd, w=w_in, wqb=wqb, wk=wk, wv=wv, g128=g128, g64=g64, gains=gains, lora=lora)
            mq, mk, mv, gq, gk, gv = odd(xm, pre_m, cos=cos64, sin=sin64, cosc=cosc, sinc=sinc, use_rope=True)
            mq_c, mk_c, mv_c, gq_c, gk_c, gv_c = odd(xc, pre_c, cos=dummy_tab, sin=dummy_tab, cosc=dummy_tab,
                                                     sinc=dummy_tab, use_rope=False)
            ex = jnp.zeros((4, LANES), F32)
            oa = _flash(mq, mk, mv, mk_c, mv_c, ex, jnp.ones((HEAD_PAD, 1), F32),
                        _logit_bound(c_q_norm[i], c_k_norm[i], C_NOPE + C_ROPE), mode="mla")
            ob = _window_gqa(gq, gk, gv, gk_c, gv_c, d_sink[i])
            if not last:
                raise NotImplementedError("context update after an odd layer is not needed at this depth")
        x1, h2 = _postmix(oa, ob, xm, w_out, post_m)
        xm = _conv_ffn(h2, x1, wup, cv, wdn, mm[5][None])
        if not last:
            xc1, hc2 = _postmix(oa_c, ob_c, xc, w_out, post_c)
            xc = _conv_ffn(hc2, xc1, wup, cv, wdn, mc[5][None])
    return xm[None]
```

```python
import functools
import math

import numpy as np
import jax
import jax.numpy as jnp
from jax import lax
from jax.experimental import pallas as pl
from jax.experimental.pallas import tpu as pltpu

D_MODEL = 1024
DEPTH = 2
GRID_W = 64
EPS = 1e-6
ROPE_BASE = 10000.0
NEG_INF = -1e30
LOG2E = math.log2(math.e)

A_HEADS = 4
A_DQK = 64
A_DV = 128
B_HEADS = 8
B_DH = 64
NA_WIN_H = 8
NA_WIN_W = 16
C_HEADS = 8
C_Q_LORA = 384
C_KV_LORA = 256
C_NOPE = 64
C_ROPE = 32
C_DV = 64
D_HEADS = 8
D_KV_HEADS = 2
D_DH = 64
D_WINDOW = 128
FFN_HIDDEN = 2816
CONV_W = 3

LANES = 128
MXU_TILE = 256
HEAD_PAD = 128
VMEM_LIMIT = 56 * 1024 * 1024
FFN_CHUNK = 256
HALO = 16
NA_ROWS = 4
ONES_ROWS = 16
FLASH_TQ = 1024
FLASH_TK = 512
FLASH_UNROLL = 4
FLASH_SAFE_BOUND = 40.0

F32 = jnp.float32
BF16 = jnp.bfloat16


def _cparams(*sem):
    return pltpu.CompilerParams(dimension_semantics=sem, vmem_limit_bytes=VMEM_LIMIT)


def _full(shape):
    n = len(shape)
    return pl.BlockSpec(shape, lambda *_: (0,) * n)


def _resident(shape):
    n = len(shape)
    return pl.BlockSpec(shape, lambda *_: (0,) * n, pipeline_mode=pl.Buffered(1))


def _dot(a, b):
    return jnp.dot(a, b, preferred_element_type=F32)


def _dot_nt(a, b):
    return lax.dot_general(a, b, (((1,), (1,)), ((), ())), preferred_element_type=F32)


def _rms(x):
    return x * lax.rsqrt(jnp.mean(x * x, axis=-1, keepdims=True) + EPS)


def _sigmoid(x):
    return 1.0 / (1.0 + jnp.exp(-x))


def _group_rms(y, gmat, inv_n, gain):
    yy = (y * y).astype(BF16)
    slab = gmat.shape[0]
    ss = jnp.concatenate([_dot(yy[:, c:c + slab], gmat) for c in range(0, y.shape[-1], slab)], axis=-1)
    return y * lax.rsqrt(ss * inv_n + EPS) * gain


def _rope(y, cos, sin, half, first_half):
    w = y.shape[-1]
    fwd = pltpu.roll(y, w - half, axis=1)
    bwd = pltpu.roll(y, half, axis=1)
    return y * cos + jnp.where(first_half, fwd, bwd) * sin


def _lane_tile(t, width):
    reps = width // t.shape[-1]
    return t if reps == 1 else jnp.concatenate([t] * reps, axis=-1)


def _lane_iota(width):
    return lax.broadcasted_iota(jnp.int32, (1, width), 1)


def _mod_kernel(ct_ref, w_ref, b_ref, o_ref):
    ct = ct_ref[...]
    s = ct * _sigmoid(ct)
    w = w_ref[0]
    rows = [jnp.sum(w * s[:, j:j + 1], axis=0, keepdims=True) for j in range(2)]
    o_ref[0] = jnp.concatenate(rows, axis=0) + b_ref[0]


def _modulation(c, c_ctx, ada_w, ada_b):
    depth, d, n = ada_w.shape
    tn = 1536
    ct = jnp.stack([c[0], c_ctx], axis=1)
    return pl.pallas_call(
        _mod_kernel,
        grid=(depth, n // tn),
        in_specs=[_full((d, 2)),
                  pl.BlockSpec((1, d, tn), lambda l, j: (l, 0, j)),
                  pl.BlockSpec((1, 1, tn), lambda l, j: (l, 0, j))],
        out_specs=pl.BlockSpec((1, 2, tn), lambda l, j: (l, 0, j)),
        out_shape=jax.ShapeDtypeStruct((depth, 2, n), F32),
        compiler_params=_cparams("arbitrary", "arbitrary"),
        name="modulation",
    )(ct, ada_w, ada_b.reshape(depth, 1, n))


def _store_values_t(vt_ref, v, dv):
    tm = v.shape[0]
    vt = v.T.astype(BF16)
    ones = jnp.ones((ONES_ROWS, tm), BF16)
    step = dv + ONES_ROWS
    for g in range(vt_ref.shape[0]):
        for h in range(HEAD_PAD // dv):
            src = g * HEAD_PAD + h * dv
            vt_ref[g, 0, h * step:h * step + dv, :] = vt[src:src + dv]
            vt_ref[g, 0, h * step + dv:(h + 1) * step, :] = ones


def _values_t_shape(t, tm, dv):
    return (4, t // tm, (HEAD_PAD // dv) * (dv + ONES_ROWS), tm)


def _premix_even_kernel(x_ref, vec_ref, w_ref, g_ref, hg_ref, cos_ref, sin_ref,
                        qa_ref, ka_ref, vat_ref, qb_ref, kb_ref, vbt_ref, *, use_rope):
    vec = vec_ref[...]
    h = (_rms(x_ref[...]) * vec[0:1] * (1.0 + vec[1:2]) + vec[2:3]).astype(BF16)
    gm = g_ref[...]
    hg = hg_ref[...]
    sw = 512
    if use_rope:
        cos = _lane_tile(cos_ref[...], sw)
        sin = _lane_tile(sin_ref[...], sw)
        first = (_lane_iota(sw) & 32) == 0

    def seg(i):
        return _dot(h, w_ref[:, i * sw:(i + 1) * sw])

    qa = _group_rms(seg(0), gm, 1.0 / A_DQK, hg[0:1])
    ka = _group_rms(seg(1), gm, 1.0 / A_DQK, hg[1:2])
    if use_rope:
        qa = _rope(qa, cos, sin, 32, first)
        ka = _rope(ka, cos, sin, 32, first)
    qa_ref[...] = (qa * (LOG2E / math.sqrt(A_DQK))).astype(BF16)
    ka_ref[...] = ka.astype(BF16)
    _store_values_t(vat_ref, seg(2), A_DV)
    qb = _group_rms(seg(3), gm, 1.0 / B_DH, hg[2:3])
    qb_ref[...] = (qb * (LOG2E / math.sqrt(B_DH))).astype(BF16)
    kb_ref[...] = _group_rms(seg(4), gm, 1.0 / B_DH, hg[3:4]).astype(BF16)
    _store_values_t(vbt_ref, seg(5), B_DH)


def _premix_even(x, vec, w, gmat, hg, cos, sin, use_rope):
    t, d = x.shape
    tm = min(FLASH_TK, t)
    row = lambda i: (i, 0)
    nat = jax.ShapeDtypeStruct((t, 512), BF16)
    nat_spec = pl.BlockSpec((tm, 512), row)
    vta = _values_t_shape(t, tm, A_DV)
    vtb = _values_t_shape(t, tm, B_DH)
    vt_spec = lambda s: pl.BlockSpec((s[0], 1) + s[2:], lambda i: (0, i, 0, 0))
    return pl.pallas_call(
        functools.partial(_premix_even_kernel, use_rope=use_rope),
        grid=(t // tm,),
        in_specs=[pl.BlockSpec((tm, d), row), _full(vec.shape), _full(w.shape), _full(gmat.shape),
                  _full(hg.shape), pl.BlockSpec((tm, LANES), row), pl.BlockSpec((tm, LANES), row)],
        out_specs=[nat_spec, nat_spec, vt_spec(vta), nat_spec, nat_spec, vt_spec(vtb)],
        out_shape=[nat, nat, jax.ShapeDtypeStruct(vta, BF16), nat, nat, jax.ShapeDtypeStruct(vtb, BF16)],
        compiler_params=_cparams("parallel"),
        name="premix_even",
    )(x, vec, w, gmat, hg, cos, sin)


_O_CQ, _O_CKV, _O_DQ, _O_DK, _O_DV, _O_CKR, _O_END = 0, 384, 640, 1152, 1408, 1664, 1792


def _premix_odd_kernel(x_ref, vec_ref, w_ref, wqb_ref, wk_ref, wv_ref, g128_ref, g64_ref, gain_ref, lora_ref,
                       cos_ref, sin_ref, cosc_ref, sinc_ref,
                       mq_ref, mk_ref, mvt_ref, gq_ref, gk_ref, gv_ref, *, use_rope):
    vec = vec_ref[...]
    h = (_rms(x_ref[...]) * vec[0:1] * (1.0 + vec[1:2]) + vec[2:3]).astype(BF16)
    gains = gain_ref[...]
    lora = lora_ref[...]
    g128 = g128_ref[...]
    g64 = g64_ref[...]
    mw = C_HEADS * HEAD_PAD
    if use_rope:
        cosc = _lane_tile(cosc_ref[...], mw)
        sinc = _lane_tile(sinc_ref[...], mw)
        first_c = (_lane_iota(mw) & 127) < (C_NOPE + C_ROPE // 2)
        cos = _lane_tile(cos_ref[...], 512)
        sin = _lane_tile(sin_ref[...], 512)
        first_d = (_lane_iota(512) & 32) == 0

    def seg(a, b):
        return _dot(h, w_ref[:, a:b])

    inv_c = 1.0 / (C_NOPE + C_ROPE)
    cq = (_rms(seg(_O_CQ, _O_CKV)) * lora[0:1, :C_Q_LORA]).astype(BF16)
    mq = _group_rms(_dot(cq, wqb_ref[...]), g128, inv_c, gains[0:1])
    if use_rope:
        mq = _rope(mq, cosc, sinc, C_ROPE // 2, first_c)
    mq_ref[...] = (mq * (LOG2E * math.sqrt(inv_c))).astype(BF16)
    ckv = (_rms(seg(_O_CKV, _O_DQ)) * lora[1:2, :C_KV_LORA]).astype(BF16)
    mk = _dot(ckv, wk_ref[...]) + _lane_tile(seg(_O_CKR, _O_END), mw)
    mk = _group_rms(mk, g128, inv_c, gains[1:2])
    if use_rope:
        mk = _rope(mk, cosc, sinc, C_ROPE // 2, first_c)
    mk_ref[...] = mk.astype(BF16)
    _store_values_t(mvt_ref, _dot(ckv, wv_ref[...]), C_DV)
    gq = _group_rms(seg(_O_DQ, _O_DK), g64, 1.0 / D_DH, gains[2:3, :512])
    gk = _group_rms(seg(_O_DK, _O_DV), g64, 1.0 / D_DH, gains[3:4, :256])
    if use_rope:
        gq = _rope(gq, cos, sin, 32, first_d)
        gk = _rope(gk, cos[:, :256], sin[:, :256], 32, first_d[:, :256])
    gq_ref[...] = (gq * (LOG2E / math.sqrt(D_DH))).astype(BF16)
    gk_ref[...] = gk.astype(BF16)
    gv_ref[...] = seg(_O_DV, _O_CKR).astype(BF16)


def _premix_odd(x, vec, w, wqb, wk, wv, g128, g64, gains, lora, cos, sin, cosc, sinc, use_rope):
    t, d = x.shape
    tm = min(FLASH_TK, t)
    row = lambda i: (i, 0)
    tab = pl.BlockSpec((tm, LANES), row)
    nat = lambda n: (pl.BlockSpec((tm, n), row), jax.ShapeDtypeStruct((t, n), BF16))
    vts = _values_t_shape(t, tm, C_DV)
    vt = (pl.BlockSpec((vts[0], 1) + vts[2:], lambda i: (0, i, 0, 0)), jax.ShapeDtypeStruct(vts, BF16))
    outs = [nat(1024), nat(1024), vt, nat(512), nat(256), nat(256)]
    return pl.pallas_call(
        functools.partial(_premix_odd_kernel, use_rope=use_rope),
        grid=(t // tm,),
        in_specs=[pl.BlockSpec((tm, d), row)] + [_full(a.shape) for a in (vec, w, wqb, wk, wv, g128, g64, gains, lora)]
                 + [tab] * 4,
        out_specs=[o[0] for o in outs],
        out_shape=[o[1] for o in outs],
        compiler_params=_cparams("parallel"),
        name="premix_odd",
    )(x, vec, w, wqb, wk, wv, g128, g64, gains, lora, cos, sin, cosc, sinc)


def _flash_kernel(*refs, mode, has_ctx, online, lam_init):
    bound_ref, q_ref, k_ref, vt_ref = refs[:4]
    kc_ref, vct_ref = refs[4:6] if has_ctx else (None, None)
    ex_ref, gcol_ref, o_ref, acc_ref, stage_ref = refs[6:11] if has_ctx else refs[4:9]
    qt = q_ref[...].astype(F32).T.astype(BF16)
    row = lax.broadcasted_iota(jnp.int32, (HEAD_PAD, 1), 0)
    if mode == "mla":
        qs = [qt[:HEAD_PAD], qt[HEAD_PAD:]]
    else:
        zero = jnp.zeros_like(qt)
        qs = [jnp.where(row < 64, qt, zero), jnp.where(row >= 64, qt, zero)]
    nk, tk = vt_ref.shape[1], vt_ref.shape[3]

    acc_ref[...] = jnp.zeros(acc_ref.shape, F32)
    rows_b = acc_ref.shape[1]

    def scores(kblk, b):
        kb = kblk[:, b * HEAD_PAD:(b + 1) * HEAD_PAD] if mode == "mla" else kblk
        return _dot(kb, qs[b])

    def values(vtblk, b):
        return vtblk if vtblk.shape[0] == rows_b else vtblk[b * rows_b:(b + 1) * rows_b]

    if online:
        m_ref, mx_ref = refs[-2:]
        m_ref[...] = jnp.full(m_ref.shape, NEG_INF, F32)

        def absorb(s, smax, vtblk, b):
            m_prev = m_ref[b]
            m_new = jnp.maximum(m_prev, smax)
            alpha = jnp.exp2(m_prev - m_new)
            p = jnp.exp2(s - m_new).astype(BF16)
            acc_ref[b] = alpha * acc_ref[b] + _dot(values(vtblk, b), p)
            m_ref[b] = m_new

        def produce(slot, kblk):
            for b in range(2):
                s = scores(kblk, b)
                stage_ref[slot, b] = s
                mx_ref[slot, b] = jnp.max(s, axis=0, keepdims=True)

        def consume(slot, vtblk):
            for b in range(2):
                absorb(stage_ref[slot, b], mx_ref[slot, b], vtblk, b)

        def direct(kblk, vtblk):
            for b in range(2):
                s = scores(kblk, b)
                absorb(s, jnp.max(s, axis=0, keepdims=True), vtblk, b)
    else:
        bound = bound_ref[0]

        def probs(kblk, b):
            return jnp.exp2(scores(kblk, b) - bound).astype(BF16)

        def produce(slot, kblk):
            for b in range(2):
                stage_ref[slot, b] = probs(kblk, b)

        def consume(slot, vtblk):
            for b in range(2):
                acc_ref[b] += _dot(values(vtblk, b), stage_ref[slot, b])

        def direct(kblk, vtblk):
            for b in range(2):
                acc_ref[b] += _dot(values(vtblk, b), probs(kblk, b))

    if nk == 1:
        if has_ctx:
            direct(kc_ref[...], vct_ref[0])
        direct(k_ref[...], vt_ref[0, 0])
    else:
        unroll = FLASH_UNROLL if nk % FLASH_UNROLL == 0 else 2
        assert nk % unroll == 0
        produce(0, k_ref[0:tk, :])
        if has_ctx:
            direct(kc_ref[...], vct_ref[0])

        def body(i, carry):
            c0 = unroll * i
            for u in range(unroll):
                nxt = jnp.minimum(c0 + u + 1, nk - 1)
                produce((u + 1) % 2, k_ref[pl.ds(pl.multiple_of(nxt * tk, tk), tk), :])
                consume(u % 2, vt_ref[0, c0 + u])
            return carry

        lax.fori_loop(0, nk // unroll, body, 0)

    dv = rows_b - ONES_ROWS
    o0 = acc_ref[0, :dv] / acc_ref[0, dv:dv + 1]
    o1 = acc_ref[1, :dv] / acc_ref[1, dv:dv + 1]
    if mode == "diff":
        ex = ex_ref[...]
        lam = (jnp.exp(jnp.sum(ex[0:1] * ex[1:2], axis=-1, keepdims=True))
               - jnp.exp(jnp.sum(ex[2:3] * ex[3:4], axis=-1, keepdims=True)) + lam_init)
        o = o0 - lam * o1
        o = o * lax.rsqrt(jnp.mean(o * o, axis=0, keepdims=True) + EPS) * gcol_ref[...] * (1.0 - lam_init)
    else:
        o = jnp.concatenate([o0, o1], axis=0)
    o_ref[...] = o.T.astype(o_ref.dtype)


def _flash(q, k, vt, kc, vct, ex, gcol, bound=None, *, mode, lam_init=0.0):
    t = q.shape[0]
    tkeys = k.shape[0]
    qw = 2 * HEAD_PAD if mode == "mla" else HEAD_PAD
    groups = q.shape[1] // qw
    tq = min(FLASH_TQ, t)
    _, nk, vrows, tk = vt.shape
    assert nk * tk == tkeys
    has_ctx = kc is not None
    dv = HEAD_PAD if mode == "diff" else HEAD_PAD // 2
    rows_b = dv + ONES_ROWS
    bound_arr = jnp.zeros((1,), F32) if bound is None else jnp.reshape(bound, (1,)).astype(F32)
    in_specs = [pl.BlockSpec(memory_space=pltpu.SMEM),
                pl.BlockSpec((tq, qw), lambda g, i: (i, g)),
                pl.BlockSpec((tkeys, qw), lambda g, i: (0, g)),
                pl.BlockSpec((1, nk, vrows, tk), lambda g, i: (g, 0, 0, 0))]
    args = [bound_arr, q, k, vt]
    if has_ctx:
        lc = kc.shape[0]
        in_specs += [pl.BlockSpec((lc, qw), lambda g, i: (0, g)),
                     pl.BlockSpec((1, vrows, lc), lambda g, i: (g, 0, 0))]
        args += [kc, vct.reshape(groups, vrows, lc)]
    in_specs += [_full(ex.shape), _full(gcol.shape)]
    args += [ex, gcol]

    def call(online):
        acc = pltpu.VMEM((2, rows_b, tq), F32)
        if online:
            scratch = [acc, pltpu.VMEM((2, 2, tk, tq), F32), pltpu.VMEM((2, 1, tq), F32), pltpu.VMEM((2, 2, 1, tq), F32)]
        else:
            scratch = [acc, pltpu.VMEM((2, 2, tk, tq), BF16)]
        return pl.pallas_call(
            functools.partial(_flash_kernel, mode=mode, has_ctx=has_ctx, online=online, lam_init=lam_init),
            grid=(groups, t // tq),
            in_specs=in_specs,
            out_specs=pl.BlockSpec((tq, HEAD_PAD), lambda g, i: (i, g)),
            out_shape=jax.ShapeDtypeStruct((t, groups * HEAD_PAD), BF16),
            scratch_shapes=scratch,
            compiler_params=_cparams("parallel", "parallel"),
            name="flash_" + mode + ("_online" if online else "_bounded"),
        )(*args)

    if bound is None:
        return call(True)
    return lax.cond(bound <= FLASH_SAFE_BOUND, lambda: call(False), lambda: call(True))


def _na_kernel(q_ref, kp_ref, kcur_ref, kn_ref, vtp_ref, vtcur_ref, vtn_ref, kc_ref, vct_ref, bias_ref, o_ref,
               s_ref, mx_ref):
    heads = q_ref.shape[1] // B_DH
    lc = kc_ref.shape[0]
    step = B_DH + ONES_ROWS
    row = lax.broadcasted_iota(jnp.int32, (HEAD_PAD, 1), 0)

    def pair_cols(h):
        return slice((h // 2) * HEAD_PAD, (h // 2 + 1) * HEAD_PAD)

    def produce(slot, h):
        ps = pair_cols(h)
        qt = q_ref[:, ps].astype(F32).T.astype(BF16)
        qb = jnp.where((row < 64) if h % 2 == 0 else (row >= 64), qt, jnp.zeros_like(qt))
        k_nb = jnp.concatenate([kp_ref[:, ps], kcur_ref[:, ps], kn_ref[:, ps]], axis=0)
        s_ctx = _dot(kc_ref[:, ps], qb)
        s_nb = _dot(k_nb, qb) + bias_ref[0, h]
        s_ref[slot, 0:lc] = s_ctx
        s_ref[slot, lc:] = s_nb
        mx_ref[slot] = jnp.maximum(jnp.max(s_ctx, axis=0, keepdims=True), jnp.max(s_nb, axis=0, keepdims=True))

    def consume(slot, h):
        pr, b = h // 2, h % 2
        vt = jnp.concatenate([vct_ref[pr, 0], vtp_ref[pr, 0], vtcur_ref[pr, 0], vtn_ref[pr, 0]], axis=1)
        p = jnp.exp2(s_ref[slot] - mx_ref[slot]).astype(BF16)
        acc = _dot(vt[b * step:(b + 1) * step], p)
        return acc[:B_DH] / acc[B_DH:B_DH + 1]

    produce(0, 0)
    outs = []
    for h in range(heads):
        if h + 1 < heads:
            produce((h + 1) % 2, h + 1)
        outs.append(consume(h % 2, h))
        if h % 2:
            o_ref[:, pair_cols(h)] = jnp.concatenate(outs[-2:], axis=0).T.astype(o_ref.dtype)


def _na_bias_table(rpb, rows):
    nr, w = NA_ROWS, GRID_W
    heads = rpb.shape[0]
    pad = w - NA_WIN_W
    padded = jnp.pad(rpb.astype(F32) * LOG2E, ((0, 0), (0, 0), (pad, pad + 1)))
    skew = jnp.tile(padded, (1, 1, w))[:, :, :w * (2 * w - 1)].reshape(heads, -1, w, 2 * w - 1)
    toep = skew[:, :, :, w - 1:]
    cq = np.arange(w)
    cs = np.clip(cq - NA_WIN_W // 2, 0, w - NA_WIN_W)
    valid_c = (cq[None, :] >= cs[:, None]) & (cq[None, :] < cs[:, None] + NA_WIN_W)
    toep_t = jnp.swapaxes(jnp.where(jnp.asarray(valid_c), toep, NEG_INF), -1, -2)
    d0 = NA_WIN_H - 1 - nr
    assert d0 - (nr - 1) >= 0 and d0 + 3 * nr <= 2 * NA_WIN_H - 1
    dense = jnp.concatenate([toep_t[:, d0 - rl:d0 - rl + 3 * nr] for rl in range(nr)], axis=-1)
    wh = min(NA_WIN_H, rows)
    rl = np.arange(nr)
    rr_rel = np.arange(3 * nr)
    valid_all = []
    for base in (0, nr, rows - nr):
        rs = np.clip(base + rl - wh // 2, 0, rows - wh)
        rr = base - nr + rr_rel
        valid_all.append((rr[:, None] >= rs[None, :]) & (rr[:, None] < rs[None, :] + wh))
    valid_r = np.repeat(np.stack(valid_all), w, axis=-1)[:, None, :, None, :]
    tab = jnp.where(jnp.asarray(valid_r), dense[None], NEG_INF)
    return tab.reshape(3, heads, 3 * nr * w, nr * w)


def _neighbourhood(q, k, vt, kc, vct, bias):
    t = q.shape[0]
    qn = NA_ROWS * GRID_W
    nb = t // qn
    lc = kc.shape[0]
    pairs, _, vrows, tk = vt.shape
    per = tk // qn
    cur = lambda i: (i, 0)
    prev = lambda i: (jnp.maximum(i - 1, 0), 0)
    nxt = lambda i: (jnp.minimum(i + 1, nb - 1), 0)
    blk = lambda f: pl.BlockSpec((qn, q.shape[1]), f)
    vblk = lambda f: pl.BlockSpec((pairs, 1, vrows, qn), lambda i: (0, f(i)[0] // per, 0, f(i)[0] % per))
    case = lambda i: (jnp.where(i == 0, 0, jnp.where(i == nb - 1, 2, 1)), 0, 0, 0)
    return pl.pallas_call(
        _na_kernel,
        grid=(nb,),
        in_specs=[blk(cur), blk(prev), blk(cur), blk(nxt), vblk(prev), vblk(cur), vblk(nxt),
                  _full(kc.shape), _full(vct.shape), pl.BlockSpec((1,) + bias.shape[1:], case)],
        out_specs=blk(cur),
        out_shape=jax.ShapeDtypeStruct(q.shape, BF16),
        scratch_shapes=[pltpu.VMEM((2, lc + 3 * qn, qn), F32), pltpu.VMEM((2, 1, qn), F32)],
        compiler_params=_cparams("arbitrary"),
        name="neighbourhood",
    )(q, k, k, k, vt, vt, vt, kc, vct, bias)


def _wgqa_kernel(sink_ref, q_ref, kp_ref, kcur_ref, kn_ref, vp_ref, vcur_ref, vn_ref, kc_ref, vc_ref, o_ref):
    i = pl.program_id(0)
    nb = pl.num_programs(0)
    qb = q_ref.shape[0]
    lc = kc_ref.shape[0]
    nkeys = lc + 3 * qb
    lane = _lane_iota(LANES)
    col = lax.broadcasted_iota(jnp.int32, (qb, nkeys), 1)
    row = lax.broadcasted_iota(jnp.int32, (qb, nkeys), 0)
    rel = col - (lc + qb) - row
    lo = jnp.where(i > 0, lc, lc + qb)
    hi = jnp.where(i < nb - 1, nkeys, lc + 2 * qb)
    valid = (col < lc) | ((jnp.abs(rel) <= D_WINDOW) & (col >= lo) & (col < hi))
    groups = D_HEADS // D_KV_HEADS
    for kv in range(D_KV_HEADS):
        ks = slice(kv * HEAD_PAD, (kv + 1) * HEAD_PAD)
        kk = jnp.concatenate([kc_ref[:, ks], kp_ref[:, ks], kcur_ref[:, ks], kn_ref[:, ks]], axis=0)
        vv = jnp.concatenate([vc_ref[:, ks], vp_ref[:, ks], vcur_ref[:, ks], vn_ref[:, ks]], axis=0)
        lhs = []
        for g in range(groups):
            h = kv * groups + g
            qp = q_ref[:, (h // 2) * HEAD_PAD:(h // 2 + 1) * HEAD_PAD]
            keep = (lane < 64) if h % 2 == 0 else (lane >= 64)
            lhs.append(jnp.where(keep, qp, jnp.zeros_like(qp)))
        s = _dot_nt(jnp.concatenate(lhs, axis=0), kk)
        outs = []
        for g in range(groups):
            sink = sink_ref[kv * groups + g] * LOG2E
            sg = jnp.where(valid, s[g * qb:(g + 1) * qb], NEG_INF)
            m = jnp.maximum(jnp.max(sg, axis=-1, keepdims=True), sink)
            p = jnp.exp2(sg - m)
            l = jnp.sum(p, axis=-1, keepdims=True) + jnp.exp2(sink - m)
            outs.append(_dot(p.astype(BF16), vv) / l)
        for pp in range(groups // 2):
            pair = kv * (groups // 2) + pp
            o_ref[:, pair * HEAD_PAD:(pair + 1) * HEAD_PAD] = jnp.where(
                lane < 64, outs[2 * pp], outs[2 * pp + 1]).astype(o_ref.dtype)


def _window_gqa(q, k, v, kc, vc, sink):
    t = q.shape[0]
    qb = D_WINDOW
    nb = t // qb
    lc = kc.shape[0]
    kw = k.shape[1]
    cur = lambda i: (i, 0)
    prev = lambda i: (jnp.maximum(i - 1, 0), 0)
    nxt = lambda i: (jnp.minimum(i + 1, nb - 1), 0)
    kblk = lambda f: pl.BlockSpec((qb, kw), f)
    return pl.pallas_call(
        _wgqa_kernel,
        grid=(nb,),
        in_specs=[pl.BlockSpec(memory_space=pltpu.SMEM), pl.BlockSpec((qb, q.shape[1]), cur),
                  kblk(prev), kblk(cur), kblk(nxt), kblk(prev), kblk(cur), kblk(nxt),
                  _full((lc, kw)), _full((lc, kw))],
        out_specs=pl.BlockSpec((qb, q.shape[1]), cur),
        out_shape=jax.ShapeDtypeStruct(q.shape, BF16),
        compiler_params=_cparams("parallel"),
        name="window_gqa",
    )(sink, q, k, k, k, v, v, v, kc, vc)


def _mix_ffn_kernel(o1_ref, o1p_ref, o1n_ref, o2_ref, o2p_ref, o2n_ref, x_ref, xp_ref, xn_ref,
                    wout_ref, vec_ref, wup_ref, cv_ref, wdn_ref, o_ref, acc_ref, hcat_ref, u_ref, x1_ref):
    i = pl.program_id(0)
    n = pl.num_programs(0)
    tm = x_ref.shape[0]
    rows = tm + 2 * HALO
    nchunk = wdn_ref.shape[0] // FFN_CHUNK
    half = o1_ref.shape[1]
    vec = vec_ref[...]
    o1 = jnp.concatenate([o1p_ref[...], o1_ref[...], o1n_ref[...]], axis=0)
    o2 = jnp.concatenate([o2p_ref[...], o2_ref[...], o2n_ref[...]], axis=0)
    x = jnp.concatenate([xp_ref[...], x_ref[...], xn_ref[...]], axis=0)
    x1 = x + vec[0:1] * (_dot(o1, wout_ref[:half, :]) + _dot(o2, wout_ref[half:, :]))
    x1_ref[...] = x1[HALO:HALO + tm]
    hcat_ref[...] = (_rms(x1) * vec[1:2] * (1.0 + vec[2:3]) + vec[3:4]).astype(BF16)

    @pl.when(i == 0)
    def _():
        hcat_ref[0:HALO] = jnp.zeros((HALO, hcat_ref.shape[1]), BF16)

    @pl.when(i == n - 1)
    def _():
        hcat_ref[HALO + tm:rows] = jnp.zeros((HALO, hcat_ref.shape[1]), BF16)

    acc_ref[...] = jnp.zeros(acc_ref.shape, F32)

    def conv(u, cv):
        before = pltpu.roll(u, 1, axis=0)
        after = pltpu.roll(u, rows - 1, axis=0)
        r = cv[3:4] + before * cv[0:1] + u * cv[1:2] + after * cv[2:3]
        return r[HALO:HALO + tm]

    def cols(c, gate):
        return pl.ds(pl.multiple_of((gate * nchunk + c) * FFN_CHUNK, FFN_CHUNK), FFN_CHUNK)

    def stage_up(slot, c):
        hcat = hcat_ref[...]
        u_ref[slot, 0] = _dot(hcat, wup_ref[:, cols(c, 0)])
        u_ref[slot, 1] = _dot(hcat, wup_ref[:, cols(c, 1)])

    def stage_down(slot, c):
        a = conv(u_ref[slot, 0], cv_ref[:, cols(c, 0)])
        g = conv(u_ref[slot, 1], cv_ref[:, cols(c, 1)])
        act = (a * (g * _sigmoid(g))).astype(BF16)
        acc_ref[...] += _dot(act, wdn_ref[cols(c, 0), :])

    assert nchunk % 2 == 1
    stage_up(0, 0)

    def body(j, carry):
        c = 2 * j
        stage_up(1, c + 1)
        stage_down(0, c)
        stage_up(0, c + 2)
        stage_down(1, c + 1)
        return carry

    lax.fori_loop(0, nchunk // 2, body, 0)
    stage_down(0, nchunk - 1)
    o_ref[...] = x1_ref[...] + vec[4:5] * acc_ref[...]


def _mix_ffn(o1, o2, x, wout, vec, wup, cv, wdn):
    t, d = x.shape
    tm = min(512, t)
    nt = t // tm
    hb = tm // HALO
    row = lambda i: (i, 0)
    prev = lambda i: (jnp.maximum(i * hb - 1, 0), 0)
    nxt = lambda i: (jnp.minimum((i + 1) * hb, t // HALO - 1), 0)
    tiled = lambda w: [pl.BlockSpec((tm, w), row), pl.BlockSpec((HALO, w), prev), pl.BlockSpec((HALO, w), nxt)]
    return pl.pallas_call(
        _mix_ffn_kernel,
        grid=(nt,),
        in_specs=tiled(o1.shape[1]) + tiled(o2.shape[1]) + tiled(d)
                 + [_resident(a.shape) for a in (wout, vec, wup, cv, wdn)],
        out_specs=pl.BlockSpec((tm, d), row),
        out_shape=jax.ShapeDtypeStruct((t, d), F32),
        scratch_shapes=[pltpu.VMEM((tm, d), F32), pltpu.VMEM((tm + 2 * HALO, d), BF16),
                        pltpu.VMEM((2, 2, tm + 2 * HALO, FFN_CHUNK), F32), pltpu.VMEM((tm, d), F32)],
        compiler_params=_cparams("parallel"),
        name="mix_ffn",
    )(o1, o1, o1, o2, o2, o2, x, x, x, wout, vec, wup, cv, wdn)


def _axial_tables(n_tokens, dim):
    t = jnp.arange(n_tokens, dtype=jnp.int32)
    row = (t // GRID_W).astype(F32)
    col = (t % GRID_W).astype(F32)
    quarter = dim // 4
    inv_freq = ROPE_BASE ** (-jnp.arange(quarter, dtype=F32) / quarter)
    ang = jnp.concatenate([row[:, None] * inv_freq, col[:, None] * inv_freq], axis=-1)
    return jnp.cos(ang), jnp.sin(ang)


def _rope_tables_head64(t):
    cos, sin = _axial_tables(t, 64)
    return jnp.tile(jnp.concatenate([cos, cos], -1), (1, 2)), jnp.tile(jnp.concatenate([-sin, sin], -1), (1, 2))


def _rope_tables_latent(t):
    cos, sin = _axial_tables(t, C_ROPE)
    ones = jnp.ones((t, C_NOPE), F32)
    tail = jnp.ones((t, HEAD_PAD - C_NOPE - C_ROPE), F32)
    cosc = jnp.concatenate([ones, cos, cos, tail], -1)
    sinc = jnp.concatenate([0 * ones, -sin, sin, 0 * tail], -1)
    return cosc, sinc


def _group_ones(width, group):
    return jnp.asarray(np.kron(np.eye(width // group), np.ones((group, group))), BF16)


def _pad_heads(a, heads, dim):
    a = a.reshape(a.shape[:-1] + (heads, dim))
    a = jnp.pad(a, [(0, 0)] * (a.ndim - 1) + [(0, HEAD_PAD - dim)])
    return a.reshape(a.shape[:-2] + (heads * HEAD_PAD,))


def _logit_bound(gq, gk, dim):
    return 1.02 * LOG2E * math.sqrt(dim) * jnp.max(jnp.abs(gq)) * jnp.max(jnp.abs(gk))


def _pad_row(v, width):
    return jnp.pad(v, (0, width - v.shape[0]))


def _ffn_weights(w_up, conv_w, conv_b, w_down):
    cv = jnp.concatenate([conv_w, conv_b[None]], axis=0)
    return w_up.astype(BF16), cv, w_down.astype(BF16)


def kernel(x, c, ctx, c_ctx, ada_w, ada_b, norm_mix, norm_ffn, ffn_up, ffn_conv_w, ffn_conv_b, ffn_down, ev_w_in, ev_w_out, a_q_norm, a_k_norm, a_lam_q1, a_lam_k1, a_lam_q2, a_lam_k2, a_subln, b_q_norm, b_k_norm, b_rpb, od_w_in, od_w_out, c_q_a_norm, c_w_qb, c_kv_a_norm, c_w_kvb, c_q_norm, c_k_norm, d_q_norm, d_k_norm, d_sink):
    assert x.shape[0] == 1 and ctx.shape[0] == 1
    xm = x[0]
    xc = ctx[0]
    t, d = xm.shape
    lc = xc.shape[0]
    mods = _modulation(c, c_ctx, ada_w, ada_b)
    cos64, sin64 = _rope_tables_head64(t)
    cosc, sinc = _rope_tables_latent(t)
    g64 = _group_ones(MXU_TILE, 64)
    g128 = _group_ones(MXU_TILE, HEAD_PAD)
    dummy_tab = jnp.zeros((lc, LANES), F32)

    for l in range(DEPTH):
        last = l == DEPTH - 1
        i = l // 2
        mm = mods[l, 0].reshape(6, d)
        mc = mods[l, 1].reshape(6, d)
        pre_m = jnp.stack([norm_mix[l], mm[1], mm[0]])
        pre_c = jnp.stack([norm_mix[l], mc[1], mc[0]])
        post_m = jnp.stack([mm[2], norm_ffn[l], mm[4], mm[3], mm[5]])
        post_c = jnp.stack([mc[2], norm_ffn[l], mc[4], mc[3], mc[5]])
        wup, cv, wdn = _ffn_weights(ffn_up[l], ffn_conv_w[l], ffn_conv_b[l], ffn_down[l])
        if l % 2 == 0:
            lam_init = 0.8 - 0.6 * math.exp(-0.3 * l)
            w_in = ev_w_in[i].astype(BF16)
            w_out = ev_w_out[i].astype(BF16)
            hg = jnp.stack([jnp.tile(a_q_norm[i], 8), jnp.tile(a_k_norm[i], 8),
                            jnp.tile(b_q_norm[i], 8), jnp.tile(b_k_norm[i], 8)])
            ex = jnp.stack([_pad_row(a_lam_q1[i], LANES), _pad_row(a_lam_k1[i], LANES),
                            _pad_row(a_lam_q2[i], LANES), _pad_row(a_lam_k2[i], LANES)])
            gcol = a_subln[i][:, None]
            qa, ka, va, qb, kb, vb = _premix_even(xm, pre_m, w_in, g64, hg, cos64, sin64, True)
            qa_c, ka_c, va_c, qb_c, kb_c, vb_c = _premix_even(xc, pre_c, w_in, g64, hg, dummy_tab, dummy_tab, False)
            oa = _flash(qa, ka, va, ka_c, va_c, ex, gcol, _logit_bound(a_q_norm[i], a_k_norm[i], A_DQK),
                        mode="diff", lam_init=lam_init)
            ob = _neighbourhood(qb, kb, vb, kb_c, vb_c, _na_bias_table(b_rpb[i], t // GRID_W))
            if not last:
                oa_c = _flash(qa_c, ka_c, va_c, None, None, ex, gcol, mode="diff", lam_init=lam_init)
                ob_c = _flash(qb_c, kb_c, vb_c, None, None, ex, gcol, mode="pair")
        else:
            w = od_w_in[i]
            cq, ckv, ckr, dq, dk, dv = jnp.split(w, np.cumsum([C_Q_LORA, C_KV_LORA, C_ROPE, 512, 128]).tolist(), axis=1)
            dup = lambda a: jnp.concatenate([a[:, :64], a[:, :64], a[:, 64:], a[:, 64:]], axis=1)
            ckr_pad = jnp.pad(ckr, ((0, 0), (C_NOPE, HEAD_PAD - C_NOPE - C_ROPE)))
            w_in = jnp.concatenate([cq, ckv, dq, dup(dk), dup(dv), ckr_pad], axis=1).astype(BF16)
            w_out = od_w_out[i].astype(BF16)
            wqb = _pad_heads(c_w_qb[i], C_HEADS, C_NOPE + C_ROPE).astype(BF16)
            kvb = c_w_kvb[i].reshape(C_KV_LORA, C_HEADS, C_NOPE + C_DV)
            wk = _pad_heads(kvb[:, :, :C_NOPE].reshape(C_KV_LORA, -1), C_HEADS, C_NOPE).astype(BF16)
            wv = kvb[:, :, C_NOPE:].reshape(C_KV_LORA, C_HEADS * C_DV).astype(BF16)
            gains = jnp.stack([jnp.tile(_pad_row(c_q_norm[i], HEAD_PAD), C_HEADS),
                               jnp.tile(_pad_row(c_k_norm[i], HEAD_PAD), C_HEADS),
                               _pad_row(jnp.tile(d_q_norm[i], 8), 1024), _pad_row(jnp.tile(d_k_norm[i], 4), 1024)])
            lora = jnp.stack([_pad_row(c_q_a_norm[i], 512), _pad_row(c_kv_a_norm[i], 512)])
            odd = functools.partial(_premix_odd, w=w_in, wqb=wqb, wk=wk, wv=wv, g128=g128, g64=g64, gains=gains, lora=lora)
            mq, mk, mv, gq, gk, gv = odd(xm, pre_m, cos=cos64, sin=sin64, cosc=cosc, sinc=sinc, use_rope=True)
            mq_c, mk_c, mv_c, gq_c, gk_c, gv_c = odd(xc, pre_c, cos=dummy_tab, sin=dummy_tab, cosc=dummy_tab,
                                                     sinc=dummy_tab, use_rope=False)
            ex = jnp.zeros((4, LANES), F32)
            oa = _flash(mq, mk, mv, mk_c, mv_c, ex, jnp.ones((HEAD_PAD, 1), F32),
                        _logit_bound(c_q_norm[i], c_k_norm[i], C_NOPE + C_ROPE), mode="mla")
            ob = _window_gqa(gq, gk, gv, gk_c, gv_c, d_sink[i])
            if not last:
                raise NotImplementedError("context update after an odd layer is not needed at this depth")
        xm = _mix_ffn(oa, ob, xm, w_out, post_m, wup, cv, wdn)
        if not last:
            xc = _mix_ffn(oa_c, ob_c, xc, w_out, post_c, wup, cv, wdn)
    return xm[None]
```

```python
import functools
import math

import numpy as np
import jax
import jax.numpy as jnp
from jax import lax
from jax.experimental import pallas as pl
from jax.experimental.pallas import tpu as pltpu

D_MODEL = 1024
DEPTH = 2
GRID_W = 64
EPS = 1e-6
ROPE_BASE = 10000.0
NEG_INF = -1e30
LOG2E = math.log2(math.e)

A_HEADS = 4
A_DQK = 64
A_DV = 128
B_HEADS = 8
B_DH = 64
NA_WIN_H = 8
NA_WIN_W = 16
C_HEADS = 8
C_Q_LORA = 384
C_KV_LORA = 256
C_NOPE = 64
C_ROPE = 32
C_DV = 64
D_HEADS = 8
D_KV_HEADS = 2
D_DH = 64
D_WINDOW = 128
FFN_HIDDEN = 2816
CONV_W = 3

LANES = 128
MXU_TILE = 256
HEAD_PAD = 128
VMEM_LIMIT = 56 * 1024 * 1024
FFN_CHUNK = 256
HALO = 16
NA_ROWS = 4
ONES_ROWS = 16
FLASH_TQ = 2048
FLASH_TK = 512
FLASH_UNROLL = 4
FLASH_SAFE_BOUND = 40.0

F32 = jnp.float32
BF16 = jnp.bfloat16


def _cparams(*sem):
    return pltpu.CompilerParams(dimension_semantics=sem, vmem_limit_bytes=VMEM_LIMIT)


def _full(shape):
    n = len(shape)
    return pl.BlockSpec(shape, lambda *_: (0,) * n)


def _resident(shape):
    n = len(shape)
    return pl.BlockSpec(shape, lambda *_: (0,) * n, pipeline_mode=pl.Buffered(1))


def _dot(a, b):
    return jnp.dot(a, b, preferred_element_type=F32)


def _dot_nt(a, b):
    return lax.dot_general(a, b, (((1,), (1,)), ((), ())), preferred_element_type=F32)


def _rms(x):
    return x * lax.rsqrt(jnp.mean(x * x, axis=-1, keepdims=True) + EPS)


def _sigmoid(x):
    return 1.0 / (1.0 + jnp.exp(-x))


def _group_rms(y, gmat, inv_n, gain):
    yy = (y * y).astype(BF16)
    slab = gmat.shape[0]
    ss = jnp.concatenate([_dot(yy[:, c:c + slab], gmat) for c in range(0, y.shape[-1], slab)], axis=-1)
    return y * lax.rsqrt(ss * inv_n + EPS) * gain


def _rope(y, cos, sin, half, first_half):
    w = y.shape[-1]
    fwd = pltpu.roll(y, w - half, axis=1)
    bwd = pltpu.roll(y, half, axis=1)
    return y * cos + jnp.where(first_half, fwd, bwd) * sin


def _lane_tile(t, width):
    reps = width // t.shape[-1]
    return t if reps == 1 else jnp.concatenate([t] * reps, axis=-1)


def _lane_iota(width):
    return lax.broadcasted_iota(jnp.int32, (1, width), 1)


def _mod_kernel(ct_ref, w_ref, b_ref, o_ref):
    ct = ct_ref[...]
    s = ct * _sigmoid(ct)
    w = w_ref[0]
    rows = [jnp.sum(w * s[:, j:j + 1], axis=0, keepdims=True) for j in range(2)]
    o_ref[0] = jnp.concatenate(rows, axis=0) + b_ref[0]


def _modulation(c, c_ctx, ada_w, ada_b):
    depth, d, n = ada_w.shape
    tn = 1536
    ct = jnp.stack([c[0], c_ctx], axis=1)
    return pl.pallas_call(
        _mod_kernel,
        grid=(depth, n // tn),
        in_specs=[_full((d, 2)),
                  pl.BlockSpec((1, d, tn), lambda l, j: (l, 0, j)),
                  pl.BlockSpec((1, 1, tn), lambda l, j: (l, 0, j))],
        out_specs=pl.BlockSpec((1, 2, tn), lambda l, j: (l, 0, j)),
        out_shape=jax.ShapeDtypeStruct((depth, 2, n), F32),
        compiler_params=_cparams("arbitrary", "arbitrary"),
        name="modulation",
    )(ct, ada_w, ada_b.reshape(depth, 1, n))


def _store_values_t(vt_ref, v, dv):
    tm = v.shape[0]
    vt = v.T.astype(BF16)
    ones = jnp.ones((ONES_ROWS, tm), BF16)
    step = dv + ONES_ROWS
    for g in range(vt_ref.shape[0]):
        for h in range(HEAD_PAD // dv):
            src = g * HEAD_PAD + h * dv
            vt_ref[g, 0, h * step:h * step + dv, :] = vt[src:src + dv]
            vt_ref[g, 0, h * step + dv:(h + 1) * step, :] = ones


def _values_t_shape(t, tm, dv):
    return (4, t // tm, (HEAD_PAD // dv) * (dv + ONES_ROWS), tm)


def _premix_even_kernel(x_ref, vec_ref, w_ref, g_ref, hg_ref, cos_ref, sin_ref,
                        qa_ref, ka_ref, vat_ref, qb_ref, kb_ref, vbt_ref, *, use_rope):
    vec = vec_ref[...]
    h = (_rms(x_ref[...]) * vec[0:1] * (1.0 + vec[1:2]) + vec[2:3]).astype(BF16)
    gm = g_ref[...]
    hg = hg_ref[...]
    sw = 512
    if use_rope:
        cos = _lane_tile(cos_ref[...], sw)
        sin = _lane_tile(sin_ref[...], sw)
        first = (_lane_iota(sw) & 32) == 0

    def seg(i):
        return _dot(h, w_ref[:, i * sw:(i + 1) * sw])

    qa = _group_rms(seg(0), gm, 1.0 / A_DQK, hg[0:1])
    ka = _group_rms(seg(1), gm, 1.0 / A_DQK, hg[1:2])
    if use_rope:
        qa = _rope(qa, cos, sin, 32, first)
        ka = _rope(ka, cos, sin, 32, first)
    qa_ref[...] = (qa * (LOG2E / math.sqrt(A_DQK))).astype(BF16)
    ka_ref[...] = ka.astype(BF16)
    _store_values_t(vat_ref, seg(2), A_DV)
    qb = _group_rms(seg(3), gm, 1.0 / B_DH, hg[2:3])
    qb_ref[...] = (qb * (LOG2E / math.sqrt(B_DH))).astype(BF16)
    kb_ref[...] = _group_rms(seg(4), gm, 1.0 / B_DH, hg[3:4]).astype(BF16)
    _store_values_t(vbt_ref, seg(5), B_DH)


def _premix_even(x, vec, w, gmat, hg, cos, sin, use_rope):
    t, d = x.shape
    tm = min(FLASH_TK, t)
    row = lambda i: (i, 0)
    nat = jax.ShapeDtypeStruct((t, 512), BF16)
    nat_spec = pl.BlockSpec((tm, 512), row)
    vta = _values_t_shape(t, tm, A_DV)
    vtb = _values_t_shape(t, tm, B_DH)
    vt_spec = lambda s: pl.BlockSpec((s[0], 1) + s[2:], lambda i: (0, i, 0, 0))
    return pl.pallas_call(
        functools.partial(_premix_even_kernel, use_rope=use_rope),
        grid=(t // tm,),
        in_specs=[pl.BlockSpec((tm, d), row), _full(vec.shape), _full(w.shape), _full(gmat.shape),
                  _full(hg.shape), pl.BlockSpec((tm, LANES), row), pl.BlockSpec((tm, LANES), row)],
        out_specs=[nat_spec, nat_spec, vt_spec(vta), nat_spec, nat_spec, vt_spec(vtb)],
        out_shape=[nat, nat, jax.ShapeDtypeStruct(vta, BF16), nat, nat, jax.ShapeDtypeStruct(vtb, BF16)],
        compiler_params=_cparams("parallel"),
        name="premix_even",
    )(x, vec, w, gmat, hg, cos, sin)


_O_CQ, _O_CKV, _O_DQ, _O_DK, _O_DV, _O_CKR, _O_END = 0, 384, 640, 1152, 1408, 1664, 1792


def _premix_odd_kernel(x_ref, vec_ref, w_ref, wqb_ref, wk_ref, wv_ref, g128_ref, g64_ref, gain_ref, lora_ref,
                       cos_ref, sin_ref, cosc_ref, sinc_ref,
                       mq_ref, mk_ref, mvt_ref, gq_ref, gk_ref, gv_ref, *, use_rope):
    vec = vec_ref[...]
    h = (_rms(x_ref[...]) * vec[0:1] * (1.0 + vec[1:2]) + vec[2:3]).astype(BF16)
    gains = gain_ref[...]
    lora = lora_ref[...]
    g128 = g128_ref[...]
    g64 = g64_ref[...]
    mw = C_HEADS * HEAD_PAD
    if use_rope:
        cosc = _lane_tile(cosc_ref[...], mw)
        sinc = _lane_tile(sinc_ref[...], mw)
        first_c = (_lane_iota(mw) & 127) < (C_NOPE + C_ROPE // 2)
        cos = _lane_tile(cos_ref[...], 512)
        sin = _lane_tile(sin_ref[...], 512)
        first_d = (_lane_iota(512) & 32) == 0

    def seg(a, b):
        return _dot(h, w_ref[:, a:b])

    inv_c = 1.0 / (C_NOPE + C_ROPE)
    cq = (_rms(seg(_O_CQ, _O_CKV)) * lora[0:1, :C_Q_LORA]).astype(BF16)
    mq = _group_rms(_dot(cq, wqb_ref[...]), g128, inv_c, gains[0:1])
    if use_rope:
        mq = _rope(mq, cosc, sinc, C_ROPE // 2, first_c)
    mq_ref[...] = (mq * (LOG2E * math.sqrt(inv_c))).astype(BF16)
    ckv = (_rms(seg(_O_CKV, _O_DQ)) * lora[1:2, :C_KV_LORA]).astype(BF16)
    mk = _dot(ckv, wk_ref[...]) + _lane_tile(seg(_O_CKR, _O_END), mw)
    mk = _group_rms(mk, g128, inv_c, gains[1:2])
    if use_rope:
        mk = _rope(mk, cosc, sinc, C_ROPE // 2, first_c)
    mk_ref[...] = mk.astype(BF16)
    _store_values_t(mvt_ref, _dot(ckv, wv_ref[...]), C_DV)
    gq = _group_rms(seg(_O_DQ, _O_DK), g64, 1.0 / D_DH, gains[2:3, :512])
    gk = _group_rms(seg(_O_DK, _O_DV), g64, 1.0 / D_DH, gains[3:4, :256])
    if use_rope:
        gq = _rope(gq, cos, sin, 32, first_d)
        gk = _rope(gk, cos[:, :256], sin[:, :256], 32, first_d[:, :256])
    gq_ref[...] = (gq * (LOG2E / math.sqrt(D_DH))).astype(BF16)
    gk_ref[...] = gk.astype(BF16)
    gv_ref[...] = seg(_O_DV, _O_CKR).astype(BF16)


def _premix_odd(x, vec, w, wqb, wk, wv, g128, g64, gains, lora, cos, sin, cosc, sinc, use_rope):
    t, d = x.shape
    tm = min(FLASH_TK, t)
    row = lambda i: (i, 0)
    tab = pl.BlockSpec((tm, LANES), row)
    nat = lambda n: (pl.BlockSpec((tm, n), row), jax.ShapeDtypeStruct((t, n), BF16))
    vts = _values_t_shape(t, tm, C_DV)
    vt = (pl.BlockSpec((vts[0], 1) + vts[2:], lambda i: (0, i, 0, 0)), jax.ShapeDtypeStruct(vts, BF16))
    outs = [nat(1024), nat(1024), vt, nat(512), nat(256), nat(256)]
    return pl.pallas_call(
        functools.partial(_premix_odd_kernel, use_rope=use_rope),
        grid=(t // tm,),
        in_specs=[pl.BlockSpec((tm, d), row)] + [_full(a.shape) for a in (vec, w, wqb, wk, wv, g128, g64, gains, lora)]
                 + [tab] * 4,
        out_specs=[o[0] for o in outs],
        out_shape=[o[1] for o in outs],
        compiler_params=_cparams("parallel"),
        name="premix_odd",
    )(x, vec, w, wqb, wk, wv, g128, g64, gains, lora, cos, sin, cosc, sinc)


def _flash_kernel(*refs, mode, has_ctx, online, lam_init):
    bound_ref, q_ref, k_ref, vt_ref = refs[:4]
    kc_ref, vct_ref = refs[4:6] if has_ctx else (None, None)
    ex_ref, gcol_ref, o_ref, acc_ref, stage_ref = refs[6:11] if has_ctx else refs[4:9]
    qt = q_ref[...].astype(F32).T.astype(BF16)
    row = lax.broadcasted_iota(jnp.int32, (HEAD_PAD, 1), 0)
    if mode == "mla":
        qs = [qt[:HEAD_PAD], qt[HEAD_PAD:]]
    else:
        zero = jnp.zeros_like(qt)
        qs = [jnp.where(row < 64, qt, zero), jnp.where(row >= 64, qt, zero)]
    nk, tk = vt_ref.shape[1], vt_ref.shape[3]

    acc_ref[...] = jnp.zeros(acc_ref.shape, F32)
    rows_b = acc_ref.shape[1]

    def scores(kblk, b):
        kb = kblk[:, b * HEAD_PAD:(b + 1) * HEAD_PAD] if mode == "mla" else kblk
        return _dot(kb, qs[b])

    def values(vtblk, b):
        return vtblk if vtblk.shape[0] == rows_b else vtblk[b * rows_b:(b + 1) * rows_b]

    if online:
        m_ref, mx_ref = refs[-2:]
        m_ref[...] = jnp.full(m_ref.shape, NEG_INF, F32)

        def absorb(s, smax, vtblk, b):
            m_prev = m_ref[b]
            m_new = jnp.maximum(m_prev, smax)
            alpha = jnp.exp2(m_prev - m_new)
            p = jnp.exp2(s - m_new).astype(BF16)
            acc_ref[b] = alpha * acc_ref[b] + _dot(values(vtblk, b), p)
            m_ref[b] = m_new

        def produce(slot, kblk):
            for b in range(2):
                s = scores(kblk, b)
                stage_ref[slot, b] = s
                mx_ref[slot, b] = jnp.max(s, axis=0, keepdims=True)

        def consume(slot, vtblk):
            for b in range(2):
                absorb(stage_ref[slot, b], mx_ref[slot, b], vtblk, b)

        def direct(kblk, vtblk):
            for b in range(2):
                s = scores(kblk, b)
                absorb(s, jnp.max(s, axis=0, keepdims=True), vtblk, b)
    else:
        bound = bound_ref[0]

        def probs(kblk, b):
            return jnp.exp2(scores(kblk, b) - bound).astype(BF16)

        def produce(slot, kblk):
            for b in range(2):
                stage_ref[slot, b] = probs(kblk, b)

        def consume(slot, vtblk):
            for b in range(2):
                acc_ref[b] += _dot(values(vtblk, b), stage_ref[slot, b])

        def direct(kblk, vtblk):
            for b in range(2):
                acc_ref[b] += _dot(values(vtblk, b), probs(kblk, b))

    if nk == 1:
        if has_ctx:
            direct(kc_ref[...], vct_ref[0])
        direct(k_ref[...], vt_ref[0, 0])
    else:
        unroll = FLASH_UNROLL if nk % FLASH_UNROLL == 0 else 2
        assert nk % unroll == 0
        produce(0, k_ref[0:tk, :])
        if has_ctx:
            direct(kc_ref[...], vct_ref[0])

        def body(i, carry):
            c0 = unroll * i
            for u in range(unroll):
                nxt = jnp.minimum(c0 + u + 1, nk - 1)
                produce((u + 1) % 2, k_ref[pl.ds(pl.multiple_of(nxt * tk, tk), tk), :])
                consume(u % 2, vt_ref[0, c0 + u])
            return carry

        lax.fori_loop(0, nk // unroll, body, 0)

    dv = rows_b - ONES_ROWS
    o0 = acc_ref[0, :dv] / acc_ref[0, dv:dv + 1]
    o1 = acc_ref[1, :dv] / acc_ref[1, dv:dv + 1]
    if mode == "diff":
        ex = ex_ref[...]
        lam = (jnp.exp(jnp.sum(ex[0:1] * ex[1:2], axis=-1, keepdims=True))
               - jnp.exp(jnp.sum(ex[2:3] * ex[3:4], axis=-1, keepdims=True)) + lam_init)
        o = o0 - lam * o1
        o = o * lax.rsqrt(jnp.mean(o * o, axis=0, keepdims=True) + EPS) * gcol_ref[...] * (1.0 - lam_init)
    else:
        o = jnp.concatenate([o0, o1], axis=0)
    o_ref[...] = o.T.astype(o_ref.dtype)


def _flash(q, k, vt, kc, vct, ex, gcol, bound=None, *, mode, lam_init=0.0):
    t = q.shape[0]
    tkeys = k.shape[0]
    qw = 2 * HEAD_PAD if mode == "mla" else HEAD_PAD
    groups = q.shape[1] // qw
    tq = min(FLASH_TQ, t)
    _, nk, vrows, tk = vt.shape
    assert nk * tk == tkeys
    has_ctx = kc is not None
    dv = HEAD_PAD if mode == "diff" else HEAD_PAD // 2
    rows_b = dv + ONES_ROWS
    bound_arr = jnp.zeros((1,), F32) if bound is None else jnp.reshape(bound, (1,)).astype(F32)
    in_specs = [pl.BlockSpec(memory_space=pltpu.SMEM),
                pl.BlockSpec((tq, qw), lambda g, i: (i, g)),
                pl.BlockSpec((tkeys, qw), lambda g, i: (0, g)),
                pl.BlockSpec((1, nk, vrows, tk), lambda g, i: (g, 0, 0, 0))]
    args = [bound_arr, q, k, vt]
    if has_ctx:
        lc = kc.shape[0]
        in_specs += [pl.BlockSpec((lc, qw), lambda g, i: (0, g)),
                     pl.BlockSpec((1, vrows, lc), lambda g, i: (g, 0, 0))]
        args += [kc, vct.reshape(groups, vrows, lc)]
    in_specs += [_full(ex.shape), _full(gcol.shape)]
    args += [ex, gcol]

    def call(online):
        acc = pltpu.VMEM((2, rows_b, tq), F32)
        if online:
            scratch = [acc, pltpu.VMEM((2, 2, tk, tq), F32), pltpu.VMEM((2, 1, tq), F32), pltpu.VMEM((2, 2, 1, tq), F32)]
        else:
            scratch = [acc, pltpu.VMEM((2, 2, tk, tq), BF16)]
        return pl.pallas_call(
            functools.partial(_flash_kernel, mode=mode, has_ctx=has_ctx, online=online, lam_init=lam_init),
            grid=(groups, t // tq),
            in_specs=in_specs,
            out_specs=pl.BlockSpec((tq, HEAD_PAD), lambda g, i: (i, g)),
            out_shape=jax.ShapeDtypeStruct((t, groups * HEAD_PAD), BF16),
            scratch_shapes=scratch,
            compiler_params=_cparams("parallel", "parallel"),
            name="flash_" + mode + ("_online" if online else "_bounded"),
        )(*args)

    if bound is None:
        return call(True)
    return lax.cond(bound <= FLASH_SAFE_BOUND, lambda: call(False), lambda: call(True))


def _na_kernel(q_ref, kp_ref, kcur_ref, kn_ref, vtp_ref, vtcur_ref, vtn_ref, kc_ref, vct_ref, bias_ref, o_ref,
               s_ref, mx_ref):
    heads = q_ref.shape[1] // B_DH
    lc = kc_ref.shape[0]
    step = B_DH + ONES_ROWS
    row = lax.broadcasted_iota(jnp.int32, (HEAD_PAD, 1), 0)

    def pair_cols(h):
        return slice((h // 2) * HEAD_PAD, (h // 2 + 1) * HEAD_PAD)

    def produce(slot, h):
        ps = pair_cols(h)
        qt = q_ref[:, ps].astype(F32).T.astype(BF16)
        qb = jnp.where((row < 64) if h % 2 == 0 else (row >= 64), qt, jnp.zeros_like(qt))
        k_nb = jnp.concatenate([kp_ref[:, ps], kcur_ref[:, ps], kn_ref[:, ps]], axis=0)
        s_ctx = _dot(kc_ref[:, ps], qb)
        s_nb = _dot(k_nb, qb) + bias_ref[0, h]
        s_ref[slot, 0:lc] = s_ctx
        s_ref[slot, lc:] = s_nb
        mx_ref[slot] = jnp.maximum(jnp.max(s_ctx, axis=0, keepdims=True), jnp.max(s_nb, axis=0, keepdims=True))

    def consume(slot, h):
        pr, b = h // 2, h % 2
        vt = jnp.concatenate([vct_ref[pr, 0], vtp_ref[pr, 0], vtcur_ref[pr, 0], vtn_ref[pr, 0]], axis=1)
        p = jnp.exp2(s_ref[slot] - mx_ref[slot]).astype(BF16)
        acc = _dot(vt[b * step:(b + 1) * step], p)
        return acc[:B_DH] / acc[B_DH:B_DH + 1]

    produce(0, 0)
    outs = []
    for h in range(heads):
        if h + 1 < heads:
            produce((h + 1) % 2, h + 1)
        outs.append(consume(h % 2, h))
        if h % 2:
            o_ref[:, pair_cols(h)] = jnp.concatenate(outs[-2:], axis=0).T.astype(o_ref.dtype)


def _na_bias_table(rpb, rows):
    nr, w = NA_ROWS, GRID_W
    heads = rpb.shape[0]
    pad = w - NA_WIN_W
    padded = jnp.pad(rpb.astype(F32) * LOG2E, ((0, 0), (0, 0), (pad, pad + 1)))
    skew = jnp.tile(padded, (1, 1, w))[:, :, :w * (2 * w - 1)].reshape(heads, -1, w, 2 * w - 1)
    toep = skew[:, :, :, w - 1:]
    cq = np.arange(w)
    cs = np.clip(cq - NA_WIN_W // 2, 0, w - NA_WIN_W)
    valid_c = (cq[None, :] >= cs[:, None]) & (cq[None, :] < cs[:, None] + NA_WIN_W)
    toep_t = jnp.swapaxes(jnp.where(jnp.asarray(valid_c), toep, NEG_INF), -1, -2)
    d0 = NA_WIN_H - 1 - nr
    assert d0 - (nr - 1) >= 0 and d0 + 3 * nr <= 2 * NA_WIN_H - 1
    dense = jnp.concatenate([toep_t[:, d0 - rl:d0 - rl + 3 * nr] for rl in range(nr)], axis=-1)
    wh = min(NA_WIN_H, rows)
    rl = np.arange(nr)
    rr_rel = np.arange(3 * nr)
    valid_all = []
    for base in (0, nr, rows - nr):
        rs = np.clip(base + rl - wh // 2, 0, rows - wh)
        rr = base - nr + rr_rel
        valid_all.append((rr[:, None] >= rs[None, :]) & (rr[:, None] < rs[None, :] + wh))
    valid_r = np.repeat(np.stack(valid_all), w, axis=-1)[:, None, :, None, :]
    tab = jnp.where(jnp.asarray(valid_r), dense[None], NEG_INF)
    return tab.reshape(3, heads, 3 * nr * w, nr * w)


def _neighbourhood(q, k, vt, kc, vct, bias):
    t = q.shape[0]
    qn = NA_ROWS * GRID_W
    nb = t // qn
    lc = kc.shape[0]
    pairs, _, vrows, tk = vt.shape
    per = tk // qn
    cur = lambda i: (i, 0)
    prev = lambda i: (jnp.maximum(i - 1, 0), 0)
    nxt = lambda i: (jnp.minimum(i + 1, nb - 1), 0)
    blk = lambda f: pl.BlockSpec((qn, q.shape[1]), f)
    vblk = lambda f: pl.BlockSpec((pairs, 1, vrows, qn), lambda i: (0, f(i)[0] // per, 0, f(i)[0] % per))
    case = lambda i: (jnp.where(i == 0, 0, jnp.where(i == nb - 1, 2, 1)), 0, 0, 0)
    return pl.pallas_call(
        _na_kernel,
        grid=(nb,),
        in_specs=[blk(cur), blk(prev), blk(cur), blk(nxt), vblk(prev), vblk(cur), vblk(nxt),
                  _full(kc.shape), _full(vct.shape), pl.BlockSpec((1,) + bias.shape[1:], case)],
        out_specs=blk(cur),
        out_shape=jax.ShapeDtypeStruct(q.shape, BF16),
        scratch_shapes=[pltpu.VMEM((2, lc + 3 * qn, qn), F32), pltpu.VMEM((2, 1, qn), F32)],
        compiler_params=_cparams("arbitrary"),
        name="neighbourhood",
    )(q, k, k, k, vt, vt, vt, kc, vct, bias)


def _wgqa_kernel(sink_ref, q_ref, kp_ref, kcur_ref, kn_ref, vp_ref, vcur_ref, vn_ref, kc_ref, vc_ref, o_ref):
    i = pl.program_id(0)
    nb = pl.num_programs(0)
    qb = q_ref.shape[0]
    lc = kc_ref.shape[0]
    nkeys = lc + 3 * qb
    lane = _lane_iota(LANES)
    col = lax.broadcasted_iota(jnp.int32, (qb, nkeys), 1)
    row = lax.broadcasted_iota(jnp.int32, (qb, nkeys), 0)
    rel = col - (lc + qb) - row
    lo = jnp.where(i > 0, lc, lc + qb)
    hi = jnp.where(i < nb - 1, nkeys, lc + 2 * qb)
    valid = (col < lc) | ((jnp.abs(rel) <= D_WINDOW) & (col >= lo) & (col < hi))
    groups = D_HEADS // D_KV_HEADS
    for kv in range(D_KV_HEADS):
        ks = slice(kv * HEAD_PAD, (kv + 1) * HEAD_PAD)
        kk = jnp.concatenate([kc_ref[:, ks], kp_ref[:, ks], kcur_ref[:, ks], kn_ref[:, ks]], axis=0)
        vv = jnp.concatenate([vc_ref[:, ks], vp_ref[:, ks], vcur_ref[:, ks], vn_ref[:, ks]], axis=0)
        lhs = []
        for g in range(groups):
            h = kv * groups + g
            qp = q_ref[:, (h // 2) * HEAD_PAD:(h // 2 + 1) * HEAD_PAD]
            keep = (lane < 64) if h % 2 == 0 else (lane >= 64)
            lhs.append(jnp.where(keep, qp, jnp.zeros_like(qp)))
        s = _dot_nt(jnp.concatenate(lhs, axis=0), kk)
        outs = []
        for g in range(groups):
            sink = sink_ref[kv * groups + g] * LOG2E
            sg = jnp.where(valid, s[g * qb:(g + 1) * qb], NEG_INF)
            m = jnp.maximum(jnp.max(sg, axis=-1, keepdims=True), sink)
            p = jnp.exp2(sg - m)
            l = jnp.sum(p, axis=-1, keepdims=True) + jnp.exp2(sink - m)
            outs.append(_dot(p.astype(BF16), vv) / l)
        for pp in range(groups // 2):
            pair = kv * (groups // 2) + pp
            o_ref[:, pair * HEAD_PAD:(pair + 1) * HEAD_PAD] = jnp.where(
                lane < 64, outs[2 * pp], outs[2 * pp + 1]).astype(o_ref.dtype)


def _window_gqa(q, k, v, kc, vc, sink):
    t = q.shape[0]
    qb = D_WINDOW
    nb = t // qb
    lc = kc.shape[0]
    kw = k.shape[1]
    cur = lambda i: (i, 0)
    prev = lambda i: (jnp.maximum(i - 1, 0), 0)
    nxt = lambda i: (jnp.minimum(i + 1, nb - 1), 0)
    kblk = lambda f: pl.BlockSpec((qb, kw), f)
    return pl.pallas_call(
        _wgqa_kernel,
        grid=(nb,),
        in_specs=[pl.BlockSpec(memory_space=pltpu.SMEM), pl.BlockSpec((qb, q.shape[1]), cur),
                  kblk(prev), kblk(cur), kblk(nxt), kblk(prev), kblk(cur), kblk(nxt),
                  _full((lc, kw)), _full((lc, kw))],
        out_specs=pl.BlockSpec((qb, q.shape[1]), cur),
        out_shape=jax.ShapeDtypeStruct(q.shape, BF16),
        compiler_params=_cparams("parallel"),
        name="window_gqa",
    )(sink, q, k, k, k, v, v, v, kc, vc)


def _mix_ffn_kernel(o1_ref, o1p_ref, o1n_ref, o2_ref, o2p_ref, o2n_ref, x_ref, xp_ref, xn_ref,
                    wout_ref, vec_ref, wup_ref, cv_ref, wdn_ref, o_ref, acc_ref, hcat_ref, u_ref, x1_ref):
    i = pl.program_id(0)
    n = pl.num_programs(0)
    tm = x_ref.shape[0]
    rows = tm + 2 * HALO
    nchunk = wdn_ref.shape[0] // FFN_CHUNK
    half = o1_ref.shape[1]
    vec = vec_ref[...]
    o1 = jnp.concatenate([o1p_ref[...], o1_ref[...], o1n_ref[...]], axis=0)
    o2 = jnp.concatenate([o2p_ref[...], o2_ref[...], o2n_ref[...]], axis=0)
    x = jnp.concatenate([xp_ref[...], x_ref[...], xn_ref[...]], axis=0)
    x1 = x + vec[0:1] * (_dot(o1, wout_ref[:half, :]) + _dot(o2, wout_ref[half:, :]))
    x1_ref[...] = x1[HALO:HALO + tm]
    hcat_ref[...] = (_rms(x1) * vec[1:2] * (1.0 + vec[2:3]) + vec[3:4]).astype(BF16)

    @pl.when(i == 0)
    def _():
        hcat_ref[0:HALO] = jnp.zeros((HALO, hcat_ref.shape[1]), BF16)

    @pl.when(i == n - 1)
    def _():
        hcat_ref[HALO + tm:rows] = jnp.zeros((HALO, hcat_ref.shape[1]), BF16)

    acc_ref[...] = jnp.zeros(acc_ref.shape, F32)

    def conv(u, cv):
        before = pltpu.roll(u, 1, axis=0)
        after = pltpu.roll(u, rows - 1, axis=0)
        r = cv[3:4] + before * cv[0:1] + u * cv[1:2] + after * cv[2:3]
        return r[HALO:HALO + tm]

    def cols(c, gate):
        return pl.ds(pl.multiple_of((gate * nchunk + c) * FFN_CHUNK, FFN_CHUNK), FFN_CHUNK)

    def stage_up(slot, c):
        hcat = hcat_ref[...]
        u_ref[slot, 0] = _dot(hcat, wup_ref[:, cols(c, 0)])
        u_ref[slot, 1] = _dot(hcat, wup_ref[:, cols(c, 1)])

    def stage_down(slot, c):
        a = conv(u_ref[slot, 0], cv_ref[:, cols(c, 0)])
        g = conv(u_ref[slot, 1], cv_ref[:, cols(c, 1)])
        act = (a * (g * _sigmoid(g))).astype(BF16)
        acc_ref[...] += _dot(act, wdn_ref[cols(c, 0), :])

    assert nchunk % 2 == 1
    stage_up(0, 0)

    def body(j, carry):
        c = 2 * j
        stage_up(1, c + 1)
        stage_down(0, c)
        stage_up(0, c + 2)
        stage_down(1, c + 1)
        return carry

    lax.fori_loop(0, nchunk // 2, body, 0)
    stage_down(0, nchunk - 1)
    o_ref[...] = x1_ref[...] + vec[4:5] * acc_ref[...]


def _mix_ffn(o1, o2, x, wout, vec, wup, cv, wdn):
    t, d = x.shape
    tm = min(512, t)
    nt = t // tm
    hb = tm // HALO
    row = lambda i: (i, 0)
    prev = lambda i: (jnp.maximum(i * hb - 1, 0), 0)
    nxt = lambda i: (jnp.minimum((i + 1) * hb, t // HALO - 1), 0)
    tiled = lambda w: [pl.BlockSpec((tm, w), row), pl.BlockSpec((HALO, w), prev), pl.BlockSpec((HALO, w), nxt)]
    return pl.pallas_call(
        _mix_ffn_kernel,
        grid=(nt,),
        in_specs=tiled(o1.shape[1]) + tiled(o2.shape[1]) + tiled(d)
                 + [_resident(a.shape) for a in (wout, vec, wup, cv, wdn)],
        out_specs=pl.BlockSpec((tm, d), row),
        out_shape=jax.ShapeDtypeStruct((t, d), F32),
        scratch_shapes=[pltpu.VMEM((tm, d), F32), pltpu.VMEM((tm + 2 * HALO, d), BF16),
                        pltpu.VMEM((2, 2, tm + 2 * HALO, FFN_CHUNK), F32), pltpu.VMEM((tm, d), F32)],
        compiler_params=_cparams("parallel"),
        name="mix_ffn",
    )(o1, o1, o1, o2, o2, o2, x, x, x, wout, vec, wup, cv, wdn)


def _axial_tables(n_tokens, dim):
    t = jnp.arange(n_tokens, dtype=jnp.int32)
    row = (t // GRID_W).astype(F32)
    col = (t % GRID_W).astype(F32)
    quarter = dim // 4
    inv_freq = ROPE_BASE ** (-jnp.arange(quarter, dtype=F32) / quarter)
    ang = jnp.concatenate([row[:, None] * inv_freq, col[:, None] * inv_freq], axis=-1)
    return jnp.cos(ang), jnp.sin(ang)


def _rope_tables_head64(t):
    cos, sin = _axial_tables(t, 64)
    return jnp.tile(jnp.concatenate([cos, cos], -1), (1, 2)), jnp.tile(jnp.concatenate([-sin, sin], -1), (1, 2))


def _rope_tables_latent(t):
    cos, sin = _axial_tables(t, C_ROPE)
    ones = jnp.ones((t, C_NOPE), F32)
    tail = jnp.ones((t, HEAD_PAD - C_NOPE - C_ROPE), F32)
    cosc = jnp.concatenate([ones, cos, cos, tail], -1)
    sinc = jnp.concatenate([0 * ones, -sin, sin, 0 * tail], -1)
    return cosc, sinc


def _group_ones(width, group):
    return jnp.asarray(np.kron(np.eye(width // group), np.ones((group, group))), BF16)


def _pad_heads(a, heads, dim):
    a = a.reshape(a.shape[:-1] + (heads, dim))
    a = jnp.pad(a, [(0, 0)] * (a.ndim - 1) + [(0, HEAD_PAD - dim)])
    return a.reshape(a.shape[:-2] + (heads * HEAD_PAD,))


def _logit_bound(gq, gk, dim):
    return 1.02 * LOG2E * math.sqrt(dim) * jnp.max(jnp.abs(gq)) * jnp.max(jnp.abs(gk))


def _pad_row(v, width):
    return jnp.pad(v, (0, width - v.shape[0]))


def _ffn_weights(w_up, conv_w, conv_b, w_down):
    cv = jnp.concatenate([conv_w, conv_b[None]], axis=0)
    return w_up.astype(BF16), cv, w_down.astype(BF16)


def kernel(x, c, ctx, c_ctx, ada_w, ada_b, norm_mix, norm_ffn, ffn_up, ffn_conv_w, ffn_conv_b, ffn_down, ev_w_in, ev_w_out, a_q_norm, a_k_norm, a_lam_q1, a_lam_k1, a_lam_q2, a_lam_k2, a_subln, b_q_norm, b_k_norm, b_rpb, od_w_in, od_w_out, c_q_a_norm, c_w_qb, c_kv_a_norm, c_w_kvb, c_q_norm, c_k_norm, d_q_norm, d_k_norm, d_sink):
    assert x.shape[0] == 1 and ctx.shape[0] == 1
    xm = x[0]
    xc = ctx[0]
    t, d = xm.shape
    lc = xc.shape[0]
    mods = _modulation(c, c_ctx, ada_w, ada_b)
    cos64, sin64 = _rope_tables_head64(t)
    cosc, sinc = _rope_tables_latent(t)
    g64 = _group_ones(MXU_TILE, 64)
    g128 = _group_ones(MXU_TILE, HEAD_PAD)
    dummy_tab = jnp.zeros((lc, LANES), F32)

    for l in range(DEPTH):
        last = l == DEPTH - 1
        i = l // 2
        mm = mods[l, 0].reshape(6, d)
        mc = mods[l, 1].reshape(6, d)
        pre_m = jnp.stack([norm_mix[l], mm[1], mm[0]])
        pre_c = jnp.stack([norm_mix[l], mc[1], mc[0]])
        post_m = jnp.stack([mm[2], norm_ffn[l], mm[4], mm[3], mm[5]])
        post_c = jnp.stack([mc[2], norm_ffn[l], mc[4], mc[3], mc[5]])
        wup, cv, wdn = _ffn_weights(ffn_up[l], ffn_conv_w[l], ffn_conv_b[l], ffn_down[l])
        if l % 2 == 0:
            lam_init = 0.8 - 0.6 * math.exp(-0.3 * l)
            w_in = ev_w_in[i].astype(BF16)
            w_out = ev_w_out[i].astype(BF16)
            hg = jnp.stack([jnp.tile(a_q_norm[i], 8), jnp.tile(a_k_norm[i], 8),
                            jnp.tile(b_q_norm[i], 8), jnp.tile(b_k_norm[i], 8)])
            ex = jnp.stack([_pad_row(a_lam_q1[i], LANES), _pad_row(a_lam_k1[i], LANES),
                            _pad_row(a_lam_q2[i], LANES), _pad_row(a_lam_k2[i], LANES)])
            gcol = a_subln[i][:, None]
            qa, ka, va, qb, kb, vb = _premix_even(xm, pre_m, w_in, g64, hg, cos64, sin64, True)
            qa_c, ka_c, va_c, qb_c, kb_c, vb_c = _premix_even(xc, pre_c, w_in, g64, hg, dummy_tab, dummy_tab, False)
            oa = _flash(qa, ka, va, ka_c, va_c, ex, gcol, _logit_bound(a_q_norm[i], a_k_norm[i], A_DQK),
                        mode="diff", lam_init=lam_init)
            ob = _neighbourhood(qb, kb, vb, kb_c, vb_c, _na_bias_table(b_rpb[i], t // GRID_W))
            if not last:
                oa_c = _flash(qa_c, ka_c, va_c, None, None, ex, gcol, mode="diff", lam_init=lam_init)
                ob_c = _flash(qb_c, kb_c, vb_c, None, None, ex, gcol, mode="pair")
        else:
            w = od_w_in[i]
            cq, ckv, ckr, dq, dk, dv = jnp.split(w, np.cumsum([C_Q_LORA, C_KV_LORA, C_ROPE, 512, 128]).tolist(), axis=1)
            dup = lambda a: jnp.concatenate([a[:, :64], a[:, :64], a[:, 64:], a[:, 64:]], axis=1)
            ckr_pad = jnp.pad(ckr, ((0, 0), (C_NOPE, HEAD_PAD - C_NOPE - C_ROPE)))
            w_in = jnp.concatenate([cq, ckv, dq, dup(dk), dup(dv), ckr_pad], axis=1).astype(BF16)
            w_out = od_w_out[i].astype(BF16)
            wqb = _pad_heads(c_w_qb[i], C_HEADS, C_NOPE + C_ROPE).astype(BF16)
            kvb = c_w_kvb[i].reshape(C_KV_LORA, C_HEADS, C_NOPE + C_DV)
            wk = _pad_heads(kvb[:, :, :C_NOPE].reshape(C_KV_LORA, -1), C_HEADS, C_NOPE).astype(BF16)
            wv = kvb[:, :, C_NOPE:].reshape(C_KV_LORA, C_HEADS * C_DV).astype(BF16)
            gains = jnp.stack([jnp.tile(_pad_row(c_q_norm[i], HEAD_PAD), C_HEADS),
                               jnp.tile(_pad_row(c_k_norm[i], HEAD_PAD), C_HEADS),
                               _pad_row(jnp.tile(d_q_norm[i], 8), 1024), _pad_row(jnp.tile(d_k_norm[i], 4), 1024)])
            lora = jnp.stack([_pad_row(c_q_a_norm[i], 512), _pad_row(c_kv_a_norm[i], 512)])
            odd = functools.partial(_premix_odd, w=w_in, wqb=wqb, wk=wk, wv=wv, g128=g128, g64=g64, gains=gains, lora=lora)
            mq, mk, mv, gq, gk, gv = odd(xm, pre_m, cos=cos64, sin=sin64, cosc=cosc, sinc=sinc, use_rope=True)
            mq_c, mk_c, mv_c, gq_c, gk_c, gv_c = odd(xc, pre_c, cos=dummy_tab, sin=dummy_tab, cosc=dummy_tab,
                                                     sinc=dummy_tab, use_rope=False)
            ex = jnp.zeros((4, LANES), F32)
            oa = _flash(mq, mk, mv, mk_c, mv_c, ex, jnp.ones((HEAD_PAD, 1), F32),
                        _logit_bound(c_q_norm[i], c_k_norm[i], C_NOPE + C_ROPE), mode="mla")
            ob = _window_gqa(gq, gk, gv, gk_c, gv_c, d_sink[i])
            if not last:
                raise NotImplementedError("context update after an odd layer is not needed at this depth")
        xm = _mix_ffn(oa, ob, xm, w_out, post_m, wup, cv, wdn)
        if not last:
            xc = _mix_ffn(oa_c, ob_c, xc, w_out, post_c, wup, cv, wdn)
    return xm[None]
```

```python
import functools
import math

import numpy as np
import jax
import jax.numpy as jnp
from jax import lax
from jax.experimental import pallas as pl
from jax.experimental.pallas import tpu as pltpu

D_MODEL = 1024
DEPTH = 2
GRID_W = 64
EPS = 1e-6
ROPE_BASE = 10000.0
NEG_INF = -1e30
LOG2E = math.log2(math.e)

A_HEADS = 4
A_DQK = 64
A_DV = 128
B_HEADS = 8
B_DH = 64
NA_WIN_H = 8
NA_WIN_W = 16
C_HEADS = 8
C_Q_LORA = 384
C_KV_LORA = 256
C_NOPE = 64
C_ROPE = 32
C_DV = 64
D_HEADS = 8
D_KV_HEADS = 2
D_DH = 64
D_WINDOW = 128
FFN_HIDDEN = 2816
CONV_W = 3

LANES = 128
MXU_TILE = 256
HEAD_PAD = 128
VMEM_LIMIT = 56 * 1024 * 1024
FFN_CHUNK = 256
HALO = 16
NA_ROWS = 4
WG_Q = 256
ONES_ROWS = 16
FLASH_TQ = 4096
FLASH_TQ_ONLINE = 1024
FLASH_TK = 512
FLASH_UNROLL = 4
FLASH_SAFE_BOUND = 40.0

F32 = jnp.float32
BF16 = jnp.bfloat16


def _cparams(*sem):
    return pltpu.CompilerParams(dimension_semantics=sem, vmem_limit_bytes=VMEM_LIMIT)


def _full(shape):
    n = len(shape)
    return pl.BlockSpec(shape, lambda *_: (0,) * n)


def _resident(shape):
    n = len(shape)
    return pl.BlockSpec(shape, lambda *_: (0,) * n, pipeline_mode=pl.Buffered(1))


def _dot(a, b):
    return jnp.dot(a, b, preferred_element_type=F32)


def _dot_nt(a, b):
    return lax.dot_general(a, b, (((1,), (1,)), ((), ())), preferred_element_type=F32)


def _rms(x):
    return x * lax.rsqrt(jnp.mean(x * x, axis=-1, keepdims=True) + EPS)


def _sigmoid(x):
    return 1.0 / (1.0 + jnp.exp(-x))


def _group_rms(y, gmat, inv_n, gain):
    yy = (y * y).astype(BF16)
    slab = gmat.shape[0]
    ss = jnp.concatenate([_dot(yy[:, c:c + slab], gmat) for c in range(0, y.shape[-1], slab)], axis=-1)
    return y * lax.rsqrt(ss * inv_n + EPS) * gain


def _rope(y, cos, sin, half, first_half):
    w = y.shape[-1]
    fwd = pltpu.roll(y, w - half, axis=1)
    bwd = pltpu.roll(y, half, axis=1)
    return y * cos + jnp.where(first_half, fwd, bwd) * sin


def _lane_tile(t, width):
    reps = width // t.shape[-1]
    return t if reps == 1 else jnp.concatenate([t] * reps, axis=-1)


def _lane_iota(width):
    return lax.broadcasted_iota(jnp.int32, (1, width), 1)


def _mod_kernel(ct_ref, w_ref, b_ref, o_ref):
    ct = ct_ref[...]
    s = ct * _sigmoid(ct)
    w = w_ref[0]
    rows = [jnp.sum(w * s[:, j:j + 1], axis=0, keepdims=True) for j in range(2)]
    o_ref[0] = jnp.concatenate(rows, axis=0) + b_ref[0]


def _modulation(c, c_ctx, ada_w, ada_b):
    depth, d, n = ada_w.shape
    tn = 1536
    ct = jnp.stack([c[0], c_ctx], axis=1)
    return pl.pallas_call(
        _mod_kernel,
        grid=(depth, n // tn),
        in_specs=[_full((d, 2)),
                  pl.BlockSpec((1, d, tn), lambda l, j: (l, 0, j)),
                  pl.BlockSpec((1, 1, tn), lambda l, j: (l, 0, j))],
        out_specs=pl.BlockSpec((1, 2, tn), lambda l, j: (l, 0, j)),
        out_shape=jax.ShapeDtypeStruct((depth, 2, n), F32),
        compiler_params=_cparams("arbitrary", "arbitrary"),
        name="modulation",
    )(ct, ada_w, ada_b.reshape(depth, 1, n))


def _store_values_t(vt_ref, v, dv):
    tm = v.shape[0]
    vt = v.T.astype(BF16)
    ones = jnp.ones((ONES_ROWS, tm), BF16)
    step = dv + ONES_ROWS
    for g in range(vt_ref.shape[0]):
        for h in range(HEAD_PAD // dv):
            src = g * HEAD_PAD + h * dv
            vt_ref[g, 0, h * step:h * step + dv, :] = vt[src:src + dv]
            vt_ref[g, 0, h * step + dv:(h + 1) * step, :] = ones


def _values_t_shape(t, tm, dv, groups=4):
    return (groups, t // tm, (HEAD_PAD // dv) * (dv + ONES_ROWS), tm)


def _premix_even_kernel(x_ref, vec_ref, w_ref, g_ref, hg_ref, cos_ref, sin_ref,
                        qa_ref, ka_ref, vat_ref, qb_ref, kb_ref, vbt_ref, *, use_rope):
    vec = vec_ref[...]
    h = (_rms(x_ref[...]) * vec[0:1] * (1.0 + vec[1:2]) + vec[2:3]).astype(BF16)
    gm = g_ref[...]
    hg = hg_ref[...]
    sw = 512
    if use_rope:
        cos = _lane_tile(cos_ref[...], sw)
        sin = _lane_tile(sin_ref[...], sw)
        first = (_lane_iota(sw) & 32) == 0

    def seg(i):
        return _dot(h, w_ref[:, i * sw:(i + 1) * sw])

    qa = _group_rms(seg(0), gm, 1.0 / A_DQK, hg[0:1])
    ka = _group_rms(seg(1), gm, 1.0 / A_DQK, hg[1:2])
    if use_rope:
        qa = _rope(qa, cos, sin, 32, first)
        ka = _rope(ka, cos, sin, 32, first)
    qa_ref[...] = (qa * (LOG2E / math.sqrt(A_DQK))).astype(BF16)
    ka_ref[...] = ka.astype(BF16)
    _store_values_t(vat_ref, seg(2), A_DV)
    qb = _group_rms(seg(3), gm, 1.0 / B_DH, hg[2:3])
    qb_ref[...] = (qb * (LOG2E / math.sqrt(B_DH))).astype(BF16)
    kb_ref[...] = _group_rms(seg(4), gm, 1.0 / B_DH, hg[3:4]).astype(BF16)
    _store_values_t(vbt_ref, seg(5), B_DH)


def _premix_even(x, vec, w, gmat, hg, cos, sin, use_rope):
    t, d = x.shape
    tm = min(FLASH_TK, t)
    row = lambda i: (i, 0)
    nat = jax.ShapeDtypeStruct((t, 512), BF16)
    nat_spec = pl.BlockSpec((tm, 512), row)
    vta = _values_t_shape(t, tm, A_DV)
    vtb = _values_t_shape(t, tm, B_DH)
    vt_spec = lambda s: pl.BlockSpec((s[0], 1) + s[2:], lambda i: (0, i, 0, 0))
    return pl.pallas_call(
        functools.partial(_premix_even_kernel, use_rope=use_rope),
        grid=(t // tm,),
        in_specs=[pl.BlockSpec((tm, d), row), _full(vec.shape), _full(w.shape), _full(gmat.shape),
                  _full(hg.shape), pl.BlockSpec((tm, LANES), row), pl.BlockSpec((tm, LANES), row)],
        out_specs=[nat_spec, nat_spec, vt_spec(vta), nat_spec, nat_spec, vt_spec(vtb)],
        out_shape=[nat, nat, jax.ShapeDtypeStruct(vta, BF16), nat, nat, jax.ShapeDtypeStruct(vtb, BF16)],
        compiler_params=_cparams("parallel"),
        name="premix_even",
    )(x, vec, w, gmat, hg, cos, sin)


_O_CQ, _O_CKV, _O_DQ, _O_DK, _O_DV, _O_CKR, _O_END = 0, 384, 640, 1152, 1408, 1536, 1664


def _premix_odd_kernel(x_ref, vec_ref, w_ref, wqb_ref, wk_ref, wv_ref, g128_ref, g64_ref, gain_ref, lora_ref,
                       cos_ref, sin_ref, cosc_ref, sinc_ref,
                       mq_ref, mk_ref, mvt_ref, gq_ref, gk_ref, gvt_ref, *, use_rope):
    vec = vec_ref[...]
    h = (_rms(x_ref[...]) * vec[0:1] * (1.0 + vec[1:2]) + vec[2:3]).astype(BF16)
    gains = gain_ref[...]
    lora = lora_ref[...]
    g128 = g128_ref[...]
    g64 = g64_ref[...]
    mw = C_HEADS * HEAD_PAD
    if use_rope:
        cosc = _lane_tile(cosc_ref[...], mw)
        sinc = _lane_tile(sinc_ref[...], mw)
        first_c = (_lane_iota(mw) & 127) < (C_NOPE + C_ROPE // 2)
        cos = _lane_tile(cos_ref[...], 512)
        sin = _lane_tile(sin_ref[...], 512)
        first_d = (_lane_iota(512) & 32) == 0

    def seg(a, b):
        return _dot(h, w_ref[:, a:b])

    inv_c = 1.0 / (C_NOPE + C_ROPE)
    cq = (_rms(seg(_O_CQ, _O_CKV)) * lora[0:1, :C_Q_LORA]).astype(BF16)
    mq = _group_rms(_dot(cq, wqb_ref[...]), g128, inv_c, gains[0:1])
    if use_rope:
        mq = _rope(mq, cosc, sinc, C_ROPE // 2, first_c)
    mq_ref[...] = (mq * (LOG2E * math.sqrt(inv_c))).astype(BF16)
    ckv = (_rms(seg(_O_CKV, _O_DQ)) * lora[1:2, :C_KV_LORA]).astype(BF16)
    mk = _dot(ckv, wk_ref[...]) + _lane_tile(seg(_O_CKR, _O_END), mw)
    mk = _group_rms(mk, g128, inv_c, gains[1:2])
    if use_rope:
        mk = _rope(mk, cosc, sinc, C_ROPE // 2, first_c)
    mk_ref[...] = mk.astype(BF16)
    _store_values_t(mvt_ref, _dot(ckv, wv_ref[...]), C_DV)
    gq = _group_rms(seg(_O_DQ, _O_DK), g64, 1.0 / D_DH, gains[2:3, :512])
    gk = _group_rms(seg(_O_DK, _O_DV), g64, 1.0 / D_DH, gains[3:4, :256])
    if use_rope:
        gq = _rope(gq, cos, sin, 32, first_d)
        gk = _rope(gk, cos[:, :256], sin[:, :256], 32, first_d[:, :256])
    gq_ref[...] = (gq * (LOG2E / math.sqrt(D_DH))).astype(BF16)
    gk_ref[...] = gk.astype(BF16)
    _store_values_t(gvt_ref, seg(_O_DV, _O_CKR), D_DH)


def _premix_odd(x, vec, w, wqb, wk, wv, g128, g64, gains, lora, cos, sin, cosc, sinc, use_rope):
    t, d = x.shape
    tm = min(FLASH_TK, t)
    row = lambda i: (i, 0)
    tab = pl.BlockSpec((tm, LANES), row)
    nat = lambda n: (pl.BlockSpec((tm, n), row), jax.ShapeDtypeStruct((t, n), BF16))
    def vt(shape):
        return pl.BlockSpec((shape[0], 1) + shape[2:], lambda i: (0, i, 0, 0)), jax.ShapeDtypeStruct(shape, BF16)

    outs = [nat(1024), nat(1024), vt(_values_t_shape(t, tm, C_DV)), nat(512), nat(256),
            vt(_values_t_shape(t, tm, D_DH, groups=1))]
    return pl.pallas_call(
        functools.partial(_premix_odd_kernel, use_rope=use_rope),
        grid=(t // tm,),
        in_specs=[pl.BlockSpec((tm, d), row)] + [_full(a.shape) for a in (vec, w, wqb, wk, wv, g128, g64, gains, lora)]
                 + [tab] * 4,
        out_specs=[o[0] for o in outs],
        out_shape=[o[1] for o in outs],
        compiler_params=_cparams("parallel"),
        name="premix_odd",
    )(x, vec, w, wqb, wk, wv, g128, g64, gains, lora, cos, sin, cosc, sinc)


def _flash_kernel(*refs, mode, has_ctx, online, lam_init):
    bound_ref, q_ref, k_ref, vt_ref = refs[:4]
    kc_ref, vct_ref = refs[4:6] if has_ctx else (None, None)
    ex_ref, gcol_ref, o_ref, acc_ref, stage_ref = refs[6:11] if has_ctx else refs[4:9]
    qt = q_ref[...].astype(F32).T.astype(BF16)
    row = lax.broadcasted_iota(jnp.int32, (HEAD_PAD, 1), 0)
    if mode == "mla":
        qs = [qt[:HEAD_PAD], qt[HEAD_PAD:]]
    else:
        zero = jnp.zeros_like(qt)
        qs = [jnp.where(row < 64, qt, zero), jnp.where(row >= 64, qt, zero)]
    nk, tk = vt_ref.shape[1], vt_ref.shape[3]

    acc_ref[...] = jnp.zeros(acc_ref.shape, F32)
    rows_b = acc_ref.shape[1]

    def scores(kblk, b):
        kb = kblk[:, b * HEAD_PAD:(b + 1) * HEAD_PAD] if mode == "mla" else kblk
        return _dot(kb, qs[b])

    def values(vtblk, b):
        return vtblk if vtblk.shape[0] == rows_b else vtblk[b * rows_b:(b + 1) * rows_b]

    if online:
        m_ref, mx_ref = refs[-2:]
        m_ref[...] = jnp.full(m_ref.shape, NEG_INF, F32)

        def absorb(s, smax, vtblk, b):
            m_prev = m_ref[b]
            m_new = jnp.maximum(m_prev, smax)
            alpha = jnp.exp2(m_prev - m_new)
            p = jnp.exp2(s - m_new).astype(BF16)
            acc_ref[b] = alpha * acc_ref[b] + _dot(values(vtblk, b), p)
            m_ref[b] = m_new

        def produce(slot, kblk):
            for b in range(2):
                s = scores(kblk, b)
                stage_ref[slot, b] = s
                mx_ref[slot, b] = jnp.max(s, axis=0, keepdims=True)

        def consume(slot, vtblk):
            for b in range(2):
                absorb(stage_ref[slot, b], mx_ref[slot, b], vtblk, b)

        def direct(kblk, vtblk):
            for b in range(2):
                s = scores(kblk, b)
                absorb(s, jnp.max(s, axis=0, keepdims=True), vtblk, b)
    else:
        bound = bound_ref[0]

        def probs(kblk, b):
            return jnp.exp2(scores(kblk, b) - bound).astype(BF16)

        def produce(slot, kblk):
            for b in range(2):
                stage_ref[slot, b] = probs(kblk, b)

        def consume(slot, vtblk):
            for b in range(2):
                acc_ref[b] += _dot(values(vtblk, b), stage_ref[slot, b])

        def direct(kblk, vtblk):
            for b in range(2):
                acc_ref[b] += _dot(values(vtblk, b), probs(kblk, b))

    if nk == 1:
        if has_ctx:
            direct(kc_ref[...], vct_ref[0])
        direct(k_ref[...], vt_ref[0, 0])
    else:
        unroll = FLASH_UNROLL if nk % FLASH_UNROLL == 0 else 2
        assert nk % unroll == 0
        produce(0, k_ref[0:tk, :])
        if has_ctx:
            direct(kc_ref[...], vct_ref[0])

        def body(i, carry):
            c0 = unroll * i
            for u in range(unroll):
                nxt = jnp.minimum(c0 + u + 1, nk - 1)
                produce((u + 1) % 2, k_ref[pl.ds(pl.multiple_of(nxt * tk, tk), tk), :])
                consume(u % 2, vt_ref[0, c0 + u])
            return carry

        lax.fori_loop(0, nk // unroll, body, 0)

    dv = rows_b - ONES_ROWS
    o0 = acc_ref[0, :dv] / acc_ref[0, dv:dv + 1]
    o1 = acc_ref[1, :dv] / acc_ref[1, dv:dv + 1]
    if mode == "diff":
        ex = ex_ref[...]
        lam = (jnp.exp(jnp.sum(ex[0:1] * ex[1:2], axis=-1, keepdims=True))
               - jnp.exp(jnp.sum(ex[2:3] * ex[3:4], axis=-1, keepdims=True)) + lam_init)
        o = o0 - lam * o1
        o = o * lax.rsqrt(jnp.mean(o * o, axis=0, keepdims=True) + EPS) * gcol_ref[...] * (1.0 - lam_init)
    else:
        o = jnp.concatenate([o0, o1], axis=0)
    o_ref[...] = o.T.astype(o_ref.dtype)


def _flash(q, k, vt, kc, vct, ex, gcol, bound=None, *, mode, lam_init=0.0):
    t = q.shape[0]
    tkeys = k.shape[0]
    qw = 2 * HEAD_PAD if mode == "mla" else HEAD_PAD
    groups = q.shape[1] // qw
    _, nk, vrows, tk = vt.shape
    assert nk * tk == tkeys
    has_ctx = kc is not None
    dv = HEAD_PAD if mode == "diff" else HEAD_PAD // 2
    rows_b = dv + ONES_ROWS
    bound_arr = jnp.zeros((1,), F32) if bound is None else jnp.reshape(bound, (1,)).astype(F32)
    args = [bound_arr, q, k, vt]
    if has_ctx:
        lc = kc.shape[0]
        args += [kc, vct.reshape(groups, vrows, lc)]
    args += [ex, gcol]

    def call(online):
        tq = min(FLASH_TQ_ONLINE if online else FLASH_TQ, t)
        in_specs = [pl.BlockSpec(memory_space=pltpu.SMEM),
                    pl.BlockSpec((tq, qw), lambda g, i: (i, g)),
                    pl.BlockSpec((tkeys, qw), lambda g, i: (0, g), pipeline_mode=pl.Buffered(1)),
                    pl.BlockSpec((1, nk, vrows, tk), lambda g, i: (g, 0, 0, 0), pipeline_mode=pl.Buffered(1))]
        if has_ctx:
            in_specs += [pl.BlockSpec((lc, qw), lambda g, i: (0, g)),
                         pl.BlockSpec((1, vrows, lc), lambda g, i: (g, 0, 0))]
        in_specs += [_full(ex.shape), _full(gcol.shape)]
        acc = pltpu.VMEM((2, rows_b, tq), F32)
        if online:
            scratch = [acc, pltpu.VMEM((2, 2, tk, tq), F32), pltpu.VMEM((2, 1, tq), F32), pltpu.VMEM((2, 2, 1, tq), F32)]
        else:
            scratch = [acc, pltpu.VMEM((2, 2, tk, tq), BF16)]
        return pl.pallas_call(
            functools.partial(_flash_kernel, mode=mode, has_ctx=has_ctx, online=online, lam_init=lam_init),
            grid=(groups, t // tq),
            in_specs=in_specs,
            out_specs=pl.BlockSpec((tq, HEAD_PAD), lambda g, i: (i, g)),
            out_shape=jax.ShapeDtypeStruct((t, groups * HEAD_PAD), BF16),
            scratch_shapes=scratch,
            compiler_params=_cparams("parallel", "parallel"),
            name="flash_" + mode + ("_online" if online else "_bounded"),
        )(*args)

    if bound is None:
        return call(True)
    return lax.cond(bound <= FLASH_SAFE_BOUND, lambda: call(False), lambda: call(True))


def _na_kernel(q_ref, kp_ref, kcur_ref, kn_ref, vtp_ref, vtcur_ref, vtn_ref, kc_ref, vct_ref, bias_ref, o_ref,
               s_ref, mx_ref):
    heads = q_ref.shape[1] // B_DH
    lc = kc_ref.shape[0]
    step = B_DH + ONES_ROWS
    row = lax.broadcasted_iota(jnp.int32, (HEAD_PAD, 1), 0)

    def pair_cols(h):
        return slice((h // 2) * HEAD_PAD, (h // 2 + 1) * HEAD_PAD)

    def produce(slot, h):
        ps = pair_cols(h)
        qt = q_ref[:, ps].astype(F32).T.astype(BF16)
        qb = jnp.where((row < 64) if h % 2 == 0 else (row >= 64), qt, jnp.zeros_like(qt))
        k_nb = jnp.concatenate([kp_ref[:, ps], kcur_ref[:, ps], kn_ref[:, ps]], axis=0)
        s_ctx = _dot(kc_ref[:, ps], qb)
        s_nb = _dot(k_nb, qb) + bias_ref[0, h]
        s_ref[slot, 0:lc] = s_ctx
        s_ref[slot, lc:] = s_nb
        mx_ref[slot] = jnp.maximum(jnp.max(s_ctx, axis=0, keepdims=True), jnp.max(s_nb, axis=0, keepdims=True))

    def consume(slot, h):
        pr, b = h // 2, h % 2
        vt = jnp.concatenate([vct_ref[pr, 0], vtp_ref[pr, 0], vtcur_ref[pr, 0], vtn_ref[pr, 0]], axis=1)
        p = jnp.exp2(s_ref[slot] - mx_ref[slot]).astype(BF16)
        acc = _dot(vt[b * step:(b + 1) * step], p)
        return acc[:B_DH] / acc[B_DH:B_DH + 1]

    produce(0, 0)
    outs = []
    for h in range(heads):
        if h + 1 < heads:
            produce((h + 1) % 2, h + 1)
        outs.append(consume(h % 2, h))
        if h % 2:
            o_ref[:, pair_cols(h)] = jnp.concatenate(outs[-2:], axis=0).T.astype(o_ref.dtype)


def _na_bias_table(rpb, rows):
    nr, w = NA_ROWS, GRID_W
    heads = rpb.shape[0]
    pad = w - NA_WIN_W
    padded = jnp.pad(rpb.astype(F32) * LOG2E, ((0, 0), (0, 0), (pad, pad + 1)))
    skew = jnp.tile(padded, (1, 1, w))[:, :, :w * (2 * w - 1)].reshape(heads, -1, w, 2 * w - 1)
    toep = skew[:, :, :, w - 1:]
    cq = np.arange(w)
    cs = np.clip(cq - NA_WIN_W // 2, 0, w - NA_WIN_W)
    valid_c = (cq[None, :] >= cs[:, None]) & (cq[None, :] < cs[:, None] + NA_WIN_W)
    toep_t = jnp.swapaxes(jnp.where(jnp.asarray(valid_c), toep, NEG_INF), -1, -2)
    d0 = NA_WIN_H - 1 - nr
    assert d0 - (nr - 1) >= 0 and d0 + 3 * nr <= 2 * NA_WIN_H - 1
    dense = jnp.concatenate([toep_t[:, d0 - rl:d0 - rl + 3 * nr] for rl in range(nr)], axis=-1)
    wh = min(NA_WIN_H, rows)
    rl = np.arange(nr)
    rr_rel = np.arange(3 * nr)
    valid_all = []
    for base in (0, nr, rows - nr):
        rs = np.clip(base + rl - wh // 2, 0, rows - wh)
        rr = base - nr + rr_rel
        valid_all.append((rr[:, None] >= rs[None, :]) & (rr[:, None] < rs[None, :] + wh))
    valid_r = np.repeat(np.stack(valid_all), w, axis=-1)[:, None, :, None, :]
    tab = jnp.where(jnp.asarray(valid_r), dense[None], NEG_INF)
    return tab.reshape(3, heads, 3 * nr * w, nr * w)


def _neighbourhood(q, k, vt, kc, vct, bias):
    t = q.shape[0]
    qn = NA_ROWS * GRID_W
    nb = t // qn
    lc = kc.shape[0]
    pairs, _, vrows, tk = vt.shape
    per = tk // qn
    cur = lambda i: (i, 0)
    prev = lambda i: (jnp.maximum(i - 1, 0), 0)
    nxt = lambda i: (jnp.minimum(i + 1, nb - 1), 0)
    blk = lambda f: pl.BlockSpec((qn, q.shape[1]), f)
    vblk = lambda f: pl.BlockSpec((pairs, 1, vrows, qn), lambda i: (0, f(i)[0] // per, 0, f(i)[0] % per))
    case = lambda i: (jnp.where(i == 0, 0, jnp.where(i == nb - 1, 2, 1)), 0, 0, 0)
    return pl.pallas_call(
        _na_kernel,
        grid=(nb,),
        in_specs=[blk(cur), blk(prev), blk(cur), blk(nxt), vblk(prev), vblk(cur), vblk(nxt),
                  _full(kc.shape), _full(vct.shape), pl.BlockSpec((1,) + bias.shape[1:], case)],
        out_specs=blk(cur),
        out_shape=jax.ShapeDtypeStruct(q.shape, BF16),
        scratch_shapes=[pltpu.VMEM((2, lc + 3 * qn, qn), F32), pltpu.VMEM((2, 1, qn), F32)],
        compiler_params=_cparams("arbitrary"),
        name="neighbourhood",
    )(q, k, k, k, vt, vt, vt, kc, vct, bias)


def _wgqa_kernel(sink_ref, q_ref, *refs):
    nkb = WG_Q // D_WINDOW + 2
    k_refs, vt_refs = refs[:nkb], refs[nkb:2 * nkb]
    kc_ref, vct_ref, mask_ref, o_ref, s_ref, mx_ref = refs[2 * nkb:]
    qn = q_ref.shape[0]
    pairs = D_HEADS // 2
    pairs_per_kv = pairs // D_KV_HEADS
    step = D_DH + ONES_ROWS
    row = lax.broadcasted_iota(jnp.int32, (HEAD_PAD, 1), 0)
    mask = jnp.concatenate([mask_ref[0]] * 2, axis=1)

    def produce(slot, pair):
        kv = pair // pairs_per_kv
        ks = slice(kv * HEAD_PAD, (kv + 1) * HEAD_PAD)
        qt = q_ref[:, pair * HEAD_PAD:(pair + 1) * HEAD_PAD].astype(F32).T.astype(BF16)
        zero = jnp.zeros_like(qt)
        q2 = jnp.concatenate([jnp.where(row < 64, qt, zero), jnp.where(row >= 64, qt, zero)], axis=1)
        kk = jnp.concatenate([kc_ref[:, ks]] + [r[:, ks] for r in k_refs], axis=0)
        s = _dot(kk, q2) + mask
        s_ref[slot] = s
        mx_ref[slot] = jnp.max(s, axis=0, keepdims=True)

    def consume(slot, pair):
        kv = pair // pairs_per_kv
        sk = jnp.concatenate([jnp.full((1, qn), sink_ref[2 * pair + b] * LOG2E, F32) for b in range(2)], axis=1)
        m = jnp.maximum(mx_ref[slot], sk)
        p = jnp.exp2(s_ref[slot] - m).astype(BF16)
        vt = jnp.concatenate([vct_ref[0, 0]] + [r[0, 0] for r in vt_refs], axis=1)[kv * step:(kv + 1) * step]
        acc = _dot(vt, p)
        o = acc[:D_DH] / (acc[D_DH:D_DH + 1] + jnp.exp2(sk - m))
        both = jnp.concatenate([o[:, :qn], o[:, qn:]], axis=0)
        o_ref[:, pair * HEAD_PAD:(pair + 1) * HEAD_PAD] = both.T.astype(o_ref.dtype)

    produce(0, 0)
    for pair in range(pairs):
        if pair + 1 < pairs:
            produce((pair + 1) % 2, pair + 1)
        consume(pair % 2, pair)


def _wgqa_mask(lc):
    nkeys = WG_Q + 2 * D_WINDOW
    kb = np.arange(nkeys)[:, None]
    qi = np.arange(WG_Q)[None, :]
    band = np.abs(kb - D_WINDOW - qi) <= D_WINDOW
    cases = [band & (kb >= D_WINDOW), band, band & (kb < nkeys - D_WINDOW)]
    tab = np.where(np.stack(cases), 0.0, NEG_INF).astype(np.float32)
    return jnp.asarray(np.concatenate([np.zeros((3, lc, WG_Q), np.float32), tab], axis=1))


def _window_gqa(q, k, vt, kc, vct, sink):
    t = q.shape[0]
    nb = t // WG_Q
    per_q = WG_Q // D_WINDOW
    nkb = t // D_WINDOW
    lc = kc.shape[0]
    kw = k.shape[1]
    _, _, vrows, tk = vt.shape
    per_chunk = tk // D_WINDOW
    first_blk = [lambda i, j=j: jnp.clip(i * per_q - 1 + j, 0, nkb - 1) for j in range(per_q + 2)]
    kspecs = [pl.BlockSpec((D_WINDOW, kw), lambda i, f=f: (f(i), 0)) for f in first_blk]
    vspecs = [pl.BlockSpec((1, 1, vrows, D_WINDOW), lambda i, f=f: (0, f(i) // per_chunk, 0, f(i) % per_chunk))
              for f in first_blk]
    mask = _wgqa_mask(lc)
    case = lambda i: (jnp.where(i == 0, 0, jnp.where(i == nb - 1, 2, 1)), 0, 0)
    cur = lambda i: (i, 0)
    nkeys = lc + WG_Q + 2 * D_WINDOW
    lanes = 2 * WG_Q
    return pl.pallas_call(
        _wgqa_kernel,
        grid=(nb,),
        in_specs=[pl.BlockSpec(memory_space=pltpu.SMEM), pl.BlockSpec((WG_Q, q.shape[1]), cur)] + kspecs + vspecs
                 + [_full(kc.shape), _full(vct.shape), pl.BlockSpec((1,) + mask.shape[1:], case)],
        out_specs=pl.BlockSpec((WG_Q, q.shape[1]), cur),
        out_shape=jax.ShapeDtypeStruct(q.shape, BF16),
        scratch_shapes=[pltpu.VMEM((2, nkeys, lanes), F32), pltpu.VMEM((2, 1, lanes), F32)],
        compiler_params=_cparams("parallel"),
        name="window_gqa",
    )(sink, q, *([k] * (per_q + 2)), *([vt] * (per_q + 2)), kc, vct, mask)


def _mix_ffn_kernel(o1_ref, o1p_ref, o1n_ref, o2_ref, o2p_ref, o2n_ref, x_ref, xp_ref, xn_ref,
                    wout_ref, vec_ref, wup_ref, cv_ref, wdn_ref, o_ref, acc_ref, hcat_ref, u_ref, x1_ref):
    i = pl.program_id(0)
    n = pl.num_programs(0)
    tm = x_ref.shape[0]
    rows = tm + 2 * HALO
    nchunk = wdn_ref.shape[0] // FFN_CHUNK
    half = o1_ref.shape[1]
    vec = vec_ref[...]
    o1 = jnp.concatenate([o1p_ref[...], o1_ref[...], o1n_ref[...]], axis=0)
    o2 = jnp.concatenate([o2p_ref[...], o2_ref[...], o2n_ref[...]], axis=0)
    x = jnp.concatenate([xp_ref[...], x_ref[...], xn_ref[...]], axis=0)
    x1 = x + vec[0:1] * (_dot(o1, wout_ref[:half, :]) + _dot(o2, wout_ref[half:, :]))
    x1_ref[...] = x1[HALO:HALO + tm]
    hcat_ref[...] = (_rms(x1) * vec[1:2] * (1.0 + vec[2:3]) + vec[3:4]).astype(BF16)

    @pl.when(i == 0)
    def _():
        hcat_ref[0:HALO] = jnp.zeros((HALO, hcat_ref.shape[1]), BF16)

    @pl.when(i == n - 1)
    def _():
        hcat_ref[HALO + tm:rows] = jnp.zeros((HALO, hcat_ref.shape[1]), BF16)

    acc_ref[...] = jnp.zeros(acc_ref.shape, F32)

    def conv(u, cv):
        before = pltpu.roll(u, 1, axis=0)
        after = pltpu.roll(u, rows - 1, axis=0)
        r = cv[3:4] + before * cv[0:1] + u * cv[1:2] + after * cv[2:3]
        return r[HALO:HALO + tm]

    def cols(c, gate):
        return pl.ds(pl.multiple_of((gate * nchunk + c) * FFN_CHUNK, FFN_CHUNK), FFN_CHUNK)

    def stage_up(slot, c):
        hcat = hcat_ref[...]
        u_ref[slot, 0] = _dot(hcat, wup_ref[:, cols(c, 0)])
        u_ref[slot, 1] = _dot(hcat, wup_ref[:, cols(c, 1)])

    def stage_down(slot, c):
        a = conv(u_ref[slot, 0], cv_ref[:, cols(c, 0)])
        g = conv(u_ref[slot, 1], cv_ref[:, cols(c, 1)])
        act = (a * (g * _sigmoid(g))).astype(BF16)
        acc_ref[...] += _dot(act, wdn_ref[cols(c, 0), :])

    assert nchunk % 2 == 1
    stage_up(0, 0)

    def body(j, carry):
        c = 2 * j
        stage_up(1, c + 1)
        stage_down(0, c)
        stage_up(0, c + 2)
        stage_down(1, c + 1)
        return carry

    lax.fori_loop(0, nchunk // 2, body, 0)
    stage_down(0, nchunk - 1)
    o_ref[...] = x1_ref[...] + vec[4:5] * acc_ref[...]


def _mix_ffn(o1, o2, x, wout, vec, wup, cv, wdn):
    t, d = x.shape
    tm = min(512, t)
    nt = t // tm
    hb = tm // HALO
    row = lambda i: (i, 0)
    prev = lambda i: (jnp.maximum(i * hb - 1, 0), 0)
    nxt = lambda i: (jnp.minimum((i + 1) * hb, t // HALO - 1), 0)
    tiled = lambda w: [pl.BlockSpec((tm, w), row), pl.BlockSpec((HALO, w), prev), pl.BlockSpec((HALO, w), nxt)]
    return pl.pallas_call(
        _mix_ffn_kernel,
        grid=(nt,),
        in_specs=tiled(o1.shape[1]) + tiled(o2.shape[1]) + tiled(d)
                 + [_resident(a.shape) for a in (wout, vec, wup, cv, wdn)],
        out_specs=pl.BlockSpec((tm, d), row),
        out_shape=jax.ShapeDtypeStruct((t, d), F32),
        scratch_shapes=[pltpu.VMEM((tm, d), F32), pltpu.VMEM((tm + 2 * HALO, d), BF16),
                        pltpu.VMEM((2, 2, tm + 2 * HALO, FFN_CHUNK), F32), pltpu.VMEM((tm, d), F32)],
        compiler_params=_cparams("parallel"),
        name="mix_ffn",
    )(o1, o1, o1, o2, o2, o2, x, x, x, wout, vec, wup, cv, wdn)


def _axial_tables(n_tokens, dim):
    t = jnp.arange(n_tokens, dtype=jnp.int32)
    row = (t // GRID_W).astype(F32)
    col = (t % GRID_W).astype(F32)
    quarter = dim // 4
    inv_freq = ROPE_BASE ** (-jnp.arange(quarter, dtype=F32) / quarter)
    ang = jnp.concatenate([row[:, None] * inv_freq, col[:, None] * inv_freq], axis=-1)
    return jnp.cos(ang), jnp.sin(ang)


def _rope_tables_head64(t):
    cos, sin = _axial_tables(t, 64)
    return jnp.tile(jnp.concatenate([cos, cos], -1), (1, 2)), jnp.tile(jnp.concatenate([-sin, sin], -1), (1, 2))


def _rope_tables_latent(t):
    cos, sin = _axial_tables(t, C_ROPE)
    ones = jnp.ones((t, C_NOPE), F32)
    tail = jnp.ones((t, HEAD_PAD - C_NOPE - C_ROPE), F32)
    cosc = jnp.concatenate([ones, cos, cos, tail], -1)
    sinc = jnp.concatenate([0 * ones, -sin, sin, 0 * tail], -1)
    return cosc, sinc


def _group_ones(width, group):
    return jnp.asarray(np.kron(np.eye(width // group), np.ones((group, group))), BF16)


def _pad_heads(a, heads, dim):
    a = a.reshape(a.shape[:-1] + (heads, dim))
    a = jnp.pad(a, [(0, 0)] * (a.ndim - 1) + [(0, HEAD_PAD - dim)])
    return a.reshape(a.shape[:-2] + (heads * HEAD_PAD,))


def _logit_bound(gq, gk, dim):
    return 1.02 * LOG2E * math.sqrt(dim) * jnp.max(jnp.abs(gq)) * jnp.max(jnp.abs(gk))


def _pad_row(v, width):
    return jnp.pad(v, (0, width - v.shape[0]))


def _ffn_weights(w_up, conv_w, conv_b, w_down):
    cv = jnp.concatenate([conv_w, conv_b[None]], axis=0)
    return w_up.astype(BF16), cv, w_down.astype(BF16)


def kernel(x, c, ctx, c_ctx, ada_w, ada_b, norm_mix, norm_ffn, ffn_up, ffn_conv_w, ffn_conv_b, ffn_down, ev_w_in, ev_w_out, a_q_norm, a_k_norm, a_lam_q1, a_lam_k1, a_lam_q2, a_lam_k2, a_subln, b_q_norm, b_k_norm, b_rpb, od_w_in, od_w_out, c_q_a_norm, c_w_qb, c_kv_a_norm, c_w_kvb, c_q_norm, c_k_norm, d_q_norm, d_k_norm, d_sink):
    assert x.shape[0] == 1 and ctx.shape[0] == 1
    xm = x[0]
    xc = ctx[0]
    t, d = xm.shape
    lc = xc.shape[0]
    mods = _modulation(c, c_ctx, ada_w, ada_b)
    cos64, sin64 = _rope_tables_head64(t)
    cosc, sinc = _rope_tables_latent(t)
    g64 = _group_ones(MXU_TILE, 64)
    g128 = _group_ones(MXU_TILE, HEAD_PAD)
    dummy_tab = jnp.zeros((lc, LANES), F32)

    for l in range(DEPTH):
        last = l == DEPTH - 1
        i = l // 2
        mm = mods[l, 0].reshape(6, d)
        mc = mods[l, 1].reshape(6, d)
        pre_m = jnp.stack([norm_mix[l], mm[1], mm[0]])
        pre_c = jnp.stack([norm_mix[l], mc[1], mc[0]])
        post_m = jnp.stack([mm[2], norm_ffn[l], mm[4], mm[3], mm[5]])
        post_c = jnp.stack([mc[2], norm_ffn[l], mc[4], mc[3], mc[5]])
        wup, cv, wdn = _ffn_weights(ffn_up[l], ffn_conv_w[l], ffn_conv_b[l], ffn_down[l])
        if l % 2 == 0:
            lam_init = 0.8 - 0.6 * math.exp(-0.3 * l)
            w_in = ev_w_in[i].astype(BF16)
            w_out = ev_w_out[i].astype(BF16)
            hg = jnp.stack([jnp.tile(a_q_norm[i], 8), jnp.tile(a_k_norm[i], 8),
                            jnp.tile(b_q_norm[i], 8), jnp.tile(b_k_norm[i], 8)])
            ex = jnp.stack([_pad_row(a_lam_q1[i], LANES), _pad_row(a_lam_k1[i], LANES),
                            _pad_row(a_lam_q2[i], LANES), _pad_row(a_lam_k2[i], LANES)])
            gcol = a_subln[i][:, None]
            qa, ka, va, qb, kb, vb = _premix_even(xm, pre_m, w_in, g64, hg, cos64, sin64, True)
            qa_c, ka_c, va_c, qb_c, kb_c, vb_c = _premix_even(xc, pre_c, w_in, g64, hg, dummy_tab, dummy_tab, False)
            oa = _flash(qa, ka, va, ka_c, va_c, ex, gcol, _logit_bound(a_q_norm[i], a_k_norm[i], A_DQK),
                        mode="diff", lam_init=lam_init)
            ob = _neighbourhood(qb, kb, vb, kb_c, vb_c, _na_bias_table(b_rpb[i], t // GRID_W))
            if not last:
                oa_c = _flash(qa_c, ka_c, va_c, None, None, ex, gcol, mode="diff", lam_init=lam_init)
                ob_c = _flash(qb_c, kb_c, vb_c, None, None, ex, gcol, mode="pair")
        else:
            w = od_w_in[i]
            cq, ckv, ckr, dq, dk, dv = jnp.split(w, np.cumsum([C_Q_LORA, C_KV_LORA, C_ROPE, 512, 128]).tolist(), axis=1)
            dup = lambda a: jnp.concatenate([a[:, :64], a[:, :64], a[:, 64:], a[:, 64:]], axis=1)
            ckr_pad = jnp.pad(ckr, ((0, 0), (C_NOPE, HEAD_PAD - C_NOPE - C_ROPE)))
            w_in = jnp.concatenate([cq, ckv, dq, dup(dk), dv, ckr_pad], axis=1).astype(BF16)
            w_out = od_w_out[i].astype(BF16)
            wqb = _pad_heads(c_w_qb[i], C_HEADS, C_NOPE + C_ROPE).astype(BF16)
            kvb = c_w_kvb[i].reshape(C_KV_LORA, C_HEADS, C_NOPE + C_DV)
            wk = _pad_heads(kvb[:, :, :C_NOPE].reshape(C_KV_LORA, -1), C_HEADS, C_NOPE).astype(BF16)
            wv = kvb[:, :, C_NOPE:].reshape(C_KV_LORA, C_HEADS * C_DV).astype(BF16)
            gains = jnp.stack([jnp.tile(_pad_row(c_q_norm[i], HEAD_PAD), C_HEADS),
                               jnp.tile(_pad_row(c_k_norm[i], HEAD_PAD), C_HEADS),
                               _pad_row(jnp.tile(d_q_norm[i], 8), 1024), _pad_row(jnp.tile(d_k_norm[i], 4), 1024)])
            lora = jnp.stack([_pad_row(c_q_a_norm[i], 512), _pad_row(c_kv_a_norm[i], 512)])
            odd = functools.partial(_premix_odd, w=w_in, wqb=wqb, wk=wk, wv=wv, g128=g128, g64=g64, gains=gains, lora=lora)
            mq, mk, mv, gq, gk, gv = odd(xm, pre_m, cos=cos64, sin=sin64, cosc=cosc, sinc=sinc, use_rope=True)
            mq_c, mk_c, mv_c, gq_c, gk_c, gv_c = odd(xc, pre_c, cos=dummy_tab, sin=dummy_tab, cosc=dummy_tab,
                                                     sinc=dummy_tab, use_rope=False)
            ex = jnp.zeros((4, LANES), F32)
            oa = _flash(mq, mk, mv, mk_c, mv_c, ex, jnp.ones((HEAD_PAD, 1), F32),
                        _logit_bound(c_q_norm[i], c_k_norm[i], C_NOPE + C_ROPE), mode="mla")
            ob = _window_gqa(gq, gk, gv, gk_c, gv_c, d_sink[i])
            if not last:
                raise NotImplementedError("context update after an odd layer is not needed at this depth")
        xm = _mix_ffn(oa, ob, xm, w_out, post_m, wup, cv, wdn)
        if not last:
            xc = _mix_ffn(oa_c, ob_c, xc, w_out, post_c, wup, cv, wdn)
    return xm[None]
```

```python
import functools
import math

import numpy as np
import jax
import jax.numpy as jnp
from jax import lax
from jax.experimental import pallas as pl
from jax.experimental.pallas import tpu as pltpu

DEPTH = 2
GRID_W = 64
EPS = 1e-6
ROPE_BASE = 10000.0
NEG_INF = -1e30
LOG2E = math.log2(math.e)

A_DQK = 64
A_DV = 128
B_DH = 64
NA_WIN_H = 8
NA_WIN_W = 16
C_HEADS = 8
C_Q_LORA = 384
C_KV_LORA = 256
C_NOPE = 64
C_ROPE = 32
C_DV = 64
D_HEADS = 8
D_KV_HEADS = 2
D_DH = 64
D_WINDOW = 128

LANES = 128
MXU_TILE = 256
HEAD_PAD = 128
VMEM_LIMIT = 56 * 1024 * 1024
FFN_CHUNK = 256
HALO = 16
NA_ROWS = 4
WG_Q = 256
ONES_ROWS = 16
FLASH_TQ = 4096
FLASH_TQ_ONLINE = 1024
FLASH_TK = 512
FLASH_UNROLL = 4
FLASH_SAFE_BOUND = 40.0

F32 = jnp.float32
BF16 = jnp.bfloat16


def _cparams(*sem):
    return pltpu.CompilerParams(dimension_semantics=sem, vmem_limit_bytes=VMEM_LIMIT)


def _full(shape):
    n = len(shape)
    return pl.BlockSpec(shape, lambda *_: (0,) * n)


def _resident(shape):
    n = len(shape)
    return pl.BlockSpec(shape, lambda *_: (0,) * n, pipeline_mode=pl.Buffered(1))


def _dot(a, b):
    return jnp.dot(a, b, preferred_element_type=F32)


def _rms(x):
    return x * lax.rsqrt(jnp.mean(x * x, axis=-1, keepdims=True) + EPS)


def _sigmoid(x):
    return 1.0 / (1.0 + jnp.exp(-x))


def _group_rms(y, gmat, inv_n, gain):
    yy = (y * y).astype(BF16)
    slab = gmat.shape[0]
    ss = jnp.concatenate([_dot(yy[:, c:c + slab], gmat) for c in range(0, y.shape[-1], slab)], axis=-1)
    return y * lax.rsqrt(ss * inv_n + EPS) * gain


def _rope(y, cos, sin, half, first_half):
    w = y.shape[-1]
    fwd = pltpu.roll(y, w - half, axis=1)
    bwd = pltpu.roll(y, half, axis=1)
    return y * cos + jnp.where(first_half, fwd, bwd) * sin


def _lane_tile(t, width):
    reps = width // t.shape[-1]
    return t if reps == 1 else jnp.concatenate([t] * reps, axis=-1)


def _lane_iota(width):
    return lax.broadcasted_iota(jnp.int32, (1, width), 1)


def _mod_kernel(ct_ref, w_ref, b_ref, o_ref):
    ct = ct_ref[...]
    s = ct * _sigmoid(ct)
    w = w_ref[0]
    rows = [jnp.sum(w * s[:, j:j + 1], axis=0, keepdims=True) for j in range(2)]
    o_ref[0] = jnp.concatenate(rows, axis=0) + b_ref[0]


def _modulation(c, c_ctx, ada_w, ada_b):
    depth, d, n = ada_w.shape
    tn = 1536
    ct = jnp.stack([c[0], c_ctx], axis=1)
    return pl.pallas_call(
        _mod_kernel,
        grid=(depth, n // tn),
        in_specs=[_full((d, 2)),
                  pl.BlockSpec((1, d, tn), lambda l, j: (l, 0, j)),
                  pl.BlockSpec((1, 1, tn), lambda l, j: (l, 0, j))],
        out_specs=pl.BlockSpec((1, 2, tn), lambda l, j: (l, 0, j)),
        out_shape=jax.ShapeDtypeStruct((depth, 2, n), F32),
        compiler_params=_cparams("arbitrary", "arbitrary"),
        name="modulation",
    )(ct, ada_w, ada_b.reshape(depth, 1, n))


def _store_values_t(vt_ref, v, dv):
    tm = v.shape[0]
    vt = v.T.astype(BF16)
    ones = jnp.ones((ONES_ROWS, tm), BF16)
    step = dv + ONES_ROWS
    for g in range(vt_ref.shape[0]):
        for h in range(HEAD_PAD // dv):
            src = g * HEAD_PAD + h * dv
            vt_ref[g, 0, h * step:h * step + dv, :] = vt[src:src + dv]
            vt_ref[g, 0, h * step + dv:(h + 1) * step, :] = ones


def _values_t_shape(t, tm, dv, groups=4):
    return (groups, t // tm, (HEAD_PAD // dv) * (dv + ONES_ROWS), tm)


def _premix_even_kernel(x_ref, vec_ref, w_ref, g_ref, hg_ref, cos_ref, sin_ref,
                        qa_ref, ka_ref, vat_ref, qb_ref, kb_ref, vbt_ref, *, use_rope):
    vec = vec_ref[...]
    h = (_rms(x_ref[...]) * vec[0:1] * (1.0 + vec[1:2]) + vec[2:3]).astype(BF16)
    gm = g_ref[...]
    hg = hg_ref[...]
    sw = 512
    if use_rope:
        cos = _lane_tile(cos_ref[...], sw)
        sin = _lane_tile(sin_ref[...], sw)
        first = (_lane_iota(sw) & 32) == 0

    def seg(i):
        return _dot(h, w_ref[:, i * sw:(i + 1) * sw])

    qa = _group_rms(seg(0), gm, 1.0 / A_DQK, hg[0:1])
    ka = _group_rms(seg(1), gm, 1.0 / A_DQK, hg[1:2])
    if use_rope:
        qa = _rope(qa, cos, sin, 32, first)
        ka = _rope(ka, cos, sin, 32, first)
    qa_ref[...] = (qa * (LOG2E / math.sqrt(A_DQK))).astype(BF16)
    ka_ref[...] = ka.astype(BF16)
    _store_values_t(vat_ref, seg(2), A_DV)
    qb = _group_rms(seg(3), gm, 1.0 / B_DH, hg[2:3])
    qb_ref[...] = (qb * (LOG2E / math.sqrt(B_DH))).astype(BF16)
    kb_ref[...] = _group_rms(seg(4), gm, 1.0 / B_DH, hg[3:4]).astype(BF16)
    _store_values_t(vbt_ref, seg(5), B_DH)


def _premix_even(x, vec, w, gmat, hg, cos, sin, use_rope):
    t, d = x.shape
    tm = min(FLASH_TK, t)
    row = lambda i: (i, 0)
    nat = jax.ShapeDtypeStruct((t, 512), BF16)
    nat_spec = pl.BlockSpec((tm, 512), row)
    vta = _values_t_shape(t, tm, A_DV)
    vtb = _values_t_shape(t, tm, B_DH)
    vt_spec = lambda s: pl.BlockSpec((s[0], 1) + s[2:], lambda i: (0, i, 0, 0))
    return pl.pallas_call(
        functools.partial(_premix_even_kernel, use_rope=use_rope),
        grid=(t // tm,),
        in_specs=[pl.BlockSpec((tm, d), row), _full(vec.shape), _full(w.shape), _full(gmat.shape),
                  _full(hg.shape), pl.BlockSpec((tm, LANES), row), pl.BlockSpec((tm, LANES), row)],
        out_specs=[nat_spec, nat_spec, vt_spec(vta), nat_spec, nat_spec, vt_spec(vtb)],
        out_shape=[nat, nat, jax.ShapeDtypeStruct(vta, BF16), nat, nat, jax.ShapeDtypeStruct(vtb, BF16)],
        compiler_params=_cparams("parallel"),
        name="premix_even",
    )(x, vec, w, gmat, hg, cos, sin)


_O_CQ, _O_CKV, _O_DQ, _O_DK, _O_DV, _O_CKR, _O_END = 0, 384, 640, 1152, 1408, 1536, 1664


def _premix_odd_kernel(x_ref, vec_ref, w_ref, wqb_ref, wk_ref, wv_ref, g128_ref, g64_ref, gain_ref, lora_ref,
                       cos_ref, sin_ref, cosc_ref, sinc_ref,
                       mq_ref, mk_ref, mvt_ref, gq_ref, gk_ref, gvt_ref, *, use_rope):
    vec = vec_ref[...]
    h = (_rms(x_ref[...]) * vec[0:1] * (1.0 + vec[1:2]) + vec[2:3]).astype(BF16)
    gains = gain_ref[...]
    lora = lora_ref[...]
    g128 = g128_ref[...]
    g64 = g64_ref[...]
    mw = C_HEADS * HEAD_PAD
    if use_rope:
        cosc = _lane_tile(cosc_ref[...], mw)
        sinc = _lane_tile(sinc_ref[...], mw)
        first_c = (_lane_iota(mw) & 127) < (C_NOPE + C_ROPE // 2)
        cos = _lane_tile(cos_ref[...], 512)
        sin = _lane_tile(sin_ref[...], 512)
        first_d = (_lane_iota(512) & 32) == 0

    def seg(a, b):
        return _dot(h, w_ref[:, a:b])

    inv_c = 1.0 / (C_NOPE + C_ROPE)
    cq = (_rms(seg(_O_CQ, _O_CKV)) * lora[0:1, :C_Q_LORA]).astype(BF16)
    mq = _group_rms(_dot(cq, wqb_ref[...]), g128, inv_c, gains[0:1])
    if use_rope:
        mq = _rope(mq, cosc, sinc, C_ROPE // 2, first_c)
    mq_ref[...] = (mq * (LOG2E * math.sqrt(inv_c))).astype(BF16)
    ckv = (_rms(seg(_O_CKV, _O_DQ)) * lora[1:2, :C_KV_LORA]).astype(BF16)
    mk = _dot(ckv, wk_ref[...]) + _lane_tile(seg(_O_CKR, _O_END), mw)
    mk = _group_rms(mk, g128, inv_c, gains[1:2])
    if use_rope:
        mk = _rope(mk, cosc, sinc, C_ROPE // 2, first_c)
    mk_ref[...] = mk.astype(BF16)
    _store_values_t(mvt_ref, _dot(ckv, wv_ref[...]), C_DV)
    gq = _group_rms(seg(_O_DQ, _O_DK), g64, 1.0 / D_DH, gains[2:3, :512])
    gk = _group_rms(seg(_O_DK, _O_DV), g64, 1.0 / D_DH, gains[3:4, :256])
    if use_rope:
        gq = _rope(gq, cos, sin, 32, first_d)
        gk = _rope(gk, cos[:, :256], sin[:, :256], 32, first_d[:, :256])
    gq_ref[...] = (gq * (LOG2E / math.sqrt(D_DH))).astype(BF16)
    gk_ref[...] = gk.astype(BF16)
    _store_values_t(gvt_ref, seg(_O_DV, _O_CKR), D_DH)


def _premix_odd(x, vec, w, wqb, wk, wv, g128, g64, gains, lora, cos, sin, cosc, sinc, use_rope):
    t, d = x.shape
    tm = min(FLASH_TK, t)
    row = lambda i: (i, 0)
    tab = pl.BlockSpec((tm, LANES), row)
    nat = lambda n: (pl.BlockSpec((tm, n), row), jax.ShapeDtypeStruct((t, n), BF16))
    def vt(shape):
        return pl.BlockSpec((shape[0], 1) + shape[2:], lambda i: (0, i, 0, 0)), jax.ShapeDtypeStruct(shape, BF16)

    outs = [nat(1024), nat(1024), vt(_values_t_shape(t, tm, C_DV)), nat(512), nat(256),
            vt(_values_t_shape(t, tm, D_DH, groups=1))]
    return pl.pallas_call(
        functools.partial(_premix_odd_kernel, use_rope=use_rope),
        grid=(t // tm,),
        in_specs=[pl.BlockSpec((tm, d), row)] + [_full(a.shape) for a in (vec, w, wqb, wk, wv, g128, g64, gains, lora)]
                 + [tab] * 4,
        out_specs=[o[0] for o in outs],
        out_shape=[o[1] for o in outs],
        compiler_params=_cparams("parallel"),
        name="premix_odd",
    )(x, vec, w, wqb, wk, wv, g128, g64, gains, lora, cos, sin, cosc, sinc)


def _flash_kernel(*refs, mode, has_ctx, online, lam_init):
    bound_ref, q_ref, k_ref, vt_ref = refs[:4]
    kc_ref, vct_ref = refs[4:6] if has_ctx else (None, None)
    ex_ref, gcol_ref, o_ref, acc_ref, stage_ref = refs[6:11] if has_ctx else refs[4:9]
    qt = q_ref[...].astype(F32).T.astype(BF16)
    row = lax.broadcasted_iota(jnp.int32, (HEAD_PAD, 1), 0)
    if mode == "mla":
        qs = [qt[:HEAD_PAD], qt[HEAD_PAD:]]
    else:
        zero = jnp.zeros_like(qt)
        qs = [jnp.where(row < 64, qt, zero), jnp.where(row >= 64, qt, zero)]
    nk, tk = vt_ref.shape[1], vt_ref.shape[3]

    acc_ref[...] = jnp.zeros(acc_ref.shape, F32)
    rows_b = acc_ref.shape[1]

    def scores(kblk, b):
        kb = kblk[:, b * HEAD_PAD:(b + 1) * HEAD_PAD] if mode == "mla" else kblk
        return _dot(kb, qs[b])

    def values(vtblk, b):
        return vtblk if vtblk.shape[0] == rows_b else vtblk[b * rows_b:(b + 1) * rows_b]

    if online:
        m_ref, mx_ref = refs[-2:]
        m_ref[...] = jnp.full(m_ref.shape, NEG_INF, F32)

        def absorb(s, smax, vtblk, b):
            m_prev = m_ref[b]
            m_new = jnp.maximum(m_prev, smax)
            alpha = jnp.exp2(m_prev - m_new)
            p = jnp.exp2(s - m_new).astype(BF16)
            acc_ref[b] = alpha * acc_ref[b] + _dot(values(vtblk, b), p)
            m_ref[b] = m_new

        def produce(slot, kblk):
            for b in range(2):
                s = scores(kblk, b)
                stage_ref[slot, b] = s
                mx_ref[slot, b] = jnp.max(s, axis=0, keepdims=True)

        def consume(slot, vtblk):
            for b in range(2):
                absorb(stage_ref[slot, b], mx_ref[slot, b], vtblk, b)

        def direct(kblk, vtblk):
            for b in range(2):
                s = scores(kblk, b)
                absorb(s, jnp.max(s, axis=0, keepdims=True), vtblk, b)
    else:
        bound = bound_ref[0]

        def probs(kblk, b):
            return jnp.exp2(scores(kblk, b) - bound).astype(BF16)

        def produce(slot, kblk):
            for b in range(2):
                stage_ref[slot, b] = probs(kblk, b)

        def consume(slot, vtblk):
            for b in range(2):
                acc_ref[b] += _dot(values(vtblk, b), stage_ref[slot, b])

        def direct(kblk, vtblk):
            for b in range(2):
                acc_ref[b] += _dot(values(vtblk, b), probs(kblk, b))

    if nk == 1:
        if has_ctx:
            direct(kc_ref[...], vct_ref[0])
        direct(k_ref[...], vt_ref[0, 0])
    else:
        unroll = FLASH_UNROLL if nk % FLASH_UNROLL == 0 else 2
        assert nk % unroll == 0
        produce(0, k_ref[0:tk, :])
        if has_ctx:
            direct(kc_ref[...], vct_ref[0])

        def body(i, carry):
            c0 = unroll * i
            for u in range(unroll):
                nxt = jnp.minimum(c0 + u + 1, nk - 1)
                produce((u + 1) % 2, k_ref[pl.ds(pl.multiple_of(nxt * tk, tk), tk), :])
                consume(u % 2, vt_ref[0, c0 + u])
            return carry

        lax.fori_loop(0, nk // unroll, body, 0)

    dv = rows_b - ONES_ROWS
    o0 = acc_ref[0, :dv] / acc_ref[0, dv:dv + 1]
    o1 = acc_ref[1, :dv] / acc_ref[1, dv:dv + 1]
    if mode == "diff":
        ex = ex_ref[...]
        lam = (jnp.exp(jnp.sum(ex[0:1] * ex[1:2], axis=-1, keepdims=True))
               - jnp.exp(jnp.sum(ex[2:3] * ex[3:4], axis=-1, keepdims=True)) + lam_init)
        o = o0 - lam * o1
        o = o * lax.rsqrt(jnp.mean(o * o, axis=0, keepdims=True) + EPS) * gcol_ref[...] * (1.0 - lam_init)
    else:
        o = jnp.concatenate([o0, o1], axis=0)
    o_ref[...] = o.T.astype(o_ref.dtype)


def _flash(q, k, vt, kc, vct, ex, gcol, bound=None, *, mode, lam_init=0.0):
    t = q.shape[0]
    tkeys = k.shape[0]
    qw = 2 * HEAD_PAD if mode == "mla" else HEAD_PAD
    groups = q.shape[1] // qw
    _, nk, vrows, tk = vt.shape
    assert nk * tk == tkeys
    has_ctx = kc is not None
    dv = HEAD_PAD if mode == "diff" else HEAD_PAD // 2
    rows_b = dv + ONES_ROWS
    bound_arr = jnp.zeros((1,), F32) if bound is None else jnp.reshape(bound, (1,)).astype(F32)
    args = [bound_arr, q, k, vt]
    if has_ctx:
        lc = kc.shape[0]
        args += [kc, vct.reshape(groups, vrows, lc)]
    args += [ex, gcol]

    def call(online):
        tq = min(FLASH_TQ_ONLINE if online else FLASH_TQ, t)
        in_specs = [pl.BlockSpec(memory_space=pltpu.SMEM),
                    pl.BlockSpec((tq, qw), lambda g, i: (i, g)),
                    pl.BlockSpec((tkeys, qw), lambda g, i: (0, g), pipeline_mode=pl.Buffered(1)),
                    pl.BlockSpec((1, nk, vrows, tk), lambda g, i: (g, 0, 0, 0), pipeline_mode=pl.Buffered(1))]
        if has_ctx:
            in_specs += [pl.BlockSpec((lc, qw), lambda g, i: (0, g)),
                         pl.BlockSpec((1, vrows, lc), lambda g, i: (g, 0, 0))]
        in_specs += [_full(ex.shape), _full(gcol.shape)]
        acc = pltpu.VMEM((2, rows_b, tq), F32)
        if online:
            scratch = [acc, pltpu.VMEM((2, 2, tk, tq), F32), pltpu.VMEM((2, 1, tq), F32), pltpu.VMEM((2, 2, 1, tq), F32)]
        else:
            scratch = [acc, pltpu.VMEM((2, 2, tk, tq), BF16)]
        return pl.pallas_call(
            functools.partial(_flash_kernel, mode=mode, has_ctx=has_ctx, online=online, lam_init=lam_init),
            grid=(groups, t // tq),
            in_specs=in_specs,
            out_specs=pl.BlockSpec((tq, HEAD_PAD), lambda g, i: (i, g)),
            out_shape=jax.ShapeDtypeStruct((t, groups * HEAD_PAD), BF16),
            scratch_shapes=scratch,
            compiler_params=_cparams("parallel", "parallel"),
            name="flash_" + mode + ("_online" if online else "_bounded"),
        )(*args)

    if bound is None:
        return call(True)
    return lax.cond(bound <= FLASH_SAFE_BOUND, lambda: call(False), lambda: call(True))


def _na_kernel(q_ref, kp_ref, kcur_ref, kn_ref, vtp_ref, vtcur_ref, vtn_ref, kc_ref, vct_ref, bias_ref, o_ref,
               s_ref, mx_ref):
    heads = q_ref.shape[1] // B_DH
    lc = kc_ref.shape[0]
    step = B_DH + ONES_ROWS
    row = lax.broadcasted_iota(jnp.int32, (HEAD_PAD, 1), 0)

    def pair_cols(h):
        return slice((h // 2) * HEAD_PAD, (h // 2 + 1) * HEAD_PAD)

    def produce(slot, h):
        ps = pair_cols(h)
        qt = q_ref[:, ps].astype(F32).T.astype(BF16)
        qb = jnp.where((row < 64) if h % 2 == 0 else (row >= 64), qt, jnp.zeros_like(qt))
        k_nb = jnp.concatenate([kp_ref[:, ps], kcur_ref[:, ps], kn_ref[:, ps]], axis=0)
        s_ctx = _dot(kc_ref[:, ps], qb)
        s_nb = _dot(k_nb, qb) + bias_ref[0, h]
        s_ref[slot, 0:lc] = s_ctx
        s_ref[slot, lc:] = s_nb
        mx_ref[slot] = jnp.maximum(jnp.max(s_ctx, axis=0, keepdims=True), jnp.max(s_nb, axis=0, keepdims=True))

    def consume(slot, h):
        pr, b = h // 2, h % 2
        vt = jnp.concatenate([vct_ref[pr, 0], vtp_ref[pr, 0], vtcur_ref[pr, 0], vtn_ref[pr, 0]], axis=1)
        p = jnp.exp2(s_ref[slot] - mx_ref[slot]).astype(BF16)
        acc = _dot(vt[b * step:(b + 1) * step], p)
        return acc[:B_DH] / acc[B_DH:B_DH + 1]

    produce(0, 0)
    outs = []
    for h in range(heads):
        if h + 1 < heads:
            produce((h + 1) % 2, h + 1)
        outs.append(consume(h % 2, h))
        if h % 2:
            o_ref[:, pair_cols(h)] = jnp.concatenate(outs[-2:], axis=0).T.astype(o_ref.dtype)


def _na_bias_table(rpb, rows):
    nr, w = NA_ROWS, GRID_W
    heads = rpb.shape[0]
    pad = w - NA_WIN_W
    padded = jnp.pad(rpb.astype(F32) * LOG2E, ((0, 0), (0, 0), (pad, pad + 1)))
    skew = jnp.tile(padded, (1, 1, w))[:, :, :w * (2 * w - 1)].reshape(heads, -1, w, 2 * w - 1)
    toep = skew[:, :, :, w - 1:]
    cq = np.arange(w)
    cs = np.clip(cq - NA_WIN_W // 2, 0, w - NA_WIN_W)
    valid_c = (cq[None, :] >= cs[:, None]) & (cq[None, :] < cs[:, None] + NA_WIN_W)
    toep_t = jnp.swapaxes(jnp.where(jnp.asarray(valid_c), toep, NEG_INF), -1, -2)
    d0 = NA_WIN_H - 1 - nr
    assert d0 - (nr - 1) >= 0 and d0 + 3 * nr <= 2 * NA_WIN_H - 1
    dense = jnp.concatenate([toep_t[:, d0 - rl:d0 - rl + 3 * nr] for rl in range(nr)], axis=-1)
    wh = min(NA_WIN_H, rows)
    rl = np.arange(nr)
    rr_rel = np.arange(3 * nr)
    valid_all = []
    for base in (0, nr, rows - nr):
        rs = np.clip(base + rl - wh // 2, 0, rows - wh)
        rr = base - nr + rr_rel
        valid_all.append((rr[:, None] >= rs[None, :]) & (rr[:, None] < rs[None, :] + wh))
    valid_r = np.repeat(np.stack(valid_all), w, axis=-1)[:, None, :, None, :]
    tab = jnp.where(jnp.asarray(valid_r), dense[None], NEG_INF)
    return tab.reshape(3, heads, 3 * nr * w, nr * w)


def _neighbourhood(q, k, vt, kc, vct, bias):
    t = q.shape[0]
    qn = NA_ROWS * GRID_W
    nb = t // qn
    lc = kc.shape[0]
    pairs, _, vrows, tk = vt.shape
    per = tk // qn
    cur = lambda i: (i, 0)
    prev = lambda i: (jnp.maximum(i - 1, 0), 0)
    nxt = lambda i: (jnp.minimum(i + 1, nb - 1), 0)
    blk = lambda f: pl.BlockSpec((qn, q.shape[1]), f)
    vblk = lambda f: pl.BlockSpec((pairs, 1, vrows, qn), lambda i: (0, f(i)[0] // per, 0, f(i)[0] % per))
    case = lambda i: (jnp.where(i == 0, 0, jnp.where(i == nb - 1, 2, 1)), 0, 0, 0)
    return pl.pallas_call(
        _na_kernel,
        grid=(nb,),
        in_specs=[blk(cur), blk(prev), blk(cur), blk(nxt), vblk(prev), vblk(cur), vblk(nxt),
                  _full(kc.shape), _full(vct.shape), pl.BlockSpec((1,) + bias.shape[1:], case)],
        out_specs=blk(cur),
        out_shape=jax.ShapeDtypeStruct(q.shape, BF16),
        scratch_shapes=[pltpu.VMEM((2, lc + 3 * qn, qn), F32), pltpu.VMEM((2, 1, qn), F32)],
        compiler_params=_cparams("arbitrary"),
        name="neighbourhood",
    )(q, k, k, k, vt, vt, vt, kc, vct, bias)


def _wgqa_kernel(sink_ref, q_ref, *refs):
    nkb = WG_Q // D_WINDOW + 2
    k_refs, vt_refs = refs[:nkb], refs[nkb:2 * nkb]
    kc_ref, vct_ref, mask_ref, o_ref, s_ref, mx_ref = refs[2 * nkb:]
    qn = q_ref.shape[0]
    pairs = D_HEADS // 2
    pairs_per_kv = pairs // D_KV_HEADS
    step = D_DH + ONES_ROWS
    row = lax.broadcasted_iota(jnp.int32, (HEAD_PAD, 1), 0)
    mask = jnp.concatenate([mask_ref[0]] * 2, axis=1)

    def produce(slot, pair):
        kv = pair // pairs_per_kv
        ks = slice(kv * HEAD_PAD, (kv + 1) * HEAD_PAD)
        qt = q_ref[:, pair * HEAD_PAD:(pair + 1) * HEAD_PAD].astype(F32).T.astype(BF16)
        zero = jnp.zeros_like(qt)
        q2 = jnp.concatenate([jnp.where(row < 64, qt, zero), jnp.where(row >= 64, qt, zero)], axis=1)
        kk = jnp.concatenate([kc_ref[:, ks]] + [r[:, ks] for r in k_refs], axis=0)
        s = _dot(kk, q2) + mask
        s_ref[slot] = s
        mx_ref[slot] = jnp.max(s, axis=0, keepdims=True)

    def consume(slot, pair):
        kv = pair // pairs_per_kv
        sk = jnp.concatenate([jnp.full((1, qn), sink_ref[2 * pair + b] * LOG2E, F32) for b in range(2)], axis=1)
        m = jnp.maximum(mx_ref[slot], sk)
        p = jnp.exp2(s_ref[slot] - m).astype(BF16)
        vt = jnp.concatenate([vct_ref[0, 0]] + [r[0, 0] for r in vt_refs], axis=1)[kv * step:(kv + 1) * step]
        acc = _dot(vt, p)
        o = acc[:D_DH] / (acc[D_DH:D_DH + 1] + jnp.exp2(sk - m))
        both = jnp.concatenate([o[:, :qn], o[:, qn:]], axis=0)
        o_ref[:, pair * HEAD_PAD:(pair + 1) * HEAD_PAD] = both.T.astype(o_ref.dtype)

    produce(0, 0)
    for pair in range(pairs):
        if pair + 1 < pairs:
            produce((pair + 1) % 2, pair + 1)
        consume(pair % 2, pair)


def _wgqa_mask(lc):
    nkeys = WG_Q + 2 * D_WINDOW
    kb = np.arange(nkeys)[:, None]
    qi = np.arange(WG_Q)[None, :]
    band = np.abs(kb - D_WINDOW - qi) <= D_WINDOW
    cases = [band & (kb >= D_WINDOW), band, band & (kb < nkeys - D_WINDOW)]
    tab = np.where(np.stack(cases), 0.0, NEG_INF).astype(np.float32)
    return jnp.asarray(np.concatenate([np.zeros((3, lc, WG_Q), np.float32), tab], axis=1))


def _window_gqa(q, k, vt, kc, vct, sink):
    t = q.shape[0]
    nb = t // WG_Q
    per_q = WG_Q // D_WINDOW
    nkb = t // D_WINDOW
    lc = kc.shape[0]
    kw = k.shape[1]
    _, _, vrows, tk = vt.shape
    per_chunk = tk // D_WINDOW
    first_blk = [lambda i, j=j: jnp.clip(i * per_q - 1 + j, 0, nkb - 1) for j in range(per_q + 2)]
    kspecs = [pl.BlockSpec((D_WINDOW, kw), lambda i, f=f: (f(i), 0)) for f in first_blk]
    vspecs = [pl.BlockSpec((1, 1, vrows, D_WINDOW), lambda i, f=f: (0, f(i) // per_chunk, 0, f(i) % per_chunk))
              for f in first_blk]
    mask = _wgqa_mask(lc)
    case = lambda i: (jnp.where(i == 0, 0, jnp.where(i == nb - 1, 2, 1)), 0, 0)
    cur = lambda i: (i, 0)
    nkeys = lc + WG_Q + 2 * D_WINDOW
    lanes = 2 * WG_Q
    return pl.pallas_call(
        _wgqa_kernel,
        grid=(nb,),
        in_specs=[pl.BlockSpec(memory_space=pltpu.SMEM), pl.BlockSpec((WG_Q, q.shape[1]), cur)] + kspecs + vspecs
                 + [_full(kc.shape), _full(vct.shape), pl.BlockSpec((1,) + mask.shape[1:], case)],
        out_specs=pl.BlockSpec((WG_Q, q.shape[1]), cur),
        out_shape=jax.ShapeDtypeStruct(q.shape, BF16),
        scratch_shapes=[pltpu.VMEM((2, nkeys, lanes), F32), pltpu.VMEM((2, 1, lanes), F32)],
        compiler_params=_cparams("parallel"),
        name="window_gqa",
    )(sink, q, *([k] * (per_q + 2)), *([vt] * (per_q + 2)), kc, vct, mask)


def _mix_ffn_kernel(o1_ref, o1p_ref, o1n_ref, o2_ref, o2p_ref, o2n_ref, x_ref, xp_ref, xn_ref,
                    wout_ref, vec_ref, wup_ref, cv_ref, wdn_ref, o_ref, acc_ref, hcat_ref, u_ref, x1_ref):
    i = pl.program_id(0)
    n = pl.num_programs(0)
    tm = x_ref.shape[0]
    rows = tm + 2 * HALO
    nchunk = wdn_ref.shape[0] // FFN_CHUNK
    half = o1_ref.shape[1]
    vec = vec_ref[...]
    o1 = jnp.concatenate([o1p_ref[...], o1_ref[...], o1n_ref[...]], axis=0)
    o2 = jnp.concatenate([o2p_ref[...], o2_ref[...], o2n_ref[...]], axis=0)
    x = jnp.concatenate([xp_ref[...], x_ref[...], xn_ref[...]], axis=0)
    x1 = x + vec[0:1] * (_dot(o1, wout_ref[:half, :]) + _dot(o2, wout_ref[half:, :]))
    x1_ref[...] = x1[HALO:HALO + tm]
    hcat_ref[...] = (_rms(x1) * vec[1:2] * (1.0 + vec[2:3]) + vec[3:4]).astype(BF16)

    @pl.when(i == 0)
    def _():
        hcat_ref[0:HALO] = jnp.zeros((HALO, hcat_ref.shape[1]), BF16)

    @pl.when(i == n - 1)
    def _():
        hcat_ref[HALO + tm:rows] = jnp.zeros((HALO, hcat_ref.shape[1]), BF16)

    acc_ref[...] = jnp.zeros(acc_ref.shape, F32)

    def conv(u, cv):
        before = pltpu.roll(u, 1, axis=0)
        after = pltpu.roll(u, rows - 1, axis=0)
        r = cv[3:4] + before * cv[0:1] + u * cv[1:2] + after * cv[2:3]
        return r[HALO:HALO + tm]

    def cols(c, gate):
        return pl.ds(pl.multiple_of((gate * nchunk + c) * FFN_CHUNK, FFN_CHUNK), FFN_CHUNK)

    def stage_up(slot, c):
        hcat = hcat_ref[...]
        u_ref[slot, 0] = _dot(hcat, wup_ref[:, cols(c, 0)])
        u_ref[slot, 1] = _dot(hcat, wup_ref[:, cols(c, 1)])

    def stage_down(slot, c):
        a = conv(u_ref[slot, 0], cv_ref[:, cols(c, 0)])
        g = conv(u_ref[slot, 1], cv_ref[:, cols(c, 1)])
        act = (a * (g * _sigmoid(g))).astype(BF16)
        acc_ref[...] += _dot(act, wdn_ref[cols(c, 0), :])

    assert nchunk % 2 == 1
    stage_up(0, 0)

    def body(j, carry):
        c = 2 * j
        stage_up(1, c + 1)
        stage_down(0, c)
        stage_up(0, c + 2)
        stage_down(1, c + 1)
        return carry

    lax.fori_loop(0, nchunk // 2, body, 0)
    stage_down(0, nchunk - 1)
    o_ref[...] = x1_ref[...] + vec[4:5] * acc_ref[...]


def _mix_ffn(o1, o2, x, wout, vec, wup, cv, wdn):
    t, d = x.shape
    tm = min(512, t)
    nt = t // tm
    hb = tm // HALO
    row = lambda i: (i, 0)
    prev = lambda i: (jnp.maximum(i * hb - 1, 0), 0)
    nxt = lambda i: (jnp.minimum((i + 1) * hb, t // HALO - 1), 0)
    tiled = lambda w: [pl.BlockSpec((tm, w), row), pl.BlockSpec((HALO, w), prev), pl.BlockSpec((HALO, w), nxt)]
    return pl.pallas_call(
        _mix_ffn_kernel,
        grid=(nt,),
        in_specs=tiled(o1.shape[1]) + tiled(o2.shape[1]) + tiled(d)
                 + [_resident(a.shape) for a in (wout, vec, wup, cv, wdn)],
        out_specs=pl.BlockSpec((tm, d), row),
        out_shape=jax.ShapeDtypeStruct((t, d), F32),
        scratch_shapes=[pltpu.VMEM((tm, d), F32), pltpu.VMEM((tm + 2 * HALO, d), BF16),
                        pltpu.VMEM((2, 2, tm + 2 * HALO, FFN_CHUNK), F32), pltpu.VMEM((tm, d), F32)],
        compiler_params=_cparams("parallel"),
        name="mix_ffn",
    )(o1, o1, o1, o2, o2, o2, x, x, x, wout, vec, wup, cv, wdn)


def _axial_tables(n_tokens, dim):
    t = jnp.arange(n_tokens, dtype=jnp.int32)
    row = (t // GRID_W).astype(F32)
    col = (t % GRID_W).astype(F32)
    quarter = dim // 4
    inv_freq = ROPE_BASE ** (-jnp.arange(quarter, dtype=F32) / quarter)
    ang = jnp.concatenate([row[:, None] * inv_freq, col[:, None] * inv_freq], axis=-1)
    return jnp.cos(ang), jnp.sin(ang)


def _rope_tables_head64(t):
    cos, sin = _axial_tables(t, 64)
    return jnp.tile(jnp.concatenate([cos, cos], -1), (1, 2)), jnp.tile(jnp.concatenate([-sin, sin], -1), (1, 2))


def _rope_tables_latent(t):
    cos, sin = _axial_tables(t, C_ROPE)
    ones = jnp.ones((t, C_NOPE), F32)
    tail = jnp.ones((t, HEAD_PAD - C_NOPE - C_ROPE), F32)
    cosc = jnp.concatenate([ones, cos, cos, tail], -1)
    sinc = jnp.concatenate([0 * ones, -sin, sin, 0 * tail], -1)
    return cosc, sinc


def _group_ones(width, group):
    return jnp.asarray(np.kron(np.eye(width // group), np.ones((group, group))), BF16)


def _pad_heads(a, heads, dim):
    a = a.reshape(a.shape[:-1] + (heads, dim))
    a = jnp.pad(a, [(0, 0)] * (a.ndim - 1) + [(0, HEAD_PAD - dim)])
    return a.reshape(a.shape[:-2] + (heads * HEAD_PAD,))


def _logit_bound(gq, gk, dim):
    return 1.02 * LOG2E * math.sqrt(dim) * jnp.max(jnp.abs(gq)) * jnp.max(jnp.abs(gk))


def _pad_row(v, width):
    return jnp.pad(v, (0, width - v.shape[0]))


def _ffn_weights(w_up, conv_w, conv_b, w_down):
    cv = jnp.concatenate([conv_w, conv_b[None]], axis=0)
    return w_up.astype(BF16), cv, w_down.astype(BF16)


def kernel(x, c, ctx, c_ctx, ada_w, ada_b, norm_mix, norm_ffn, ffn_up, ffn_conv_w, ffn_conv_b, ffn_down, ev_w_in, ev_w_out, a_q_norm, a_k_norm, a_lam_q1, a_lam_k1, a_lam_q2, a_lam_k2, a_subln, b_q_norm, b_k_norm, b_rpb, od_w_in, od_w_out, c_q_a_norm, c_w_qb, c_kv_a_norm, c_w_kvb, c_q_norm, c_k_norm, d_q_norm, d_k_norm, d_sink):
    assert x.shape[0] == 1 and ctx.shape[0] == 1
    xm = x[0]
    xc = ctx[0]
    t, d = xm.shape
    lc = xc.shape[0]
    mods = _modulation(c, c_ctx, ada_w, ada_b)
    cos64, sin64 = _rope_tables_head64(t)
    cosc, sinc = _rope_tables_latent(t)
    g64 = _group_ones(MXU_TILE, 64)
    g128 = _group_ones(MXU_TILE, HEAD_PAD)
    dummy_tab = jnp.zeros((lc, LANES), F32)

    for l in range(DEPTH):
        last = l == DEPTH - 1
        i = l // 2
        mm = mods[l, 0].reshape(6, d)
        mc = mods[l, 1].reshape(6, d)
        pre_m = jnp.stack([norm_mix[l], mm[1], mm[0]])
        pre_c = jnp.stack([norm_mix[l], mc[1], mc[0]])
        post_m = jnp.stack([mm[2], norm_ffn[l], mm[4], mm[3], mm[5]])
        post_c = jnp.stack([mc[2], norm_ffn[l], mc[4], mc[3], mc[5]])
        wup, cv, wdn = _ffn_weights(ffn_up[l], ffn_conv_w[l], ffn_conv_b[l], ffn_down[l])
        if l % 2 == 0:
            lam_init = 0.8 - 0.6 * math.exp(-0.3 * l)
            w_in = ev_w_in[i].astype(BF16)
            w_out = ev_w_out[i].astype(BF16)
            hg = jnp.stack([jnp.tile(a_q_norm[i], 8), jnp.tile(a_k_norm[i], 8),
                            jnp.tile(b_q_norm[i], 8), jnp.tile(b_k_norm[i], 8)])
            ex = jnp.stack([_pad_row(a_lam_q1[i], LANES), _pad_row(a_lam_k1[i], LANES),
                            _pad_row(a_lam_q2[i], LANES), _pad_row(a_lam_k2[i], LANES)])
            gcol = a_subln[i][:, None]
            qa, ka, va, qb, kb, vb = _premix_even(xm, pre_m, w_in, g64, hg, cos64, sin64, True)
            qa_c, ka_c, va_c, qb_c, kb_c, vb_c = _premix_even(xc, pre_c, w_in, g64, hg, dummy_tab, dummy_tab, False)
            oa = _flash(qa, ka, va, ka_c, va_c, ex, gcol, _logit_bound(a_q_norm[i], a_k_norm[i], A_DQK),
                        mode="diff", lam_init=lam_init)
            ob = _neighbourhood(qb, kb, vb, kb_c, vb_c, _na_bias_table(b_rpb[i], t // GRID_W))
            if not last:
                oa_c = _flash(qa_c, ka_c, va_c, None, None, ex, gcol, mode="diff", lam_init=lam_init)
                ob_c = _flash(qb_c, kb_c, vb_c, None, None, ex, gcol, mode="pair")
        else:
            w = od_w_in[i]
            cq, ckv, ckr, dq, dk, dv = jnp.split(w, np.cumsum([C_Q_LORA, C_KV_LORA, C_ROPE, 512, 128]).tolist(), axis=1)
            dup = lambda a: jnp.concatenate([a[:, :64], a[:, :64], a[:, 64:], a[:, 64:]], axis=1)
            ckr_pad = jnp.pad(ckr, ((0, 0), (C_NOPE, HEAD_PAD - C_NOPE - C_ROPE)))
            w_in = jnp.concatenate([cq, ckv, dq, dup(dk), dv, ckr_pad], axis=1).astype(BF16)
            w_out = od_w_out[i].astype(BF16)
            wqb = _pad_heads(c_w_qb[i], C_HEADS, C_NOPE + C_ROPE).astype(BF16)
            kvb = c_w_kvb[i].reshape(C_KV_LORA, C_HEADS, C_NOPE + C_DV)
            wk = _pad_heads(kvb[:, :, :C_NOPE].reshape(C_KV_LORA, -1), C_HEADS, C_NOPE).astype(BF16)
            wv = kvb[:, :, C_NOPE:].reshape(C_KV_LORA, C_HEADS * C_DV).astype(BF16)
            gains = jnp.stack([jnp.tile(_pad_row(c_q_norm[i], HEAD_PAD), C_HEADS),
                               jnp.tile(_pad_row(c_k_norm[i], HEAD_PAD), C_HEADS),
                               _pad_row(jnp.tile(d_q_norm[i], 8), 1024), _pad_row(jnp.tile(d_k_norm[i], 4), 1024)])
            lora = jnp.stack([_pad_row(c_q_a_norm[i], 512), _pad_row(c_kv_a_norm[i], 512)])
            odd = functools.partial(_premix_odd, w=w_in, wqb=wqb, wk=wk, wv=wv, g128=g128, g64=g64, gains=gains, lora=lora)
            mq, mk, mv, gq, gk, gv = odd(xm, pre_m, cos=cos64, sin=sin64, cosc=cosc, sinc=sinc, use_rope=True)
            mq_c, mk_c, mv_c, gq_c, gk_c, gv_c = odd(xc, pre_c, cos=dummy_tab, sin=dummy_tab, cosc=dummy_tab,
                                                     sinc=dummy_tab, use_rope=False)
            ex = jnp.zeros((4, LANES), F32)
            oa = _flash(mq, mk, mv, mk_c, mv_c, ex, jnp.ones((HEAD_PAD, 1), F32),
                        _logit_bound(c_q_norm[i], c_k_norm[i], C_NOPE + C_ROPE), mode="mla")
            ob = _window_gqa(gq, gk, gv, gk_c, gv_c, d_sink[i])
            if not last:
                raise NotImplementedError("context update after an odd layer is not needed at this depth")
        xm = _mix_ffn(oa, ob, xm, w_out, post_m, wup, cv, wdn)
        if not last:
            xc = _mix_ffn(oa_c, ob_c, xc, w_out, post_c, wup, cv, wdn)
    return xm[None]
```

```python
import functools
import math

import numpy as np
import jax
import jax.numpy as jnp
from jax import lax
from jax.experimental import pallas as pl
from jax.experimental.pallas import tpu as pltpu

DEPTH = 2
GRID_W = 64
EPS = 1e-6
ROPE_BASE = 10000.0
NEG_INF = -1e30
LOG2E = math.log2(math.e)

A_DQK = 64
A_DV = 128
B_DH = 64
NA_WIN_H = 8
NA_WIN_W = 16
C_HEADS = 8
C_Q_LORA = 384
C_KV_LORA = 256
C_NOPE = 64
C_ROPE = 32
C_DV = 64
D_HEADS = 8
D_KV_HEADS = 2
D_DH = 64
D_WINDOW = 128

LANES = 128
MXU_TILE = 256
HEAD_PAD = 128
VMEM_LIMIT = 56 * 1024 * 1024
FFN_CHUNK = 256
HALO = 16
NA_ROWS = 4
WG_Q = 256
ONES_ROWS = 16
FLASH_TQ = 4096
FLASH_TQ_ONLINE = 1024
FLASH_TK = 512
FLASH_UNROLL = 4
FLASH_SAFE_BOUND = 40.0

F32 = jnp.float32
BF16 = jnp.bfloat16


def _cparams(*sem):
    return pltpu.CompilerParams(dimension_semantics=sem, vmem_limit_bytes=VMEM_LIMIT)


def _full(shape):
    n = len(shape)
    return pl.BlockSpec(shape, lambda *_: (0,) * n)


def _resident(shape):
    n = len(shape)
    return pl.BlockSpec(shape, lambda *_: (0,) * n, pipeline_mode=pl.Buffered(1))


def _dot(a, b):
    return jnp.dot(a, b, preferred_element_type=F32)


def _rms(x):
    return x * lax.rsqrt(jnp.mean(x * x, axis=-1, keepdims=True) + EPS)


def _sigmoid(x):
    return 1.0 / (1.0 + jnp.exp(-x))


def _group_rms(y, gmat, inv_n, gain):
    yy = (y * y).astype(BF16)
    slab = gmat.shape[0]
    ss = jnp.concatenate([_dot(yy[:, c:c + slab], gmat) for c in range(0, y.shape[-1], slab)], axis=-1)
    return y * lax.rsqrt(ss * inv_n + EPS) * gain


def _rope(y, cos, sin, half, first_half):
    w = y.shape[-1]
    fwd = pltpu.roll(y, w - half, axis=1)
    bwd = pltpu.roll(y, half, axis=1)
    return y * cos + jnp.where(first_half, fwd, bwd) * sin


def _lane_tile(t, width):
    reps = width // t.shape[-1]
    return t if reps == 1 else jnp.concatenate([t] * reps, axis=-1)


def _lane_iota(width):
    return lax.broadcasted_iota(jnp.int32, (1, width), 1)


def _mod_kernel(ct_ref, w_ref, b_ref, o_ref):
    ct = ct_ref[...]
    s = ct * _sigmoid(ct)
    w = w_ref[0]
    rows = [jnp.sum(w * s[:, j:j + 1], axis=0, keepdims=True) for j in range(2)]
    o_ref[0] = jnp.concatenate(rows, axis=0) + b_ref[0]


def _modulation(c, c_ctx, ada_w, ada_b):
    depth, d, n = ada_w.shape
    tn = 1536
    ct = jnp.stack([c[0], c_ctx], axis=1)
    return pl.pallas_call(
        _mod_kernel,
        grid=(depth, n // tn),
        in_specs=[_full((d, 2)),
                  pl.BlockSpec((1, d, tn), lambda l, j: (l, 0, j)),
                  pl.BlockSpec((1, 1, tn), lambda l, j: (l, 0, j))],
        out_specs=pl.BlockSpec((1, 2, tn), lambda l, j: (l, 0, j)),
        out_shape=jax.ShapeDtypeStruct((depth, 2, n), F32),
        compiler_params=_cparams("arbitrary", "arbitrary"),
        name="modulation",
    )(ct, ada_w, ada_b.reshape(depth, 1, n))


def _store_values_t(vt_ref, v, dv):
    tm = v.shape[0]
    vt = v.T.astype(BF16)
    ones = jnp.ones((ONES_ROWS, tm), BF16)
    step = dv + ONES_ROWS
    for g in range(vt_ref.shape[0]):
        for h in range(HEAD_PAD // dv):
            src = g * HEAD_PAD + h * dv
            vt_ref[g, 0, h * step:h * step + dv, :] = vt[src:src + dv]
            vt_ref[g, 0, h * step + dv:(h + 1) * step, :] = ones


def _values_t_shape(t, tm, dv, groups=4):
    return (groups, t // tm, (HEAD_PAD // dv) * (dv + ONES_ROWS), tm)


def _premix_even_kernel(x_ref, vec_ref, w_ref, g_ref, hg_ref, cos_ref, sin_ref,
                        qa_ref, ka_ref, vat_ref, qb_ref, kb_ref, vbt_ref, *, use_rope):
    vec = vec_ref[...]
    h = (_rms(x_ref[...]) * vec[0:1] * (1.0 + vec[1:2]) + vec[2:3]).astype(BF16)
    gm = g_ref[...]
    hg = hg_ref[...]
    sw = 512
    if use_rope:
        cos = _lane_tile(cos_ref[...], sw)
        sin = _lane_tile(sin_ref[...], sw)
        first = (_lane_iota(sw) & 32) == 0

    def seg(i):
        return _dot(h, w_ref[:, i * sw:(i + 1) * sw])

    qa = _group_rms(seg(0), gm, 1.0 / A_DQK, hg[0:1])
    ka = _group_rms(seg(1), gm, 1.0 / A_DQK, hg[1:2])
    if use_rope:
        qa = _rope(qa, cos, sin, 32, first)
        ka = _rope(ka, cos, sin, 32, first)
    qa_ref[...] = (qa * (LOG2E / math.sqrt(A_DQK))).astype(BF16)
    ka_ref[...] = ka.astype(BF16)
    _store_values_t(vat_ref, seg(2), A_DV)
    qb = _group_rms(seg(3), gm, 1.0 / B_DH, hg[2:3])
    qb_ref[...] = (qb * (LOG2E / math.sqrt(B_DH))).astype(BF16)
    kb_ref[...] = _group_rms(seg(4), gm, 1.0 / B_DH, hg[3:4]).astype(BF16)
    _store_values_t(vbt_ref, seg(5), B_DH)


def _premix_even(x, vec, w, gmat, hg, cos, sin, use_rope):
    t, d = x.shape
    tm = min(FLASH_TK, t)
    row = lambda i: (i, 0)
    nat = jax.ShapeDtypeStruct((t, 512), BF16)
    nat_spec = pl.BlockSpec((tm, 512), row)
    vta = _values_t_shape(t, tm, A_DV)
    vtb = _values_t_shape(t, tm, B_DH)
    vt_spec = lambda s: pl.BlockSpec((s[0], 1) + s[2:], lambda i: (0, i, 0, 0))
    return pl.pallas_call(
        functools.partial(_premix_even_kernel, use_rope=use_rope),
        grid=(t // tm,),
        in_specs=[pl.BlockSpec((tm, d), row), _full(vec.shape), _full(w.shape), _full(gmat.shape),
                  _full(hg.shape), pl.BlockSpec((tm, LANES), row), pl.BlockSpec((tm, LANES), row)],
        out_specs=[nat_spec, nat_spec, vt_spec(vta), nat_spec, nat_spec, vt_spec(vtb)],
        out_shape=[nat, nat, jax.ShapeDtypeStruct(vta, BF16), nat, nat, jax.ShapeDtypeStruct(vtb, BF16)],
        compiler_params=_cparams("parallel"),
        name="premix_even",
    )(x, vec, w, gmat, hg, cos, sin)


_O_CQ, _O_CKV, _O_DQ, _O_DK, _O_DV, _O_CKR, _O_END = 0, 384, 640, 1152, 1408, 1536, 1664


def _premix_odd_kernel(x_ref, vec_ref, w_ref, wqb_ref, wk_ref, wv_ref, g128_ref, g64_ref, gain_ref, lora_ref,
                       cos_ref, sin_ref, cosc_ref, sinc_ref,
                       mq_ref, mk_ref, mvt_ref, gq_ref, gk_ref, gvt_ref, *, use_rope):
    vec = vec_ref[...]
    h = (_rms(x_ref[...]) * vec[0:1] * (1.0 + vec[1:2]) + vec[2:3]).astype(BF16)
    gains = gain_ref[...]
    lora = lora_ref[...]
    g128 = g128_ref[...]
    g64 = g64_ref[...]
    mw = C_HEADS * HEAD_PAD
    if use_rope:
        cosc = _lane_tile(cosc_ref[...], mw)
        sinc = _lane_tile(sinc_ref[...], mw)
        first_c = (_lane_iota(mw) & 127) < (C_NOPE + C_ROPE // 2)
        cos = _lane_tile(cos_ref[...], 512)
        sin = _lane_tile(sin_ref[...], 512)
        first_d = (_lane_iota(512) & 32) == 0

    def seg(a, b):
        return _dot(h, w_ref[:, a:b])

    inv_c = 1.0 / (C_NOPE + C_ROPE)
    cq = (_rms(seg(_O_CQ, _O_CKV)) * lora[0:1, :C_Q_LORA]).astype(BF16)
    mq = _group_rms(_dot(cq, wqb_ref[...]), g128, inv_c, gains[0:1])
    if use_rope:
        mq = _rope(mq, cosc, sinc, C_ROPE // 2, first_c)
    mq_ref[...] = (mq * (LOG2E * math.sqrt(inv_c))).astype(BF16)
    ckv = (_rms(seg(_O_CKV, _O_DQ)) * lora[1:2, :C_KV_LORA]).astype(BF16)
    mk = _dot(ckv, wk_ref[...]) + _lane_tile(seg(_O_CKR, _O_END), mw)
    mk = _group_rms(mk, g128, inv_c, gains[1:2])
    if use_rope:
        mk = _rope(mk, cosc, sinc, C_ROPE // 2, first_c)
    mk_ref[...] = mk.astype(BF16)
    _store_values_t(mvt_ref, _dot(ckv, wv_ref[...]), C_DV)
    gq = _group_rms(seg(_O_DQ, _O_DK), g64, 1.0 / D_DH, gains[2:3, :512])
    gk = _group_rms(seg(_O_DK, _O_DV), g64, 1.0 / D_DH, gains[3:4, :256])
    if use_rope:
        gq = _rope(gq, cos, sin, 32, first_d)
        gk = _rope(gk, cos[:, :256], sin[:, :256], 32, first_d[:, :256])
    gq_ref[...] = (gq * (LOG2E / math.sqrt(D_DH))).astype(BF16)
    gk_ref[...] = gk.astype(BF16)
    _store_values_t(gvt_ref, seg(_O_DV, _O_CKR), D_DH)


def _premix_odd(x, vec, w, wqb, wk, wv, g128, g64, gains, lora, cos, sin, cosc, sinc, use_rope):
    t, d = x.shape
    tm = min(FLASH_TK, t)
    row = lambda i: (i, 0)
    tab = pl.BlockSpec((tm, LANES), row)
    nat = lambda n: (pl.BlockSpec((tm, n), row), jax.ShapeDtypeStruct((t, n), BF16))
    def vt(shape):
        return pl.BlockSpec((shape[0], 1) + shape[2:], lambda i: (0, i, 0, 0)), jax.ShapeDtypeStruct(shape, BF16)

    outs = [nat(1024), nat(1024), vt(_values_t_shape(t, tm, C_DV)), nat(512), nat(256),
            vt(_values_t_shape(t, tm, D_DH, groups=1))]
    return pl.pallas_call(
        functools.partial(_premix_odd_kernel, use_rope=use_rope),
        grid=(t // tm,),
        in_specs=[pl.BlockSpec((tm, d), row)] + [_full(a.shape) for a in (vec, w, wqb, wk, wv, g128, g64, gains, lora)]
                 + [tab] * 4,
        out_specs=[o[0] for o in outs],
        out_shape=[o[1] for o in outs],
        compiler_params=_cparams("parallel"),
        name="premix_odd",
    )(x, vec, w, wqb, wk, wv, g128, g64, gains, lora, cos, sin, cosc, sinc)


def _flash_kernel(*refs, mode, has_ctx, online, lam_init):
    bound_ref, q_ref, k_ref, vt_ref = refs[:4]
    kc_ref, vct_ref = refs[4:6] if has_ctx else (None, None)
    ex_ref, gcol_ref, o_ref, acc_ref, stage_ref = refs[6:11] if has_ctx else refs[4:9]
    qt = q_ref[...].astype(F32).T.astype(BF16)
    row = lax.broadcasted_iota(jnp.int32, (HEAD_PAD, 1), 0)
    if mode == "mla":
        qs = [qt[:HEAD_PAD], qt[HEAD_PAD:]]
    else:
        zero = jnp.zeros_like(qt)
        qs = [jnp.where(row < 64, qt, zero), jnp.where(row >= 64, qt, zero)]
    nk, tk = vt_ref.shape[1], vt_ref.shape[3]

    acc_ref[...] = jnp.zeros(acc_ref.shape, F32)
    rows_b = acc_ref.shape[1]

    def scores(kblk, b):
        kb = kblk[:, b * HEAD_PAD:(b + 1) * HEAD_PAD] if mode == "mla" else kblk
        return _dot(kb, qs[b])

    def values(vtblk, b):
        return vtblk if vtblk.shape[0] == rows_b else vtblk[b * rows_b:(b + 1) * rows_b]

    if online:
        m_ref, mx_ref = refs[-2:]
        m_ref[...] = jnp.full(m_ref.shape, NEG_INF, F32)

        def absorb(s, smax, vtblk, b):
            m_prev = m_ref[b]
            m_new = jnp.maximum(m_prev, smax)
            alpha = jnp.exp2(m_prev - m_new)
            p = jnp.exp2(s - m_new).astype(BF16)
            acc_ref[b] = alpha * acc_ref[b] + _dot(values(vtblk, b), p)
            m_ref[b] = m_new

        def produce(slot, kblk):
            for b in range(2):
                s = scores(kblk, b)
                stage_ref[slot, b] = s
                mx_ref[slot, b] = jnp.max(s, axis=0, keepdims=True)

        def consume(slot, vtblk):
            for b in range(2):
                absorb(stage_ref[slot, b], mx_ref[slot, b], vtblk, b)

        def direct(kblk, vtblk):
            for b in range(2):
                s = scores(kblk, b)
                absorb(s, jnp.max(s, axis=0, keepdims=True), vtblk, b)
    else:
        bound = bound_ref[0]

        def probs(kblk, b):
            return jnp.exp2(scores(kblk, b) - bound).astype(BF16)

        def produce(slot, kblk):
            for b in range(2):
                stage_ref[slot, b] = probs(kblk, b)

        def consume(slot, vtblk):
            for b in range(2):
                acc_ref[b] += _dot(values(vtblk, b), stage_ref[slot, b])

        def direct(kblk, vtblk):
            for b in range(2):
                acc_ref[b] += _dot(values(vtblk, b), probs(kblk, b))

    if nk == 1:
        if has_ctx:
            direct(kc_ref[...], vct_ref[0])
        direct(k_ref[...], vt_ref[0, 0])
    else:
        unroll = FLASH_UNROLL if nk % FLASH_UNROLL == 0 else 2
        assert nk % unroll == 0
        produce(0, k_ref[0:tk, :])
        if has_ctx:
            direct(kc_ref[...], vct_ref[0])

        def body(i, carry):
            c0 = unroll * i
            for u in range(unroll):
                nxt = jnp.minimum(c0 + u + 1, nk - 1)
                produce((u + 1) % 2, k_ref[pl.ds(pl.multiple_of(nxt * tk, tk), tk), :])
                consume(u % 2, vt_ref[0, c0 + u])
            return carry

        lax.fori_loop(0, nk // unroll, body, 0)

    dv = rows_b - ONES_ROWS
    o0 = acc_ref[0, :dv] / acc_ref[0, dv:dv + 1]
    o1 = acc_ref[1, :dv] / acc_ref[1, dv:dv + 1]
    if mode == "diff":
        ex = ex_ref[...]
        lam = (jnp.exp(jnp.sum(ex[0:1] * ex[1:2], axis=-1, keepdims=True))
               - jnp.exp(jnp.sum(ex[2:3] * ex[3:4], axis=-1, keepdims=True)) + lam_init)
        o = o0 - lam * o1
        o = o * lax.rsqrt(jnp.mean(o * o, axis=0, keepdims=True) + EPS) * gcol_ref[...] * (1.0 - lam_init)
    else:
        o = jnp.concatenate([o0, o1], axis=0)
    o_ref[...] = o.T.astype(o_ref.dtype)


def _flash(q, k, vt, kc, vct, ex, gcol, bound=None, *, mode, lam_init=0.0):
    t = q.shape[0]
    tkeys = k.shape[0]
    qw = 2 * HEAD_PAD if mode == "mla" else HEAD_PAD
    groups = q.shape[1] // qw
    _, nk, vrows, tk = vt.shape
    assert nk * tk == tkeys
    has_ctx = kc is not None
    dv = HEAD_PAD if mode == "diff" else HEAD_PAD // 2
    rows_b = dv + ONES_ROWS
    bound_arr = jnp.zeros((1,), F32) if bound is None else jnp.reshape(bound, (1,)).astype(F32)
    args = [bound_arr, q, k, vt]
    if has_ctx:
        lc = kc.shape[0]
        args += [kc, vct.reshape(groups, vrows, lc)]
    args += [ex, gcol]

    def call(online):
        tq = min(FLASH_TQ_ONLINE if online else FLASH_TQ, t)
        in_specs = [pl.BlockSpec(memory_space=pltpu.SMEM),
                    pl.BlockSpec((tq, qw), lambda g, i: (i, g)),
                    pl.BlockSpec((tkeys, qw), lambda g, i: (0, g), pipeline_mode=pl.Buffered(1)),
                    pl.BlockSpec((1, nk, vrows, tk), lambda g, i: (g, 0, 0, 0), pipeline_mode=pl.Buffered(1))]
        if has_ctx:
            in_specs += [pl.BlockSpec((lc, qw), lambda g, i: (0, g)),
                         pl.BlockSpec((1, vrows, lc), lambda g, i: (g, 0, 0))]
        in_specs += [_full(ex.shape), _full(gcol.shape)]
        acc = pltpu.VMEM((2, rows_b, tq), F32)
        if online:
            scratch = [acc, pltpu.VMEM((2, 2, tk, tq), F32), pltpu.VMEM((2, 1, tq), F32), pltpu.VMEM((2, 2, 1, tq), F32)]
        else:
            scratch = [acc, pltpu.VMEM((2, 2, tk, tq), BF16)]
        return pl.pallas_call(
            functools.partial(_flash_kernel, mode=mode, has_ctx=has_ctx, online=online, lam_init=lam_init),
            grid=(groups, t // tq),
            in_specs=in_specs,
            out_specs=pl.BlockSpec((tq, HEAD_PAD), lambda g, i: (i, g)),
            out_shape=jax.ShapeDtypeStruct((t, groups * HEAD_PAD), BF16),
            scratch_shapes=scratch,
            compiler_params=_cparams("parallel", "parallel"),
            name="flash_" + mode + ("_online" if online else "_bounded"),
        )(*args)

    if bound is None:
        return call(True)
    return lax.cond(bound <= FLASH_SAFE_BOUND, lambda: call(False), lambda: call(True))


def _na_kernel(shift_ref, q_ref, kp_ref, kcur_ref, kn_ref, vtp_ref, vtcur_ref, vtn_ref, kc_ref, vct_ref, bias_ref,
               o_ref, stage_ref, *stat_refs, online):
    pairs = q_ref.shape[1] // HEAD_PAD
    qn = q_ref.shape[0]
    lc = kc_ref.shape[0]
    step = B_DH + ONES_ROWS
    row = lax.broadcasted_iota(jnp.int32, (HEAD_PAD, 1), 0)

    def produce(slot, pr):
        ps = slice(pr * HEAD_PAD, (pr + 1) * HEAD_PAD)
        qt = q_ref[:, ps].astype(F32).T.astype(BF16)
        zero = jnp.zeros_like(qt)
        q2 = jnp.concatenate([jnp.where(row < 64, qt, zero), jnp.where(row >= 64, qt, zero)], axis=1)
        k_nb = jnp.concatenate([kp_ref[:, ps], kcur_ref[:, ps], kn_ref[:, ps]], axis=0)
        bias = jnp.concatenate([bias_ref[0, 2 * pr], bias_ref[0, 2 * pr + 1]], axis=1)
        s_ctx = _dot(kc_ref[:, ps], q2)
        s_nb = _dot(k_nb, q2) + bias
        if online:
            stage_ref[slot, 0:lc] = s_ctx
            stage_ref[slot, lc:] = s_nb
            stat_refs[0][slot] = jnp.maximum(jnp.max(s_ctx, axis=0, keepdims=True),
                                             jnp.max(s_nb, axis=0, keepdims=True))
        else:
            shift = shift_ref[0]
            stage_ref[slot, 0:lc] = jnp.exp2(s_ctx - shift).astype(BF16)
            stage_ref[slot, lc:] = jnp.exp2(s_nb - shift).astype(BF16)

    def consume(slot, pr):
        vt = jnp.concatenate([vct_ref[pr, 0], vtp_ref[pr, 0], vtcur_ref[pr, 0], vtn_ref[pr, 0]], axis=1)
        if online:
            p = jnp.exp2(stage_ref[slot] - stat_refs[0][slot]).astype(BF16)
        else:
            p = stage_ref[slot]
        outs = []
        for b in range(2):
            acc = _dot(vt[b * step:(b + 1) * step], p[:, b * qn:(b + 1) * qn])
            outs.append(acc[:B_DH] / acc[B_DH:B_DH + 1])
        o_ref[:, pr * HEAD_PAD:(pr + 1) * HEAD_PAD] = jnp.concatenate(outs, axis=0).T.astype(o_ref.dtype)

    produce(0, 0)
    for pr in range(pairs):
        if pr + 1 < pairs:
            produce((pr + 1) % 2, pr + 1)
        consume(pr % 2, pr)


def _na_bias_table(rpb, rows):
    nr, w = NA_ROWS, GRID_W
    heads = rpb.shape[0]
    pad = w - NA_WIN_W
    padded = jnp.pad(rpb.astype(F32) * LOG2E, ((0, 0), (0, 0), (pad, pad + 1)))
    skew = jnp.tile(padded, (1, 1, w))[:, :, :w * (2 * w - 1)].reshape(heads, -1, w, 2 * w - 1)
    toep = skew[:, :, :, w - 1:]
    cq = np.arange(w)
    cs = np.clip(cq - NA_WIN_W // 2, 0, w - NA_WIN_W)
    valid_c = (cq[None, :] >= cs[:, None]) & (cq[None, :] < cs[:, None] + NA_WIN_W)
    toep_t = jnp.swapaxes(jnp.where(jnp.asarray(valid_c), toep, NEG_INF), -1, -2)
    d0 = NA_WIN_H - 1 - nr
    assert d0 - (nr - 1) >= 0 and d0 + 3 * nr <= 2 * NA_WIN_H - 1
    dense = jnp.concatenate([toep_t[:, d0 - rl:d0 - rl + 3 * nr] for rl in range(nr)], axis=-1)
    wh = min(NA_WIN_H, rows)
    rl = np.arange(nr)
    rr_rel = np.arange(3 * nr)
    valid_all = []
    for base in (0, nr, rows - nr):
        rs = np.clip(base + rl - wh // 2, 0, rows - wh)
        rr = base - nr + rr_rel
        valid_all.append((rr[:, None] >= rs[None, :]) & (rr[:, None] < rs[None, :] + wh))
    valid_r = np.repeat(np.stack(valid_all), w, axis=-1)[:, None, :, None, :]
    tab = jnp.where(jnp.asarray(valid_r), dense[None], NEG_INF)
    return tab.reshape(3, heads, 3 * nr * w, nr * w)


def _neighbourhood(q, k, vt, kc, vct, bias, qk_bound, shift):
    t = q.shape[0]
    qn = NA_ROWS * GRID_W
    nb = t // qn
    lc = kc.shape[0]
    pairs, _, vrows, tk = vt.shape
    per = tk // qn
    cur = lambda i: (i, 0)
    prev = lambda i: (jnp.maximum(i - 1, 0), 0)
    nxt = lambda i: (jnp.minimum(i + 1, nb - 1), 0)
    blk = lambda f: pl.BlockSpec((qn, q.shape[1]), f)
    vblk = lambda f: pl.BlockSpec((pairs, 1, vrows, qn), lambda i: (0, f(i)[0] // per, 0, f(i)[0] % per))
    case = lambda i: (jnp.where(i == 0, 0, jnp.where(i == nb - 1, 2, 1)), 0, 0, 0)
    nkeys = lc + 3 * qn

    def call(online):
        scratch = ([pltpu.VMEM((2, nkeys, 2 * qn), F32), pltpu.VMEM((2, 1, 2 * qn), F32)] if online
                   else [pltpu.VMEM((2, nkeys, 2 * qn), BF16)])
        return pl.pallas_call(
            functools.partial(_na_kernel, online=online),
            grid=(nb,),
            in_specs=[pl.BlockSpec(memory_space=pltpu.SMEM), blk(cur), blk(prev), blk(cur), blk(nxt),
                      vblk(prev), vblk(cur), vblk(nxt),
                      _full(kc.shape), _full(vct.shape), pl.BlockSpec((1,) + bias.shape[1:], case)],
            out_specs=blk(cur),
            out_shape=jax.ShapeDtypeStruct(q.shape, BF16),
            scratch_shapes=scratch,
            compiler_params=_cparams("arbitrary"),
            name="neighbourhood" + ("_online" if online else "_bounded"),
        )(jnp.reshape(shift, (1,)).astype(F32), q, k, k, k, vt, vt, vt, kc, vct, bias)

    return lax.cond(shift + qk_bound <= 2 * FLASH_SAFE_BOUND, lambda: call(False), lambda: call(True))


def _wgqa_kernel(shift_ref, sink_ref, q_ref, *refs, online):
    nkb = WG_Q // D_WINDOW + 2
    k_refs, vt_refs = refs[:nkb], refs[nkb:2 * nkb]
    kc_ref, vct_ref, mask_ref, o_ref, stage_ref = refs[2 * nkb:2 * nkb + 5]
    mx_ref = refs[-1]
    qn = q_ref.shape[0]
    pairs = D_HEADS // 2
    pairs_per_kv = pairs // D_KV_HEADS
    step = D_DH + ONES_ROWS
    row = lax.broadcasted_iota(jnp.int32, (HEAD_PAD, 1), 0)
    mask = jnp.concatenate([mask_ref[0]] * 2, axis=1)

    def produce(slot, pair):
        kv = pair // pairs_per_kv
        ks = slice(kv * HEAD_PAD, (kv + 1) * HEAD_PAD)
        qt = q_ref[:, pair * HEAD_PAD:(pair + 1) * HEAD_PAD].astype(F32).T.astype(BF16)
        zero = jnp.zeros_like(qt)
        q2 = jnp.concatenate([jnp.where(row < 64, qt, zero), jnp.where(row >= 64, qt, zero)], axis=1)
        kk = jnp.concatenate([kc_ref[:, ks]] + [r[:, ks] for r in k_refs], axis=0)
        s = _dot(kk, q2) + mask
        if online:
            stage_ref[slot] = s
            mx_ref[slot] = jnp.max(s, axis=0, keepdims=True)
        else:
            stage_ref[slot] = jnp.exp2(s - shift_ref[0]).astype(BF16)

    def consume(slot, pair):
        kv = pair // pairs_per_kv
        sk = jnp.concatenate([jnp.full((1, qn), sink_ref[2 * pair + b] * LOG2E, F32) for b in range(2)], axis=1)
        if online:
            m = jnp.maximum(mx_ref[slot], sk)
            p = jnp.exp2(stage_ref[slot] - m).astype(BF16)
        else:
            m = shift_ref[0]
            p = stage_ref[slot]
        vt = jnp.concatenate([vct_ref[0, 0]] + [r[0, 0] for r in vt_refs], axis=1)[kv * step:(kv + 1) * step]
        acc = _dot(vt, p)
        o = acc[:D_DH] / (acc[D_DH:D_DH + 1] + jnp.exp2(sk - m))
        both = jnp.concatenate([o[:, :qn], o[:, qn:]], axis=0)
        o_ref[:, pair * HEAD_PAD:(pair + 1) * HEAD_PAD] = both.T.astype(o_ref.dtype)

    produce(0, 0)
    for pair in range(pairs):
        if pair + 1 < pairs:
            produce((pair + 1) % 2, pair + 1)
        consume(pair % 2, pair)


def _wgqa_mask(lc):
    nkeys = WG_Q + 2 * D_WINDOW
    kb = np.arange(nkeys)[:, None]
    qi = np.arange(WG_Q)[None, :]
    band = np.abs(kb - D_WINDOW - qi) <= D_WINDOW
    cases = [band & (kb >= D_WINDOW), band, band & (kb < nkeys - D_WINDOW)]
    tab = np.where(np.stack(cases), 0.0, NEG_INF).astype(np.float32)
    return jnp.asarray(np.concatenate([np.zeros((3, lc, WG_Q), np.float32), tab], axis=1))


def _window_gqa(q, k, vt, kc, vct, sink, qk_bound):
    t = q.shape[0]
    nb = t // WG_Q
    per_q = WG_Q // D_WINDOW
    nkb = t // D_WINDOW
    lc = kc.shape[0]
    kw = k.shape[1]
    _, _, vrows, tk = vt.shape
    per_chunk = tk // D_WINDOW
    first_blk = [lambda i, j=j: jnp.clip(i * per_q - 1 + j, 0, nkb - 1) for j in range(per_q + 2)]
    kspecs = [pl.BlockSpec((D_WINDOW, kw), lambda i, f=f: (f(i), 0)) for f in first_blk]
    vspecs = [pl.BlockSpec((1, 1, vrows, D_WINDOW), lambda i, f=f: (0, f(i) // per_chunk, 0, f(i) % per_chunk))
              for f in first_blk]
    mask = _wgqa_mask(lc)
    case = lambda i: (jnp.where(i == 0, 0, jnp.where(i == nb - 1, 2, 1)), 0, 0)
    cur = lambda i: (i, 0)
    nkeys = lc + WG_Q + 2 * D_WINDOW
    lanes = 2 * WG_Q
    smem = pl.BlockSpec(memory_space=pltpu.SMEM)
    shift = jnp.maximum(qk_bound, LOG2E * jnp.max(sink))

    def call(online):
        stage = pltpu.VMEM((2, nkeys, lanes), F32 if online else BF16)
        return pl.pallas_call(
            functools.partial(_wgqa_kernel, online=online),
            grid=(nb,),
            in_specs=[smem, smem, pl.BlockSpec((WG_Q, q.shape[1]), cur)] + kspecs + vspecs
                     + [_full(kc.shape), _full(vct.shape), pl.BlockSpec((1,) + mask.shape[1:], case)],
            out_specs=pl.BlockSpec((WG_Q, q.shape[1]), cur),
            out_shape=jax.ShapeDtypeStruct(q.shape, BF16),
            scratch_shapes=[stage, pltpu.VMEM((2, 1, lanes), F32)],
            compiler_params=_cparams("parallel"),
            name="window_gqa" + ("_online" if online else "_bounded"),
        )(jnp.reshape(shift, (1,)).astype(F32), sink, q, *([k] * (per_q + 2)), *([vt] * (per_q + 2)), kc, vct, mask)

    return lax.cond(shift + qk_bound <= 2 * FLASH_SAFE_BOUND, lambda: call(False), lambda: call(True))


def _mix_ffn_kernel(o1_ref, o1p_ref, o1n_ref, o2_ref, o2p_ref, o2n_ref, x_ref, xp_ref, xn_ref,
                    wout_ref, vec_ref, wup_ref, cv_ref, wdn_ref, o_ref, acc_ref, hcat_ref, u_ref, x1_ref):
    i = pl.program_id(0)
    n = pl.num_programs(0)
    tm = x_ref.shape[0]
    rows = tm + 2 * HALO
    nchunk = wdn_ref.shape[0] // FFN_CHUNK
    half = o1_ref.shape[1]
    vec = vec_ref[...]
    o1 = jnp.concatenate([o1p_ref[...], o1_ref[...], o1n_ref[...]], axis=0)
    o2 = jnp.concatenate([o2p_ref[...], o2_ref[...], o2n_ref[...]], axis=0)
    x = jnp.concatenate([xp_ref[...], x_ref[...], xn_ref[...]], axis=0)
    x1 = x + vec[0:1] * (_dot(o1, wout_ref[:half, :]) + _dot(o2, wout_ref[half:, :]))
    x1_ref[...] = x1[HALO:HALO + tm]
    hcat_ref[...] = (_rms(x1) * vec[1:2] * (1.0 + vec[2:3]) + vec[3:4]).astype(BF16)

    @pl.when(i == 0)
    def _():
        hcat_ref[0:HALO] = jnp.zeros((HALO, hcat_ref.shape[1]), BF16)

    @pl.when(i == n - 1)
    def _():
        hcat_ref[HALO + tm:rows] = jnp.zeros((HALO, hcat_ref.shape[1]), BF16)

    acc_ref[...] = jnp.zeros(acc_ref.shape, F32)

    def conv(u, cv):
        before = pltpu.roll(u, 1, axis=0)
        after = pltpu.roll(u, rows - 1, axis=0)
        r = cv[3:4] + before * cv[0:1] + u * cv[1:2] + after * cv[2:3]
        return r[HALO:HALO + tm]

    def cols(c, gate):
        return pl.ds(pl.multiple_of((gate * nchunk + c) * FFN_CHUNK, FFN_CHUNK), FFN_CHUNK)

    def stage_up(slot, c):
        hcat = hcat_ref[...]
        u_ref[slot, 0] = _dot(hcat, wup_ref[:, cols(c, 0)])
        u_ref[slot, 1] = _dot(hcat, wup_ref[:, cols(c, 1)])

    def stage_down(slot, c):
        a = conv(u_ref[slot, 0], cv_ref[:, cols(c, 0)])
        g = conv(u_ref[slot, 1], cv_ref[:, cols(c, 1)])
        act = (a * (g * _sigmoid(g))).astype(BF16)
        acc_ref[...] += _dot(act, wdn_ref[cols(c, 0), :])

    assert nchunk % 2 == 1
    stage_up(0, 0)

    def body(j, carry):
        c = 2 * j
        stage_up(1, c + 1)
        stage_down(0, c)
        stage_up(0, c + 2)
        stage_down(1, c + 1)
        return carry

    lax.fori_loop(0, nchunk // 2, body, 0)
    stage_down(0, nchunk - 1)
    o_ref[...] = x1_ref[...] + vec[4:5] * acc_ref[...]


def _mix_ffn(o1, o2, x, wout, vec, wup, cv, wdn):
    t, d = x.shape
    tm = min(512, t)
    nt = t // tm
    hb = tm // HALO
    row = lambda i: (i, 0)
    prev = lambda i: (jnp.maximum(i * hb - 1, 0), 0)
    nxt = lambda i: (jnp.minimum((i + 1) * hb, t // HALO - 1), 0)
    tiled = lambda w: [pl.BlockSpec((tm, w), row), pl.BlockSpec((HALO, w), prev), pl.BlockSpec((HALO, w), nxt)]
    return pl.pallas_call(
        _mix_ffn_kernel,
        grid=(nt,),
        in_specs=tiled(o1.shape[1]) + tiled(o2.shape[1]) + tiled(d)
                 + [_resident(a.shape) for a in (wout, vec, wup, cv, wdn)],
        out_specs=pl.BlockSpec((tm, d), row),
        out_shape=jax.ShapeDtypeStruct((t, d), F32),
        scratch_shapes=[pltpu.VMEM((tm, d), F32), pltpu.VMEM((tm + 2 * HALO, d), BF16),
                        pltpu.VMEM((2, 2, tm + 2 * HALO, FFN_CHUNK), F32), pltpu.VMEM((tm, d), F32)],
        compiler_params=_cparams("parallel"),
        name="mix_ffn",
    )(o1, o1, o1, o2, o2, o2, x, x, x, wout, vec, wup, cv, wdn)


def _axial_tables(n_tokens, dim):
    t = jnp.arange(n_tokens, dtype=jnp.int32)
    row = (t // GRID_W).astype(F32)
    col = (t % GRID_W).astype(F32)
    quarter = dim // 4
    inv_freq = ROPE_BASE ** (-jnp.arange(quarter, dtype=F32) / quarter)
    ang = jnp.concatenate([row[:, None] * inv_freq, col[:, None] * inv_freq], axis=-1)
    return jnp.cos(ang), jnp.sin(ang)


def _rope_tables_head64(t):
    cos, sin = _axial_tables(t, 64)
    return jnp.tile(jnp.concatenate([cos, cos], -1), (1, 2)), jnp.tile(jnp.concatenate([-sin, sin], -1), (1, 2))


def _rope_tables_latent(t):
    cos, sin = _axial_tables(t, C_ROPE)
    ones = jnp.ones((t, C_NOPE), F32)
    tail = jnp.ones((t, HEAD_PAD - C_NOPE - C_ROPE), F32)
    cosc = jnp.concatenate([ones, cos, cos, tail], -1)
    sinc = jnp.concatenate([0 * ones, -sin, sin, 0 * tail], -1)
    return cosc, sinc


def _group_ones(width, group):
    return jnp.asarray(np.kron(np.eye(width // group), np.ones((group, group))), BF16)


def _pad_heads(a, heads, dim):
    a = a.reshape(a.shape[:-1] + (heads, dim))
    a = jnp.pad(a, [(0, 0)] * (a.ndim - 1) + [(0, HEAD_PAD - dim)])
    return a.reshape(a.shape[:-2] + (heads * HEAD_PAD,))


def _logit_bound(gq, gk, dim):
    return 1.02 * LOG2E * math.sqrt(dim) * jnp.max(jnp.abs(gq)) * jnp.max(jnp.abs(gk))


def _pad_row(v, width):
    return jnp.pad(v, (0, width - v.shape[0]))


def _ffn_weights(w_up, conv_w, conv_b, w_down):
    cv = jnp.concatenate([conv_w, conv_b[None]], axis=0)
    return w_up.astype(BF16), cv, w_down.astype(BF16)


def kernel(x, c, ctx, c_ctx, ada_w, ada_b, norm_mix, norm_ffn, ffn_up, ffn_conv_w, ffn_conv_b, ffn_down, ev_w_in, ev_w_out, a_q_norm, a_k_norm, a_lam_q1, a_lam_k1, a_lam_q2, a_lam_k2, a_subln, b_q_norm, b_k_norm, b_rpb, od_w_in, od_w_out, c_q_a_norm, c_w_qb, c_kv_a_norm, c_w_kvb, c_q_norm, c_k_norm, d_q_norm, d_k_norm, d_sink):
    assert x.shape[0] == 1 and ctx.shape[0] == 1
    xm = x[0]
    xc = ctx[0]
    t, d = xm.shape
    lc = xc.shape[0]
    mods = _modulation(c, c_ctx, ada_w, ada_b)
    cos64, sin64 = _rope_tables_head64(t)
    cosc, sinc = _rope_tables_latent(t)
    g64 = _group_ones(MXU_TILE, 64)
    g128 = _group_ones(MXU_TILE, HEAD_PAD)
    dummy_tab = jnp.zeros((lc, LANES), F32)

    for l in range(DEPTH):
        last = l == DEPTH - 1
        i = l // 2
        mm = mods[l, 0].reshape(6, d)
        mc = mods[l, 1].reshape(6, d)
        pre_m = jnp.stack([norm_mix[l], mm[1], mm[0]])
        pre_c = jnp.stack([norm_mix[l], mc[1], mc[0]])
        post_m = jnp.stack([mm[2], norm_ffn[l], mm[4], mm[3], mm[5]])
        post_c = jnp.stack([mc[2], norm_ffn[l], mc[4], mc[3], mc[5]])
        wup, cv, wdn = _ffn_weights(ffn_up[l], ffn_conv_w[l], ffn_conv_b[l], ffn_down[l])
        if l % 2 == 0:
            lam_init = 0.8 - 0.6 * math.exp(-0.3 * l)
            w_in = ev_w_in[i].astype(BF16)
            w_out = ev_w_out[i].astype(BF16)
            hg = jnp.stack([jnp.tile(a_q_norm[i], 8), jnp.tile(a_k_norm[i], 8),
                            jnp.tile(b_q_norm[i], 8), jnp.tile(b_k_norm[i], 8)])
            ex = jnp.stack([_pad_row(a_lam_q1[i], LANES), _pad_row(a_lam_k1[i], LANES),
                            _pad_row(a_lam_q2[i], LANES), _pad_row(a_lam_k2[i], LANES)])
            gcol = a_subln[i][:, None]
            qa, ka, va, qb, kb, vb = _premix_even(xm, pre_m, w_in, g64, hg, cos64, sin64, True)
            qa_c, ka_c, va_c, qb_c, kb_c, vb_c = _premix_even(xc, pre_c, w_in, g64, hg, dummy_tab, dummy_tab, False)
            oa = _flash(qa, ka, va, ka_c, va_c, ex, gcol, _logit_bound(a_q_norm[i], a_k_norm[i], A_DQK),
                        mode="diff", lam_init=lam_init)
            qk_b = _logit_bound(b_q_norm[i], b_k_norm[i], B_DH)
            ob = _neighbourhood(qb, kb, vb, kb_c, vb_c, _na_bias_table(b_rpb[i], t // GRID_W), qk_b,
                                qk_b + LOG2E * jnp.maximum(jnp.max(b_rpb[i]), 0.0))
            if not last:
                oa_c = _flash(qa_c, ka_c, va_c, None, None, ex, gcol, mode="diff", lam_init=lam_init)
                ob_c = _flash(qb_c, kb_c, vb_c, None, None, ex, gcol, mode="pair")
        else:
            w = od_w_in[i]
            cq, ckv, ckr, dq, dk, dv = jnp.split(w, np.cumsum([C_Q_LORA, C_KV_LORA, C_ROPE, 512, 128]).tolist(), axis=1)
            dup = lambda a: jnp.concatenate([a[:, :64], a[:, :64], a[:, 64:], a[:, 64:]], axis=1)
            ckr_pad = jnp.pad(ckr, ((0, 0), (C_NOPE, HEAD_PAD - C_NOPE - C_ROPE)))
            w_in = jnp.concatenate([cq, ckv, dq, dup(dk), dv, ckr_pad], axis=1).astype(BF16)
            w_out = od_w_out[i].astype(BF16)
            wqb = _pad_heads(c_w_qb[i], C_HEADS, C_NOPE + C_ROPE).astype(BF16)
            kvb = c_w_kvb[i].reshape(C_KV_LORA, C_HEADS, C_NOPE + C_DV)
            wk = _pad_heads(kvb[:, :, :C_NOPE].reshape(C_KV_LORA, -1), C_HEADS, C_NOPE).astype(BF16)
            wv = kvb[:, :, C_NOPE:].reshape(C_KV_LORA, C_HEADS * C_DV).astype(BF16)
            gains = jnp.stack([jnp.tile(_pad_row(c_q_norm[i], HEAD_PAD), C_HEADS),
                               jnp.tile(_pad_row(c_k_norm[i], HEAD_PAD), C_HEADS),
                               _pad_row(jnp.tile(d_q_norm[i], 8), 1024), _pad_row(jnp.tile(d_k_norm[i], 4), 1024)])
            lora = jnp.stack([_pad_row(c_q_a_norm[i], 512), _pad_row(c_kv_a_norm[i], 512)])
            odd = functools.partial(_premix_odd, w=w_in, wqb=wqb, wk=wk, wv=wv, g128=g128, g64=g64, gains=gains, lora=lora)
            mq, mk, mv, gq, gk, gv = odd(xm, pre_m, cos=cos64, sin=sin64, cosc=cosc, sinc=sinc, use_rope=True)
            mq_c, mk_c, mv_c, gq_c, gk_c, gv_c = odd(xc, pre_c, cos=dummy_tab, sin=dummy_tab, cosc=dummy_tab,
                                                     sinc=dummy_tab, use_rope=False)
            ex = jnp.zeros((4, LANES), F32)
            oa = _flash(mq, mk, mv, mk_c, mv_c, ex, jnp.ones((HEAD_PAD, 1), F32),
                        _logit_bound(c_q_norm[i], c_k_norm[i], C_NOPE + C_ROPE), mode="mla")
            ob = _window_gqa(gq, gk, gv, gk_c, gv_c, d_sink[i], _logit_bound(d_q_norm[i], d_k_norm[i], D_DH))
            if not last:
                raise NotImplementedError("context update after an odd layer is not needed at this depth")
        xm = _mix_ffn(oa, ob, xm, w_out, post_m, wup, cv, wdn)
        if not last:
            xc = _mix_ffn(oa_c, ob_c, xc, w_out, post_c, wup, cv, wdn)
    return xm[None]
```

```python
import functools
import math

import numpy as np
import jax
import jax.numpy as jnp
from jax import lax
from jax.experimental import pallas as pl
from jax.experimental.pallas import tpu as pltpu

DEPTH = 2
GRID_W = 64
EPS = 1e-6
ROPE_BASE = 10000.0
NEG_INF = -1e30
LOG2E = math.log2(math.e)

A_DQK = 64
A_DV = 128
B_DH = 64
NA_WIN_H = 8
NA_WIN_W = 16
C_HEADS = 8
C_Q_LORA = 384
C_KV_LORA = 256
C_NOPE = 64
C_ROPE = 32
C_DV = 64
D_HEADS = 8
D_KV_HEADS = 2
D_DH = 64
D_WINDOW = 128

LANES = 128
MXU_TILE = 256
HEAD_PAD = 128
VMEM_LIMIT = 56 * 1024 * 1024
FFN_CHUNK = 256
HALO = 16
NA_ROWS = 4
WG_Q = 256
ONES_ROWS = 16
FLASH_TQ = 4096
FLASH_TQ_ONLINE = 1024
FLASH_TK = 512
FLASH_UNROLL = 4
FLASH_SAFE_BOUND = 40.0

F32 = jnp.float32
BF16 = jnp.bfloat16


def _cparams(*sem):
    return pltpu.CompilerParams(dimension_semantics=sem, vmem_limit_bytes=VMEM_LIMIT)


def _full(shape):
    n = len(shape)
    return pl.BlockSpec(shape, lambda *_: (0,) * n)


def _resident(shape):
    n = len(shape)
    return pl.BlockSpec(shape, lambda *_: (0,) * n, pipeline_mode=pl.Buffered(1))


def _dot(a, b):
    return jnp.dot(a, b, preferred_element_type=F32)


def _rms(x):
    return x * lax.rsqrt(jnp.mean(x * x, axis=-1, keepdims=True) + EPS)


def _sigmoid(x):
    return 1.0 / (1.0 + jnp.exp(-x))


def _group_rms(y, gmat, inv_n, gain):
    yy = (y * y).astype(BF16)
    slab = gmat.shape[0]
    ss = jnp.concatenate([_dot(yy[:, c:c + slab], gmat) for c in range(0, y.shape[-1], slab)], axis=-1)
    return y * lax.rsqrt(ss * inv_n + EPS) * gain


def _rope(y, cos, sin, half, first_half):
    w = y.shape[-1]
    fwd = pltpu.roll(y, w - half, axis=1)
    bwd = pltpu.roll(y, half, axis=1)
    return y * cos + jnp.where(first_half, fwd, bwd) * sin


def _lane_tile(t, width):
    reps = width // t.shape[-1]
    return t if reps == 1 else jnp.concatenate([t] * reps, axis=-1)


def _lane_iota(width):
    return lax.broadcasted_iota(jnp.int32, (1, width), 1)


def _mod_kernel(ct_ref, w_ref, b_ref, o_ref):
    ct = ct_ref[...]
    s = ct * _sigmoid(ct)
    w = w_ref[0]
    rows = [jnp.sum(w * s[:, j:j + 1], axis=0, keepdims=True) for j in range(2)]
    o_ref[0] = jnp.concatenate(rows, axis=0) + b_ref[0]


def _modulation(c, c_ctx, ada_w, ada_b):
    depth, d, n = ada_w.shape
    tn = 1536
    ct = jnp.stack([c[0], c_ctx], axis=1)
    return pl.pallas_call(
        _mod_kernel,
        grid=(depth, n // tn),
        in_specs=[_full((d, 2)),
                  pl.BlockSpec((1, d, tn), lambda l, j: (l, 0, j)),
                  pl.BlockSpec((1, 1, tn), lambda l, j: (l, 0, j))],
        out_specs=pl.BlockSpec((1, 2, tn), lambda l, j: (l, 0, j)),
        out_shape=jax.ShapeDtypeStruct((depth, 2, n), F32),
        compiler_params=_cparams("arbitrary", "arbitrary"),
        name="modulation",
    )(ct, ada_w, ada_b.reshape(depth, 1, n))


def _store_values_t(vt_ref, v, dv):
    tm = v.shape[0]
    vt = v.T.astype(BF16)
    ones = jnp.ones((ONES_ROWS, tm), BF16)
    step = dv + ONES_ROWS
    for g in range(vt_ref.shape[0]):
        for h in range(HEAD_PAD // dv):
            src = g * HEAD_PAD + h * dv
            vt_ref[g, 0, h * step:h * step + dv, :] = vt[src:src + dv]
            vt_ref[g, 0, h * step + dv:(h + 1) * step, :] = ones


def _values_t_shape(t, tm, dv, groups=4):
    return (groups, t // tm, (HEAD_PAD // dv) * (dv + ONES_ROWS), tm)


def _premix_even_kernel(x_ref, vec_ref, w_ref, g_ref, hg_ref, cos_ref, sin_ref,
                        qa_ref, ka_ref, vat_ref, qb_ref, kb_ref, vbt_ref, *, use_rope):
    vec = vec_ref[...]
    h = (_rms(x_ref[...]) * vec[0:1] * (1.0 + vec[1:2]) + vec[2:3]).astype(BF16)
    gm = g_ref[...]
    hg = hg_ref[...]
    sw = 512
    if use_rope:
        cos = _lane_tile(cos_ref[...], sw)
        sin = _lane_tile(sin_ref[...], sw)
        first = (_lane_iota(sw) & 32) == 0

    def seg(i):
        return _dot(h, w_ref[:, i * sw:(i + 1) * sw])

    qa = _group_rms(seg(0), gm, 1.0 / A_DQK, hg[0:1])
    ka = _group_rms(seg(1), gm, 1.0 / A_DQK, hg[1:2])
    if use_rope:
        qa = _rope(qa, cos, sin, 32, first)
        ka = _rope(ka, cos, sin, 32, first)
    qa_ref[...] = (qa * (LOG2E / math.sqrt(A_DQK))).astype(BF16)
    ka_ref[...] = ka.astype(BF16)
    _store_values_t(vat_ref, seg(2), A_DV)
    qb = _group_rms(seg(3), gm, 1.0 / B_DH, hg[2:3])
    qb_ref[...] = (qb * (LOG2E / math.sqrt(B_DH))).astype(BF16)
    kb_ref[...] = _group_rms(seg(4), gm, 1.0 / B_DH, hg[3:4]).astype(BF16)
    _store_values_t(vbt_ref, seg(5), B_DH)


def _premix_even(x, vec, w, gmat, hg, cos, sin, use_rope):
    t, d = x.shape
    tm = min(FLASH_TK, t)
    row = lambda i: (i, 0)
    nat = jax.ShapeDtypeStruct((t, 512), BF16)
    nat_spec = pl.BlockSpec((tm, 512), row)
    vta = _values_t_shape(t, tm, A_DV)
    vtb = _values_t_shape(t, tm, B_DH)
    vt_spec = lambda s: pl.BlockSpec((s[0], 1) + s[2:], lambda i: (0, i, 0, 0))
    return pl.pallas_call(
        functools.partial(_premix_even_kernel, use_rope=use_rope),
        grid=(t // tm,),
        in_specs=[pl.BlockSpec((tm, d), row), _full(vec.shape), _full(w.shape), _full(gmat.shape),
                  _full(hg.shape), pl.BlockSpec((tm, LANES), row), pl.BlockSpec((tm, LANES), row)],
        out_specs=[nat_spec, nat_spec, vt_spec(vta), nat_spec, nat_spec, vt_spec(vtb)],
        out_shape=[nat, nat, jax.ShapeDtypeStruct(vta, BF16), nat, nat, jax.ShapeDtypeStruct(vtb, BF16)],
        compiler_params=_cparams("parallel"),
        name="premix_even",
    )(x, vec, w, gmat, hg, cos, sin)


_O_CQ, _O_CKV, _O_DQ, _O_DK, _O_DV, _O_CKR, _O_END = 0, 384, 640, 1152, 1408, 1536, 1664


def _premix_odd_kernel(x_ref, vec_ref, w_ref, wqb_ref, wk_ref, wv_ref, g128_ref, g64_ref, gain_ref, lora_ref,
                       cos_ref, sin_ref, cosc_ref, sinc_ref,
                       mq_ref, mk_ref, mvt_ref, gq_ref, gk_ref, gvt_ref, *, use_rope):
    vec = vec_ref[...]
    h = (_rms(x_ref[...]) * vec[0:1] * (1.0 + vec[1:2]) + vec[2:3]).astype(BF16)
    gains = gain_ref[...]
    lora = lora_ref[...]
    g128 = g128_ref[...]
    g64 = g64_ref[...]
    mw = C_HEADS * HEAD_PAD
    if use_rope:
        cosc = _lane_tile(cosc_ref[...], mw)
        sinc = _lane_tile(sinc_ref[...], mw)
        first_c = (_lane_iota(mw) & 127) < (C_NOPE + C_ROPE // 2)
        cos = _lane_tile(cos_ref[...], 512)
        sin = _lane_tile(sin_ref[...], 512)
        first_d = (_lane_iota(512) & 32) == 0

    def seg(a, b):
        return _dot(h, w_ref[:, a:b])

    inv_c = 1.0 / (C_NOPE + C_ROPE)
    cq = (_rms(seg(_O_CQ, _O_CKV)) * lora[0:1, :C_Q_LORA]).astype(BF16)
    mq = _group_rms(_dot(cq, wqb_ref[...]), g128, inv_c, gains[0:1])
    if use_rope:
        mq = _rope(mq, cosc, sinc, C_ROPE // 2, first_c)
    mq_ref[...] = (mq * (LOG2E * math.sqrt(inv_c))).astype(BF16)
    ckv = (_rms(seg(_O_CKV, _O_DQ)) * lora[1:2, :C_KV_LORA]).astype(BF16)
    mk = _dot(ckv, wk_ref[...]) + _lane_tile(seg(_O_CKR, _O_END), mw)
    mk = _group_rms(mk, g128, inv_c, gains[1:2])
    if use_rope:
        mk = _rope(mk, cosc, sinc, C_ROPE // 2, first_c)
    mk_ref[...] = mk.astype(BF16)
    _store_values_t(mvt_ref, _dot(ckv, wv_ref[...]), C_DV)
    gq = _group_rms(seg(_O_DQ, _O_DK), g64, 1.0 / D_DH, gains[2:3, :512])
    gk = _group_rms(seg(_O_DK, _O_DV), g64, 1.0 / D_DH, gains[3:4, :256])
    if use_rope:
        gq = _rope(gq, cos, sin, 32, first_d)
        gk = _rope(gk, cos[:, :256], sin[:, :256], 32, first_d[:, :256])
    gq_ref[...] = (gq * (LOG2E / math.sqrt(D_DH))).astype(BF16)
    gk_ref[...] = gk.astype(BF16)
    _store_values_t(gvt_ref, seg(_O_DV, _O_CKR), D_DH)


def _premix_odd(x, vec, w, wqb, wk, wv, g128, g64, gains, lora, cos, sin, cosc, sinc, use_rope):
    t, d = x.shape
    tm = min(FLASH_TK, t)
    row = lambda i: (i, 0)
    tab = pl.BlockSpec((tm, LANES), row)
    nat = lambda n: (pl.BlockSpec((tm, n), row), jax.ShapeDtypeStruct((t, n), BF16))
    def vt(shape):
        return pl.BlockSpec((shape[0], 1) + shape[2:], lambda i: (0, i, 0, 0)), jax.ShapeDtypeStruct(shape, BF16)

    outs = [nat(1024), nat(1024), vt(_values_t_shape(t, tm, C_DV)), nat(512), nat(256),
            vt(_values_t_shape(t, tm, D_DH, groups=1))]
    return pl.pallas_call(
        functools.partial(_premix_odd_kernel, use_rope=use_rope),
        grid=(t // tm,),
        in_specs=[pl.BlockSpec((tm, d), row)] + [_full(a.shape) for a in (vec, w, wqb, wk, wv, g128, g64, gains, lora)]
                 + [tab] * 4,
        out_specs=[o[0] for o in outs],
        out_shape=[o[1] for o in outs],
        compiler_params=_cparams("parallel"),
        name="premix_odd",
    )(x, vec, w, wqb, wk, wv, g128, g64, gains, lora, cos, sin, cosc, sinc)


def _flash_kernel(*refs, mode, has_ctx, online, lam_init):
    bound_ref, q_ref, k_ref, vt_ref = refs[:4]
    kc_ref, vct_ref = refs[4:6] if has_ctx else (None, None)
    ex_ref, gcol_ref, o_ref, acc_ref, stage_ref = refs[6:11] if has_ctx else refs[4:9]
    qt = q_ref[...].astype(F32).T.astype(BF16)
    row = lax.broadcasted_iota(jnp.int32, (HEAD_PAD, 1), 0)
    if mode == "mla":
        qs = [qt[:HEAD_PAD], qt[HEAD_PAD:]]
    else:
        zero = jnp.zeros_like(qt)
        qs = [jnp.where(row < 64, qt, zero), jnp.where(row >= 64, qt, zero)]
    nk, tk = vt_ref.shape[1], vt_ref.shape[3]

    acc_ref[...] = jnp.zeros(acc_ref.shape, F32)
    rows_b = acc_ref.shape[1]

    def scores(kblk, b):
        kb = kblk[:, b * HEAD_PAD:(b + 1) * HEAD_PAD] if mode == "mla" else kblk
        return _dot(kb, qs[b])

    def values(vtblk, b):
        return vtblk if vtblk.shape[0] == rows_b else vtblk[b * rows_b:(b + 1) * rows_b]

    if online:
        m_ref, mx_ref = refs[-2:]
        m_ref[...] = jnp.full(m_ref.shape, NEG_INF, F32)

        def absorb(s, smax, vtblk, b):
            m_prev = m_ref[b]
            m_new = jnp.maximum(m_prev, smax)
            alpha = jnp.exp2(m_prev - m_new)
            p = jnp.exp2(s - m_new).astype(BF16)
            acc_ref[b] = alpha * acc_ref[b] + _dot(values(vtblk, b), p)
            m_ref[b] = m_new

        def produce(slot, kblk):
            for b in range(2):
                s = scores(kblk, b)
                stage_ref[slot, b] = s
                mx_ref[slot, b] = jnp.max(s, axis=0, keepdims=True)

        def consume(slot, vtblk):
            for b in range(2):
                absorb(stage_ref[slot, b], mx_ref[slot, b], vtblk, b)

        def direct(kblk, vtblk):
            for b in range(2):
                s = scores(kblk, b)
                absorb(s, jnp.max(s, axis=0, keepdims=True), vtblk, b)
    else:
        bound = bound_ref[0]
        dv = rows_b - ONES_ROWS
        vpu_den = dv == HEAD_PAD
        lpart_ref = refs[-1] if vpu_den else None

        def probs(kblk, b):
            p = jnp.exp2(scores(kblk, b) - bound)
            return p.astype(BF16), (jnp.sum(p, axis=0, keepdims=True) if vpu_den else None)

        def accumulate(b, vtblk, p, psum):
            if vpu_den:
                acc_ref[b, :dv] += _dot(values(vtblk, b)[:dv], p)
                acc_ref[b, dv:dv + 1] += psum
            else:
                acc_ref[b] += _dot(values(vtblk, b), p)

        def produce(slot, kblk):
            for b in range(2):
                p, psum = probs(kblk, b)
                stage_ref[slot, b] = p
                if vpu_den:
                    lpart_ref[slot, b] = psum

        def consume(slot, vtblk):
            for b in range(2):
                accumulate(b, vtblk, stage_ref[slot, b], lpart_ref[slot, b] if vpu_den else None)

        def direct(kblk, vtblk):
            for b in range(2):
                accumulate(b, vtblk, *probs(kblk, b))

    if nk == 1:
        if has_ctx:
            direct(kc_ref[...], vct_ref[0])
        direct(k_ref[...], vt_ref[0, 0])
    else:
        unroll = FLASH_UNROLL if nk % FLASH_UNROLL == 0 else 2
        assert nk % unroll == 0
        produce(0, k_ref[0:tk, :])
        if has_ctx:
            direct(kc_ref[...], vct_ref[0])

        def body(i, carry):
            c0 = unroll * i
            for u in range(unroll):
                nxt = jnp.minimum(c0 + u + 1, nk - 1)
                produce((u + 1) % 2, k_ref[pl.ds(pl.multiple_of(nxt * tk, tk), tk), :])
                consume(u % 2, vt_ref[0, c0 + u])
            return carry

        lax.fori_loop(0, nk // unroll, body, 0)

    dv = rows_b - ONES_ROWS
    o0 = acc_ref[0, :dv] / acc_ref[0, dv:dv + 1]
    o1 = acc_ref[1, :dv] / acc_ref[1, dv:dv + 1]
    if mode == "diff":
        ex = ex_ref[...]
        lam = (jnp.exp(jnp.sum(ex[0:1] * ex[1:2], axis=-1, keepdims=True))
               - jnp.exp(jnp.sum(ex[2:3] * ex[3:4], axis=-1, keepdims=True)) + lam_init)
        o = o0 - lam * o1
        o = o * lax.rsqrt(jnp.mean(o * o, axis=0, keepdims=True) + EPS) * gcol_ref[...] * (1.0 - lam_init)
    else:
        o = jnp.concatenate([o0, o1], axis=0)
    o_ref[...] = o.T.astype(o_ref.dtype)


def _flash(q, k, vt, kc, vct, ex, gcol, bound=None, *, mode, lam_init=0.0):
    t = q.shape[0]
    tkeys = k.shape[0]
    qw = 2 * HEAD_PAD if mode == "mla" else HEAD_PAD
    groups = q.shape[1] // qw
    _, nk, vrows, tk = vt.shape
    assert nk * tk == tkeys
    has_ctx = kc is not None
    dv = HEAD_PAD if mode == "diff" else HEAD_PAD // 2
    rows_b = dv + ONES_ROWS
    bound_arr = jnp.zeros((1,), F32) if bound is None else jnp.reshape(bound, (1,)).astype(F32)
    args = [bound_arr, q, k, vt]
    if has_ctx:
        lc = kc.shape[0]
        args += [kc, vct.reshape(groups, vrows, lc)]
    args += [ex, gcol]

    def call(online):
        tq = min(FLASH_TQ_ONLINE if online else FLASH_TQ, t)
        in_specs = [pl.BlockSpec(memory_space=pltpu.SMEM),
                    pl.BlockSpec((tq, qw), lambda g, i: (i, g)),
                    pl.BlockSpec((tkeys, qw), lambda g, i: (0, g), pipeline_mode=pl.Buffered(1)),
                    pl.BlockSpec((1, nk, vrows, tk), lambda g, i: (g, 0, 0, 0), pipeline_mode=pl.Buffered(1))]
        if has_ctx:
            in_specs += [pl.BlockSpec((lc, qw), lambda g, i: (0, g)),
                         pl.BlockSpec((1, vrows, lc), lambda g, i: (g, 0, 0))]
        in_specs += [_full(ex.shape), _full(gcol.shape)]
        acc = pltpu.VMEM((2, rows_b, tq), F32)
        if online:
            scratch = [acc, pltpu.VMEM((2, 2, tk, tq), F32), pltpu.VMEM((2, 1, tq), F32), pltpu.VMEM((2, 2, 1, tq), F32)]
        else:
            scratch = [acc, pltpu.VMEM((2, 2, tk, tq), BF16)]
            if dv == HEAD_PAD:
                scratch.append(pltpu.VMEM((2, 2, 1, tq), F32))
        return pl.pallas_call(
            functools.partial(_flash_kernel, mode=mode, has_ctx=has_ctx, online=online, lam_init=lam_init),
            grid=(groups, t // tq),
            in_specs=in_specs,
            out_specs=pl.BlockSpec((tq, HEAD_PAD), lambda g, i: (i, g)),
            out_shape=jax.ShapeDtypeStruct((t, groups * HEAD_PAD), BF16),
            scratch_shapes=scratch,
            compiler_params=_cparams("parallel", "parallel"),
            name="flash_" + mode + ("_online" if online else "_bounded"),
        )(*args)

    if bound is None:
        return call(True)
    return lax.cond(bound <= FLASH_SAFE_BOUND, lambda: call(False), lambda: call(True))


def _na_kernel(shift_ref, q_ref, kp_ref, kcur_ref, kn_ref, vtp_ref, vtcur_ref, vtn_ref, kc_ref, vct_ref, bias_ref,
               o_ref, stage_ref, *stat_refs, online):
    pairs = q_ref.shape[1] // HEAD_PAD
    qn = q_ref.shape[0]
    lc = kc_ref.shape[0]
    step = B_DH + ONES_ROWS
    row = lax.broadcasted_iota(jnp.int32, (HEAD_PAD, 1), 0)

    def produce(slot, pr):
        ps = slice(pr * HEAD_PAD, (pr + 1) * HEAD_PAD)
        qt = q_ref[:, ps].astype(F32).T.astype(BF16)
        zero = jnp.zeros_like(qt)
        q2 = jnp.concatenate([jnp.where(row < 64, qt, zero), jnp.where(row >= 64, qt, zero)], axis=1)
        k_nb = jnp.concatenate([kp_ref[:, ps], kcur_ref[:, ps], kn_ref[:, ps]], axis=0)
        bias = jnp.concatenate([bias_ref[0, 2 * pr], bias_ref[0, 2 * pr + 1]], axis=1)
        s_ctx = _dot(kc_ref[:, ps], q2)
        s_nb = _dot(k_nb, q2) + bias
        if online:
            stage_ref[slot, 0:lc] = s_ctx
            stage_ref[slot, lc:] = s_nb
            stat_refs[0][slot] = jnp.maximum(jnp.max(s_ctx, axis=0, keepdims=True),
                                             jnp.max(s_nb, axis=0, keepdims=True))
        else:
            shift = shift_ref[0]
            stage_ref[slot, 0:lc] = jnp.exp2(s_ctx - shift).astype(BF16)
            stage_ref[slot, lc:] = jnp.exp2(s_nb - shift).astype(BF16)

    def consume(slot, pr):
        vt = jnp.concatenate([vct_ref[pr, 0], vtp_ref[pr, 0], vtcur_ref[pr, 0], vtn_ref[pr, 0]], axis=1)
        if online:
            p = jnp.exp2(stage_ref[slot] - stat_refs[0][slot]).astype(BF16)
        else:
            p = stage_ref[slot]
        outs = []
        for b in range(2):
            acc = _dot(vt[b * step:(b + 1) * step], p[:, b * qn:(b + 1) * qn])
            outs.append(acc[:B_DH] / acc[B_DH:B_DH + 1])
        o_ref[:, pr * HEAD_PAD:(pr + 1) * HEAD_PAD] = jnp.concatenate(outs, axis=0).T.astype(o_ref.dtype)

    produce(0, 0)
    for pr in range(pairs):
        if pr + 1 < pairs:
            produce((pr + 1) % 2, pr + 1)
        consume(pr % 2, pr)


def _na_bias_table(rpb, rows):
    nr, w = NA_ROWS, GRID_W
    heads = rpb.shape[0]
    pad = w - NA_WIN_W
    padded = jnp.pad(rpb.astype(F32) * LOG2E, ((0, 0), (0, 0), (pad, pad + 1)))
    skew = jnp.tile(padded, (1, 1, w))[:, :, :w * (2 * w - 1)].reshape(heads, -1, w, 2 * w - 1)
    toep = skew[:, :, :, w - 1:]
    cq = np.arange(w)
    cs = np.clip(cq - NA_WIN_W // 2, 0, w - NA_WIN_W)
    valid_c = (cq[None, :] >= cs[:, None]) & (cq[None, :] < cs[:, None] + NA_WIN_W)
    toep_t = jnp.swapaxes(jnp.where(jnp.asarray(valid_c), toep, NEG_INF), -1, -2)
    d0 = NA_WIN_H - 1 - nr
    assert d0 - (nr - 1) >= 0 and d0 + 3 * nr <= 2 * NA_WIN_H - 1
    dense = jnp.concatenate([toep_t[:, d0 - rl:d0 - rl + 3 * nr] for rl in range(nr)], axis=-1)
    wh = min(NA_WIN_H, rows)
    rl = np.arange(nr)
    rr_rel = np.arange(3 * nr)
    valid_all = []
    for base in (0, nr, rows - nr):
        rs = np.clip(base + rl - wh // 2, 0, rows - wh)
        rr = base - nr + rr_rel
        valid_all.append((rr[:, None] >= rs[None, :]) & (rr[:, None] < rs[None, :] + wh))
    valid_r = np.repeat(np.stack(valid_all), w, axis=-1)[:, None, :, None, :]
    tab = jnp.where(jnp.asarray(valid_r), dense[None], NEG_INF)
    return tab.reshape(3, heads, 3 * nr * w, nr * w)


def _neighbourhood(q, k, vt, kc, vct, bias, qk_bound, shift):
    t = q.shape[0]
    qn = NA_ROWS * GRID_W
    nb = t // qn
    lc = kc.shape[0]
    pairs, _, vrows, tk = vt.shape
    per = tk // qn
    cur = lambda i: (i, 0)
    prev = lambda i: (jnp.maximum(i - 1, 0), 0)
    nxt = lambda i: (jnp.minimum(i + 1, nb - 1), 0)
    blk = lambda f: pl.BlockSpec((qn, q.shape[1]), f)
    vblk = lambda f: pl.BlockSpec((pairs, 1, vrows, qn), lambda i: (0, f(i)[0] // per, 0, f(i)[0] % per))
    case = lambda i: (jnp.where(i == 0, 0, jnp.where(i == nb - 1, 2, 1)), 0, 0, 0)
    nkeys = lc + 3 * qn

    def call(online):
        scratch = ([pltpu.VMEM((2, nkeys, 2 * qn), F32), pltpu.VMEM((2, 1, 2 * qn), F32)] if online
                   else [pltpu.VMEM((2, nkeys, 2 * qn), BF16)])
        return pl.pallas_call(
            functools.partial(_na_kernel, online=online),
            grid=(nb,),
            in_specs=[pl.BlockSpec(memory_space=pltpu.SMEM), blk(cur), blk(prev), blk(cur), blk(nxt),
                      vblk(prev), vblk(cur), vblk(nxt),
                      _full(kc.shape), _full(vct.shape), pl.BlockSpec((1,) + bias.shape[1:], case)],
            out_specs=blk(cur),
            out_shape=jax.ShapeDtypeStruct(q.shape, BF16),
            scratch_shapes=scratch,
            compiler_params=_cparams("arbitrary"),
            name="neighbourhood" + ("_online" if online else "_bounded"),
        )(jnp.reshape(shift, (1,)).astype(F32), q, k, k, k, vt, vt, vt, kc, vct, bias)

    return lax.cond(shift + qk_bound <= 2 * FLASH_SAFE_BOUND, lambda: call(False), lambda: call(True))


def _wgqa_kernel(shift_ref, sink_ref, q_ref, *refs, online):
    nkb = WG_Q // D_WINDOW + 2
    k_refs, vt_refs = refs[:nkb], refs[nkb:2 * nkb]
    kc_ref, vct_ref, mask_ref, o_ref, stage_ref = refs[2 * nkb:2 * nkb + 5]
    mx_ref = refs[-1]
    qn = q_ref.shape[0]
    pairs = D_HEADS // 2
    pairs_per_kv = pairs // D_KV_HEADS
    step = D_DH + ONES_ROWS
    row = lax.broadcasted_iota(jnp.int32, (HEAD_PAD, 1), 0)
    mask = jnp.concatenate([mask_ref[0]] * 2, axis=1)

    def produce(slot, pair):
        kv = pair // pairs_per_kv
        ks = slice(kv * HEAD_PAD, (kv + 1) * HEAD_PAD)
        qt = q_ref[:, pair * HEAD_PAD:(pair + 1) * HEAD_PAD].astype(F32).T.astype(BF16)
        zero = jnp.zeros_like(qt)
        q2 = jnp.concatenate([jnp.where(row < 64, qt, zero), jnp.where(row >= 64, qt, zero)], axis=1)
        kk = jnp.concatenate([kc_ref[:, ks]] + [r[:, ks] for r in k_refs], axis=0)
        s = _dot(kk, q2) + mask
        if online:
            stage_ref[slot] = s
            mx_ref[slot] = jnp.max(s, axis=0, keepdims=True)
        else:
            stage_ref[slot] = jnp.exp2(s - shift_ref[0]).astype(BF16)

    def consume(slot, pair):
        kv = pair // pairs_per_kv
        sk = jnp.concatenate([jnp.full((1, qn), sink_ref[2 * pair + b] * LOG2E, F32) for b in range(2)], axis=1)
        if online:
            m = jnp.maximum(mx_ref[slot], sk)
            p = jnp.exp2(stage_ref[slot] - m).astype(BF16)
        else:
            m = shift_ref[0]
            p = stage_ref[slot]
        vt = jnp.concatenate([vct_ref[0, 0]] + [r[0, 0] for r in vt_refs], axis=1)[kv * step:(kv + 1) * step]
        acc = _dot(vt, p)
        o = acc[:D_DH] / (acc[D_DH:D_DH + 1] + jnp.exp2(sk - m))
        both = jnp.concatenate([o[:, :qn], o[:, qn:]], axis=0)
        o_ref[:, pair * HEAD_PAD:(pair + 1) * HEAD_PAD] = both.T.astype(o_ref.dtype)

    produce(0, 0)
    for pair in range(pairs):
        if pair + 1 < pairs:
            produce((pair + 1) % 2, pair + 1)
        consume(pair % 2, pair)


def _wgqa_mask(lc):
    nkeys = WG_Q + 2 * D_WINDOW
    kb = np.arange(nkeys)[:, None]
    qi = np.arange(WG_Q)[None, :]
    band = np.abs(kb - D_WINDOW - qi) <= D_WINDOW
    cases = [band & (kb >= D_WINDOW), band, band & (kb < nkeys - D_WINDOW)]
    tab = np.where(np.stack(cases), 0.0, NEG_INF).astype(np.float32)
    return jnp.asarray(np.concatenate([np.zeros((3, lc, WG_Q), np.float32), tab], axis=1))


def _window_gqa(q, k, vt, kc, vct, sink, qk_bound):
    t = q.shape[0]
    nb = t // WG_Q
    per_q = WG_Q // D_WINDOW
    nkb = t // D_WINDOW
    lc = kc.shape[0]
    kw = k.shape[1]
    _, _, vrows, tk = vt.shape
    per_chunk = tk // D_WINDOW
    first_blk = [lambda i, j=j: jnp.clip(i * per_q - 1 + j, 0, nkb - 1) for j in range(per_q + 2)]
    kspecs = [pl.BlockSpec((D_WINDOW, kw), lambda i, f=f: (f(i), 0)) for f in first_blk]
    vspecs = [pl.BlockSpec((1, 1, vrows, D_WINDOW), lambda i, f=f: (0, f(i) // per_chunk, 0, f(i) % per_chunk))
              for f in first_blk]
    mask = _wgqa_mask(lc)
    case = lambda i: (jnp.where(i == 0, 0, jnp.where(i == nb - 1, 2, 1)), 0, 0)
    cur = lambda i: (i, 0)
    nkeys = lc + WG_Q + 2 * D_WINDOW
    lanes = 2 * WG_Q
    smem = pl.BlockSpec(memory_space=pltpu.SMEM)
    shift = jnp.maximum(qk_bound, LOG2E * jnp.max(sink))

    def call(online):
        stage = pltpu.VMEM((2, nkeys, lanes), F32 if online else BF16)
        return pl.pallas_call(
            functools.partial(_wgqa_kernel, online=online),
            grid=(nb,),
            in_specs=[smem, smem, pl.BlockSpec((WG_Q, q.shape[1]), cur)] + kspecs + vspecs
                     + [_full(kc.shape), _full(vct.shape), pl.BlockSpec((1,) + mask.shape[1:], case)],
            out_specs=pl.BlockSpec((WG_Q, q.shape[1]), cur),
            out_shape=jax.ShapeDtypeStruct(q.shape, BF16),
            scratch_shapes=[stage, pltpu.VMEM((2, 1, lanes), F32)],
            compiler_params=_cparams("parallel"),
            name="window_gqa" + ("_online" if online else "_bounded"),
        )(jnp.reshape(shift, (1,)).astype(F32), sink, q, *([k] * (per_q + 2)), *([vt] * (per_q + 2)), kc, vct, mask)

    return lax.cond(shift + qk_bound <= 2 * FLASH_SAFE_BOUND, lambda: call(False), lambda: call(True))


def _mix_ffn_kernel(o1_ref, o1p_ref, o1n_ref, o2_ref, o2p_ref, o2n_ref, x_ref, xp_ref, xn_ref,
                    wout_ref, vec_ref, wup_ref, cv_ref, wdn_ref, o_ref, acc_ref, hcat_ref, u_ref, x1_ref):
    i = pl.program_id(0)
    n = pl.num_programs(0)
    tm = x_ref.shape[0]
    rows = tm + 2 * HALO
    nchunk = wdn_ref.shape[0] // FFN_CHUNK
    half = o1_ref.shape[1]
    vec = vec_ref[...]
    o1 = jnp.concatenate([o1p_ref[...], o1_ref[...], o1n_ref[...]], axis=0)
    o2 = jnp.concatenate([o2p_ref[...], o2_ref[...], o2n_ref[...]], axis=0)
    x = jnp.concatenate([xp_ref[...], x_ref[...], xn_ref[...]], axis=0)
    x1 = x + vec[0:1] * (_dot(o1, wout_ref[:half, :]) + _dot(o2, wout_ref[half:, :]))
    x1_ref[...] = x1[HALO:HALO + tm]
    hcat_ref[...] = (_rms(x1) * vec[1:2] * (1.0 + vec[2:3]) + vec[3:4]).astype(BF16)

    @pl.when(i == 0)
    def _():
        hcat_ref[0:HALO] = jnp.zeros((HALO, hcat_ref.shape[1]), BF16)

    @pl.when(i == n - 1)
    def _():
        hcat_ref[HALO + tm:rows] = jnp.zeros((HALO, hcat_ref.shape[1]), BF16)

    acc_ref[...] = jnp.zeros(acc_ref.shape, F32)

    def conv(u, cv):
        before = pltpu.roll(u, 1, axis=0)
        after = pltpu.roll(u, rows - 1, axis=0)
        r = cv[3:4] + before * cv[0:1] + u * cv[1:2] + after * cv[2:3]
        return r[HALO:HALO + tm]

    def cols(c, gate):
        return pl.ds(pl.multiple_of((gate * nchunk + c) * FFN_CHUNK, FFN_CHUNK), FFN_CHUNK)

    def stage_up(slot, c):
        hcat = hcat_ref[...]
        u_ref[slot, 0] = _dot(hcat, wup_ref[:, cols(c, 0)])
        u_ref[slot, 1] = _dot(hcat, wup_ref[:, cols(c, 1)])

    def stage_down(slot, c):
        a = conv(u_ref[slot, 0], cv_ref[:, cols(c, 0)])
        g = conv(u_ref[slot, 1], cv_ref[:, cols(c, 1)])
        act = (a * (g * _sigmoid(g))).astype(BF16)
        acc_ref[...] += _dot(act, wdn_ref[cols(c, 0), :])

    assert nchunk % 2 == 1
    stage_up(0, 0)

    def body(j, carry):
        c = 2 * j
        stage_up(1, c + 1)
        stage_down(0, c)
        stage_up(0, c + 2)
        stage_down(1, c + 1)
        return carry

    lax.fori_loop(0, nchunk // 2, body, 0)
    stage_down(0, nchunk - 1)
    o_ref[...] = x1_ref[...] + vec[4:5] * acc_ref[...]


def _mix_ffn(o1, o2, x, wout, vec, wup, cv, wdn):
    t, d = x.shape
    tm = min(512, t)
    nt = t // tm
    hb = tm // HALO
    row = lambda i: (i, 0)
    prev = lambda i: (jnp.maximum(i * hb - 1, 0), 0)
    nxt = lambda i: (jnp.minimum((i + 1) * hb, t // HALO - 1), 0)
    tiled = lambda w: [pl.BlockSpec((tm, w), row), pl.BlockSpec((HALO, w), prev), pl.BlockSpec((HALO, w), nxt)]
    return pl.pallas_call(
        _mix_ffn_kernel,
        grid=(nt,),
        in_specs=tiled(o1.shape[1]) + tiled(o2.shape[1]) + tiled(d)
                 + [_resident(a.shape) for a in (wout, vec, wup, cv, wdn)],
        out_specs=pl.BlockSpec((tm, d), row),
        out_shape=jax.ShapeDtypeStruct((t, d), F32),
        scratch_shapes=[pltpu.VMEM((tm, d), F32), pltpu.VMEM((tm + 2 * HALO, d), BF16),
                        pltpu.VMEM((2, 2, tm + 2 * HALO, FFN_CHUNK), F32), pltpu.VMEM((tm, d), F32)],
        compiler_params=_cparams("parallel"),
        name="mix_ffn",
    )(o1, o1, o1, o2, o2, o2, x, x, x, wout, vec, wup, cv, wdn)


def _axial_tables(n_tokens, dim):
    t = jnp.arange(n_tokens, dtype=jnp.int32)
    row = (t // GRID_W).astype(F32)
    col = (t % GRID_W).astype(F32)
    quarter = dim // 4
    inv_freq = ROPE_BASE ** (-jnp.arange(quarter, dtype=F32) / quarter)
    ang = jnp.concatenate([row[:, None] * inv_freq, col[:, None] * inv_freq], axis=-1)
    return jnp.cos(ang), jnp.sin(ang)


def _rope_tables_head64(t):
    cos, sin = _axial_tables(t, 64)
    return jnp.tile(jnp.concatenate([cos, cos], -1), (1, 2)), jnp.tile(jnp.concatenate([-sin, sin], -1), (1, 2))


def _rope_tables_latent(t):
    cos, sin = _axial_tables(t, C_ROPE)
    ones = jnp.ones((t, C_NOPE), F32)
    tail = jnp.ones((t, HEAD_PAD - C_NOPE - C_ROPE), F32)
    cosc = jnp.concatenate([ones, cos, cos, tail], -1)
    sinc = jnp.concatenate([0 * ones, -sin, sin, 0 * tail], -1)
    return cosc, sinc


def _group_ones(width, group):
    return jnp.asarray(np.kron(np.eye(width // group), np.ones((group, group))), BF16)


def _pad_heads(a, heads, dim):
    a = a.reshape(a.shape[:-1] + (heads, dim))
    a = jnp.pad(a, [(0, 0)] * (a.ndim - 1) + [(0, HEAD_PAD - dim)])
    return a.reshape(a.shape[:-2] + (heads * HEAD_PAD,))


def _logit_bound(gq, gk, dim):
    return 1.02 * LOG2E * math.sqrt(dim) * jnp.max(jnp.abs(gq)) * jnp.max(jnp.abs(gk))


def _pad_row(v, width):
    return jnp.pad(v, (0, width - v.shape[0]))


def _ffn_weights(w_up, conv_w, conv_b, w_down):
    cv = jnp.concatenate([conv_w, conv_b[None]], axis=0)
    return w_up.astype(BF16), cv, w_down.astype(BF16)


def kernel(x, c, ctx, c_ctx, ada_w, ada_b, norm_mix, norm_ffn, ffn_up, ffn_conv_w, ffn_conv_b, ffn_down, ev_w_in, ev_w_out, a_q_norm, a_k_norm, a_lam_q1, a_lam_k1, a_lam_q2, a_lam_k2, a_subln, b_q_norm, b_k_norm, b_rpb, od_w_in, od_w_out, c_q_a_norm, c_w_qb, c_kv_a_norm, c_w_kvb, c_q_norm, c_k_norm, d_q_norm, d_k_norm, d_sink):
    assert x.shape[0] == 1 and ctx.shape[0] == 1
    xm = x[0]
    xc = ctx[0]
    t, d = xm.shape
    lc = xc.shape[0]
    mods = _modulation(c, c_ctx, ada_w, ada_b)
    cos64, sin64 = _rope_tables_head64(t)
    cosc, sinc = _rope_tables_latent(t)
    g64 = _group_ones(MXU_TILE, 64)
    g128 = _group_ones(MXU_TILE, HEAD_PAD)
    dummy_tab = jnp.zeros((lc, LANES), F32)

    for l in range(DEPTH):
        last = l == DEPTH - 1
        i = l // 2
        mm = mods[l, 0].reshape(6, d)
        mc = mods[l, 1].reshape(6, d)
        pre_m = jnp.stack([norm_mix[l], mm[1], mm[0]])
        pre_c = jnp.stack([norm_mix[l], mc[1], mc[0]])
        post_m = jnp.stack([mm[2], norm_ffn[l], mm[4], mm[3], mm[5]])
        post_c = jnp.stack([mc[2], norm_ffn[l], mc[4], mc[3], mc[5]])
        wup, cv, wdn = _ffn_weights(ffn_up[l], ffn_conv_w[l], ffn_conv_b[l], ffn_down[l])
        if l % 2 == 0:
            lam_init = 0.8 - 0.6 * math.exp(-0.3 * l)
            w_in = ev_w_in[i].astype(BF16)
            w_out = ev_w_out[i].astype(BF16)
            hg = jnp.stack([jnp.tile(a_q_norm[i], 8), jnp.tile(a_k_norm[i], 8),
                            jnp.tile(b_q_norm[i], 8), jnp.tile(b_k_norm[i], 8)])
            ex = jnp.stack([_pad_row(a_lam_q1[i], LANES), _pad_row(a_lam_k1[i], LANES),
                            _pad_row(a_lam_q2[i], LANES), _pad_row(a_lam_k2[i], LANES)])
            gcol = a_subln[i][:, None]
            qa, ka, va, qb, kb, vb = _premix_even(xm, pre_m, w_in, g64, hg, cos64, sin64, True)
            qa_c, ka_c, va_c, qb_c, kb_c, vb_c = _premix_even(xc, pre_c, w_in, g64, hg, dummy_tab, dummy_tab, False)
            oa = _flash(qa, ka, va, ka_c, va_c, ex, gcol, _logit_bound(a_q_norm[i], a_k_norm[i], A_DQK),
                        mode="diff", lam_init=lam_init)
            qk_b = _logit_bound(b_q_norm[i], b_k_norm[i], B_DH)
            ob = _neighbourhood(qb, kb, vb, kb_c, vb_c, _na_bias_table(b_rpb[i], t // GRID_W), qk_b,
                                qk_b + LOG2E * jnp.maximum(jnp.max(b_rpb[i]), 0.0))
            if not last:
                oa_c = _flash(qa_c, ka_c, va_c, None, None, ex, gcol, mode="diff", lam_init=lam_init)
                ob_c = _flash(qb_c, kb_c, vb_c, None, None, ex, gcol, mode="pair")
        else:
            w = od_w_in[i]
            cq, ckv, ckr, dq, dk, dv = jnp.split(w, np.cumsum([C_Q_LORA, C_KV_LORA, C_ROPE, 512, 128]).tolist(), axis=1)
            dup = lambda a: jnp.concatenate([a[:, :64], a[:, :64], a[:, 64:], a[:, 64:]], axis=1)
            ckr_pad = jnp.pad(ckr, ((0, 0), (C_NOPE, HEAD_PAD - C_NOPE - C_ROPE)))
            w_in = jnp.concatenate([cq, ckv, dq, dup(dk), dv, ckr_pad], axis=1).astype(BF16)
            w_out = od_w_out[i].astype(BF16)
            wqb = _pad_heads(c_w_qb[i], C_HEADS, C_NOPE + C_ROPE).astype(BF16)
            kvb = c_w_kvb[i].reshape(C_KV_LORA, C_HEADS, C_NOPE + C_DV)
            wk = _pad_heads(kvb[:, :, :C_NOPE].reshape(C_KV_LORA, -1), C_HEADS, C_NOPE).astype(BF16)
            wv = kvb[:, :, C_NOPE:].reshape(C_KV_LORA, C_HEADS * C_DV).astype(BF16)
            gains = jnp.stack([jnp.tile(_pad_row(c_q_norm[i], HEAD_PAD), C_HEADS),
                               jnp.tile(_pad_row(c_k_norm[i], HEAD_PAD), C_HEADS),
                               _pad_row(jnp.tile(d_q_norm[i], 8), 1024), _pad_row(jnp.tile(d_k_norm[i], 4), 1024)])
            lora = jnp.stack([_pad_row(c_q_a_norm[i], 512), _pad_row(c_kv_a_norm[i], 512)])
            odd = functools.partial(_premix_odd, w=w_in, wqb=wqb, wk=wk, wv=wv, g128=g128, g64=g64, gains=gains, lora=lora)
            mq, mk, mv, gq, gk, gv = odd(xm, pre_m, cos=cos64, sin=sin64, cosc=cosc, sinc=sinc, use_rope=True)
            mq_c, mk_c, mv_c, gq_c, gk_c, gv_c = odd(xc, pre_c, cos=dummy_tab, sin=dummy_tab, cosc=dummy_tab,
                                                     sinc=dummy_tab, use_rope=False)
            ex = jnp.zeros((4, LANES), F32)
            oa = _flash(mq, mk, mv, mk_c, mv_c, ex, jnp.ones((HEAD_PAD, 1), F32),
                        _logit_bound(c_q_norm[i], c_k_norm[i], C_NOPE + C_ROPE), mode="mla")
            ob = _window_gqa(gq, gk, gv, gk_c, gv_c, d_sink[i], _logit_bound(d_q_norm[i], d_k_norm[i], D_DH))
            if not last:
                raise NotImplementedError("context update after an odd layer is not needed at this depth")
        xm = _mix_ffn(oa, ob, xm, w_out, post_m, wup, cv, wdn)
        if not last:
            xc = _mix_ffn(oa_c, ob_c, xc, w_out, post_c, wup, cv, wdn)
    return xm[None]
```

```python
import functools
import math

import numpy as np
import jax
import jax.numpy as jnp
from jax import lax
from jax.experimental import pallas as pl
from jax.experimental.pallas import tpu as pltpu

DEPTH = 2
GRID_W = 64
EPS = 1e-6
ROPE_BASE = 10000.0
NEG_INF = -1e30
LOG2E = math.log2(math.e)

A_DQK = 64
A_DV = 128
B_DH = 64
NA_WIN_H = 8
NA_WIN_W = 16
C_HEADS = 8
C_Q_LORA = 384
C_KV_LORA = 256
C_NOPE = 64
C_ROPE = 32
C_DV = 64
D_HEADS = 8
D_KV_HEADS = 2
D_DH = 64
D_WINDOW = 128

LANES = 128
MXU_TILE = 256
HEAD_PAD = 128
VMEM_LIMIT = 56 * 1024 * 1024
FFN_CHUNK = 256
HALO = 16
NA_ROWS = 4
WG_Q = 256
ONES_ROWS = 16
FLASH_TQ = 4096
FLASH_TQ_ONLINE = 1024
FLASH_TK = 512
FLASH_UNROLL = 4
FLASH_SAFE_BOUND = 40.0

F32 = jnp.float32
BF16 = jnp.bfloat16


def _cparams(*sem):
    return pltpu.CompilerParams(dimension_semantics=sem, vmem_limit_bytes=VMEM_LIMIT)


def _full(shape):
    n = len(shape)
    return pl.BlockSpec(shape, lambda *_: (0,) * n)


def _resident(shape):
    n = len(shape)
    return pl.BlockSpec(shape, lambda *_: (0,) * n, pipeline_mode=pl.Buffered(1))


def _dot(a, b):
    return jnp.dot(a, b, preferred_element_type=F32)


def _rms(x):
    return x * lax.rsqrt(jnp.mean(x * x, axis=-1, keepdims=True) + EPS)


def _sigmoid(x):
    return 1.0 / (1.0 + jnp.exp(-x))


def _group_rms(y, gmat, inv_n, gain):
    yy = (y * y).astype(BF16)
    slab = gmat.shape[0]
    ss = jnp.concatenate([_dot(yy[:, c:c + slab], gmat) for c in range(0, y.shape[-1], slab)], axis=-1)
    return y * lax.rsqrt(ss * inv_n + EPS) * gain


def _rope(y, cos, sin, half, first_half):
    w = y.shape[-1]
    fwd = pltpu.roll(y, w - half, axis=1)
    bwd = pltpu.roll(y, half, axis=1)
    return y * cos + jnp.where(first_half, fwd, bwd) * sin


def _lane_tile(t, width):
    reps = width // t.shape[-1]
    return t if reps == 1 else jnp.concatenate([t] * reps, axis=-1)


def _lane_iota(width):
    return lax.broadcasted_iota(jnp.int32, (1, width), 1)


def _mod_kernel(ct_ref, w_ref, b_ref, o_ref):
    ct = ct_ref[...]
    s = ct * _sigmoid(ct)
    w = w_ref[0]
    rows = [jnp.sum(w * s[:, j:j + 1], axis=0, keepdims=True) for j in range(2)]
    o_ref[0] = jnp.concatenate(rows, axis=0) + b_ref[0]


def _modulation(c, c_ctx, ada_w, ada_b):
    depth, d, n = ada_w.shape
    tn = 1536
    ct = jnp.stack([c[0], c_ctx], axis=1)
    return pl.pallas_call(
        _mod_kernel,
        grid=(depth, n // tn),
        in_specs=[_full((d, 2)),
                  pl.BlockSpec((1, d, tn), lambda l, j: (l, 0, j)),
                  pl.BlockSpec((1, 1, tn), lambda l, j: (l, 0, j))],
        out_specs=pl.BlockSpec((1, 2, tn), lambda l, j: (l, 0, j)),
        out_shape=jax.ShapeDtypeStruct((depth, 2, n), F32),
        compiler_params=_cparams("arbitrary", "arbitrary"),
        name="modulation",
    )(ct, ada_w, ada_b.reshape(depth, 1, n))


def _store_values_t(vt_ref, v, dv):
    tm = v.shape[0]
    vt = v.T.astype(BF16)
    ones = jnp.ones((ONES_ROWS, tm), BF16)
    step = dv + ONES_ROWS
    for g in range(vt_ref.shape[0]):
        for h in range(HEAD_PAD // dv):
            src = g * HEAD_PAD + h * dv
            vt_ref[g, 0, h * step:h * step + dv, :] = vt[src:src + dv]
            vt_ref[g, 0, h * step + dv:(h + 1) * step, :] = ones


def _values_t_shape(t, tm, dv, groups=4):
    return (groups, t // tm, (HEAD_PAD // dv) * (dv + ONES_ROWS), tm)


def _premix_even_kernel(x_ref, vec_ref, w_ref, g_ref, hg_ref, cos_ref, sin_ref,
                        qa_ref, ka_ref, vat_ref, qb_ref, kb_ref, vbt_ref, *, use_rope):
    vec = vec_ref[...]
    h = (_rms(x_ref[...]) * vec[0:1] * (1.0 + vec[1:2]) + vec[2:3]).astype(BF16)
    gm = g_ref[...]
    hg = hg_ref[...]
    sw = 512
    if use_rope:
        cos = _lane_tile(cos_ref[...], sw)
        sin = _lane_tile(sin_ref[...], sw)
        first = (_lane_iota(sw) & 32) == 0

    def seg(i):
        return _dot(h, w_ref[:, i * sw:(i + 1) * sw])

    qa = _group_rms(seg(0), gm, 1.0 / A_DQK, hg[0:1])
    ka = _group_rms(seg(1), gm, 1.0 / A_DQK, hg[1:2])
    if use_rope:
        qa = _rope(qa, cos, sin, 32, first)
        ka = _rope(ka, cos, sin, 32, first)
    qa_ref[...] = (qa * (LOG2E / math.sqrt(A_DQK))).astype(BF16)
    ka_ref[...] = ka.astype(BF16)
    _store_values_t(vat_ref, seg(2), A_DV)
    qb = _group_rms(seg(3), gm, 1.0 / B_DH, hg[2:3])
    qb_ref[...] = (qb * (LOG2E / math.sqrt(B_DH))).astype(BF16)
    kb_ref[...] = _group_rms(seg(4), gm, 1.0 / B_DH, hg[3:4]).astype(BF16)
    _store_values_t(vbt_ref, seg(5), B_DH)


def _premix_even(x, vec, w, gmat, hg, cos, sin, use_rope):
    t, d = x.shape
    tm = min(FLASH_TK, t)
    row = lambda i: (i, 0)
    nat = jax.ShapeDtypeStruct((t, 512), BF16)
    nat_spec = pl.BlockSpec((tm, 512), row)
    vta = _values_t_shape(t, tm, A_DV)
    vtb = _values_t_shape(t, tm, B_DH)
    vt_spec = lambda s: pl.BlockSpec((s[0], 1) + s[2:], lambda i: (0, i, 0, 0))
    return pl.pallas_call(
        functools.partial(_premix_even_kernel, use_rope=use_rope),
        grid=(t // tm,),
        in_specs=[pl.BlockSpec((tm, d), row), _full(vec.shape), _full(w.shape), _full(gmat.shape),
                  _full(hg.shape), pl.BlockSpec((tm, LANES), row), pl.BlockSpec((tm, LANES), row)],
        out_specs=[nat_spec, nat_spec, vt_spec(vta), nat_spec, nat_spec, vt_spec(vtb)],
        out_shape=[nat, nat, jax.ShapeDtypeStruct(vta, BF16), nat, nat, jax.ShapeDtypeStruct(vtb, BF16)],
        compiler_params=_cparams("parallel"),
        name="premix_even",
    )(x, vec, w, gmat, hg, cos, sin)


_O_CQ, _O_CKV, _O_DQ, _O_DK, _O_DV, _O_CKR, _O_END = 0, 384, 640, 1152, 1408, 1536, 1664


def _premix_odd_kernel(x_ref, vec_ref, w_ref, wqb_ref, wk_ref, wv_ref, g128_ref, g64_ref, gain_ref, lora_ref,
                       cos_ref, sin_ref, cosc_ref, sinc_ref,
                       mq_ref, mk_ref, mvt_ref, gq_ref, gk_ref, gvt_ref, *, use_rope):
    vec = vec_ref[...]
    h = (_rms(x_ref[...]) * vec[0:1] * (1.0 + vec[1:2]) + vec[2:3]).astype(BF16)
    gains = gain_ref[...]
    lora = lora_ref[...]
    g128 = g128_ref[...]
    g64 = g64_ref[...]
    mw = C_HEADS * HEAD_PAD
    if use_rope:
        cosc = _lane_tile(cosc_ref[...], mw)
        sinc = _lane_tile(sinc_ref[...], mw)
        first_c = (_lane_iota(mw) & 127) < (C_NOPE + C_ROPE // 2)
        cos = _lane_tile(cos_ref[...], 512)
        sin = _lane_tile(sin_ref[...], 512)
        first_d = (_lane_iota(512) & 32) == 0

    def seg(a, b):
        return _dot(h, w_ref[:, a:b])

    inv_c = 1.0 / (C_NOPE + C_ROPE)
    cq = (_rms(seg(_O_CQ, _O_CKV)) * lora[0:1, :C_Q_LORA]).astype(BF16)
    mq = _group_rms(_dot(cq, wqb_ref[...]), g128, inv_c, gains[0:1])
    if use_rope:
        mq = _rope(mq, cosc, sinc, C_ROPE // 2, first_c)
    mq_ref[...] = (mq * (LOG2E * math.sqrt(inv_c))).astype(BF16)
    ckv = (_rms(seg(_O_CKV, _O_DQ)) * lora[1:2, :C_KV_LORA]).astype(BF16)
    mk = _dot(ckv, wk_ref[...]) + _lane_tile(seg(_O_CKR, _O_END), mw)
    mk = _group_rms(mk, g128, inv_c, gains[1:2])
    if use_rope:
        mk = _rope(mk, cosc, sinc, C_ROPE // 2, first_c)
    mk_ref[...] = mk.astype(BF16)
    _store_values_t(mvt_ref, _dot(ckv, wv_ref[...]), C_DV)
    gq = _group_rms(seg(_O_DQ, _O_DK), g64, 1.0 / D_DH, gains[2:3, :512])
    gk = _group_rms(seg(_O_DK, _O_DV), g64, 1.0 / D_DH, gains[3:4, :256])
    if use_rope:
        gq = _rope(gq, cos, sin, 32, first_d)
        gk = _rope(gk, cos[:, :256], sin[:, :256], 32, first_d[:, :256])
    gq_ref[...] = (gq * (LOG2E / math.sqrt(D_DH))).astype(BF16)
    gk_ref[...] = gk.astype(BF16)
    _store_values_t(gvt_ref, seg(_O_DV, _O_CKR), D_DH)


def _premix_odd(x, vec, w, wqb, wk, wv, g128, g64, gains, lora, cos, sin, cosc, sinc, use_rope):
    t, d = x.shape
    tm = min(FLASH_TK, t)
    row = lambda i: (i, 0)
    tab = pl.BlockSpec((tm, LANES), row)
    nat = lambda n: (pl.BlockSpec((tm, n), row), jax.ShapeDtypeStruct((t, n), BF16))
    def vt(shape):
        return pl.BlockSpec((shape[0], 1) + shape[2:], lambda i: (0, i, 0, 0)), jax.ShapeDtypeStruct(shape, BF16)

    outs = [nat(1024), nat(1024), vt(_values_t_shape(t, tm, C_DV)), nat(512), nat(256),
            vt(_values_t_shape(t, tm, D_DH, groups=1))]
    return pl.pallas_call(
        functools.partial(_premix_odd_kernel, use_rope=use_rope),
        grid=(t // tm,),
        in_specs=[pl.BlockSpec((tm, d), row)] + [_full(a.shape) for a in (vec, w, wqb, wk, wv, g128, g64, gains, lora)]
                 + [tab] * 4,
        out_specs=[o[0] for o in outs],
        out_shape=[o[1] for o in outs],
        compiler_params=_cparams("parallel"),
        name="premix_odd",
    )(x, vec, w, wqb, wk, wv, g128, g64, gains, lora, cos, sin, cosc, sinc)


def _flash_kernel(*refs, mode, has_ctx, online, lam_init):
    bound_ref, q_ref, k_ref, vt_ref = refs[:4]
    kc_ref, vct_ref = refs[4:6] if has_ctx else (None, None)
    ex_ref, gcol_ref, o_ref, acc_ref, stage_ref = refs[6:11] if has_ctx else refs[4:9]
    qt = q_ref[...].astype(F32).T.astype(BF16)
    row = lax.broadcasted_iota(jnp.int32, (HEAD_PAD, 1), 0)
    if mode == "mla":
        qs = [qt[:HEAD_PAD], qt[HEAD_PAD:]]
    else:
        zero = jnp.zeros_like(qt)
        qs = [jnp.where(row < 64, qt, zero), jnp.where(row >= 64, qt, zero)]
    nk, tk = vt_ref.shape[1], vt_ref.shape[3]

    acc_ref[...] = jnp.zeros(acc_ref.shape, F32)
    rows_b = acc_ref.shape[1]

    def scores(kblk, b):
        kb = kblk[:, b * HEAD_PAD:(b + 1) * HEAD_PAD] if mode == "mla" else kblk
        return _dot(kb, qs[b])

    def values(vtblk, b):
        return vtblk if vtblk.shape[0] == rows_b else vtblk[b * rows_b:(b + 1) * rows_b]

    if online:
        m_ref, mx_ref = refs[-2:]
        m_ref[...] = jnp.full(m_ref.shape, NEG_INF, F32)

        def absorb(s, smax, vtblk, b):
            m_prev = m_ref[b]
            m_new = jnp.maximum(m_prev, smax)
            alpha = jnp.exp2(m_prev - m_new)
            p = jnp.exp2(s - m_new).astype(BF16)
            acc_ref[b] = alpha * acc_ref[b] + _dot(values(vtblk, b), p)
            m_ref[b] = m_new

        def produce(slot, kblk):
            for b in range(2):
                s = scores(kblk, b)
                stage_ref[slot, b] = s
                mx_ref[slot, b] = jnp.max(s, axis=0, keepdims=True)

        def consume(slot, vtblk):
            for b in range(2):
                absorb(stage_ref[slot, b], mx_ref[slot, b], vtblk, b)

        def direct(kblk, vtblk):
            for b in range(2):
                s = scores(kblk, b)
                absorb(s, jnp.max(s, axis=0, keepdims=True), vtblk, b)
    else:
        bound = bound_ref[0]
        dv = rows_b - ONES_ROWS
        vpu_den = dv == HEAD_PAD
        lpart_ref = refs[-1] if vpu_den else None

        def probs(kblk, b):
            p = jnp.exp2(scores(kblk, b) - bound)
            return p.astype(BF16), (jnp.sum(p, axis=0, keepdims=True) if vpu_den else None)

        def accumulate(b, vtblk, p, psum):
            if vpu_den:
                acc_ref[b, :dv] += _dot(values(vtblk, b)[:dv], p)
                acc_ref[b, dv:dv + 1] += psum
            else:
                acc_ref[b] += _dot(values(vtblk, b), p)

        def produce(slot, kblk):
            for b in range(2):
                p, psum = probs(kblk, b)
                stage_ref[slot, b] = p
                if vpu_den:
                    lpart_ref[slot, b] = psum

        def consume(slot, vtblk):
            for b in range(2):
                accumulate(b, vtblk, stage_ref[slot, b], lpart_ref[slot, b] if vpu_den else None)

        def direct(kblk, vtblk):
            for b in range(2):
                accumulate(b, vtblk, *probs(kblk, b))

    if nk == 1:
        if has_ctx:
            direct(kc_ref[...], vct_ref[0])
        direct(k_ref[...], vt_ref[0, 0])
    else:
        unroll = FLASH_UNROLL if nk % FLASH_UNROLL == 0 else 2
        assert nk % unroll == 0
        produce(0, k_ref[0:tk, :])
        if has_ctx:
            direct(kc_ref[...], vct_ref[0])

        def body(i, carry):
            c0 = unroll * i
            for u in range(unroll):
                nxt = jnp.minimum(c0 + u + 1, nk - 1)
                produce((u + 1) % 2, k_ref[pl.ds(pl.multiple_of(nxt * tk, tk), tk), :])
                consume(u % 2, vt_ref[0, c0 + u])
            return carry

        lax.fori_loop(0, nk // unroll, body, 0)

    dv = rows_b - ONES_ROWS
    o0 = acc_ref[0, :dv] / acc_ref[0, dv:dv + 1]
    o1 = acc_ref[1, :dv] / acc_ref[1, dv:dv + 1]
    if mode == "diff":
        ex = ex_ref[...]
        lam = (jnp.exp(jnp.sum(ex[0:1] * ex[1:2], axis=-1, keepdims=True))
               - jnp.exp(jnp.sum(ex[2:3] * ex[3:4], axis=-1, keepdims=True)) + lam_init)
        o = o0 - lam * o1
        o = o * lax.rsqrt(jnp.mean(o * o, axis=0, keepdims=True) + EPS) * gcol_ref[...] * (1.0 - lam_init)
    else:
        o = jnp.concatenate([o0, o1], axis=0)
    o_ref[...] = o.T.astype(o_ref.dtype)


def _flash(q, k, vt, kc, vct, ex, gcol, bound=None, *, mode, lam_init=0.0):
    t = q.shape[0]
    tkeys = k.shape[0]
    qw = 2 * HEAD_PAD if mode == "mla" else HEAD_PAD
    groups = q.shape[1] // qw
    _, nk, vrows, tk = vt.shape
    assert nk * tk == tkeys
    has_ctx = kc is not None
    dv = HEAD_PAD if mode == "diff" else HEAD_PAD // 2
    rows_b = dv + ONES_ROWS
    bound_arr = jnp.zeros((1,), F32) if bound is None else jnp.reshape(bound, (1,)).astype(F32)
    args = [bound_arr, q, k, vt]
    if has_ctx:
        lc = kc.shape[0]
        args += [kc, vct.reshape(groups, vrows, lc)]
    args += [ex, gcol]

    def call(online):
        tq = min(FLASH_TQ_ONLINE if online else FLASH_TQ, t)
        in_specs = [pl.BlockSpec(memory_space=pltpu.SMEM),
                    pl.BlockSpec((tq, qw), lambda g, i: (i, g)),
                    pl.BlockSpec((tkeys, qw), lambda g, i: (0, g), pipeline_mode=pl.Buffered(1)),
                    pl.BlockSpec((1, nk, vrows, tk), lambda g, i: (g, 0, 0, 0), pipeline_mode=pl.Buffered(1))]
        if has_ctx:
            in_specs += [pl.BlockSpec((lc, qw), lambda g, i: (0, g)),
                         pl.BlockSpec((1, vrows, lc), lambda g, i: (g, 0, 0))]
        in_specs += [_full(ex.shape), _full(gcol.shape)]
        acc = pltpu.VMEM((2, rows_b, tq), F32)
        if online:
            scratch = [acc, pltpu.VMEM((2, 2, tk, tq), F32), pltpu.VMEM((2, 1, tq), F32), pltpu.VMEM((2, 2, 1, tq), F32)]
        else:
            scratch = [acc, pltpu.VMEM((2, 2, tk, tq), BF16)]
            if dv == HEAD_PAD:
                scratch.append(pltpu.VMEM((2, 2, 1, tq), F32))
        return pl.pallas_call(
            functools.partial(_flash_kernel, mode=mode, has_ctx=has_ctx, online=online, lam_init=lam_init),
            grid=(groups, t // tq),
            in_specs=in_specs,
            out_specs=pl.BlockSpec((tq, HEAD_PAD), lambda g, i: (i, g)),
            out_shape=jax.ShapeDtypeStruct((t, groups * HEAD_PAD), BF16),
            scratch_shapes=scratch,
            compiler_params=_cparams("parallel", "parallel"),
            name="flash_" + mode + ("_online" if online else "_bounded"),
        )(*args)

    if bound is None:
        return call(True)
    return lax.cond(bound <= FLASH_SAFE_BOUND, lambda: call(False), lambda: call(True))


def _na_kernel(shift_ref, q_ref, kp_ref, kcur_ref, kn_ref, vtp_ref, vtcur_ref, vtn_ref, kc_ref, vct_ref, bias_ref,
               o_ref, stage_ref, *stat_refs, online):
    pairs = q_ref.shape[1] // HEAD_PAD
    qn = q_ref.shape[0]
    lc = kc_ref.shape[0]
    step = B_DH + ONES_ROWS
    row = lax.broadcasted_iota(jnp.int32, (HEAD_PAD, 1), 0)

    def produce(slot, pr):
        ps = slice(pr * HEAD_PAD, (pr + 1) * HEAD_PAD)
        qt = q_ref[:, ps].astype(F32).T.astype(BF16)
        zero = jnp.zeros_like(qt)
        q2 = jnp.concatenate([jnp.where(row < 64, qt, zero), jnp.where(row >= 64, qt, zero)], axis=1)
        k_nb = jnp.concatenate([kp_ref[:, ps], kcur_ref[:, ps], kn_ref[:, ps]], axis=0)
        bias = jnp.concatenate([bias_ref[0, 2 * pr], bias_ref[0, 2 * pr + 1]], axis=1)
        s_ctx = _dot(kc_ref[:, ps], q2)
        s_nb = _dot(k_nb, q2) + bias
        if online:
            stage_ref[slot, 0:lc] = s_ctx
            stage_ref[slot, lc:] = s_nb
            stat_refs[0][slot] = jnp.maximum(jnp.max(s_ctx, axis=0, keepdims=True),
                                             jnp.max(s_nb, axis=0, keepdims=True))
        else:
            shift = shift_ref[0]
            stage_ref[slot, 0:lc] = jnp.exp2(s_ctx - shift).astype(BF16)
            stage_ref[slot, lc:] = jnp.exp2(s_nb - shift).astype(BF16)

    def consume(slot, pr):
        vt = jnp.concatenate([vct_ref[pr, 0], vtp_ref[pr, 0], vtcur_ref[pr, 0], vtn_ref[pr, 0]], axis=1)
        if online:
            p = jnp.exp2(stage_ref[slot] - stat_refs[0][slot]).astype(BF16)
        else:
            p = stage_ref[slot]
        outs = []
        for b in range(2):
            acc = _dot(vt[b * step:(b + 1) * step], p[:, b * qn:(b + 1) * qn])
            outs.append(acc[:B_DH] / acc[B_DH:B_DH + 1])
        o_ref[:, pr * HEAD_PAD:(pr + 1) * HEAD_PAD] = jnp.concatenate(outs, axis=0).T.astype(o_ref.dtype)

    produce(0, 0)
    for pr in range(pairs):
        if pr + 1 < pairs:
            produce((pr + 1) % 2, pr + 1)
        consume(pr % 2, pr)


def _na_bias_table(rpb, rows):
    nr, w = NA_ROWS, GRID_W
    heads = rpb.shape[0]
    pad = w - NA_WIN_W
    padded = jnp.pad(rpb.astype(F32) * LOG2E, ((0, 0), (0, 0), (pad, pad + 1)))
    skew = jnp.tile(padded, (1, 1, w))[:, :, :w * (2 * w - 1)].reshape(heads, -1, w, 2 * w - 1)
    toep = skew[:, :, :, w - 1:]
    cq = np.arange(w)
    cs = np.clip(cq - NA_WIN_W // 2, 0, w - NA_WIN_W)
    valid_c = (cq[None, :] >= cs[:, None]) & (cq[None, :] < cs[:, None] + NA_WIN_W)
    toep_t = jnp.swapaxes(jnp.where(jnp.asarray(valid_c), toep, NEG_INF), -1, -2)
    d0 = NA_WIN_H - 1 - nr
    assert d0 - (nr - 1) >= 0 and d0 + 3 * nr <= 2 * NA_WIN_H - 1
    dense = jnp.concatenate([toep_t[:, d0 - rl:d0 - rl + 3 * nr] for rl in range(nr)], axis=-1)
    wh = min(NA_WIN_H, rows)
    rl = np.arange(nr)
    rr_rel = np.arange(3 * nr)
    valid_all = []
    for base in (0, nr, rows - nr):
        rs = np.clip(base + rl - wh // 2, 0, rows - wh)
        rr = base - nr + rr_rel
        valid_all.append((rr[:, None] >= rs[None, :]) & (rr[:, None] < rs[None, :] + wh))
    valid_r = np.repeat(np.stack(valid_all), w, axis=-1)[:, None, :, None, :]
    tab = jnp.where(jnp.asarray(valid_r), dense[None], NEG_INF)
    return tab.reshape(3, heads, 3 * nr * w, nr * w)


def _neighbourhood(q, k, vt, kc, vct, bias, qk_bound, shift):
    t = q.shape[0]
    qn = NA_ROWS * GRID_W
    nb = t // qn
    lc = kc.shape[0]
    pairs, _, vrows, tk = vt.shape
    per = tk // qn
    cur = lambda i: (i, 0)
    prev = lambda i: (jnp.maximum(i - 1, 0), 0)
    nxt = lambda i: (jnp.minimum(i + 1, nb - 1), 0)
    blk = lambda f: pl.BlockSpec((qn, q.shape[1]), f)
    vblk = lambda f: pl.BlockSpec((pairs, 1, vrows, qn), lambda i: (0, f(i)[0] // per, 0, f(i)[0] % per))
    case = lambda i: (jnp.where(i == 0, 0, jnp.where(i == nb - 1, 2, 1)), 0, 0, 0)
    nkeys = lc + 3 * qn

    def call(online):
        scratch = ([pltpu.VMEM((2, nkeys, 2 * qn), F32), pltpu.VMEM((2, 1, 2 * qn), F32)] if online
                   else [pltpu.VMEM((2, nkeys, 2 * qn), BF16)])
        return pl.pallas_call(
            functools.partial(_na_kernel, online=online),
            grid=(nb,),
            in_specs=[pl.BlockSpec(memory_space=pltpu.SMEM), blk(cur), blk(prev), blk(cur), blk(nxt),
                      vblk(prev), vblk(cur), vblk(nxt),
                      _full(kc.shape), _full(vct.shape), pl.BlockSpec((1,) + bias.shape[1:], case)],
            out_specs=blk(cur),
            out_shape=jax.ShapeDtypeStruct(q.shape, BF16),
            scratch_shapes=scratch,
            compiler_params=_cparams("arbitrary"),
            name="neighbourhood" + ("_online" if online else "_bounded"),
        )(jnp.reshape(shift, (1,)).astype(F32), q, k, k, k, vt, vt, vt, kc, vct, bias)

    return lax.cond(shift + qk_bound <= 2 * FLASH_SAFE_BOUND, lambda: call(False), lambda: call(True))


def _wgqa_kernel(shift_ref, sink_ref, q_ref, *refs, online):
    nkb = WG_Q // D_WINDOW + 2
    k_refs, vt_refs = refs[:nkb], refs[nkb:2 * nkb]
    kc_ref, vct_ref, mask_ref, o_ref, stage_ref = refs[2 * nkb:2 * nkb + 5]
    mx_ref = refs[-1]
    qn = q_ref.shape[0]
    pairs = D_HEADS // 2
    pairs_per_kv = pairs // D_KV_HEADS
    step = D_DH + ONES_ROWS
    row = lax.broadcasted_iota(jnp.int32, (HEAD_PAD, 1), 0)
    mask = jnp.concatenate([mask_ref[0]] * 2, axis=1)

    def produce(slot, pair):
        kv = pair // pairs_per_kv
        ks = slice(kv * HEAD_PAD, (kv + 1) * HEAD_PAD)
        qt = q_ref[:, pair * HEAD_PAD:(pair + 1) * HEAD_PAD].astype(F32).T.astype(BF16)
        zero = jnp.zeros_like(qt)
        q2 = jnp.concatenate([jnp.where(row < 64, qt, zero), jnp.where(row >= 64, qt, zero)], axis=1)
        kk = jnp.concatenate([kc_ref[:, ks]] + [r[:, ks] for r in k_refs], axis=0)
        s = _dot(kk, q2) + mask
        if online:
            stage_ref[slot] = s
            mx_ref[slot] = jnp.max(s, axis=0, keepdims=True)
        else:
            stage_ref[slot] = jnp.exp2(s - shift_ref[0]).astype(BF16)

    def consume(slot, pair):
        kv = pair // pairs_per_kv
        sk = jnp.concatenate([jnp.full((1, qn), sink_ref[2 * pair + b] * LOG2E, F32) for b in range(2)], axis=1)
        if online:
            m = jnp.maximum(mx_ref[slot], sk)
            p = jnp.exp2(stage_ref[slot] - m).astype(BF16)
        else:
            m = shift_ref[0]
            p = stage_ref[slot]
        vt = jnp.concatenate([vct_ref[0, 0]] + [r[0, 0] for r in vt_refs], axis=1)[kv * step:(kv + 1) * step]
        acc = _dot(vt, p)
        o = acc[:D_DH] / (acc[D_DH:D_DH + 1] + jnp.exp2(sk - m))
        both = jnp.concatenate([o[:, :qn], o[:, qn:]], axis=0)
        o_ref[:, pair * HEAD_PAD:(pair + 1) * HEAD_PAD] = both.T.astype(o_ref.dtype)

    produce(0, 0)
    for pair in range(pairs):
        if pair + 1 < pairs:
            produce((pair + 1) % 2, pair + 1)
        consume(pair % 2, pair)


def _wgqa_mask(lc):
    nkeys = WG_Q + 2 * D_WINDOW
    kb = np.arange(nkeys)[:, None]
    qi = np.arange(WG_Q)[None, :]
    band = np.abs(kb - D_WINDOW - qi) <= D_WINDOW
    cases = [band & (kb >= D_WINDOW), band, band & (kb < nkeys - D_WINDOW)]
    tab = np.where(np.stack(cases), 0.0, NEG_INF).astype(np.float32)
    return jnp.asarray(np.concatenate([np.zeros((3, lc, WG_Q), np.float32), tab], axis=1))


def _window_gqa(q, k, vt, kc, vct, sink, qk_bound):
    t = q.shape[0]
    nb = t // WG_Q
    per_q = WG_Q // D_WINDOW
    nkb = t // D_WINDOW
    lc = kc.shape[0]
    kw = k.shape[1]
    _, _, vrows, tk = vt.shape
    per_chunk = tk // D_WINDOW
    first_blk = [lambda i, j=j: jnp.clip(i * per_q - 1 + j, 0, nkb - 1) for j in range(per_q + 2)]
    kspecs = [pl.BlockSpec((D_WINDOW, kw), lambda i, f=f: (f(i), 0)) for f in first_blk]
    vspecs = [pl.BlockSpec((1, 1, vrows, D_WINDOW), lambda i, f=f: (0, f(i) // per_chunk, 0, f(i) % per_chunk))
              for f in first_blk]
    mask = _wgqa_mask(lc)
    case = lambda i: (jnp.where(i == 0, 0, jnp.where(i == nb - 1, 2, 1)), 0, 0)
    cur = lambda i: (i, 0)
    nkeys = lc + WG_Q + 2 * D_WINDOW
    lanes = 2 * WG_Q
    smem = pl.BlockSpec(memory_space=pltpu.SMEM)
    shift = jnp.maximum(qk_bound, LOG2E * jnp.max(sink))

    def call(online):
        stage = pltpu.VMEM((2, nkeys, lanes), F32 if online else BF16)
        return pl.pallas_call(
            functools.partial(_wgqa_kernel, online=online),
            grid=(nb,),
            in_specs=[smem, smem, pl.BlockSpec((WG_Q, q.shape[1]), cur)] + kspecs + vspecs
                     + [_full(kc.shape), _full(vct.shape), pl.BlockSpec((1,) + mask.shape[1:], case)],
            out_specs=pl.BlockSpec((WG_Q, q.shape[1]), cur),
            out_shape=jax.ShapeDtypeStruct(q.shape, BF16),
            scratch_shapes=[stage, pltpu.VMEM((2, 1, lanes), F32)],
            compiler_params=_cparams("parallel"),
            name="window_gqa" + ("_online" if online else "_bounded"),
        )(jnp.reshape(shift, (1,)).astype(F32), sink, q, *([k] * (per_q + 2)), *([vt] * (per_q + 2)), kc, vct, mask)

    return lax.cond(shift + qk_bound <= 2 * FLASH_SAFE_BOUND, lambda: call(False), lambda: call(True))


def _mix_ffn_kernel(o1_ref, o1p_ref, o1n_ref, o2_ref, o2p_ref, o2n_ref, x_ref, xp_ref, xn_ref,
                    wout_ref, vec_ref, wup_ref, cv_ref, wdn_ref, o_ref, acc_ref, hcat_ref, u_ref, x1_ref):
    i = pl.program_id(0)
    n = pl.num_programs(0)
    tm = x_ref.shape[0]
    rows = tm + 2 * HALO
    nchunk = wdn_ref.shape[0] // FFN_CHUNK
    half = o1_ref.shape[1]
    vec = vec_ref[...]
    o1 = jnp.concatenate([o1p_ref[...], o1_ref[...], o1n_ref[...]], axis=0)
    o2 = jnp.concatenate([o2p_ref[...], o2_ref[...], o2n_ref[...]], axis=0)
    x = jnp.concatenate([xp_ref[...], x_ref[...], xn_ref[...]], axis=0)
    x1 = x + vec[0:1] * (_dot(o1, wout_ref[:half, :]) + _dot(o2, wout_ref[half:, :]))
    x1_ref[...] = x1[HALO:HALO + tm]
    hcat_ref[...] = (_rms(x1) * vec[1:2] * (1.0 + vec[2:3]) + vec[3:4]).astype(BF16)

    @pl.when(i == 0)
    def _():
        hcat_ref[0:HALO] = jnp.zeros((HALO, hcat_ref.shape[1]), BF16)

    @pl.when(i == n - 1)
    def _():
        hcat_ref[HALO + tm:rows] = jnp.zeros((HALO, hcat_ref.shape[1]), BF16)

    acc_ref[...] = jnp.zeros(acc_ref.shape, F32)

    def conv(u, cv):
        before = pltpu.roll(u, 1, axis=0)
        after = pltpu.roll(u, rows - 1, axis=0)
        r = cv[3:4] + before * cv[0:1] + u * cv[1:2] + after * cv[2:3]
        return r[HALO:HALO + tm]

    def cols(c, gate):
        return pl.ds(pl.multiple_of((gate * nchunk + c) * FFN_CHUNK, FFN_CHUNK), FFN_CHUNK)

    def stage_up(slot, c):
        hcat = hcat_ref[...]
        u_ref[slot, 0] = _dot(hcat, wup_ref[:, cols(c, 0)])
        u_ref[slot, 1] = _dot(hcat, wup_ref[:, cols(c, 1)])

    def stage_down(slot, c):
        a = conv(u_ref[slot, 0], cv_ref[:, cols(c, 0)])
        g = conv(u_ref[slot, 1], cv_ref[:, cols(c, 1)])
        act = (a * (g * _sigmoid(g))).astype(BF16)
        acc_ref[...] += _dot(act, wdn_ref[cols(c, 0), :])

    assert nchunk % 2 == 1
    stage_up(0, 0)

    def body(j, carry):
        c = 2 * j
        stage_up(1, c + 1)
        stage_down(0, c)
        stage_up(0, c + 2)
        stage_down(1, c + 1)
        return carry

    lax.fori_loop(0, nchunk // 2, body, 0)
    stage_down(0, nchunk - 1)
    o_ref[...] = x1_ref[...] + vec[4:5] * acc_ref[...]


def _mix_ffn(o1, o2, x, wout, vec, wup, cv, wdn):
    t, d = x.shape
    tm = min(512, t)
    nt = t // tm
    hb = tm // HALO
    row = lambda i: (i, 0)
    prev = lambda i: (jnp.maximum(i * hb - 1, 0), 0)
    nxt = lambda i: (jnp.minimum((i + 1) * hb, t // HALO - 1), 0)
    tiled = lambda w: [pl.BlockSpec((tm, w), row), pl.BlockSpec((HALO, w), prev), pl.BlockSpec((HALO, w), nxt)]
    return pl.pallas_call(
        _mix_ffn_kernel,
        grid=(nt,),
        in_specs=tiled(o1.shape[1]) + tiled(o2.shape[1]) + tiled(d)
                 + [_resident(a.shape) for a in (wout, vec, wup, cv, wdn)],
        out_specs=pl.BlockSpec((tm, d), row),
        out_shape=jax.ShapeDtypeStruct((t, d), F32),
        scratch_shapes=[pltpu.VMEM((tm, d), F32), pltpu.VMEM((tm + 2 * HALO, d), BF16),
                        pltpu.VMEM((2, 2, tm + 2 * HALO, FFN_CHUNK), F32), pltpu.VMEM((tm, d), F32)],
        compiler_params=_cparams("parallel"),
        name="mix_ffn",
    )(o1, o1, o1, o2, o2, o2, x, x, x, wout, vec, wup, cv, wdn)


def _axial_tables(n_tokens, dim):
    t = np.arange(n_tokens)
    row = (t // GRID_W).astype(np.float32)
    col = (t % GRID_W).astype(np.float32)
    quarter = dim // 4
    inv_freq = np.float32(ROPE_BASE) ** (-np.arange(quarter, dtype=np.float32) / np.float32(quarter))
    ang = np.concatenate([row[:, None] * inv_freq, col[:, None] * inv_freq], axis=-1).astype(np.float32)
    return np.cos(ang), np.sin(ang)


def _rope_tables_head64(t):
    cos, sin = _axial_tables(t, 64)
    return (jnp.asarray(np.tile(np.concatenate([cos, cos], -1), (1, 2))),
            jnp.asarray(np.tile(np.concatenate([-sin, sin], -1), (1, 2))))


def _rope_tables_latent(t):
    cos, sin = _axial_tables(t, C_ROPE)
    ones = np.ones((t, C_NOPE), np.float32)
    tail = np.ones((t, HEAD_PAD - C_NOPE - C_ROPE), np.float32)
    cosc = np.concatenate([ones, cos, cos, tail], -1)
    sinc = np.concatenate([0 * ones, -sin, sin, 0 * tail], -1)
    return jnp.asarray(cosc), jnp.asarray(sinc)


def _group_ones(width, group):
    return jnp.asarray(np.kron(np.eye(width // group), np.ones((group, group))), BF16)


def _pad_heads(a, heads, dim):
    a = a.reshape(a.shape[:-1] + (heads, dim))
    a = jnp.pad(a, [(0, 0)] * (a.ndim - 1) + [(0, HEAD_PAD - dim)])
    return a.reshape(a.shape[:-2] + (heads * HEAD_PAD,))


def _logit_bound(gq, gk, dim):
    return 1.02 * LOG2E * math.sqrt(dim) * jnp.max(jnp.abs(gq)) * jnp.max(jnp.abs(gk))


def _pad_row(v, width):
    return jnp.pad(v, (0, width - v.shape[0]))


def _ffn_weights(w_up, conv_w, conv_b, w_down):
    cv = jnp.concatenate([conv_w, conv_b[None]], axis=0)
    return w_up.astype(BF16), cv, w_down.astype(BF16)


def kernel(x, c, ctx, c_ctx, ada_w, ada_b, norm_mix, norm_ffn, ffn_up, ffn_conv_w, ffn_conv_b, ffn_down, ev_w_in, ev_w_out, a_q_norm, a_k_norm, a_lam_q1, a_lam_k1, a_lam_q2, a_lam_k2, a_subln, b_q_norm, b_k_norm, b_rpb, od_w_in, od_w_out, c_q_a_norm, c_w_qb, c_kv_a_norm, c_w_kvb, c_q_norm, c_k_norm, d_q_norm, d_k_norm, d_sink):
    assert x.shape[0] == 1 and ctx.shape[0] == 1
    xm = x[0]
    xc = ctx[0]
    t, d = xm.shape
    lc = xc.shape[0]
    mods = _modulation(c, c_ctx, ada_w, ada_b)
    cos64, sin64 = _rope_tables_head64(t)
    cosc, sinc = _rope_tables_latent(t)
    g64 = _group_ones(MXU_TILE, 64)
    g128 = _group_ones(MXU_TILE, HEAD_PAD)
    dummy_tab = jnp.zeros((lc, LANES), F32)

    for l in range(DEPTH):
        last = l == DEPTH - 1
        i = l // 2
        mm = mods[l, 0].reshape(6, d)
        mc = mods[l, 1].reshape(6, d)
        pre_m = jnp.stack([norm_mix[l], mm[1], mm[0]])
        pre_c = jnp.stack([norm_mix[l], mc[1], mc[0]])
        post_m = jnp.stack([mm[2], norm_ffn[l], mm[4], mm[3], mm[5]])
        post_c = jnp.stack([mc[2], norm_ffn[l], mc[4], mc[3], mc[5]])
        wup, cv, wdn = _ffn_weights(ffn_up[l], ffn_conv_w[l], ffn_conv_b[l], ffn_down[l])
        if l % 2 == 0:
            lam_init = 0.8 - 0.6 * math.exp(-0.3 * l)
            w_in = ev_w_in[i].astype(BF16)
            w_out = ev_w_out[i].astype(BF16)
            hg = jnp.stack([jnp.tile(a_q_norm[i], 8), jnp.tile(a_k_norm[i], 8),
                            jnp.tile(b_q_norm[i], 8), jnp.tile(b_k_norm[i], 8)])
            ex = jnp.stack([_pad_row(a_lam_q1[i], LANES), _pad_row(a_lam_k1[i], LANES),
                            _pad_row(a_lam_q2[i], LANES), _pad_row(a_lam_k2[i], LANES)])
            gcol = a_subln[i][:, None]
            qa, ka, va, qb, kb, vb = _premix_even(xm, pre_m, w_in, g64, hg, cos64, sin64, True)
            qa_c, ka_c, va_c, qb_c, kb_c, vb_c = _premix_even(xc, pre_c, w_in, g64, hg, dummy_tab, dummy_tab, False)
            oa = _flash(qa, ka, va, ka_c, va_c, ex, gcol, _logit_bound(a_q_norm[i], a_k_norm[i], A_DQK),
                        mode="diff", lam_init=lam_init)
            qk_b = _logit_bound(b_q_norm[i], b_k_norm[i], B_DH)
            ob = _neighbourhood(qb, kb, vb, kb_c, vb_c, _na_bias_table(b_rpb[i], t // GRID_W), qk_b,
                                qk_b + LOG2E * jnp.maximum(jnp.max(b_rpb[i]), 0.0))
            if not last:
                oa_c = _flash(qa_c, ka_c, va_c, None, None, ex, gcol, mode="diff", lam_init=lam_init)
                ob_c = _flash(qb_c, kb_c, vb_c, None, None, ex, gcol, mode="pair")
        else:
            w = od_w_in[i]
            cq, ckv, ckr, dq, dk, dv = jnp.split(w, np.cumsum([C_Q_LORA, C_KV_LORA, C_ROPE, 512, 128]).tolist(), axis=1)
            dup = lambda a: jnp.concatenate([a[:, :64], a[:, :64], a[:, 64:], a[:, 64:]], axis=1)
            ckr_pad = jnp.pad(ckr, ((0, 0), (C_NOPE, HEAD_PAD - C_NOPE - C_ROPE)))
            w_in = jnp.concatenate([cq, ckv, dq, dup(dk), dv, ckr_pad], axis=1).astype(BF16)
            w_out = od_w_out[i].astype(BF16)
            wqb = _pad_heads(c_w_qb[i], C_HEADS, C_NOPE + C_ROPE).astype(BF16)
            kvb = c_w_kvb[i].reshape(C_KV_LORA, C_HEADS, C_NOPE + C_DV)
            wk = _pad_heads(kvb[:, :, :C_NOPE].reshape(C_KV_LORA, -1), C_HEADS, C_NOPE).astype(BF16)
            wv = kvb[:, :, C_NOPE:].reshape(C_KV_LORA, C_HEADS * C_DV).astype(BF16)
            gains = jnp.stack([jnp.tile(_pad_row(c_q_norm[i], HEAD_PAD), C_HEADS),
                               jnp.tile(_pad_row(c_k_norm[i], HEAD_PAD), C_HEADS),
                               _pad_row(jnp.tile(d_q_norm[i], 8), 1024), _pad_row(jnp.tile(d_k_norm[i], 4), 1024)])
            lora = jnp.stack([_pad_row(c_q_a_norm[i], 512), _pad_row(c_kv_a_norm[i], 512)])
            odd = functools.partial(_premix_odd, w=w_in, wqb=wqb, wk=wk, wv=wv, g128=g128, g64=g64, gains=gains, lora=lora)
            mq, mk, mv, gq, gk, gv = odd(xm, pre_m, cos=cos64, sin=sin64, cosc=cosc, sinc=sinc, use_rope=True)
            mq_c, mk_c, mv_c, gq_c, gk_c, gv_c = odd(xc, pre_c, cos=dummy_tab, sin=dummy_tab, cosc=dummy_tab,
                                                     sinc=dummy_tab, use_rope=False)
            ex = jnp.zeros((4, LANES), F32)
            oa = _flash(mq, mk, mv, mk_c, mv_c, ex, jnp.ones((HEAD_PAD, 1), F32),
                        _logit_bound(c_q_norm[i], c_k_norm[i], C_NOPE + C_ROPE), mode="mla")
            ob = _window_gqa(gq, gk, gv, gk_c, gv_c, d_sink[i], _logit_bound(d_q_norm[i], d_k_norm[i], D_DH))
            if not last:
                raise NotImplementedError("context update after an odd layer is not needed at this depth")
        xm = _mix_ffn(oa, ob, xm, w_out, post_m, wup, cv, wdn)
        if not last:
            xc = _mix_ffn(oa_c, ob_c, xc, w_out, post_c, wup, cv, wdn)
    return xm[None]
```

```python
import functools
import math

import numpy as np
import jax
import jax.numpy as jnp
from jax import lax
from jax.experimental import pallas as pl
from jax.experimental.pallas import tpu as pltpu

DEPTH = 2
GRID_W = 64
EPS = 1e-6
ROPE_BASE = 10000.0
NEG_INF = -1e30
LOG2E = math.log2(math.e)

A_DQK = 64
A_DV = 128
B_DH = 64
NA_WIN_H = 8
NA_WIN_W = 16
C_HEADS = 8
C_Q_LORA = 384
C_KV_LORA = 256
C_NOPE = 64
C_ROPE = 32
C_DV = 64
D_HEADS = 8
D_KV_HEADS = 2
D_DH = 64
D_WINDOW = 128

LANES = 128
MXU_TILE = 256
HEAD_PAD = 128
VMEM_LIMIT = 56 * 1024 * 1024
FFN_CHUNK = 256
HALO = 16
NA_ROWS = 4
WG_Q = 256
ONES_ROWS = 16
FLASH_TQ = 4096
FLASH_TQ_ONLINE = 1024
FLASH_TK = 512
FLASH_UNROLL = 4
FLASH_SAFE_BOUND = 40.0

F32 = jnp.float32
BF16 = jnp.bfloat16


def _cparams(*sem):
    return pltpu.CompilerParams(dimension_semantics=sem, vmem_limit_bytes=VMEM_LIMIT)


def _full(shape):
    n = len(shape)
    return pl.BlockSpec(shape, lambda *_: (0,) * n)


def _resident(shape):
    n = len(shape)
    return pl.BlockSpec(shape, lambda *_: (0,) * n, pipeline_mode=pl.Buffered(1))


def _dot(a, b):
    return jnp.dot(a, b, preferred_element_type=F32)


def _rms(x):
    return x * lax.rsqrt(jnp.mean(x * x, axis=-1, keepdims=True) + EPS)


def _sigmoid(x):
    return 1.0 / (1.0 + jnp.exp(-x))


def _group_rms(y, gmat, inv_n, gain):
    yy = (y * y).astype(BF16)
    slab = gmat.shape[0]
    ss = jnp.concatenate([_dot(yy[:, c:c + slab], gmat) for c in range(0, y.shape[-1], slab)], axis=-1)
    return y * lax.rsqrt(ss * inv_n + EPS) * gain


def _rope(y, cos, sin, half, first_half):
    w = y.shape[-1]
    fwd = pltpu.roll(y, w - half, axis=1)
    bwd = pltpu.roll(y, half, axis=1)
    return y * cos + jnp.where(first_half, fwd, bwd) * sin


def _lane_tile(t, width):
    reps = width // t.shape[-1]
    return t if reps == 1 else jnp.concatenate([t] * reps, axis=-1)


def _lane_iota(width):
    return lax.broadcasted_iota(jnp.int32, (1, width), 1)


def _mod_kernel(ct_ref, w_ref, b_ref, o_ref):
    ct = ct_ref[...]
    s = ct * _sigmoid(ct)
    w = w_ref[0]
    rows = [jnp.sum(w * s[:, j:j + 1], axis=0, keepdims=True) for j in range(2)]
    o_ref[0] = jnp.concatenate(rows, axis=0) + b_ref[0]


def _modulation(c, c_ctx, ada_w, ada_b):
    depth, d, n = ada_w.shape
    tn = 1536
    ct = jnp.stack([c[0], c_ctx], axis=1)
    return pl.pallas_call(
        _mod_kernel,
        grid=(depth, n // tn),
        in_specs=[_full((d, 2)),
                  pl.BlockSpec((1, d, tn), lambda l, j: (l, 0, j)),
                  pl.BlockSpec((1, 1, tn), lambda l, j: (l, 0, j))],
        out_specs=pl.BlockSpec((1, 2, tn), lambda l, j: (l, 0, j)),
        out_shape=jax.ShapeDtypeStruct((depth, 2, n), F32),
        compiler_params=_cparams("arbitrary", "arbitrary"),
        name="modulation",
    )(ct, ada_w, ada_b.reshape(depth, 1, n))


def _store_values_t(vt_ref, v, dv):
    tm = v.shape[0]
    vt = v.T.astype(BF16)
    ones = jnp.ones((ONES_ROWS, tm), BF16)
    step = dv + ONES_ROWS
    for g in range(vt_ref.shape[0]):
        for h in range(HEAD_PAD // dv):
            src = g * HEAD_PAD + h * dv
            vt_ref[g, 0, h * step:h * step + dv, :] = vt[src:src + dv]
            vt_ref[g, 0, h * step + dv:(h + 1) * step, :] = ones


def _values_t_shape(t, tm, dv, groups=4):
    return (groups, t // tm, (HEAD_PAD // dv) * (dv + ONES_ROWS), tm)


def _premix_even_kernel(x_ref, vec_ref, w_ref, g_ref, hg_ref, cos_ref, sin_ref,
                        qa_ref, ka_ref, vat_ref, qb_ref, kb_ref, vbt_ref, *, use_rope):
    vec = vec_ref[...]
    h = (_rms(x_ref[...]) * vec[0:1] * (1.0 + vec[1:2]) + vec[2:3]).astype(BF16)
    gm = g_ref[...]
    hg = hg_ref[...]
    sw = 512
    if use_rope:
        cos = _lane_tile(cos_ref[...], sw)
        sin = _lane_tile(sin_ref[...], sw)
        first = (_lane_iota(sw) & 32) == 0

    def seg(i):
        return _dot(h, w_ref[:, i * sw:(i + 1) * sw])

    qa = _group_rms(seg(0), gm, 1.0 / A_DQK, hg[0:1])
    ka = _group_rms(seg(1), gm, 1.0 / A_DQK, hg[1:2])
    if use_rope:
        qa = _rope(qa, cos, sin, 32, first)
        ka = _rope(ka, cos, sin, 32, first)
    qa_ref[...] = (qa * (LOG2E / math.sqrt(A_DQK))).astype(BF16)
    ka_ref[...] = ka.astype(BF16)
    _store_values_t(vat_ref, seg(2), A_DV)
    qb = _group_rms(seg(3), gm, 1.0 / B_DH, hg[2:3])
    qb_ref[...] = (qb * (LOG2E / math.sqrt(B_DH))).astype(BF16)
    kb_ref[...] = _group_rms(seg(4), gm, 1.0 / B_DH, hg[3:4]).astype(BF16)
    _store_values_t(vbt_ref, seg(5), B_DH)


def _premix_even(x, vec, w, gmat, hg, cos, sin, use_rope):
    t, d = x.shape
    tm = min(FLASH_TK, t)
    row = lambda i: (i, 0)
    nat = jax.ShapeDtypeStruct((t, 512), BF16)
    nat_spec = pl.BlockSpec((tm, 512), row)
    vta = _values_t_shape(t, tm, A_DV)
    vtb = _values_t_shape(t, tm, B_DH)
    vt_spec = lambda s: pl.BlockSpec((s[0], 1) + s[2:], lambda i: (0, i, 0, 0))
    return pl.pallas_call(
        functools.partial(_premix_even_kernel, use_rope=use_rope),
        grid=(t // tm,),
        in_specs=[pl.BlockSpec((tm, d), row), _full(vec.shape), _full(w.shape), _full(gmat.shape),
                  _full(hg.shape), pl.BlockSpec((tm, LANES), row), pl.BlockSpec((tm, LANES), row)],
        out_specs=[nat_spec, nat_spec, vt_spec(vta), nat_spec, nat_spec, vt_spec(vtb)],
        out_shape=[nat, nat, jax.ShapeDtypeStruct(vta, BF16), nat, nat, jax.ShapeDtypeStruct(vtb, BF16)],
        compiler_params=_cparams("parallel"),
        name="premix_even",
    )(x, vec, w, gmat, hg, cos, sin)


_O_CQ, _O_CKV, _O_DQ, _O_DK, _O_DV, _O_CKR, _O_END = 0, 384, 640, 1152, 1408, 1536, 1664


def _premix_odd_kernel(x_ref, vec_ref, w_ref, wqb_ref, wk_ref, wv_ref, g128_ref, g64_ref, gain_ref, lora_ref,
                       cos_ref, sin_ref, cosc_ref, sinc_ref,
                       mq_ref, mk_ref, mvt_ref, gq_ref, gk_ref, gvt_ref, *, use_rope):
    vec = vec_ref[...]
    h = (_rms(x_ref[...]) * vec[0:1] * (1.0 + vec[1:2]) + vec[2:3]).astype(BF16)
    gains = gain_ref[...]
    lora = lora_ref[...]
    g128 = g128_ref[...]
    g64 = g64_ref[...]
    mw = C_HEADS * HEAD_PAD
    if use_rope:
        cosc = _lane_tile(cosc_ref[...], mw)
        sinc = _lane_tile(sinc_ref[...], mw)
        first_c = (_lane_iota(mw) & 127) < (C_NOPE + C_ROPE // 2)
        cos = _lane_tile(cos_ref[...], 512)
        sin = _lane_tile(sin_ref[...], 512)
        first_d = (_lane_iota(512) & 32) == 0

    def seg(a, b):
        return _dot(h, w_ref[:, a:b])

    inv_c = 1.0 / (C_NOPE + C_ROPE)
    half_w = mw // 2
    cq = (_rms(seg(_O_CQ, _O_CKV)) * lora[0:1, :C_Q_LORA]).astype(BF16)
    for c in range(0, mw, half_w):
        cs = slice(c, c + half_w)
        mq = _group_rms(_dot(cq, wqb_ref[:, cs]), g128, inv_c, gains[0:1, cs])
        if use_rope:
            mq = _rope(mq, cosc[:, cs], sinc[:, cs], C_ROPE // 2, first_c[:, cs])
        mq_ref[:, cs] = (mq * (LOG2E * math.sqrt(inv_c))).astype(BF16)
    ckv = (_rms(seg(_O_CKV, _O_DQ)) * lora[1:2, :C_KV_LORA]).astype(BF16)
    ckr = _lane_tile(seg(_O_CKR, _O_END), half_w)
    for c in range(0, mw, half_w):
        cs = slice(c, c + half_w)
        mk = _group_rms(_dot(ckv, wk_ref[:, cs]) + ckr, g128, inv_c, gains[1:2, cs])
        if use_rope:
            mk = _rope(mk, cosc[:, cs], sinc[:, cs], C_ROPE // 2, first_c[:, cs])
        mk_ref[:, cs] = mk.astype(BF16)
    _store_values_t(mvt_ref, _dot(ckv, wv_ref[...]), C_DV)
    gq = _group_rms(seg(_O_DQ, _O_DK), g64, 1.0 / D_DH, gains[2:3, :512])
    gk = _group_rms(seg(_O_DK, _O_DV), g64, 1.0 / D_DH, gains[3:4, :256])
    if use_rope:
        gq = _rope(gq, cos, sin, 32, first_d)
        gk = _rope(gk, cos[:, :256], sin[:, :256], 32, first_d[:, :256])
    gq_ref[...] = (gq * (LOG2E / math.sqrt(D_DH))).astype(BF16)
    gk_ref[...] = gk.astype(BF16)
    _store_values_t(gvt_ref, seg(_O_DV, _O_CKR), D_DH)


def _premix_odd(x, vec, w, wqb, wk, wv, g128, g64, gains, lora, cos, sin, cosc, sinc, use_rope):
    t, d = x.shape
    tm = min(FLASH_TK, t)
    row = lambda i: (i, 0)
    tab = pl.BlockSpec((tm, LANES), row)
    nat = lambda n: (pl.BlockSpec((tm, n), row), jax.ShapeDtypeStruct((t, n), BF16))
    def vt(shape):
        return pl.BlockSpec((shape[0], 1) + shape[2:], lambda i: (0, i, 0, 0)), jax.ShapeDtypeStruct(shape, BF16)

    outs = [nat(1024), nat(1024), vt(_values_t_shape(t, tm, C_DV)), nat(512), nat(256),
            vt(_values_t_shape(t, tm, D_DH, groups=1))]
    return pl.pallas_call(
        functools.partial(_premix_odd_kernel, use_rope=use_rope),
        grid=(t // tm,),
        in_specs=[pl.BlockSpec((tm, d), row)] + [_full(a.shape) for a in (vec, w, wqb, wk, wv, g128, g64, gains, lora)]
                 + [tab] * 4,
        out_specs=[o[0] for o in outs],
        out_shape=[o[1] for o in outs],
        compiler_params=_cparams("parallel"),
        name="premix_odd",
    )(x, vec, w, wqb, wk, wv, g128, g64, gains, lora, cos, sin, cosc, sinc)


def _flash_kernel(*refs, mode, has_ctx, online, lam_init):
    bound_ref, q_ref, k_ref, vt_ref = refs[:4]
    kc_ref, vct_ref = refs[4:6] if has_ctx else (None, None)
    ex_ref, gcol_ref, o_ref, acc_ref, stage_ref = refs[6:11] if has_ctx else refs[4:9]
    qt = q_ref[...].astype(F32).T.astype(BF16)
    row = lax.broadcasted_iota(jnp.int32, (HEAD_PAD, 1), 0)
    if mode == "mla":
        qs = [qt[:HEAD_PAD], qt[HEAD_PAD:]]
    else:
        zero = jnp.zeros_like(qt)
        qs = [jnp.where(row < 64, qt, zero), jnp.where(row >= 64, qt, zero)]
    nk, tk = vt_ref.shape[1], vt_ref.shape[3]

    acc_ref[...] = jnp.zeros(acc_ref.shape, F32)
    rows_b = acc_ref.shape[1]

    def scores(kblk, b):
        kb = kblk[:, b * HEAD_PAD:(b + 1) * HEAD_PAD] if mode == "mla" else kblk
        return _dot(kb, qs[b])

    def values(vtblk, b):
        return vtblk if vtblk.shape[0] == rows_b else vtblk[b * rows_b:(b + 1) * rows_b]

    if online:
        m_ref, mx_ref = refs[-2:]
        m_ref[...] = jnp.full(m_ref.shape, NEG_INF, F32)

        def absorb(s, smax, vtblk, b):
            m_prev = m_ref[b]
            m_new = jnp.maximum(m_prev, smax)
            alpha = jnp.exp2(m_prev - m_new)
            p = jnp.exp2(s - m_new).astype(BF16)
            acc_ref[b] = alpha * acc_ref[b] + _dot(values(vtblk, b), p)
            m_ref[b] = m_new

        def produce(slot, kblk):
            for b in range(2):
                s = scores(kblk, b)
                stage_ref[slot, b] = s
                mx_ref[slot, b] = jnp.max(s, axis=0, keepdims=True)

        def consume(slot, vtblk):
            for b in range(2):
                absorb(stage_ref[slot, b], mx_ref[slot, b], vtblk, b)

        def direct(kblk, vtblk):
            for b in range(2):
                s = scores(kblk, b)
                absorb(s, jnp.max(s, axis=0, keepdims=True), vtblk, b)
    else:
        bound = bound_ref[0]
        dv = rows_b - ONES_ROWS
        vpu_den = dv == HEAD_PAD
        lpart_ref = refs[-1] if vpu_den else None

        def probs(kblk, b):
            p = jnp.exp2(scores(kblk, b) - bound)
            return p.astype(BF16), (jnp.sum(p, axis=0, keepdims=True) if vpu_den else None)

        def accumulate(b, vtblk, p, psum):
            if vpu_den:
                acc_ref[b, :dv] += _dot(values(vtblk, b)[:dv], p)
                acc_ref[b, dv:dv + 1] += psum
            else:
                acc_ref[b] += _dot(values(vtblk, b), p)

        def produce(slot, kblk):
            for b in range(2):
                p, psum = probs(kblk, b)
                stage_ref[slot, b] = p
                if vpu_den:
                    lpart_ref[slot, b] = psum

        def consume(slot, vtblk):
            for b in range(2):
                accumulate(b, vtblk, stage_ref[slot, b], lpart_ref[slot, b] if vpu_den else None)

        def direct(kblk, vtblk):
            for b in range(2):
                accumulate(b, vtblk, *probs(kblk, b))

    if nk == 1:
        if has_ctx:
            direct(kc_ref[...], vct_ref[0])
        direct(k_ref[...], vt_ref[0, 0])
    else:
        unroll = FLASH_UNROLL if nk % FLASH_UNROLL == 0 else 2
        assert nk % unroll == 0
        produce(0, k_ref[0:tk, :])
        if has_ctx:
            direct(kc_ref[...], vct_ref[0])

        def body(i, carry):
            c0 = unroll * i
            for u in range(unroll):
                nxt = jnp.minimum(c0 + u + 1, nk - 1)
                produce((u + 1) % 2, k_ref[pl.ds(pl.multiple_of(nxt * tk, tk), tk), :])
                consume(u % 2, vt_ref[0, c0 + u])
            return carry

        lax.fori_loop(0, nk // unroll, body, 0)

    dv = rows_b - ONES_ROWS
    o0 = acc_ref[0, :dv] / acc_ref[0, dv:dv + 1]
    o1 = acc_ref[1, :dv] / acc_ref[1, dv:dv + 1]
    if mode == "diff":
        ex = ex_ref[...]
        lam = (jnp.exp(jnp.sum(ex[0:1] * ex[1:2], axis=-1, keepdims=True))
               - jnp.exp(jnp.sum(ex[2:3] * ex[3:4], axis=-1, keepdims=True)) + lam_init)
        o = o0 - lam * o1
        o = o * lax.rsqrt(jnp.mean(o * o, axis=0, keepdims=True) + EPS) * gcol_ref[...] * (1.0 - lam_init)
    else:
        o = jnp.concatenate([o0, o1], axis=0)
    o_ref[...] = o.T.astype(o_ref.dtype)


def _flash(q, k, vt, kc, vct, ex, gcol, bound=None, *, mode, lam_init=0.0):
    t = q.shape[0]
    tkeys = k.shape[0]
    qw = 2 * HEAD_PAD if mode == "mla" else HEAD_PAD
    groups = q.shape[1] // qw
    _, nk, vrows, tk = vt.shape
    assert nk * tk == tkeys
    has_ctx = kc is not None
    dv = HEAD_PAD if mode == "diff" else HEAD_PAD // 2
    rows_b = dv + ONES_ROWS
    bound_arr = jnp.zeros((1,), F32) if bound is None else jnp.reshape(bound, (1,)).astype(F32)
    args = [bound_arr, q, k, vt]
    if has_ctx:
        lc = kc.shape[0]
        args += [kc, vct.reshape(groups, vrows, lc)]
    args += [ex, gcol]

    def call(online):
        tq = min(FLASH_TQ_ONLINE if online else FLASH_TQ, t)
        in_specs = [pl.BlockSpec(memory_space=pltpu.SMEM),
                    pl.BlockSpec((tq, qw), lambda g, i: (i, g)),
                    pl.BlockSpec((tkeys, qw), lambda g, i: (0, g), pipeline_mode=pl.Buffered(1)),
                    pl.BlockSpec((1, nk, vrows, tk), lambda g, i: (g, 0, 0, 0), pipeline_mode=pl.Buffered(1))]
        if has_ctx:
            in_specs += [pl.BlockSpec((lc, qw), lambda g, i: (0, g)),
                         pl.BlockSpec((1, vrows, lc), lambda g, i: (g, 0, 0))]
        in_specs += [_full(ex.shape), _full(gcol.shape)]
        acc = pltpu.VMEM((2, rows_b, tq), F32)
        if online:
            scratch = [acc, pltpu.VMEM((2, 2, tk, tq), F32), pltpu.VMEM((2, 1, tq), F32), pltpu.VMEM((2, 2, 1, tq), F32)]
        else:
            scratch = [acc, pltpu.VMEM((2, 2, tk, tq), BF16)]
            if dv == HEAD_PAD:
                scratch.append(pltpu.VMEM((2, 2, 1, tq), F32))
        return pl.pallas_call(
            functools.partial(_flash_kernel, mode=mode, has_ctx=has_ctx, online=online, lam_init=lam_init),
            grid=(groups, t // tq),
            in_specs=in_specs,
            out_specs=pl.BlockSpec((tq, HEAD_PAD), lambda g, i: (i, g)),
            out_shape=jax.ShapeDtypeStruct((t, groups * HEAD_PAD), BF16),
            scratch_shapes=scratch,
            compiler_params=_cparams("parallel", "parallel"),
            name="flash_" + mode + ("_online" if online else "_bounded"),
        )(*args)

    if bound is None:
        return call(True)
    return lax.cond(bound <= FLASH_SAFE_BOUND, lambda: call(False), lambda: call(True))


def _na_kernel(shift_ref, q_ref, kp_ref, kcur_ref, kn_ref, vtp_ref, vtcur_ref, vtn_ref, kc_ref, vct_ref, bias_ref,
               o_ref, stage_ref, *stat_refs, online):
    pairs = q_ref.shape[1] // HEAD_PAD
    qn = q_ref.shape[0]
    lc = kc_ref.shape[0]
    step = B_DH + ONES_ROWS
    row = lax.broadcasted_iota(jnp.int32, (HEAD_PAD, 1), 0)

    def produce(slot, pr):
        ps = slice(pr * HEAD_PAD, (pr + 1) * HEAD_PAD)
        qt = q_ref[:, ps].astype(F32).T.astype(BF16)
        zero = jnp.zeros_like(qt)
        q2 = jnp.concatenate([jnp.where(row < 64, qt, zero), jnp.where(row >= 64, qt, zero)], axis=1)
        k_nb = jnp.concatenate([kp_ref[:, ps], kcur_ref[:, ps], kn_ref[:, ps]], axis=0)
        bias = jnp.concatenate([bias_ref[0, 2 * pr], bias_ref[0, 2 * pr + 1]], axis=1)
        s_ctx = _dot(kc_ref[:, ps], q2)
        s_nb = _dot(k_nb, q2) + bias
        if online:
            stage_ref[slot, 0:lc] = s_ctx
            stage_ref[slot, lc:] = s_nb
            stat_refs[0][slot] = jnp.maximum(jnp.max(s_ctx, axis=0, keepdims=True),
                                             jnp.max(s_nb, axis=0, keepdims=True))
        else:
            shift = shift_ref[0]
            stage_ref[slot, 0:lc] = jnp.exp2(s_ctx - shift).astype(BF16)
            stage_ref[slot, lc:] = jnp.exp2(s_nb - shift).astype(BF16)

    def consume(slot, pr):
        vt = jnp.concatenate([vct_ref[pr, 0], vtp_ref[pr, 0], vtcur_ref[pr, 0], vtn_ref[pr, 0]], axis=1)
        if online:
            p = jnp.exp2(stage_ref[slot] - stat_refs[0][slot]).astype(BF16)
        else:
            p = stage_ref[slot]
        outs = []
        for b in range(2):
            acc = _dot(vt[b * step:(b + 1) * step], p[:, b * qn:(b + 1) * qn])
            outs.append(acc[:B_DH] / acc[B_DH:B_DH + 1])
        o_ref[:, pr * HEAD_PAD:(pr + 1) * HEAD_PAD] = jnp.concatenate(outs, axis=0).T.astype(o_ref.dtype)

    produce(0, 0)
    for pr in range(pairs):
        if pr + 1 < pairs:
            produce((pr + 1) % 2, pr + 1)
        consume(pr % 2, pr)


def _na_bias_table(rpb, rows):
    nr, w = NA_ROWS, GRID_W
    heads = rpb.shape[0]
    pad = w - NA_WIN_W
    padded = jnp.pad(rpb.astype(F32) * LOG2E, ((0, 0), (0, 0), (pad, pad + 1)))
    skew = jnp.tile(padded, (1, 1, w))[:, :, :w * (2 * w - 1)].reshape(heads, -1, w, 2 * w - 1)
    toep = skew[:, :, :, w - 1:]
    cq = np.arange(w)
    cs = np.clip(cq - NA_WIN_W // 2, 0, w - NA_WIN_W)
    valid_c = (cq[None, :] >= cs[:, None]) & (cq[None, :] < cs[:, None] + NA_WIN_W)
    toep_t = jnp.swapaxes(jnp.where(jnp.asarray(valid_c), toep, NEG_INF), -1, -2)
    d0 = NA_WIN_H - 1 - nr
    assert d0 - (nr - 1) >= 0 and d0 + 3 * nr <= 2 * NA_WIN_H - 1
    dense = jnp.concatenate([toep_t[:, d0 - rl:d0 - rl + 3 * nr] for rl in range(nr)], axis=-1)
    wh = min(NA_WIN_H, rows)
    rl = np.arange(nr)
    rr_rel = np.arange(3 * nr)
    valid_all = []
    for base in (0, nr, rows - nr):
        rs = np.clip(base + rl - wh // 2, 0, rows - wh)
        rr = base - nr + rr_rel
        valid_all.append((rr[:, None] >= rs[None, :]) & (rr[:, None] < rs[None, :] + wh))
    valid_r = np.repeat(np.stack(valid_all), w, axis=-1)[:, None, :, None, :]
    tab = jnp.where(jnp.asarray(valid_r), dense[None], NEG_INF)
    return tab.reshape(3, heads, 3 * nr * w, nr * w)


def _neighbourhood(q, k, vt, kc, vct, bias, qk_bound, shift):
    t = q.shape[0]
    qn = NA_ROWS * GRID_W
    nb = t // qn
    lc = kc.shape[0]
    pairs, _, vrows, tk = vt.shape
    per = tk // qn
    cur = lambda i: (i, 0)
    prev = lambda i: (jnp.maximum(i - 1, 0), 0)
    nxt = lambda i: (jnp.minimum(i + 1, nb - 1), 0)
    blk = lambda f: pl.BlockSpec((qn, q.shape[1]), f)
    vblk = lambda f: pl.BlockSpec((pairs, 1, vrows, qn), lambda i: (0, f(i)[0] // per, 0, f(i)[0] % per))
    case = lambda i: (jnp.where(i == 0, 0, jnp.where(i == nb - 1, 2, 1)), 0, 0, 0)
    nkeys = lc + 3 * qn

    def call(online):
        scratch = ([pltpu.VMEM((2, nkeys, 2 * qn), F32), pltpu.VMEM((2, 1, 2 * qn), F32)] if online
                   else [pltpu.VMEM((2, nkeys, 2 * qn), BF16)])
        return pl.pallas_call(
            functools.partial(_na_kernel, online=online),
            grid=(nb,),
            in_specs=[pl.BlockSpec(memory_space=pltpu.SMEM), blk(cur), blk(prev), blk(cur), blk(nxt),
                      vblk(prev), vblk(cur), vblk(nxt),
                      _full(kc.shape), _full(vct.shape), pl.BlockSpec((1,) + bias.shape[1:], case)],
            out_specs=blk(cur),
            out_shape=jax.ShapeDtypeStruct(q.shape, BF16),
            scratch_shapes=scratch,
            compiler_params=_cparams("arbitrary"),
            name="neighbourhood" + ("_online" if online else "_bounded"),
        )(jnp.reshape(shift, (1,)).astype(F32), q, k, k, k, vt, vt, vt, kc, vct, bias)

    return lax.cond(shift + qk_bound <= 2 * FLASH_SAFE_BOUND, lambda: call(False), lambda: call(True))


def _wgqa_kernel(shift_ref, sink_ref, q_ref, *refs, online):
    nkb = WG_Q // D_WINDOW + 2
    k_refs, vt_refs = refs[:nkb], refs[nkb:2 * nkb]
    kc_ref, vct_ref, mask_ref, o_ref, stage_ref = refs[2 * nkb:2 * nkb + 5]
    mx_ref = refs[-1]
    qn = q_ref.shape[0]
    pairs = D_HEADS // 2
    pairs_per_kv = pairs // D_KV_HEADS
    step = D_DH + ONES_ROWS
    row = lax.broadcasted_iota(jnp.int32, (HEAD_PAD, 1), 0)
    mask = jnp.concatenate([mask_ref[0]] * 2, axis=1)

    def produce(slot, pair):
        kv = pair // pairs_per_kv
        ks = slice(kv * HEAD_PAD, (kv + 1) * HEAD_PAD)
        qt = q_ref[:, pair * HEAD_PAD:(pair + 1) * HEAD_PAD].astype(F32).T.astype(BF16)
        zero = jnp.zeros_like(qt)
        q2 = jnp.concatenate([jnp.where(row < 64, qt, zero), jnp.where(row >= 64, qt, zero)], axis=1)
        kk = jnp.concatenate([kc_ref[:, ks]] + [r[:, ks] for r in k_refs], axis=0)
        s = _dot(kk, q2) + mask
        if online:
            stage_ref[slot] = s
            mx_ref[slot] = jnp.max(s, axis=0, keepdims=True)
        else:
            stage_ref[slot] = jnp.exp2(s - shift_ref[0]).astype(BF16)

    def consume(slot, pair):
        kv = pair // pairs_per_kv
        sk = jnp.concatenate([jnp.full((1, qn), sink_ref[2 * pair + b] * LOG2E, F32) for b in range(2)], axis=1)
        if online:
            m = jnp.maximum(mx_ref[slot], sk)
            p = jnp.exp2(stage_ref[slot] - m).astype(BF16)
        else:
            m = shift_ref[0]
            p = stage_ref[slot]
        vt = jnp.concatenate([vct_ref[0, 0]] + [r[0, 0] for r in vt_refs], axis=1)[kv * step:(kv + 1) * step]
        acc = _dot(vt, p)
        o = acc[:D_DH] / (acc[D_DH:D_DH + 1] + jnp.exp2(sk - m))
        both = jnp.concatenate([o[:, :qn], o[:, qn:]], axis=0)
        o_ref[:, pair * HEAD_PAD:(pair + 1) * HEAD_PAD] = both.T.astype(o_ref.dtype)

    produce(0, 0)
    for pair in range(pairs):
        if pair + 1 < pairs:
            produce((pair + 1) % 2, pair + 1)
        consume(pair % 2, pair)


def _wgqa_mask(lc):
    nkeys = WG_Q + 2 * D_WINDOW
    kb = np.arange(nkeys)[:, None]
    qi = np.arange(WG_Q)[None, :]
    band = np.abs(kb - D_WINDOW - qi) <= D_WINDOW
    cases = [band & (kb >= D_WINDOW), band, band & (kb < nkeys - D_WINDOW)]
    tab = np.where(np.stack(cases), 0.0, NEG_INF).astype(np.float32)
    return jnp.asarray(np.concatenate([np.zeros((3, lc, WG_Q), np.float32), tab], axis=1))


def _window_gqa(q, k, vt, kc, vct, sink, qk_bound):
    t = q.shape[0]
    nb = t // WG_Q
    per_q = WG_Q // D_WINDOW
    nkb = t // D_WINDOW
    lc = kc.shape[0]
    kw = k.shape[1]
    _, _, vrows, tk = vt.shape
    per_chunk = tk // D_WINDOW
    first_blk = [lambda i, j=j: jnp.clip(i * per_q - 1 + j, 0, nkb - 1) for j in range(per_q + 2)]
    kspecs = [pl.BlockSpec((D_WINDOW, kw), lambda i, f=f: (f(i), 0)) for f in first_blk]
    vspecs = [pl.BlockSpec((1, 1, vrows, D_WINDOW), lambda i, f=f: (0, f(i) // per_chunk, 0, f(i) % per_chunk))
              for f in first_blk]
    mask = _wgqa_mask(lc)
    case = lambda i: (jnp.where(i == 0, 0, jnp.where(i == nb - 1, 2, 1)), 0, 0)
    cur = lambda i: (i, 0)
    nkeys = lc + WG_Q + 2 * D_WINDOW
    lanes = 2 * WG_Q
    smem = pl.BlockSpec(memory_space=pltpu.SMEM)
    shift = jnp.maximum(qk_bound, LOG2E * jnp.max(sink))

    def call(online):
        stage = pltpu.VMEM((2, nkeys, lanes), F32 if online else BF16)
        return pl.pallas_call(
            functools.partial(_wgqa_kernel, online=online),
            grid=(nb,),
            in_specs=[smem, smem, pl.BlockSpec((WG_Q, q.shape[1]), cur)] + kspecs + vspecs
                     + [_full(kc.shape), _full(vct.shape), pl.BlockSpec((1,) + mask.shape[1:], case)],
            out_specs=pl.BlockSpec((WG_Q, q.shape[1]), cur),
            out_shape=jax.ShapeDtypeStruct(q.shape, BF16),
            scratch_shapes=[stage, pltpu.VMEM((2, 1, lanes), F32)],
            compiler_params=_cparams("parallel"),
            name="window_gqa" + ("_online" if online else "_bounded"),
        )(jnp.reshape(shift, (1,)).astype(F32), sink, q, *([k] * (per_q + 2)), *([vt] * (per_q + 2)), kc, vct, mask)

    return lax.cond(shift + qk_bound <= 2 * FLASH_SAFE_BOUND, lambda: call(False), lambda: call(True))


def _mix_ffn_kernel(o1_ref, o1p_ref, o1n_ref, o2_ref, o2p_ref, o2n_ref, x_ref, xp_ref, xn_ref,
                    wout_ref, vec_ref, wup_ref, cv_ref, wdn_ref, o_ref, acc_ref, hcat_ref, u_ref, x1_ref):
    i = pl.program_id(0)
    n = pl.num_programs(0)
    tm = x_ref.shape[0]
    rows = tm + 2 * HALO
    nchunk = wdn_ref.shape[0] // FFN_CHUNK
    half = o1_ref.shape[1]
    vec = vec_ref[...]
    o1 = jnp.concatenate([o1p_ref[...], o1_ref[...], o1n_ref[...]], axis=0)
    o2 = jnp.concatenate([o2p_ref[...], o2_ref[...], o2n_ref[...]], axis=0)
    x = jnp.concatenate([xp_ref[...], x_ref[...], xn_ref[...]], axis=0)
    x1 = x + vec[0:1] * (_dot(o1, wout_ref[:half, :]) + _dot(o2, wout_ref[half:, :]))
    x1_ref[...] = x1[HALO:HALO + tm]
    hcat_ref[...] = (_rms(x1) * vec[1:2] * (1.0 + vec[2:3]) + vec[3:4]).astype(BF16)

    @pl.when(i == 0)
    def _():
        hcat_ref[0:HALO] = jnp.zeros((HALO, hcat_ref.shape[1]), BF16)

    @pl.when(i == n - 1)
    def _():
        hcat_ref[HALO + tm:rows] = jnp.zeros((HALO, hcat_ref.shape[1]), BF16)

    acc_ref[...] = jnp.zeros(acc_ref.shape, F32)

    def conv(u, cv):
        before = pltpu.roll(u, 1, axis=0)
        after = pltpu.roll(u, rows - 1, axis=0)
        r = cv[3:4] + before * cv[0:1] + u * cv[1:2] + after * cv[2:3]
        return r[HALO:HALO + tm]

    def cols(c, gate):
        return pl.ds(pl.multiple_of((gate * nchunk + c) * FFN_CHUNK, FFN_CHUNK), FFN_CHUNK)

    def stage_up(slot, c):
        hcat = hcat_ref[...]
        u_ref[slot, 0] = _dot(hcat, wup_ref[:, cols(c, 0)])
        u_ref[slot, 1] = _dot(hcat, wup_ref[:, cols(c, 1)])

    def stage_down(slot, c):
        a = conv(u_ref[slot, 0], cv_ref[:, cols(c, 0)])
        g = conv(u_ref[slot, 1], cv_ref[:, cols(c, 1)])
        act = (a * (g * _sigmoid(g))).astype(BF16)
        acc_ref[...] += _dot(act, wdn_ref[cols(c, 0), :])

    assert nchunk % 2 == 1
    stage_up(0, 0)

    def body(j, carry):
        c = 2 * j
        stage_up(1, c + 1)
        stage_down(0, c)
        stage_up(0, c + 2)
        stage_down(1, c + 1)
        return carry

    lax.fori_loop(0, nchunk // 2, body, 0)
    stage_down(0, nchunk - 1)
    o_ref[...] = x1_ref[...] + vec[4:5] * acc_ref[...]


def _mix_ffn(o1, o2, x, wout, vec, wup, cv, wdn):
    t, d = x.shape
    tm = min(512, t)
    nt = t // tm
    hb = tm // HALO
    row = lambda i: (i, 0)
    prev = lambda i: (jnp.maximum(i * hb - 1, 0), 0)
    nxt = lambda i: (jnp.minimum((i + 1) * hb, t // HALO - 1), 0)
    tiled = lambda w: [pl.BlockSpec((tm, w), row), pl.BlockSpec((HALO, w), prev), pl.BlockSpec((HALO, w), nxt)]
    return pl.pallas_call(
        _mix_ffn_kernel,
        grid=(nt,),
        in_specs=tiled(o1.shape[1]) + tiled(o2.shape[1]) + tiled(d)
                 + [_resident(a.shape) for a in (wout, vec, wup, cv, wdn)],
        out_specs=pl.BlockSpec((tm, d), row),
        out_shape=jax.ShapeDtypeStruct((t, d), F32),
        scratch_shapes=[pltpu.VMEM((tm, d), F32), pltpu.VMEM((tm + 2 * HALO, d), BF16),
                        pltpu.VMEM((2, 2, tm + 2 * HALO, FFN_CHUNK), F32), pltpu.VMEM((tm, d), F32)],
        compiler_params=_cparams("parallel"),
        name="mix_ffn",
    )(o1, o1, o1, o2, o2, o2, x, x, x, wout, vec, wup, cv, wdn)


def _axial_tables(n_tokens, dim):
    t = np.arange(n_tokens)
    row = (t // GRID_W).astype(np.float32)
    col = (t % GRID_W).astype(np.float32)
    quarter = dim // 4
    inv_freq = np.float32(ROPE_BASE) ** (-np.arange(quarter, dtype=np.float32) / np.float32(quarter))
    ang = np.concatenate([row[:, None] * inv_freq, col[:, None] * inv_freq], axis=-1).astype(np.float32)
    return np.cos(ang), np.sin(ang)


def _rope_tables_head64(t):
    cos, sin = _axial_tables(t, 64)
    return (jnp.asarray(np.tile(np.concatenate([cos, cos], -1), (1, 2))),
            jnp.asarray(np.tile(np.concatenate([-sin, sin], -1), (1, 2))))


def _rope_tables_latent(t):
    cos, sin = _axial_tables(t, C_ROPE)
    ones = np.ones((t, C_NOPE), np.float32)
    tail = np.ones((t, HEAD_PAD - C_NOPE - C_ROPE), np.float32)
    cosc = np.concatenate([ones, cos, cos, tail], -1)
    sinc = np.concatenate([0 * ones, -sin, sin, 0 * tail], -1)
    return jnp.asarray(cosc), jnp.asarray(sinc)


def _group_ones(width, group):
    return jnp.asarray(np.kron(np.eye(width // group), np.ones((group, group))), BF16)


def _pad_heads(a, heads, dim):
    a = a.reshape(a.shape[:-1] + (heads, dim))
    a = jnp.pad(a, [(0, 0)] * (a.ndim - 1) + [(0, HEAD_PAD - dim)])
    return a.reshape(a.shape[:-2] + (heads * HEAD_PAD,))


def _logit_bound(gq, gk, dim):
    return 1.02 * LOG2E * math.sqrt(dim) * jnp.max(jnp.abs(gq)) * jnp.max(jnp.abs(gk))


def _pad_row(v, width):
    return jnp.pad(v, (0, width - v.shape[0]))


def _ffn_weights(w_up, conv_w, conv_b, w_down):
    cv = jnp.concatenate([conv_w, conv_b[None]], axis=0)
    return w_up.astype(BF16), cv, w_down.astype(BF16)


def kernel(x, c, ctx, c_ctx, ada_w, ada_b, norm_mix, norm_ffn, ffn_up, ffn_conv_w, ffn_conv_b, ffn_down, ev_w_in, ev_w_out, a_q_norm, a_k_norm, a_lam_q1, a_lam_k1, a_lam_q2, a_lam_k2, a_subln, b_q_norm, b_k_norm, b_rpb, od_w_in, od_w_out, c_q_a_norm, c_w_qb, c_kv_a_norm, c_w_kvb, c_q_norm, c_k_norm, d_q_norm, d_k_norm, d_sink):
    assert x.shape[0] == 1 and ctx.shape[0] == 1
    xm = x[0]
    xc = ctx[0]
    t, d = xm.shape
    lc = xc.shape[0]
    mods = _modulation(c, c_ctx, ada_w, ada_b)
    cos64, sin64 = _rope_tables_head64(t)
    cosc, sinc = _rope_tables_latent(t)
    g64 = _group_ones(MXU_TILE, 64)
    g128 = _group_ones(MXU_TILE, HEAD_PAD)
    dummy_tab = jnp.zeros((lc, LANES), F32)

    for l in range(DEPTH):
        last = l == DEPTH - 1
        i = l // 2
        mm = mods[l, 0].reshape(6, d)
        mc = mods[l, 1].reshape(6, d)
        pre_m = jnp.stack([norm_mix[l], mm[1], mm[0]])
        pre_c = jnp.stack([norm_mix[l], mc[1], mc[0]])
        post_m = jnp.stack([mm[2], norm_ffn[l], mm[4], mm[3], mm[5]])
        post_c = jnp.stack([mc[2], norm_ffn[l], mc[4], mc[3], mc[5]])
        wup, cv, wdn = _ffn_weights(ffn_up[l], ffn_conv_w[l], ffn_conv_b[l], ffn_down[l])
        if l % 2 == 0:
            lam_init = 0.8 - 0.6 * math.exp(-0.3 * l)
            w_in = ev_w_in[i].astype(BF16)
            w_out = ev_w_out[i].astype(BF16)
            hg = jnp.stack([jnp.tile(a_q_norm[i], 8), jnp.tile(a_k_norm[i], 8),
                            jnp.tile(b_q_norm[i], 8), jnp.tile(b_k_norm[i], 8)])
            ex = jnp.stack([_pad_row(a_lam_q1[i], LANES), _pad_row(a_lam_k1[i], LANES),
                            _pad_row(a_lam_q2[i], LANES), _pad_row(a_lam_k2[i], LANES)])
            gcol = a_subln[i][:, None]
            qa, ka, va, qb, kb, vb = _premix_even(xm, pre_m, w_in, g64, hg, cos64, sin64, True)
            qa_c, ka_c, va_c, qb_c, kb_c, vb_c = _premix_even(xc, pre_c, w_in, g64, hg, dummy_tab, dummy_tab, False)
            oa = _flash(qa, ka, va, ka_c, va_c, ex, gcol, _logit_bound(a_q_norm[i], a_k_norm[i], A_DQK),
                        mode="diff", lam_init=lam_init)
            qk_b = _logit_bound(b_q_norm[i], b_k_norm[i], B_DH)
            ob = _neighbourhood(qb, kb, vb, kb_c, vb_c, _na_bias_table(b_rpb[i], t // GRID_W), qk_b,
                                qk_b + LOG2E * jnp.maximum(jnp.max(b_rpb[i]), 0.0))
            if not last:
                oa_c = _flash(qa_c, ka_c, va_c, None, None, ex, gcol, mode="diff", lam_init=lam_init)
                ob_c = _flash(qb_c, kb_c, vb_c, None, None, ex, gcol, mode="pair")
        else:
            w = od_w_in[i]
            cq, ckv, ckr, dq, dk, dv = jnp.split(w, np.cumsum([C_Q_LORA, C_KV_LORA, C_ROPE, 512, 128]).tolist(), axis=1)
            dup = lambda a: jnp.concatenate([a[:, :64], a[:, :64], a[:, 64:], a[:, 64:]], axis=1)
            ckr_pad = jnp.pad(ckr, ((0, 0), (C_NOPE, HEAD_PAD - C_NOPE - C_ROPE)))
            w_in = jnp.concatenate([cq, ckv, dq, dup(dk), dv, ckr_pad], axis=1).astype(BF16)
            w_out = od_w_out[i].astype(BF16)
            wqb = _pad_heads(c_w_qb[i], C_HEADS, C_NOPE + C_ROPE).astype(BF16)
            kvb = c_w_kvb[i].reshape(C_KV_LORA, C_HEADS, C_NOPE + C_DV)
            wk = _pad_heads(kvb[:, :, :C_NOPE].reshape(C_KV_LORA, -1), C_HEADS, C_NOPE).astype(BF16)
            wv = kvb[:, :, C_NOPE:].reshape(C_KV_LORA, C_HEADS * C_DV).astype(BF16)
            gains = jnp.stack([jnp.tile(_pad_row(c_q_norm[i], HEAD_PAD), C_HEADS),
                               jnp.tile(_pad_row(c_k_norm[i], HEAD_PAD), C_HEADS),
                               _pad_row(jnp.tile(d_q_norm[i], 8), 1024), _pad_row(jnp.tile(d_k_norm[i], 4), 1024)])
            lora = jnp.stack([_pad_row(c_q_a_norm[i], 512), _pad_row(c_kv_a_norm[i], 512)])
            odd = functools.partial(_premix_odd, w=w_in, wqb=wqb, wk=wk, wv=wv, g128=g128, g64=g64, gains=gains, lora=lora)
            mq, mk, mv, gq, gk, gv = odd(xm, pre_m, cos=cos64, sin=sin64, cosc=cosc, sinc=sinc, use_rope=True)
            mq_c, mk_c, mv_c, gq_c, gk_c, gv_c = odd(xc, pre_c, cos=dummy_tab, sin=dummy_tab, cosc=dummy_tab,
                                                     sinc=dummy_tab, use_rope=False)
            ex = jnp.zeros((4, LANES), F32)
            oa = _flash(mq, mk, mv, mk_c, mv_c, ex, jnp.ones((HEAD_PAD, 1), F32),
                        _logit_bound(c_q_norm[i], c_k_norm[i], C_NOPE + C_ROPE), mode="mla")
            ob = _window_gqa(gq, gk, gv, gk_c, gv_c, d_sink[i], _logit_bound(d_q_norm[i], d_k_norm[i], D_DH))
            if not last:
                raise NotImplementedError("context update after an odd layer is not needed at this depth")
        xm = _mix_ffn(oa, ob, xm, w_out, post_m, wup, cv, wdn)
        if not last:
            xc = _mix_ffn(oa_c, ob_c, xc, w_out, post_c, wup, cv, wdn)
    return xm[None]
```

```python
import functools
import math

import numpy as np
import jax
import jax.numpy as jnp
from jax import lax
from jax.experimental import pallas as pl
from jax.experimental.pallas import tpu as pltpu

DEPTH = 2
GRID_W = 64
EPS = 1e-6
ROPE_BASE = 10000.0
NEG_INF = -1e30
LOG2E = math.log2(math.e)

A_DQK = 64
A_DV = 128
B_DH = 64
NA_WIN_H = 8
NA_WIN_W = 16
C_HEADS = 8
C_Q_LORA = 384
C_KV_LORA = 256
C_NOPE = 64
C_ROPE = 32
C_DV = 64
D_HEADS = 8
D_KV_HEADS = 2
D_DH = 64
D_WINDOW = 128

LANES = 128
MXU_TILE = 256
HEAD_PAD = 128
VMEM_LIMIT = 56 * 1024 * 1024
FFN_CHUNK = 256
HALO = 16
NA_ROWS = 4
WG_Q = 256
ONES_ROWS = 16
FLASH_TQ = 4096
FLASH_TQ_ONLINE = 1024
FLASH_TK = 512
FLASH_UNROLL = 4
FLASH_SAFE_BOUND = 40.0

F32 = jnp.float32
BF16 = jnp.bfloat16


def _cparams(*sem):
    return pltpu.CompilerParams(dimension_semantics=sem, vmem_limit_bytes=VMEM_LIMIT)


def _full(shape):
    n = len(shape)
    return pl.BlockSpec(shape, lambda *_: (0,) * n)


def _resident(shape):
    n = len(shape)
    return pl.BlockSpec(shape, lambda *_: (0,) * n, pipeline_mode=pl.Buffered(1))


def _dot(a, b):
    return jnp.dot(a, b, preferred_element_type=F32)


def _rms(x):
    return x * lax.rsqrt(jnp.mean(x * x, axis=-1, keepdims=True) + EPS)


def _sigmoid(x):
    return 1.0 / (1.0 + jnp.exp(-x))


def _group_rms(y, gmat, inv_n, gain, scale=1.0):
    yy = (y * y).astype(BF16)
    slab = gmat.shape[0]
    ss = jnp.concatenate([_dot(yy[:, c:c + slab], gmat) for c in range(0, y.shape[-1], slab)], axis=-1)
    return y * lax.rsqrt(ss + EPS / inv_n) * (gain * (scale / math.sqrt(inv_n)))


def _rope(y, cos, sin, half, first_half):
    w = y.shape[-1]
    fwd = pltpu.roll(y, w - half, axis=1)
    bwd = pltpu.roll(y, half, axis=1)
    return y * cos + jnp.where(first_half, fwd, bwd) * sin


def _lane_tile(t, width):
    reps = width // t.shape[-1]
    return t if reps == 1 else jnp.concatenate([t] * reps, axis=-1)


def _lane_iota(width):
    return lax.broadcasted_iota(jnp.int32, (1, width), 1)


def _mod_kernel(ct_ref, w_ref, b_ref, o_ref):
    ct = ct_ref[...]
    s = ct * _sigmoid(ct)
    w = w_ref[0]
    rows = [jnp.sum(w * s[:, j:j + 1], axis=0, keepdims=True) for j in range(2)]
    o_ref[0] = jnp.concatenate(rows, axis=0) + b_ref[0]


def _modulation(c, c_ctx, ada_w, ada_b):
    depth, d, n = ada_w.shape
    tn = 1536
    ct = jnp.stack([c[0], c_ctx], axis=1)
    return pl.pallas_call(
        _mod_kernel,
        grid=(depth, n // tn),
        in_specs=[_full((d, 2)),
                  pl.BlockSpec((1, d, tn), lambda l, j: (l, 0, j)),
                  pl.BlockSpec((1, 1, tn), lambda l, j: (l, 0, j))],
        out_specs=pl.BlockSpec((1, 2, tn), lambda l, j: (l, 0, j)),
        out_shape=jax.ShapeDtypeStruct((depth, 2, n), F32),
        compiler_params=_cparams("arbitrary", "arbitrary"),
        name="modulation",
    )(ct, ada_w, ada_b.reshape(depth, 1, n))


def _store_values_t(vt_ref, v, dv):
    tm = v.shape[0]
    vt = v.T.astype(BF16)
    ones = jnp.ones((ONES_ROWS, tm), BF16)
    step = dv + ONES_ROWS
    for g in range(vt_ref.shape[0]):
        for h in range(HEAD_PAD // dv):
            src = g * HEAD_PAD + h * dv
            vt_ref[g, 0, h * step:h * step + dv, :] = vt[src:src + dv]
            vt_ref[g, 0, h * step + dv:(h + 1) * step, :] = ones


def _values_t_shape(t, tm, dv, groups=4):
    return (groups, t // tm, (HEAD_PAD // dv) * (dv + ONES_ROWS), tm)


def _premix_even_kernel(x_ref, vec_ref, w_ref, g_ref, hg_ref, cos_ref, sin_ref,
                        qa_ref, ka_ref, vat_ref, qb_ref, kb_ref, vbt_ref, *, use_rope):
    vec = vec_ref[...]
    h = (_rms(x_ref[...]) * vec[0:1] * (1.0 + vec[1:2]) + vec[2:3]).astype(BF16)
    gm = g_ref[...]
    hg = hg_ref[...]
    sw = 512
    if use_rope:
        cos = _lane_tile(cos_ref[...], sw)
        sin = _lane_tile(sin_ref[...], sw)
        first = (_lane_iota(sw) & 32) == 0

    def seg(i):
        return _dot(h, w_ref[:, i * sw:(i + 1) * sw])

    qa = _group_rms(seg(0), gm, 1.0 / A_DQK, hg[0:1], LOG2E / math.sqrt(A_DQK))
    ka = _group_rms(seg(1), gm, 1.0 / A_DQK, hg[1:2])
    if use_rope:
        qa = _rope(qa, cos, sin, 32, first)
        ka = _rope(ka, cos, sin, 32, first)
    qa_ref[...] = qa.astype(BF16)
    ka_ref[...] = ka.astype(BF16)
    _store_values_t(vat_ref, seg(2), A_DV)
    qb_ref[...] = _group_rms(seg(3), gm, 1.0 / B_DH, hg[2:3], LOG2E / math.sqrt(B_DH)).astype(BF16)
    kb_ref[...] = _group_rms(seg(4), gm, 1.0 / B_DH, hg[3:4]).astype(BF16)
    _store_values_t(vbt_ref, seg(5), B_DH)


def _premix_even(x, vec, w, gmat, hg, cos, sin, use_rope):
    t, d = x.shape
    tm = min(FLASH_TK, t)
    row = lambda i: (i, 0)
    nat = jax.ShapeDtypeStruct((t, 512), BF16)
    nat_spec = pl.BlockSpec((tm, 512), row)
    vta = _values_t_shape(t, tm, A_DV)
    vtb = _values_t_shape(t, tm, B_DH)
    vt_spec = lambda s: pl.BlockSpec((s[0], 1) + s[2:], lambda i: (0, i, 0, 0))
    return pl.pallas_call(
        functools.partial(_premix_even_kernel, use_rope=use_rope),
        grid=(t // tm,),
        in_specs=[pl.BlockSpec((tm, d), row), _full(vec.shape), _full(w.shape), _full(gmat.shape),
                  _full(hg.shape), pl.BlockSpec((tm, LANES), row), pl.BlockSpec((tm, LANES), row)],
        out_specs=[nat_spec, nat_spec, vt_spec(vta), nat_spec, nat_spec, vt_spec(vtb)],
        out_shape=[nat, nat, jax.ShapeDtypeStruct(vta, BF16), nat, nat, jax.ShapeDtypeStruct(vtb, BF16)],
        compiler_params=_cparams("parallel"),
        name="premix_even",
    )(x, vec, w, gmat, hg, cos, sin)


_O_CQ, _O_CKV, _O_DQ, _O_DK, _O_DV, _O_CKR, _O_END = 0, 384, 640, 1152, 1408, 1536, 1664


def _premix_odd_kernel(x_ref, vec_ref, w_ref, wqb_ref, wk_ref, wv_ref, g128_ref, g64_ref, gain_ref, lora_ref,
                       cos_ref, sin_ref, cosc_ref, sinc_ref,
                       mq_ref, mk_ref, mvt_ref, gq_ref, gk_ref, gvt_ref, *, use_rope):
    vec = vec_ref[...]
    h = (_rms(x_ref[...]) * vec[0:1] * (1.0 + vec[1:2]) + vec[2:3]).astype(BF16)
    gains = gain_ref[...]
    lora = lora_ref[...]
    g128 = g128_ref[...]
    g64 = g64_ref[...]
    mw = C_HEADS * HEAD_PAD
    if use_rope:
        cosc = _lane_tile(cosc_ref[...], mw)
        sinc = _lane_tile(sinc_ref[...], mw)
        first_c = (_lane_iota(mw) & 127) < (C_NOPE + C_ROPE // 2)
        cos = _lane_tile(cos_ref[...], 512)
        sin = _lane_tile(sin_ref[...], 512)
        first_d = (_lane_iota(512) & 32) == 0

    def seg(a, b):
        return _dot(h, w_ref[:, a:b])

    inv_c = 1.0 / (C_NOPE + C_ROPE)
    half_w = mw // 2
    cq = (_rms(seg(_O_CQ, _O_CKV)) * lora[0:1, :C_Q_LORA]).astype(BF16)
    for c in range(0, mw, half_w):
        cs = slice(c, c + half_w)
        mq = _group_rms(_dot(cq, wqb_ref[:, cs]), g128, inv_c, gains[0:1, cs], LOG2E * math.sqrt(inv_c))
        if use_rope:
            mq = _rope(mq, cosc[:, cs], sinc[:, cs], C_ROPE // 2, first_c[:, cs])
        mq_ref[:, cs] = mq.astype(BF16)
    ckv = (_rms(seg(_O_CKV, _O_DQ)) * lora[1:2, :C_KV_LORA]).astype(BF16)
    ckr = _lane_tile(seg(_O_CKR, _O_END), half_w)
    for c in range(0, mw, half_w):
        cs = slice(c, c + half_w)
        mk = _group_rms(_dot(ckv, wk_ref[:, cs]) + ckr, g128, inv_c, gains[1:2, cs])
        if use_rope:
            mk = _rope(mk, cosc[:, cs], sinc[:, cs], C_ROPE // 2, first_c[:, cs])
        mk_ref[:, cs] = mk.astype(BF16)
    _store_values_t(mvt_ref, _dot(ckv, wv_ref[...]), C_DV)
    gq = _group_rms(seg(_O_DQ, _O_DK), g64, 1.0 / D_DH, gains[2:3, :512], LOG2E / math.sqrt(D_DH))
    gk = _group_rms(seg(_O_DK, _O_DV), g64, 1.0 / D_DH, gains[3:4, :256])
    if use_rope:
        gq = _rope(gq, cos, sin, 32, first_d)
        gk = _rope(gk, cos[:, :256], sin[:, :256], 32, first_d[:, :256])
    gq_ref[...] = gq.astype(BF16)
    gk_ref[...] = gk.astype(BF16)
    _store_values_t(gvt_ref, seg(_O_DV, _O_CKR), D_DH)


def _premix_odd(x, vec, w, wqb, wk, wv, g128, g64, gains, lora, cos, sin, cosc, sinc, use_rope):
    t, d = x.shape
    tm = min(FLASH_TK, t)
    row = lambda i: (i, 0)
    tab = pl.BlockSpec((tm, LANES), row)
    nat = lambda n: (pl.BlockSpec((tm, n), row), jax.ShapeDtypeStruct((t, n), BF16))
    def vt(shape):
        return pl.BlockSpec((shape[0], 1) + shape[2:], lambda i: (0, i, 0, 0)), jax.ShapeDtypeStruct(shape, BF16)

    outs = [nat(1024), nat(1024), vt(_values_t_shape(t, tm, C_DV)), nat(512), nat(256),
            vt(_values_t_shape(t, tm, D_DH, groups=1))]
    return pl.pallas_call(
        functools.partial(_premix_odd_kernel, use_rope=use_rope),
        grid=(t // tm,),
        in_specs=[pl.BlockSpec((tm, d), row)] + [_full(a.shape) for a in (vec, w, wqb, wk, wv, g128, g64, gains, lora)]
                 + [tab] * 4,
        out_specs=[o[0] for o in outs],
        out_shape=[o[1] for o in outs],
        compiler_params=_cparams("parallel"),
        name="premix_odd",
    )(x, vec, w, wqb, wk, wv, g128, g64, gains, lora, cos, sin, cosc, sinc)


def _flash_kernel(*refs, mode, has_ctx, online, lam_init):
    bound_ref, q_ref, k_ref, vt_ref = refs[:4]
    kc_ref, vct_ref = refs[4:6] if has_ctx else (None, None)
    ex_ref, gcol_ref, o_ref, acc_ref, stage_ref = refs[6:11] if has_ctx else refs[4:9]
    qt = q_ref[...].astype(F32).T.astype(BF16)
    row = lax.broadcasted_iota(jnp.int32, (HEAD_PAD, 1), 0)
    if mode == "mla":
        qs = [qt[:HEAD_PAD], qt[HEAD_PAD:]]
    else:
        zero = jnp.zeros_like(qt)
        qs = [jnp.where(row < 64, qt, zero), jnp.where(row >= 64, qt, zero)]
    nk, tk = vt_ref.shape[1], vt_ref.shape[3]

    acc_ref[...] = jnp.zeros(acc_ref.shape, F32)
    rows_b = acc_ref.shape[1]

    def scores(kblk, b):
        kb = kblk[:, b * HEAD_PAD:(b + 1) * HEAD_PAD] if mode == "mla" else kblk
        return _dot(kb, qs[b])

    def values(vtblk, b):
        return vtblk if vtblk.shape[0] == rows_b else vtblk[b * rows_b:(b + 1) * rows_b]

    if online:
        m_ref, mx_ref = refs[-2:]
        m_ref[...] = jnp.full(m_ref.shape, NEG_INF, F32)

        def absorb(s, smax, vtblk, b):
            m_prev = m_ref[b]
            m_new = jnp.maximum(m_prev, smax)
            alpha = jnp.exp2(m_prev - m_new)
            p = jnp.exp2(s - m_new).astype(BF16)
            acc_ref[b] = alpha * acc_ref[b] + _dot(values(vtblk, b), p)
            m_ref[b] = m_new

        def produce(slot, kblk):
            for b in range(2):
                s = scores(kblk, b)
                stage_ref[slot, b] = s
                mx_ref[slot, b] = jnp.max(s, axis=0, keepdims=True)

        def consume(slot, vtblk):
            for b in range(2):
                absorb(stage_ref[slot, b], mx_ref[slot, b], vtblk, b)

        def direct(kblk, vtblk):
            for b in range(2):
                s = scores(kblk, b)
                absorb(s, jnp.max(s, axis=0, keepdims=True), vtblk, b)
    else:
        bound = bound_ref[0]
        dv = rows_b - ONES_ROWS
        vpu_den = dv == HEAD_PAD
        lpart_ref = refs[-1] if vpu_den else None

        def probs(kblk, b):
            p = jnp.exp2(scores(kblk, b) - bound)
            return p.astype(BF16), (jnp.sum(p, axis=0, keepdims=True) if vpu_den else None)

        def accumulate(b, vtblk, p, psum):
            if vpu_den:
                acc_ref[b, :dv] += _dot(values(vtblk, b)[:dv], p)
                acc_ref[b, dv:dv + 1] += psum
            else:
                acc_ref[b] += _dot(values(vtblk, b), p)

        def produce(slot, kblk):
            for b in range(2):
                p, psum = probs(kblk, b)
                stage_ref[slot, b] = p
                if vpu_den:
                    lpart_ref[slot, b] = psum

        def consume(slot, vtblk):
            for b in range(2):
                accumulate(b, vtblk, stage_ref[slot, b], lpart_ref[slot, b] if vpu_den else None)

        def direct(kblk, vtblk):
            for b in range(2):
                accumulate(b, vtblk, *probs(kblk, b))

    if nk == 1:
        if has_ctx:
            direct(kc_ref[...], vct_ref[0])
        direct(k_ref[...], vt_ref[0, 0])
    else:
        unroll = FLASH_UNROLL if nk % FLASH_UNROLL == 0 else 2
        assert nk % unroll == 0
        produce(0, k_ref[0:tk, :])
        if has_ctx:
            direct(kc_ref[...], vct_ref[0])

        def body(i, carry):
            c0 = unroll * i
            for u in range(unroll):
                nxt = jnp.minimum(c0 + u + 1, nk - 1)
                produce((u + 1) % 2, k_ref[pl.ds(pl.multiple_of(nxt * tk, tk), tk), :])
                consume(u % 2, vt_ref[0, c0 + u])
            return carry

        lax.fori_loop(0, nk // unroll, body, 0)

    dv = rows_b - ONES_ROWS
    o0 = acc_ref[0, :dv] / acc_ref[0, dv:dv + 1]
    o1 = acc_ref[1, :dv] / acc_ref[1, dv:dv + 1]
    if mode == "diff":
        ex = ex_ref[...]
        lam = (jnp.exp(jnp.sum(ex[0:1] * ex[1:2], axis=-1, keepdims=True))
               - jnp.exp(jnp.sum(ex[2:3] * ex[3:4], axis=-1, keepdims=True)) + lam_init)
        o = o0 - lam * o1
        o = o * lax.rsqrt(jnp.mean(o * o, axis=0, keepdims=True) + EPS) * gcol_ref[...] * (1.0 - lam_init)
    else:
        o = jnp.concatenate([o0, o1], axis=0)
    o_ref[...] = o.T.astype(o_ref.dtype)


def _flash(q, k, vt, kc, vct, ex, gcol, bound=None, *, mode, lam_init=0.0):
    t = q.shape[0]
    tkeys = k.shape[0]
    qw = 2 * HEAD_PAD if mode == "mla" else HEAD_PAD
    groups = q.shape[1] // qw
    _, nk, vrows, tk = vt.shape
    assert nk * tk == tkeys
    has_ctx = kc is not None
    dv = HEAD_PAD if mode == "diff" else HEAD_PAD // 2
    rows_b = dv + ONES_ROWS
    bound_arr = jnp.zeros((1,), F32) if bound is None else jnp.reshape(bound, (1,)).astype(F32)
    args = [bound_arr, q, k, vt]
    if has_ctx:
        lc = kc.shape[0]
        args += [kc, vct.reshape(groups, vrows, lc)]
    args += [ex, gcol]

    def call(online):
        tq = min(FLASH_TQ_ONLINE if online else FLASH_TQ, t)
        in_specs = [pl.BlockSpec(memory_space=pltpu.SMEM),
                    pl.BlockSpec((tq, qw), lambda g, i: (i, g)),
                    pl.BlockSpec((tkeys, qw), lambda g, i: (0, g), pipeline_mode=pl.Buffered(1)),
                    pl.BlockSpec((1, nk, vrows, tk), lambda g, i: (g, 0, 0, 0), pipeline_mode=pl.Buffered(1))]
        if has_ctx:
            in_specs += [pl.BlockSpec((lc, qw), lambda g, i: (0, g)),
                         pl.BlockSpec((1, vrows, lc), lambda g, i: (g, 0, 0))]
        in_specs += [_full(ex.shape), _full(gcol.shape)]
        acc = pltpu.VMEM((2, rows_b, tq), F32)
        if online:
            scratch = [acc, pltpu.VMEM((2, 2, tk, tq), F32), pltpu.VMEM((2, 1, tq), F32), pltpu.VMEM((2, 2, 1, tq), F32)]
        else:
            scratch = [acc, pltpu.VMEM((2, 2, tk, tq), BF16)]
            if dv == HEAD_PAD:
                scratch.append(pltpu.VMEM((2, 2, 1, tq), F32))
        return pl.pallas_call(
            functools.partial(_flash_kernel, mode=mode, has_ctx=has_ctx, online=online, lam_init=lam_init),
            grid=(groups, t // tq),
            in_specs=in_specs,
            out_specs=pl.BlockSpec((tq, HEAD_PAD), lambda g, i: (i, g)),
            out_shape=jax.ShapeDtypeStruct((t, groups * HEAD_PAD), BF16),
            scratch_shapes=scratch,
            compiler_params=_cparams("parallel", "parallel"),
            name="flash_" + mode + ("_online" if online else "_bounded"),
        )(*args)

    if bound is None:
        return call(True)
    return lax.cond(bound <= FLASH_SAFE_BOUND, lambda: call(False), lambda: call(True))


def _na_kernel(shift_ref, q_ref, kp_ref, kcur_ref, kn_ref, vtp_ref, vtcur_ref, vtn_ref, kc_ref, vct_ref, bias_ref,
               o_ref, stage_ref, *stat_refs, online):
    pairs = q_ref.shape[1] // HEAD_PAD
    qn = q_ref.shape[0]
    lc = kc_ref.shape[0]
    step = B_DH + ONES_ROWS
    row = lax.broadcasted_iota(jnp.int32, (HEAD_PAD, 1), 0)

    def produce(slot, pr):
        ps = slice(pr * HEAD_PAD, (pr + 1) * HEAD_PAD)
        qt = q_ref[:, ps].astype(F32).T.astype(BF16)
        zero = jnp.zeros_like(qt)
        q2 = jnp.concatenate([jnp.where(row < 64, qt, zero), jnp.where(row >= 64, qt, zero)], axis=1)
        k_nb = jnp.concatenate([kp_ref[:, ps], kcur_ref[:, ps], kn_ref[:, ps]], axis=0)
        bias = jnp.concatenate([bias_ref[0, 2 * pr], bias_ref[0, 2 * pr + 1]], axis=1)
        s_ctx = _dot(kc_ref[:, ps], q2)
        s_nb = _dot(k_nb, q2) + bias
        if online:
            stage_ref[slot, 0:lc] = s_ctx
            stage_ref[slot, lc:] = s_nb
            stat_refs[0][slot] = jnp.maximum(jnp.max(s_ctx, axis=0, keepdims=True),
                                             jnp.max(s_nb, axis=0, keepdims=True))
        else:
            shift = shift_ref[0]
            stage_ref[slot, 0:lc] = jnp.exp2(s_ctx - shift).astype(BF16)
            stage_ref[slot, lc:] = jnp.exp2(s_nb - shift).astype(BF16)

    def consume(slot, pr):
        vt = jnp.concatenate([vct_ref[pr, 0], vtp_ref[pr, 0], vtcur_ref[pr, 0], vtn_ref[pr, 0]], axis=1)
        if online:
            p = jnp.exp2(stage_ref[slot] - stat_refs[0][slot]).astype(BF16)
        else:
            p = stage_ref[slot]
        outs = []
        for b in range(2):
            acc = _dot(vt[b * step:(b + 1) * step], p[:, b * qn:(b + 1) * qn])
            outs.append(acc[:B_DH] / acc[B_DH:B_DH + 1])
        o_ref[:, pr * HEAD_PAD:(pr + 1) * HEAD_PAD] = jnp.concatenate(outs, axis=0).T.astype(o_ref.dtype)

    produce(0, 0)
    for pr in range(pairs):
        if pr + 1 < pairs:
            produce((pr + 1) % 2, pr + 1)
        consume(pr % 2, pr)


def _na_bias_table(rpb, rows):
    nr, w = NA_ROWS, GRID_W
    heads = rpb.shape[0]
    pad = w - NA_WIN_W
    padded = jnp.pad(rpb.astype(F32) * LOG2E, ((0, 0), (0, 0), (pad, pad + 1)))
    skew = jnp.tile(padded, (1, 1, w))[:, :, :w * (2 * w - 1)].reshape(heads, -1, w, 2 * w - 1)
    toep = skew[:, :, :, w - 1:]
    cq = np.arange(w)
    cs = np.clip(cq - NA_WIN_W // 2, 0, w - NA_WIN_W)
    valid_c = (cq[None, :] >= cs[:, None]) & (cq[None, :] < cs[:, None] + NA_WIN_W)
    toep_t = jnp.swapaxes(jnp.where(jnp.asarray(valid_c), toep, NEG_INF), -1, -2)
    d0 = NA_WIN_H - 1 - nr
    assert d0 - (nr - 1) >= 0 and d0 + 3 * nr <= 2 * NA_WIN_H - 1
    dense = jnp.concatenate([toep_t[:, d0 - rl:d0 - rl + 3 * nr] for rl in range(nr)], axis=-1)
    wh = min(NA_WIN_H, rows)
    rl = np.arange(nr)
    rr_rel = np.arange(3 * nr)
    valid_all = []
    for base in (0, nr, rows - nr):
        rs = np.clip(base + rl - wh // 2, 0, rows - wh)
        rr = base - nr + rr_rel
        valid_all.append((rr[:, None] >= rs[None, :]) & (rr[:, None] < rs[None, :] + wh))
    valid_r = np.repeat(np.stack(valid_all), w, axis=-1)[:, None, :, None, :]
    tab = jnp.where(jnp.asarray(valid_r), dense[None], NEG_INF)
    return tab.reshape(3, heads, 3 * nr * w, nr * w)


def _neighbourhood(q, k, vt, kc, vct, bias, qk_bound, shift):
    t = q.shape[0]
    qn = NA_ROWS * GRID_W
    nb = t // qn
    lc = kc.shape[0]
    pairs, _, vrows, tk = vt.shape
    per = tk // qn
    cur = lambda i: (i, 0)
    prev = lambda i: (jnp.maximum(i - 1, 0), 0)
    nxt = lambda i: (jnp.minimum(i + 1, nb - 1), 0)
    blk = lambda f: pl.BlockSpec((qn, q.shape[1]), f)
    vblk = lambda f: pl.BlockSpec((pairs, 1, vrows, qn), lambda i: (0, f(i)[0] // per, 0, f(i)[0] % per))
    case = lambda i: (jnp.where(i == 0, 0, jnp.where(i == nb - 1, 2, 1)), 0, 0, 0)
    nkeys = lc + 3 * qn

    def call(online):
        scratch = ([pltpu.VMEM((2, nkeys, 2 * qn), F32), pltpu.VMEM((2, 1, 2 * qn), F32)] if online
                   else [pltpu.VMEM((2, nkeys, 2 * qn), BF16)])
        return pl.pallas_call(
            functools.partial(_na_kernel, online=online),
            grid=(nb,),
            in_specs=[pl.BlockSpec(memory_space=pltpu.SMEM), blk(cur), blk(prev), blk(cur), blk(nxt),
                      vblk(prev), vblk(cur), vblk(nxt),
                      _full(kc.shape), _full(vct.shape), pl.BlockSpec((1,) + bias.shape[1:], case)],
            out_specs=blk(cur),
            out_shape=jax.ShapeDtypeStruct(q.shape, BF16),
            scratch_shapes=scratch,
            compiler_params=_cparams("arbitrary"),
            name="neighbourhood" + ("_online" if online else "_bounded"),
        )(jnp.reshape(shift, (1,)).astype(F32), q, k, k, k, vt, vt, vt, kc, vct, bias)

    return lax.cond(shift + qk_bound <= 2 * FLASH_SAFE_BOUND, lambda: call(False), lambda: call(True))


def _wgqa_kernel(shift_ref, sink_ref, q_ref, *refs, online):
    nkb = WG_Q // D_WINDOW + 2
    k_refs, vt_refs = refs[:nkb], refs[nkb:2 * nkb]
    kc_ref, vct_ref, mask_ref, o_ref, stage_ref = refs[2 * nkb:2 * nkb + 5]
    mx_ref = refs[-1]
    qn = q_ref.shape[0]
    pairs = D_HEADS // 2
    pairs_per_kv = pairs // D_KV_HEADS
    step = D_DH + ONES_ROWS
    row = lax.broadcasted_iota(jnp.int32, (HEAD_PAD, 1), 0)
    mask = jnp.concatenate([mask_ref[0]] * 2, axis=1)

    def produce(slot, pair):
        kv = pair // pairs_per_kv
        ks = slice(kv * HEAD_PAD, (kv + 1) * HEAD_PAD)
        qt = q_ref[:, pair * HEAD_PAD:(pair + 1) * HEAD_PAD].astype(F32).T.astype(BF16)
        zero = jnp.zeros_like(qt)
        q2 = jnp.concatenate([jnp.where(row < 64, qt, zero), jnp.where(row >= 64, qt, zero)], axis=1)
        kk = jnp.concatenate([kc_ref[:, ks]] + [r[:, ks] for r in k_refs], axis=0)
        s = _dot(kk, q2) + mask
        if online:
            stage_ref[slot] = s
            mx_ref[slot] = jnp.max(s, axis=0, keepdims=True)
        else:
            stage_ref[slot] = jnp.exp2(s - shift_ref[0]).astype(BF16)

    def consume(slot, pair):
        kv = pair // pairs_per_kv
        sk = jnp.concatenate([jnp.full((1, qn), sink_ref[2 * pair + b] * LOG2E, F32) for b in range(2)], axis=1)
        if online:
            m = jnp.maximum(mx_ref[slot], sk)
            p = jnp.exp2(stage_ref[slot] - m).astype(BF16)
        else:
            m = shift_ref[0]
            p = stage_ref[slot]
        vt = jnp.concatenate([vct_ref[0, 0]] + [r[0, 0] for r in vt_refs], axis=1)[kv * step:(kv + 1) * step]
        acc = _dot(vt, p)
        o = acc[:D_DH] / (acc[D_DH:D_DH + 1] + jnp.exp2(sk - m))
        both = jnp.concatenate([o[:, :qn], o[:, qn:]], axis=0)
        o_ref[:, pair * HEAD_PAD:(pair + 1) * HEAD_PAD] = both.T.astype(o_ref.dtype)

    produce(0, 0)
    for pair in range(pairs):
        if pair + 1 < pairs:
            produce((pair + 1) % 2, pair + 1)
        consume(pair % 2, pair)


def _wgqa_mask(lc):
    nkeys = WG_Q + 2 * D_WINDOW
    kb = np.arange(nkeys)[:, None]
    qi = np.arange(WG_Q)[None, :]
    band = np.abs(kb - D_WINDOW - qi) <= D_WINDOW
    cases = [band & (kb >= D_WINDOW), band, band & (kb < nkeys - D_WINDOW)]
    tab = np.where(np.stack(cases), 0.0, NEG_INF).astype(np.float32)
    return jnp.asarray(np.concatenate([np.zeros((3, lc, WG_Q), np.float32), tab], axis=1))


def _window_gqa(q, k, vt, kc, vct, sink, qk_bound):
    t = q.shape[0]
    nb = t // WG_Q
    per_q = WG_Q // D_WINDOW
    nkb = t // D_WINDOW
    lc = kc.shape[0]
    kw = k.shape[1]
    _, _, vrows, tk = vt.shape
    per_chunk = tk // D_WINDOW
    first_blk = [lambda i, j=j: jnp.clip(i * per_q - 1 + j, 0, nkb - 1) for j in range(per_q + 2)]
    kspecs = [pl.BlockSpec((D_WINDOW, kw), lambda i, f=f: (f(i), 0)) for f in first_blk]
    vspecs = [pl.BlockSpec((1, 1, vrows, D_WINDOW), lambda i, f=f: (0, f(i) // per_chunk, 0, f(i) % per_chunk))
              for f in first_blk]
    mask = _wgqa_mask(lc)
    case = lambda i: (jnp.where(i == 0, 0, jnp.where(i == nb - 1, 2, 1)), 0, 0)
    cur = lambda i: (i, 0)
    nkeys = lc + WG_Q + 2 * D_WINDOW
    lanes = 2 * WG_Q
    smem = pl.BlockSpec(memory_space=pltpu.SMEM)
    shift = jnp.maximum(qk_bound, LOG2E * jnp.max(sink))

    def call(online):
        stage = pltpu.VMEM((2, nkeys, lanes), F32 if online else BF16)
        return pl.pallas_call(
            functools.partial(_wgqa_kernel, online=online),
            grid=(nb,),
            in_specs=[smem, smem, pl.BlockSpec((WG_Q, q.shape[1]), cur)] + kspecs + vspecs
                     + [_full(kc.shape), _full(vct.shape), pl.BlockSpec((1,) + mask.shape[1:], case)],
            out_specs=pl.BlockSpec((WG_Q, q.shape[1]), cur),
            out_shape=jax.ShapeDtypeStruct(q.shape, BF16),
            scratch_shapes=[stage, pltpu.VMEM((2, 1, lanes), F32)],
            compiler_params=_cparams("parallel"),
            name="window_gqa" + ("_online" if online else "_bounded"),
        )(jnp.reshape(shift, (1,)).astype(F32), sink, q, *([k] * (per_q + 2)), *([vt] * (per_q + 2)), kc, vct, mask)

    return lax.cond(shift + qk_bound <= 2 * FLASH_SAFE_BOUND, lambda: call(False), lambda: call(True))


def _mix_ffn_kernel(o1_ref, o1p_ref, o1n_ref, o2_ref, o2p_ref, o2n_ref, x_ref, xp_ref, xn_ref,
                    wout_ref, vec_ref, wup_ref, cv_ref, wdn_ref, o_ref, acc_ref, hcat_ref, u_ref, x1_ref):
    i = pl.program_id(0)
    n = pl.num_programs(0)
    tm = x_ref.shape[0]
    rows = tm + 2 * HALO
    nchunk = wdn_ref.shape[0] // FFN_CHUNK
    half = o1_ref.shape[1]
    vec = vec_ref[...]
    o1 = jnp.concatenate([o1p_ref[...], o1_ref[...], o1n_ref[...]], axis=0)
    o2 = jnp.concatenate([o2p_ref[...], o2_ref[...], o2n_ref[...]], axis=0)
    x = jnp.concatenate([xp_ref[...], x_ref[...], xn_ref[...]], axis=0)
    x1 = x + vec[0:1] * (_dot(o1, wout_ref[:half, :]) + _dot(o2, wout_ref[half:, :]))
    x1_ref[...] = x1[HALO:HALO + tm]
    hcat_ref[...] = (_rms(x1) * vec[1:2] * (1.0 + vec[2:3]) + vec[3:4]).astype(BF16)

    @pl.when(i == 0)
    def _():
        hcat_ref[0:HALO] = jnp.zeros((HALO, hcat_ref.shape[1]), BF16)

    @pl.when(i == n - 1)
    def _():
        hcat_ref[HALO + tm:rows] = jnp.zeros((HALO, hcat_ref.shape[1]), BF16)

    acc_ref[...] = jnp.zeros(acc_ref.shape, F32)

    def conv(u, cv):
        before = pltpu.roll(u, 1, axis=0)
        after = pltpu.roll(u, rows - 1, axis=0)
        r = cv[3:4] + before * cv[0:1] + u * cv[1:2] + after * cv[2:3]
        return r[HALO:HALO + tm]

    def cols(c, gate):
        return pl.ds(pl.multiple_of((gate * nchunk + c) * FFN_CHUNK, FFN_CHUNK), FFN_CHUNK)

    def stage_up(slot, c):
        hcat = hcat_ref[...]
        u_ref[slot, 0] = _dot(hcat, wup_ref[:, cols(c, 0)])
        u_ref[slot, 1] = _dot(hcat, wup_ref[:, cols(c, 1)])

    def stage_down(slot, c):
        a = conv(u_ref[slot, 0], cv_ref[:, cols(c, 0)])
        g = conv(u_ref[slot, 1], cv_ref[:, cols(c, 1)])
        act = (a * (g * _sigmoid(g))).astype(BF16)
        acc_ref[...] += _dot(act, wdn_ref[cols(c, 0), :])

    assert nchunk % 2 == 1
    stage_up(0, 0)

    def body(j, carry):
        c = 2 * j
        stage_up(1, c + 1)
        stage_down(0, c)
        stage_up(0, c + 2)
        stage_down(1, c + 1)
        return carry

    lax.fori_loop(0, nchunk // 2, body, 0)
    stage_down(0, nchunk - 1)
    o_ref[...] = x1_ref[...] + vec[4:5] * acc_ref[...]


def _mix_ffn(o1, o2, x, wout, vec, wup, cv, wdn):
    t, d = x.shape
    tm = min(512, t)
    nt = t // tm
    hb = tm // HALO
    row = lambda i: (i, 0)
    prev = lambda i: (jnp.maximum(i * hb - 1, 0), 0)
    nxt = lambda i: (jnp.minimum((i + 1) * hb, t // HALO - 1), 0)
    tiled = lambda w: [pl.BlockSpec((tm, w), row), pl.BlockSpec((HALO, w), prev), pl.BlockSpec((HALO, w), nxt)]
    return pl.pallas_call(
        _mix_ffn_kernel,
        grid=(nt,),
        in_specs=tiled(o1.shape[1]) + tiled(o2.shape[1]) + tiled(d)
                 + [_resident(a.shape) for a in (wout, vec, wup, cv, wdn)],
        out_specs=pl.BlockSpec((tm, d), row),
        out_shape=jax.ShapeDtypeStruct((t, d), F32),
        scratch_shapes=[pltpu.VMEM((tm, d), F32), pltpu.VMEM((tm + 2 * HALO, d), BF16),
                        pltpu.VMEM((2, 2, tm + 2 * HALO, FFN_CHUNK), F32), pltpu.VMEM((tm, d), F32)],
        compiler_params=_cparams("parallel"),
        name="mix_ffn",
    )(o1, o1, o1, o2, o2, o2, x, x, x, wout, vec, wup, cv, wdn)


def _axial_tables(n_tokens, dim):
    t = np.arange(n_tokens)
    row = (t // GRID_W).astype(np.float32)
    col = (t % GRID_W).astype(np.float32)
    quarter = dim // 4
    inv_freq = np.float32(ROPE_BASE) ** (-np.arange(quarter, dtype=np.float32) / np.float32(quarter))
    ang = np.concatenate([row[:, None] * inv_freq, col[:, None] * inv_freq], axis=-1).astype(np.float32)
    return np.cos(ang), np.sin(ang)


def _rope_tables_head64(t):
    cos, sin = _axial_tables(t, 64)
    return (jnp.asarray(np.tile(np.concatenate([cos, cos], -1), (1, 2))),
            jnp.asarray(np.tile(np.concatenate([-sin, sin], -1), (1, 2))))


def _rope_tables_latent(t):
    cos, sin = _axial_tables(t, C_ROPE)
    ones = np.ones((t, C_NOPE), np.float32)
    tail = np.ones((t, HEAD_PAD - C_NOPE - C_ROPE), np.float32)
    cosc = np.concatenate([ones, cos, cos, tail], -1)
    sinc = np.concatenate([0 * ones, -sin, sin, 0 * tail], -1)
    return jnp.asarray(cosc), jnp.asarray(sinc)


def _group_ones(width, group):
    return jnp.asarray(np.kron(np.eye(width // group), np.ones((group, group))), BF16)


def _pad_heads(a, heads, dim):
    a = a.reshape(a.shape[:-1] + (heads, dim))
    a = jnp.pad(a, [(0, 0)] * (a.ndim - 1) + [(0, HEAD_PAD - dim)])
    return a.reshape(a.shape[:-2] + (heads * HEAD_PAD,))


def _logit_bound(gq, gk, dim):
    return 1.02 * LOG2E * math.sqrt(dim) * jnp.max(jnp.abs(gq)) * jnp.max(jnp.abs(gk))


def _pad_row(v, width):
    return jnp.pad(v, (0, width - v.shape[0]))


def _ffn_weights(w_up, conv_w, conv_b, w_down):
    cv = jnp.concatenate([conv_w, conv_b[None]], axis=0)
    return w_up.astype(BF16), cv, w_down.astype(BF16)


def kernel(x, c, ctx, c_ctx, ada_w, ada_b, norm_mix, norm_ffn, ffn_up, ffn_conv_w, ffn_conv_b, ffn_down, ev_w_in, ev_w_out, a_q_norm, a_k_norm, a_lam_q1, a_lam_k1, a_lam_q2, a_lam_k2, a_subln, b_q_norm, b_k_norm, b_rpb, od_w_in, od_w_out, c_q_a_norm, c_w_qb, c_kv_a_norm, c_w_kvb, c_q_norm, c_k_norm, d_q_norm, d_k_norm, d_sink):
    assert x.shape[0] == 1 and ctx.shape[0] == 1
    xm = x[0]
    xc = ctx[0]
    t, d = xm.shape
    lc = xc.shape[0]
    mods = _modulation(c, c_ctx, ada_w, ada_b)
    cos64, sin64 = _rope_tables_head64(t)
    cosc, sinc = _rope_tables_latent(t)
    g64 = _group_ones(MXU_TILE, 64)
    g128 = _group_ones(MXU_TILE, HEAD_PAD)
    dummy_tab = jnp.zeros((lc, LANES), F32)

    for l in range(DEPTH):
        last = l == DEPTH - 1
        i = l // 2
        mm = mods[l, 0].reshape(6, d)
        mc = mods[l, 1].reshape(6, d)
        pre_m = jnp.stack([norm_mix[l], mm[1], mm[0]])
        pre_c = jnp.stack([norm_mix[l], mc[1], mc[0]])
        post_m = jnp.stack([mm[2], norm_ffn[l], mm[4], mm[3], mm[5]])
        post_c = jnp.stack([mc[2], norm_ffn[l], mc[4], mc[3], mc[5]])
        wup, cv, wdn = _ffn_weights(ffn_up[l], ffn_conv_w[l], ffn_conv_b[l], ffn_down[l])
        if l % 2 == 0:
            lam_init = 0.8 - 0.6 * math.exp(-0.3 * l)
            w_in = ev_w_in[i].astype(BF16)
            w_out = ev_w_out[i].astype(BF16)
            hg = jnp.stack([jnp.tile(a_q_norm[i], 8), jnp.tile(a_k_norm[i], 8),
                            jnp.tile(b_q_norm[i], 8), jnp.tile(b_k_norm[i], 8)])
            ex = jnp.stack([_pad_row(a_lam_q1[i], LANES), _pad_row(a_lam_k1[i], LANES),
                            _pad_row(a_lam_q2[i], LANES), _pad_row(a_lam_k2[i], LANES)])
            gcol = a_subln[i][:, None]
            qa, ka, va, qb, kb, vb = _premix_even(xm, pre_m, w_in, g64, hg, cos64, sin64, True)
            qa_c, ka_c, va_c, qb_c, kb_c, vb_c = _premix_even(xc, pre_c, w_in, g64, hg, dummy_tab, dummy_tab, False)
            oa = _flash(qa, ka, va, ka_c, va_c, ex, gcol, _logit_bound(a_q_norm[i], a_k_norm[i], A_DQK),
                        mode="diff", lam_init=lam_init)
            qk_b = _logit_bound(b_q_norm[i], b_k_norm[i], B_DH)
            ob = _neighbourhood(qb, kb, vb, kb_c, vb_c, _na_bias_table(b_rpb[i], t // GRID_W), qk_b,
                                qk_b + LOG2E * jnp.maximum(jnp.max(b_rpb[i]), 0.0))
            if not last:
                oa_c = _flash(qa_c, ka_c, va_c, None, None, ex, gcol, mode="diff", lam_init=lam_init)
                ob_c = _flash(qb_c, kb_c, vb_c, None, None, ex, gcol, mode="pair")
        else:
            w = od_w_in[i]
            cq, ckv, ckr, dq, dk, dv = jnp.split(w, np.cumsum([C_Q_LORA, C_KV_LORA, C_ROPE, 512, 128]).tolist(), axis=1)
            dup = lambda a: jnp.concatenate([a[:, :64], a[:, :64], a[:, 64:], a[:, 64:]], axis=1)
            ckr_pad = jnp.pad(ckr, ((0, 0), (C_NOPE, HEAD_PAD - C_NOPE - C_ROPE)))
            w_in = jnp.concatenate([cq, ckv, dq, dup(dk), dv, ckr_pad], axis=1).astype(BF16)
            w_out = od_w_out[i].astype(BF16)
            wqb = _pad_heads(c_w_qb[i], C_HEADS, C_NOPE + C_ROPE).astype(BF16)
            kvb = c_w_kvb[i].reshape(C_KV_LORA, C_HEADS, C_NOPE + C_DV)
            wk = _pad_heads(kvb[:, :, :C_NOPE].reshape(C_KV_LORA, -1), C_HEADS, C_NOPE).astype(BF16)
            wv = kvb[:, :, C_NOPE:].reshape(C_KV_LORA, C_HEADS * C_DV).astype(BF16)
            gains = jnp.stack([jnp.tile(_pad_row(c_q_norm[i], HEAD_PAD), C_HEADS),
                               jnp.tile(_pad_row(c_k_norm[i], HEAD_PAD), C_HEADS),
                               _pad_row(jnp.tile(d_q_norm[i], 8), 1024), _pad_row(jnp.tile(d_k_norm[i], 4), 1024)])
            lora = jnp.stack([_pad_row(c_q_a_norm[i], 512), _pad_row(c_kv_a_norm[i], 512)])
            odd = functools.partial(_premix_odd, w=w_in, wqb=wqb, wk=wk, wv=wv, g128=g128, g64=g64, gains=gains, lora=lora)
            mq, mk, mv, gq, gk, gv = odd(xm, pre_m, cos=cos64, sin=sin64, cosc=cosc, sinc=sinc, use_rope=True)
            mq_c, mk_c, mv_c, gq_c, gk_c, gv_c = odd(xc, pre_c, cos=dummy_tab, sin=dummy_tab, cosc=dummy_tab,
                                                     sinc=dummy_tab, use_rope=False)
            ex = jnp.zeros((4, LANES), F32)
            oa = _flash(mq, mk, mv, mk_c, mv_c, ex, jnp.ones((HEAD_PAD, 1), F32),
                        _logit_bound(c_q_norm[i], c_k_norm[i], C_NOPE + C_ROPE), mode="mla")
            ob = _window_gqa(gq, gk, gv, gk_c, gv_c, d_sink[i], _logit_bound(d_q_norm[i], d_k_norm[i], D_DH))
            if not last:
                raise NotImplementedError("context update after an odd layer is not needed at this depth")
        xm = _mix_ffn(oa, ob, xm, w_out, post_m, wup, cv, wdn)
        if not last:
            xc = _mix_ffn(oa_c, ob_c, xc, w_out, post_c, wup, cv, wdn)
    return xm[None]
```
